```python
import math
import jax, jax.numpy as jnp
from jax import lax
import numpy as np

D_MODEL = 2048
BATCH = 8
SEQ = 2048
DEPTH = 1
DEC_BATCH = 32
DEC_SEQ = 32
PAST_LEN = 1024

CHUNK = 64
EPS = 1e-6
SSD_INNER = 2 * D_MODEL
SSD_HEAD_DIM = 64
SSD_HEADS = SSD_INNER // SSD_HEAD_DIM
SSD_GROUPS = 8
SSD_STATE = 128
CONV_W = 4
CONV_DIM = SSD_INNER + 2 * SSD_GROUPS * SSD_STATE
MLA_HEADS = 16
Q_LORA = 512
KV_LORA = 512
QK_NOPE = 128
QK_ROPE = 64
V_HEAD = 128
ROPE_THETA = 10000.0
ATTN_Q_BLOCK = 128
N_EXPERTS = 64
TOP_K = 8
N_GROUPS = 8
TOPK_GROUPS = 4
EXPERT_HIDDEN = 512
SHARED_HIDDEN = 512
ROUTED_SCALE = 2.5
MOE_BLOCK = 128
OFF_XBC = SSD_INNER
OFF_DT = OFF_XBC + CONV_DIM
OFF_CQ = OFF_DT + SSD_HEADS
OFF_CKV = OFF_CQ + Q_LORA
OFF_KR = OFF_CKV + KV_LORA
OFF_GATE = OFF_KR + QK_ROPE
IN_TOTAL = OFF_GATE + 2 * D_MODEL

kernel_name = "streaming_ssd_mla_moe_hybrid_step"


def rms_norm(x, g):
    xf = x.astype(jnp.float32)
    y = xf * lax.rsqrt(jnp.mean(xf * xf, axis=-1, keepdims=True) + EPS)
    return (y * g.astype(jnp.float32)).astype(x.dtype)


def rope(x, pos):
    half = QK_ROPE // 2
    freqs = ROPE_THETA ** (-jnp.arange(half, dtype=jnp.float32) / half)
    ang = pos.astype(jnp.float32)[:, None] * freqs[None, :]
    ang = ang.reshape((pos.shape[0],) + (1,) * (x.ndim - 3) + (half,))
    cos, sin = jnp.cos(ang), jnp.sin(ang)
    xf = x.astype(jnp.float32)
    x1, x2 = xf[..., :half], xf[..., half:]
    return jnp.concatenate([x1 * cos - x2 * sin, x2 * cos + x1 * sin], axis=-1).astype(x.dtype)


def causal_conv(xbc, buf, w, bias):
    xp = jnp.concatenate([buf, xbc], axis=1)
    L = xbc.shape[1]
    y = bias
    for k in range(CONV_W):
        y = y + xp[:, k:k + L] * w[k]
    return jax.nn.silu(y), xp[:, xp.shape[1] - (CONV_W - 1):]


def ssd_scan(x, dt, a, bmat, cmat, h0):
    b, L, H, P = x.shape
    G, N = SSD_GROUPS, SSD_STATE
    hg = H // G
    q = min(CHUNK, L)
    nc = L // q
    la = (dt * a).reshape(b, nc, q, G, hg)
    la_cum = jnp.cumsum(la, axis=2)
    xdt = (x * dt[..., None]).reshape(b, nc, q, G, hg, P)
    bc = bmat.reshape(b, nc, q, G, N)
    cc = cmat.reshape(b, nc, q, G, N)
    lt = jnp.moveaxis(la_cum, 2, -1)
    seg = lt[..., :, None] - lt[..., None, :]
    causal = jnp.tril(jnp.ones((q, q), dtype=bool))
    decay_in = jnp.exp(jnp.where(causal, seg, -jnp.inf))
    cb = jnp.einsum('bcign,bcjgn->bcgij', cc, bc)
    y_diag = jnp.einsum('bcghij,bcjghp->bcighp', cb[:, :, :, None] * decay_in, xdt)
    decay_end = jnp.exp(la_cum[:, :, -1:] - la_cum)
    chunk_states = jnp.einsum('bcjgn,bcjghp->bcghpn', bc, xdt * decay_end[..., None])
    chunk_decay = jnp.exp(la_cum[:, :, -1])
    decay_start = jnp.exp(la_cum)

    def step(h, inp):
        st, dec, c_c, ds = inp
        y_off = jnp.einsum('bign,bghpn->bighp', c_c, h) * ds[..., None]
        return h * dec[..., None, None] + st, y_off

    xs = (jnp.moveaxis(chunk_states, 1, 0), jnp.moveaxis(chunk_decay, 1, 0),
          jnp.moveaxis(cc, 1, 0), jnp.moveaxis(decay_start, 1, 0))
    h_last, y_off = lax.scan(step, h0.reshape(b, G, hg, P, N), xs)
    y = y_diag + jnp.moveaxis(y_off, 0, 1)
    return y.reshape(b, L, H, P), h_last.reshape(b, H, P, N)


def ssd_branch(z, xbc_raw, dt_raw, conv_buf, h0, p):
    xbc, new_buf = causal_conv(xbc_raw, conv_buf, p['conv_w'], p['conv_b'])
    b, L, _ = xbc.shape
    gn = SSD_GROUPS * SSD_STATE
    xs = xbc[..., :SSD_INNER].reshape(b, L, SSD_HEADS, SSD_HEAD_DIM).astype(jnp.float32)
    bm = xbc[..., SSD_INNER:SSD_INNER + gn].reshape(b, L, SSD_GROUPS, SSD_STATE).astype(jnp.float32)
    cm = xbc[..., SSD_INNER + gn:].reshape(b, L, SSD_GROUPS, SSD_STATE).astype(jnp.float32)
    dt = jax.nn.softplus(dt_raw.astype(jnp.float32) + p['dt_bias'].astype(jnp.float32))
    a = -jnp.exp(p['a_log'].astype(jnp.float32))
    y, h_last = ssd_scan(xs, dt, a, bm, cm, h0.astype(jnp.float32))
    y = y + xs * p['d_skip'].astype(jnp.float32)[:, None]
    y = y.reshape(b, L, SSD_INNER) * jax.nn.silu(z.astype(jnp.float32))
    y = rms_norm(y, p['ssd_norm_g']).astype(z.dtype)
    return y, new_buf, h_last.astype(h0.dtype)


def mla_prompt(q_nope, q_rope, ckv_n, kr, p):
    b, S = ckv_n.shape[0], ckv_n.shape[1]
    k_nope = jnp.einsum('bkr,rhd->bkhd', ckv_n, p['w_uk'])
    v = jnp.einsum('bkr,rhd->bkhd', ckv_n, p['w_uv'])
    key_chunk = jnp.arange(S) // CHUNK
    scale = 1.0 / math.sqrt(QK_NOPE + QK_ROPE)

    def block(i):
        start = i * ATTN_Q_BLOCK
        qn = lax.dynamic_slice_in_dim(q_nope, start, ATTN_Q_BLOCK, axis=1)
        qr = lax.dynamic_slice_in_dim(q_rope, start, ATTN_Q_BLOCK, axis=1)
        s = (jnp.einsum('bqhd,bkhd->bhqk', qn, k_nope)
             + jnp.einsum('bqhd,bkd->bhqk', qr, kr)).astype(jnp.float32) * scale
        q_chunk = (start + jnp.arange(ATTN_Q_BLOCK)) // CHUNK
        s = jnp.where(key_chunk[None, :] <= q_chunk[:, None], s, -jnp.inf)
        pr = jax.nn.softmax(s, axis=-1).astype(v.dtype)
        return jnp.einsum('bhqk,bkhd->bqhd', pr, v)

    o = lax.map(block, jnp.arange(S // ATTN_Q_BLOCK))
    return jnp.moveaxis(o, 0, 1).reshape(b, S, MLA_HEADS * V_HEAD)


def mla_sample(q_nope, q_rope, ckv_n, kr, past_ckv, past_kr, p):
    b, L = ckv_n.shape[0], ckv_n.shape[1]
    P = past_ckv.shape[1]
    ckv_all = jnp.concatenate([past_ckv, ckv_n], axis=1)
    kr_all = jnp.concatenate([past_kr, kr], axis=1)
    q_lat = jnp.einsum('blhd,rhd->blhr', q_nope, p['w_uk'])
    scale = 1.0 / math.sqrt(QK_NOPE + QK_ROPE)
    s = (jnp.einsum('blhr,bkr->bhlk', q_lat, ckv_all)
         + jnp.einsum('blhd,bkd->bhlk', q_rope, kr_all)).astype(jnp.float32) * scale
    key_chunk = jnp.arange(P + L) // CHUNK
    q_chunk = (P + jnp.arange(L)) // CHUNK
    s = jnp.where(key_chunk[None, :] <= q_chunk[:, None], s, -jnp.inf)
    pr = jax.nn.softmax(s, axis=-1).astype(ckv_all.dtype)
    o_lat = jnp.einsum('bhlk,bkr->blhr', pr, ckv_all)
    o = jnp.einsum('blhr,rhd->blhd', o_lat, p['w_uv'])
    return o.reshape(b, L, MLA_HEADS * V_HEAD)


def swiglu(x, wg, wu, wd):
    return (jax.nn.silu(x @ wg) * (x @ wu)) @ wd


def route(h, w_router, e_bias):
    T = h.shape[0]
    scores = jax.nn.sigmoid((h @ w_router).astype(jnp.float32))
    choice = scores + e_bias.astype(jnp.float32)
    grp = choice.reshape(T, N_GROUPS, N_EXPERTS // N_GROUPS)
    grp_score = jnp.sum(lax.top_k(grp, 2)[0], axis=-1)
    _, top_g = lax.top_k(grp_score, TOPK_GROUPS)
    gmask = jnp.any(top_g[..., None] == jnp.arange(N_GROUPS), axis=1)
    choice = jnp.where(jnp.repeat(gmask, N_EXPERTS // N_GROUPS, axis=-1), choice, -jnp.inf)
    _, idx = lax.top_k(choice, TOP_K)
    w = jnp.take_along_axis(scores, idx, axis=-1)
    w = w / jnp.sum(w, axis=-1, keepdims=True) * ROUTED_SCALE
    return idx, w


def routed_experts(h, idx, w, w_gate, w_up, w_down):
    T, D = h.shape
    A = T * TOP_K
    flat_e = idx.reshape(-1)
    flat_tok = (jnp.arange(A) // TOP_K).astype(jnp.int32)
    flat_w = w.reshape(-1)
    order = jnp.argsort(flat_e)
    e_sorted = flat_e[order]
    counts = jnp.bincount(flat_e, length=N_EXPERTS)
    padded = (counts + MOE_BLOCK - 1) // MOE_BLOCK * MOE_BLOCK
    pad_end = jnp.cumsum(padded)
    pad_start = pad_end - padded
    start = jnp.cumsum(counts) - counts
    dest = pad_start[e_sorted] + jnp.arange(A) - start[e_sorted]
    n_blocks = -(-A // MOE_BLOCK) + N_EXPERTS
    n_slots = n_blocks * MOE_BLOCK
    slot_tok = jnp.full((n_slots,), T, jnp.int32).at[dest].set(flat_tok[order])
    slot_w = jnp.zeros((n_slots,), h.dtype).at[dest].set(flat_w[order].astype(h.dtype))
    block_e = jnp.minimum(jnp.searchsorted(pad_end, jnp.arange(n_blocks) * MOE_BLOCK, side='right'),
                          N_EXPERTS - 1)
    h_pad = jnp.concatenate([h, jnp.zeros((1, D), h.dtype)], axis=0)

    def block(args):
        tok, sw, e = args
        return swiglu(h_pad[tok], w_gate[e], w_up[e], w_down[e]) * sw[:, None]

    yb = lax.map(block, (slot_tok.reshape(n_blocks, MOE_BLOCK), slot_w.reshape(n_blocks, MOE_BLOCK), block_e))
    y = jnp.zeros_like(h_pad).at[slot_tok].add(yb.reshape(n_slots, D))
    return y[:T]


def layer_forward(x, c, conv_buf, ssm0, past_ckv, past_kr, p):
    b, L, D = x.shape
    mod = jax.nn.silu(c) @ p['w_ada'] + p['b_ada']
    sh1, sc1, g1, sh2, sc2, g2 = jnp.split(mod[:, None, :], 6, axis=-1)
    h = rms_norm(x, p['norm1_g']) * (1 + sc1) + sh1
    proj = h @ p['w_in']
    z, xbc, dt_raw, cq, ckv, kr, gates = jnp.split(
        proj, [OFF_XBC, OFF_DT, OFF_CQ, OFF_CKV, OFF_KR, OFF_GATE], axis=-1)
    offset = 0 if past_ckv is None else past_ckv.shape[1]
    pos = offset + jnp.arange(L)
    y_ssd, new_conv, new_ssm = ssd_branch(z, xbc, dt_raw, conv_buf, ssm0, p)
    q = jnp.einsum('blr,rhd->blhd', rms_norm(cq, p['q_norm_g']), p['w_uq'])
    q_nope = q[..., :QK_NOPE]
    q_rope = rope(q[..., QK_NOPE:], pos)
    ckv_n = rms_norm(ckv, p['kv_norm_g'])
    kr_r = rope(kr, pos)
    if past_ckv is None:
        y_mla = mla_prompt(q_nope, q_rope, ckv_n, kr_r, p)
    else:
        y_mla = mla_sample(q_nope, q_rope, ckv_n, kr_r, past_ckv, past_kr, p)
    g_ssd, g_mla = jnp.split(gates, 2, axis=-1)
    merged = jax.nn.sigmoid(g_ssd) * (y_ssd @ p['w_ssd_out']) + jax.nn.sigmoid(g_mla) * (y_mla @ p['w_mla_out'])
    x = x + g1 * (merged @ p['w_merge_out'])
    h2 = (rms_norm(x, p['norm2_g']) * (1 + sc2) + sh2).reshape(b * L, D)
    idx, w = route(h2, p['w_router'], p['e_bias'])
    moe = swiglu(h2, p['w_sh_gate'], p['w_sh_up'], p['w_sh_down']) + routed_experts(
        h2, idx, w, p['w_exp_gate'], p['w_exp_up'], p['w_exp_down'])
    x = x + g2 * moe.reshape(b, L, D)
    return x, new_conv, new_ssm, ckv_n, kr_r


def setup_inputs(seed: int = 0) -> dict:
    key = jax.random.key(seed)
    ks = iter(jax.random.split(key, 48))
    f32 = jnp.float32

    def nrm(shape, scale):
        return jax.random.normal(next(ks), shape, f32) * scale

    def gain(shape):
        return 1.0 + nrm(shape, 0.02)

    dt0 = jnp.exp(jax.random.uniform(next(ks), (DEPTH, SSD_HEADS), f32, math.log(1e-3), math.log(1e-1)))
    dt_bias = dt0 + jnp.log(-jnp.expm1(-dt0))
    a_log = jnp.log(jax.random.uniform(next(ks), (DEPTH, SSD_HEADS), f32, 1.0, 16.0))
    return {
        "x_prompt": nrm((BATCH, SEQ, D_MODEL), 1.0),
        "x_sample": nrm((DEC_BATCH, DEC_SEQ, D_MODEL), 1.0),
        "c_prompt": nrm((BATCH, D_MODEL), 1.0),
        "c_sample": nrm((DEC_BATCH, D_MODEL), 1.0),
        "cache_conv": nrm((DEPTH, DEC_BATCH, CONV_W - 1, CONV_DIM), 1.0),
        "state_ssm": nrm((DEPTH, DEC_BATCH, SSD_HEADS, SSD_HEAD_DIM, SSD_STATE), 0.1),
        "cache_ckv": nrm((DEPTH, DEC_BATCH, PAST_LEN, KV_LORA), 1.0),
        "cache_kr": nrm((DEPTH, DEC_BATCH, PAST_LEN, QK_ROPE), 1.0),
        "w_ada": nrm((DEPTH, D_MODEL, 6 * D_MODEL), 0.5 * D_MODEL ** -0.5),
        "b_ada": nrm((DEPTH, 6 * D_MODEL), 0.02),
        "norm1_g": gain((DEPTH, D_MODEL)),
        "norm2_g": gain((DEPTH, D_MODEL)),
        "w_in": nrm((DEPTH, D_MODEL, IN_TOTAL), D_MODEL ** -0.5),
        "conv_w": nrm((DEPTH, CONV_W, CONV_DIM), CONV_W ** -0.5),
        "conv_b": nrm((DEPTH, CONV_DIM), 0.02),
        "dt_bias": dt_bias,
        "a_log": a_log,
        "d_skip": gain((DEPTH, SSD_HEADS)) + nrm((DEPTH, SSD_HEADS), 0.1),
        "ssd_norm_g": gain((DEPTH, SSD_INNER)),
        "w_ssd_out": nrm((DEPTH, SSD_INNER, D_MODEL), SSD_INNER ** -0.5),
        "q_norm_g": gain((DEPTH, Q_LORA)),
        "w_uq": nrm((DEPTH, Q_LORA, MLA_HEADS, QK_NOPE + QK_ROPE), Q_LORA ** -0.5),
        "kv_norm_g": gain((DEPTH, KV_LORA)),
        "w_uk": nrm((DEPTH, KV_LORA, MLA_HEADS, QK_NOPE), KV_LORA ** -0.5),
        "w_uv": nrm((DEPTH, KV_LORA, MLA_HEADS, V_HEAD), KV_LORA ** -0.5),
        "w_mla_out": nrm((DEPTH, MLA_HEADS * V_HEAD, D_MODEL), (MLA_HEADS * V_HEAD) ** -0.5),
        "w_merge_out": nrm((DEPTH, D_MODEL, D_MODEL), D_MODEL ** -0.5),
        "w_router": nrm((DEPTH, D_MODEL, N_EXPERTS), D_MODEL ** -0.5),
        "e_bias": nrm((DEPTH, N_EXPERTS), 0.01),
        "w_exp_gate": nrm((DEPTH, N_EXPERTS, D_MODEL, EXPERT_HIDDEN), D_MODEL ** -0.5),
        "w_exp_up": nrm((DEPTH, N_EXPERTS, D_MODEL, EXPERT_HIDDEN), D_MODEL ** -0.5),
        "w_exp_down": nrm((DEPTH, N_EXPERTS, EXPERT_HIDDEN, D_MODEL), EXPERT_HIDDEN ** -0.5),
        "w_sh_gate": nrm((DEPTH, D_MODEL, SHARED_HIDDEN), D_MODEL ** -0.5),
        "w_sh_up": nrm((DEPTH, D_MODEL, SHARED_HIDDEN), D_MODEL ** -0.5),
        "w_sh_down": nrm((DEPTH, SHARED_HIDDEN, D_MODEL), SHARED_HIDDEN ** -0.5),
        "final_norm_g": gain((D_MODEL,)),
    }


def reference(x_prompt, x_sample, c_prompt, c_sample, cache_conv, state_ssm, cache_ckv, cache_kr,
              w_ada, b_ada, norm1_g, norm2_g, w_in, conv_w, conv_b, dt_bias, a_log, d_skip,
              ssd_norm_g, w_ssd_out, q_norm_g, w_uq, kv_norm_g, w_uk, w_uv, w_mla_out, w_merge_out,
              w_router, e_bias, w_exp_gate, w_exp_up, w_exp_down, w_sh_gate, w_sh_up, w_sh_down,
              final_norm_g):
    bp = x_prompt.shape[0]
    xp, xs = x_prompt, x_sample
    st_p, st_s = [], []
    for l in range(DEPTH):
        p = dict(w_ada=w_ada[l], b_ada=b_ada[l], norm1_g=norm1_g[l], norm2_g=norm2_g[l], w_in=w_in[l],
                 conv_w=conv_w[l], conv_b=conv_b[l], dt_bias=dt_bias[l], a_log=a_log[l], d_skip=d_skip[l],
                 ssd_norm_g=ssd_norm_g[l], w_ssd_out=w_ssd_out[l], q_norm_g=q_norm_g[l], w_uq=w_uq[l],
                 kv_norm_g=kv_norm_g[l], w_uk=w_uk[l], w_uv=w_uv[l], w_mla_out=w_mla_out[l],
                 w_merge_out=w_merge_out[l], w_router=w_router[l], e_bias=e_bias[l],
                 w_exp_gate=w_exp_gate[l], w_exp_up=w_exp_up[l], w_exp_down=w_exp_down[l],
                 w_sh_gate=w_sh_gate[l], w_sh_up=w_sh_up[l], w_sh_down=w_sh_down[l])
        conv0 = jnp.zeros((bp, CONV_W - 1, CONV_DIM), xp.dtype)
        ssm0 = jnp.zeros((bp, SSD_HEADS, SSD_HEAD_DIM, SSD_STATE), xp.dtype)
        xp, cp, sp, kvp, krp = layer_forward(xp, c_prompt, conv0, ssm0, None, None, p)
        xs, cs, ss, kvs, krs = layer_forward(xs, c_sample, cache_conv[l], state_ssm[l],
                                             cache_ckv[l], cache_kr[l], p)
        st_p.append((cp, sp, kvp, krp))
        st_s.append((cs, ss, kvs, krs))
    y_prompt = rms_norm(xp, final_norm_g)
    y_sample = rms_norm(xs, final_norm_g)
    new_conv_prompt = jnp.stack([s[0] for s in st_p])
    new_ssm_prompt = jnp.stack([s[1] for s in st_p])
    new_ckv_prompt = jnp.stack([s[2] for s in st_p])
    new_kr_prompt = jnp.stack([s[3] for s in st_p])
    new_conv_sample = jnp.stack([s[0] for s in st_s])
    new_ssm_sample = jnp.stack([s[1] for s in st_s])
    new_ckv_sample = jnp.stack([s[2] for s in st_s])
    new_kr_sample = jnp.stack([s[3] for s in st_s])
    return (y_prompt, y_sample, new_conv_prompt, new_ssm_prompt, new_ckv_prompt, new_kr_prompt,
            new_conv_sample, new_ssm_sample, new_ckv_sample, new_kr_sample)
```

```python
import functools
import math

import numpy as np
import jax
import jax.numpy as jnp
from jax import lax
from jax.experimental import pallas as pl
from jax.experimental.pallas import tpu as pltpu

F32 = jnp.float32
BF16 = jnp.bfloat16
I32 = jnp.int32

EPS = 1e-6
CHUNK = 64
SSD_HEAD_DIM = 64
SSD_GROUPS = 8
SSD_STATE = 128
CONV_W = 4
MLA_HEADS = 16
QK_NOPE = 128
QK_ROPE = 64
V_HEAD = 128
ROPE_THETA = 10000.0
TOP_K = 8
N_GROUPS = 8
TOPK_GROUPS = 4
ROUTED_SCALE = 2.5

LANES = 128
MOD_ROWS = 32
MOE_ROWS = 256
VMEM_LIMIT = 56 * 1024 * 1024


def _params(*sem):
    return pltpu.CompilerParams(dimension_semantics=sem, vmem_limit_bytes=VMEM_LIMIT)


def _pick(n, cands):
    for c in cands:
        if n % c == 0:
            return c
    raise ValueError(f"no tile in {cands} divides {n}")


def _sigmoid(x):
    return 1.0 / (1.0 + jnp.exp(-x))


def _silu(x):
    return x * _sigmoid(x)


def _dot(a, b):
    return jnp.dot(a, b, preferred_element_type=F32)


def _dot_t(a, b):
    return lax.dot_general(a, b, (((1,), (1,)), ((), ())), preferred_element_type=F32)


def _split(x, n):
    parts = []
    for _ in range(n - 1):
        p = x.astype(BF16)
        parts.append(p)
        x = x - p.astype(F32)
    parts.append(x.astype(BF16))
    return parts


def _dot_exact_rhs(a, b_bf16, n=3):
    acc = None
    for p in _split(a, n):
        t = _dot(p, b_bf16)
        acc = t if acc is None else acc + t
    return acc


def _dot3(a, b, dot=_dot):
    ah, al = _split(a, 2)
    bh, bl = _split(b, 2)
    return dot(ah, bh) + (dot(ah, bl) + dot(al, bh))


def _rms(x):
    return x * lax.rsqrt(jnp.mean(x * x, axis=-1, keepdims=True) + EPS)


def _modulate(y, sc_ref, sh_ref):
    rows, d = y.shape
    y3 = y.reshape(rows // MOD_ROWS, MOD_ROWS, d)
    return (y3 * (1.0 + sc_ref[...]) + sh_ref[...]).reshape(rows, d)


def _ada_body(c_ref, w_ref, b_ref, o_ref):
    o_ref[...] = _dot3(_silu(c_ref[...]), w_ref[...]) + b_ref[...]


def _ada(c_all, w_ada, b_ada):
    nb, d = c_all.shape
    n = w_ada.shape[1]
    tn = _pick(n, (1024, 512, 256, 128))
    return pl.pallas_call(
        _ada_body,
        grid=(n // tn,),
        in_specs=[pl.BlockSpec((nb, d), lambda j: (0, 0)),
                  pl.BlockSpec((d, tn), lambda j: (0, j)),
                  pl.BlockSpec((1, tn), lambda j: (0, j))],
        out_specs=pl.BlockSpec((nb, tn), lambda j: (0, j)),
        out_shape=jax.ShapeDtypeStruct((nb, n), F32),
        compiler_params=_params("arbitrary"),
    )(c_all, w_ada, b_ada.reshape(1, n))


def _inproj_body(x_ref, sc_ref, sh_ref, g_ref, w_ref, o_ref, h_scr):
    @pl.when(pl.program_id(1) == 0)
    def _():
        h = _modulate(_rms(x_ref[...]) * g_ref[...], sc_ref, sh_ref)
        h_scr[...] = h.astype(BF16)

    o_ref[...] = _dot(h_scr[...], w_ref[...])


def _inproj(x_all, modg, norm_g, w_r, d):
    t = x_all.shape[0]
    n = w_r.shape[1]
    tm = _pick(t, (1024, 512, 256, 128))
    tn = 512
    ng = tm // MOD_ROWS
    return pl.pallas_call(
        _inproj_body,
        grid=(t // tm, n // tn),
        in_specs=[pl.BlockSpec((tm, d), lambda i, j: (i, 0)),
                  pl.BlockSpec((ng, 1, d), lambda i, j: (i, 0, 1)),
                  pl.BlockSpec((ng, 1, d), lambda i, j: (i, 0, 0)),
                  pl.BlockSpec((1, d), lambda i, j: (0, 0)),
                  pl.BlockSpec((d, tn), lambda i, j: (0, j))],
        out_specs=pl.BlockSpec((tm, tn), lambda i, j: (i, j)),
        out_shape=jax.ShapeDtypeStruct((t, n), F32),
        scratch_shapes=[pltpu.VMEM((tm, d), BF16)],
        compiler_params=_params("arbitrary", "arbitrary"),
    )(x_all, modg, modg, norm_g.reshape(1, d), w_r)


def _ssd_body(z_ref, x_ref, bc_ref, dt_ref, conv0_ref, h0_ref, cw_ref, cb_ref, dtb_ref, a_ref,
              dskip_ref, ng_ref, y_ref, nconv_ref, nssm_ref,
              carry_x, carry_bc, xp_x, xp_bc, state, y_scr, *, lb, inner):
    q = CHUNK
    c = pl.program_id(1)
    nc = pl.num_programs(1)
    gn = SSD_GROUPS * SSD_STATE

    @pl.when(c == 0)
    def _():
        carry_x[...] = jnp.zeros_like(carry_x)
        carry_bc[...] = jnp.zeros_like(carry_bc)
        carry_x[8 - (CONV_W - 1):8, :] = conv0_ref[0, :, :inner]
        carry_bc[8 - (CONV_W - 1):8, :] = conv0_ref[0, :, inner:]
        state[...] = h0_ref[0]

    def pad_rows(v):
        if lb == q:
            return v
        return jnp.concatenate([v, jnp.zeros((q - lb, v.shape[1]), v.dtype)], axis=0)

    def conv(raw, carry, xp, w_lo, w_hi):
        xp[0:8, :] = carry[...]
        xp[8:8 + q, :] = raw
        acc = cb_ref[:, w_lo:w_hi]
        for j in range(CONV_W):
            acc = acc + xp[8 - j:8 - j + q, :] * cw_ref[CONV_W - 1 - j:CONV_W - j, w_lo:w_hi]
        return _silu(acc)

    x_raw = pad_rows(x_ref[...])
    bc_raw = pad_rows(bc_ref[...])
    xc = conv(x_raw, carry_x, xp_x, 0, inner)
    bcc = conv(bc_raw, carry_bc, xp_bc, inner, inner + 2 * gn)
    if lb == q:
        carry_x[...] = x_raw[q - 8:q, :]
        carry_bc[...] = bc_raw[q - 8:q, :]

    @pl.when(c == nc - 1)
    def _():
        nconv_ref[0, :, :inner] = x_raw[lb - (CONV_W - 1):lb, :]
        nconv_ref[0, :, inner:] = bc_raw[lb - (CONV_W - 1):lb, :]

    dtr = dt_ref[...] + dtb_ref[...]
    dtv = jnp.maximum(dtr, 0.0) + jnp.log1p(jnp.exp(-jnp.abs(dtr)))
    dtv = pad_rows(dtv)
    la = dtv * a_ref[...]

    row = lax.broadcasted_iota(I32, (q, q), 0)
    col = lax.broadcasted_iota(I32, (q, q), 1)
    tri = jnp.where(row >= col, 1.0, 0.0).astype(BF16)
    parts = _split(la, 3)
    a_cum = _dot(tri, parts[0]) + _dot(tri, parts[1]) + _dot(tri, parts[2])
    a_last = a_cum[q - 1:q, :]

    r2 = lax.broadcasted_iota(I32, (q, 2 * q), 0)
    c2 = lax.broadcasted_iota(I32, (q, 2 * q), 1)
    u_e = jnp.where((c2 < q) & (r2 <= c2), 1.0, 0.0).astype(BF16)
    u_o = jnp.where((c2 >= q) & (r2 <= c2 - q), 1.0, 0.0).astype(BF16)
    i_e = jnp.where(r2 == c2, 1.0, 0.0).astype(BF16)
    i_o = jnp.where(r2 == c2 - q, 1.0, 0.0).astype(BF16)
    la_t = la.T
    dt_t = dtv.T
    hp = la_t.shape[0] // 4
    acol = _dot_exact_rhs(la_t[0:hp], u_e) + _dot_exact_rhs(la_t[hp:2 * hp], u_o)
    dtrow = _dot_exact_rhs(dt_t[0:hp], i_e) + _dot_exact_rhs(dt_t[hp:2 * hp], i_o)
    w_t = (dtv * jnp.exp(a_last - a_cum)).T
    dec_b = jnp.broadcast_to(jnp.exp(jnp.sum(la_t, axis=1, keepdims=True)), (la_t.shape[0], SSD_STATE))

    lane = lax.broadcasted_iota(I32, (q, 2 * q), 1)
    causal2 = lax.broadcasted_iota(I32, (q, 2 * q), 0) >= jnp.where(lane < q, lane, lane - q)
    first_half = lane < q
    rr = lax.broadcasted_iota(I32, (2 * SSD_HEAD_DIM, 2 * SSD_HEAD_DIM), 0)
    cc = lax.broadcasted_iota(I32, (2 * SSD_HEAD_DIM, 2 * SSD_HEAD_DIM), 1)
    bd_mask = (rr < SSD_HEAD_DIM) == (cc < SSD_HEAD_DIM)
    top_rows_q = lax.broadcasted_iota(I32, (2 * SSD_HEAD_DIM, q), 0) < SSD_HEAD_DIM
    top_rows_n = lax.broadcasted_iota(I32, (2 * SSD_HEAD_DIM, SSD_STATE), 0) < SSD_HEAD_DIM

    pairs_per_group = (inner // SSD_HEAD_DIM) // SSD_GROUPS // 2
    for g in range(SSD_GROUPS):
        b_g = bcc[:, g * SSD_STATE:(g + 1) * SSD_STATE].astype(BF16)
        c_g = bcc[:, gn + g * SSD_STATE:gn + (g + 1) * SSD_STATE].astype(BF16)
        cb2 = _dot_t(c_g, jnp.concatenate([b_g, b_g], axis=0))
        for kk in range(pairs_per_group):
            k = g * pairs_per_group + kk
            lo, hi = k * 2 * SSD_HEAD_DIM, (k + 1) * 2 * SSD_HEAD_DIM
            xp = xc[:, lo:hi]
            arow = jnp.where(first_half, a_cum[:, k:k + 1], a_cum[:, hp + k:hp + k + 1])
            seg = jnp.where(causal2, arow - acol[k:k + 1, :], -jnp.inf)
            m = (jnp.exp(seg) * cb2 * dtrow[k:k + 1, :]).astype(BF16)
            xbd = jnp.where(bd_mask, jnp.concatenate([xp, xp], axis=0), 0.0).astype(BF16)
            y_diag = _dot(m, xbd)
            s_k = state[lo:hi, :]
            y_off = _dot_t(c_g, s_k.astype(BF16)) * jnp.exp(arow)
            y_scr[:, lo:hi] = y_diag + y_off + xp * dskip_ref[:, lo:hi]
            w2 = jnp.where(top_rows_q, w_t[k:k + 1, :], w_t[hp + k:hp + k + 1, :])
            contrib = _dot((xp.T * w2).astype(BF16), b_g)
            d_k = jnp.where(top_rows_n, dec_b[k:k + 1, :], dec_b[hp + k:hp + k + 1, :])
            state[lo:hi, :] = s_k * d_k + contrib

    @pl.when(c == nc - 1)
    def _():
        nssm_ref[0] = state[...]

    y = y_scr[0:lb, :] * _silu(z_ref[...])
    y_ref[...] = (_rms(y) * ng_ref[...]).astype(y_ref.dtype)


def _ssd(proj, conv0, h0, conv_w, conv_b, dtb, a_neg, dskip, norm_g, *, nb, seq, row0, offs, inner):
    q = CHUNK
    lb = min(seq, q)
    assert seq % lb == 0 and row0 % lb == 0 and lb % 16 == 0
    nc = seq // lb
    rb0 = row0 // lb
    cdim = conv_w.shape[1]
    hp_rows = inner
    kern = functools.partial(_ssd_body, lb=lb, inner=inner)
    rows = lambda b, c: rb0 + b * nc + c
    in_specs = [
        pl.BlockSpec((lb, inner), lambda b, c: (rows(b, c), offs["z"] // inner)),
        pl.BlockSpec((lb, inner), lambda b, c: (rows(b, c), offs["x"] // inner)),
        pl.BlockSpec((lb, cdim - inner), lambda b, c: (rows(b, c), offs["bc"] // (cdim - inner))),
        pl.BlockSpec((lb, LANES), lambda b, c: (rows(b, c), offs["dt"] // LANES)),
        pl.BlockSpec((1, CONV_W - 1, cdim), lambda b, c: (b, 0, 0)),
        pl.BlockSpec((1, hp_rows, SSD_STATE), lambda b, c: (b, 0, 0)),
        pl.BlockSpec((CONV_W, cdim), lambda b, c: (0, 0)),
        pl.BlockSpec((1, cdim), lambda b, c: (0, 0)),
        pl.BlockSpec((1, LANES), lambda b, c: (0, 0)),
        pl.BlockSpec((1, LANES), lambda b, c: (0, 0)),
        pl.BlockSpec((1, inner), lambda b, c: (0, 0)),
        pl.BlockSpec((1, inner), lambda b, c: (0, 0)),
    ]
    args = [proj, proj, proj, proj, conv0, h0, conv_w, conv_b, dtb, a_neg, dskip, norm_g]
    return pl.pallas_call(
        kern,
        grid=(nb, nc),
        in_specs=in_specs,
        out_specs=[pl.BlockSpec((lb, inner), lambda b, c: (b * nc + c, 0)),
                   pl.BlockSpec((1, CONV_W - 1, cdim), lambda b, c: (b, 0, 0)),
                   pl.BlockSpec((1, hp_rows, SSD_STATE), lambda b, c: (b, 0, 0))],
        out_shape=[jax.ShapeDtypeStruct((nb * seq, inner), BF16),
                   jax.ShapeDtypeStruct((nb, CONV_W - 1, cdim), F32),
                   jax.ShapeDtypeStruct((nb, hp_rows, SSD_STATE), F32)],
        scratch_shapes=[pltpu.VMEM((8, inner), F32), pltpu.VMEM((8, cdim - inner), F32),
                        pltpu.VMEM((8 + q, inner), F32), pltpu.VMEM((8 + q, cdim - inner), F32),
                        pltpu.VMEM((hp_rows, SSD_STATE), F32), pltpu.VMEM((q, inner), F32)],
        compiler_params=_params("arbitrary", "arbitrary"),
    )(*args)


def _rope128(x, cos, sin):
    lane = lax.broadcasted_iota(I32, x.shape, 1)
    half = QK_ROPE // 2
    swapped = jnp.where((lane % QK_ROPE) < half, pltpu.roll(x, LANES - half, 1), pltpu.roll(x, half, 1))
    return x * cos + swapped * sin


def _mla_prep_body(cq_ref, ckv_ref, kr_ref, cos_ref, sin_ref, qg_ref, kvg_ref, wq_ref, wk_ref, wv_ref,
                   *out_refs, absorbed, scale):
    cos, sin = cos_ref[...], sin_ref[...]
    ckv_n = _rms(ckv_ref[...]) * kvg_ref[...]
    kr_r = _rope128(kr_ref[...], cos, sin)
    qn = (_rms(cq_ref[...]) * qg_ref[...]).astype(BF16)
    hw = QK_NOPE + LANES
    if absorbed:
        ckvn_ref, krr_ref, qlat_ref, qrope_ref = out_refs
    else:
        ckvn_ref, krr_ref, q_ref, k_ref, v_ref = out_refs
        ckv_b = ckv_n.astype(BF16)
        v_ref[...] = _dot(ckv_b, wv_ref[...]).astype(BF16)
        kr_b = kr_r.astype(BF16)
    ckvn_ref[...] = ckv_n
    krr_ref[...] = kr_r
    for h in range(MLA_HEADS):
        qh = _dot(qn, wq_ref[:, h * hw:(h + 1) * hw]) * scale
        q_nope = qh[:, :QK_NOPE]
        q_rope = _rope128(qh[:, QK_NOPE:], cos, sin)
        if absorbed:
            qlat_ref[h] = _dot_t(q_nope.astype(BF16), wk_ref[:, h * QK_NOPE:(h + 1) * QK_NOPE]).astype(BF16)
            qrope_ref[h] = q_rope[:, :QK_ROPE].astype(BF16)
        else:
            q_ref[:, h * hw:h * hw + QK_NOPE] = q_nope.astype(BF16)
            q_ref[:, h * hw + QK_NOPE:(h + 1) * hw] = q_rope.astype(BF16)
            k_ref[:, h * hw:h * hw + QK_NOPE] = _dot(ckv_b, wk_ref[:, h * QK_NOPE:(h + 1) * QK_NOPE]).astype(BF16)
            k_ref[:, h * hw + QK_NOPE:(h + 1) * hw] = kr_b


def _mla_prep(proj, cos_t, sin_t, q_g, kv_g, wq_r, wk, wv, *, row0, nrows, offs, absorbed):
    qlora, kvlora = q_g.shape[0], kv_g.shape[0]
    tm = _pick(nrows, (512, 256, 128))
    assert row0 % tm == 0
    rb0 = row0 // tm
    hw = QK_NOPE + LANES
    scale = 1.0 / math.sqrt(QK_NOPE + QK_ROPE)
    const = lambda i: (0, 0)
    in_specs = [pl.BlockSpec((tm, qlora), lambda i: (rb0 + i, offs["cq"] // qlora)),
                pl.BlockSpec((tm, kvlora), lambda i: (rb0 + i, offs["ckv"] // kvlora)),
                pl.BlockSpec((tm, LANES), lambda i: (rb0 + i, offs["kr"] // LANES)),
                pl.BlockSpec((tm, LANES), lambda i: (rb0 + i, 0)),
                pl.BlockSpec((tm, LANES), lambda i: (rb0 + i, 0)),
                pl.BlockSpec((1, qlora), const), pl.BlockSpec((1, kvlora), const),
                pl.BlockSpec(wq_r.shape, const), pl.BlockSpec(wk.shape, const), pl.BlockSpec(wv.shape, const)]
    out_specs = [pl.BlockSpec((tm, kvlora), lambda i: (i, 0)), pl.BlockSpec((tm, LANES), lambda i: (i, 0))]
    out_shape = [jax.ShapeDtypeStruct((nrows, kvlora), F32), jax.ShapeDtypeStruct((nrows, LANES), F32)]
    if absorbed:
        out_specs += [pl.BlockSpec((MLA_HEADS, tm, kvlora), lambda i: (0, i, 0)),
                      pl.BlockSpec((MLA_HEADS, tm, QK_ROPE), lambda i: (0, i, 0))]
        out_shape += [jax.ShapeDtypeStruct((MLA_HEADS, nrows, kvlora), BF16),
                      jax.ShapeDtypeStruct((MLA_HEADS, nrows, QK_ROPE), BF16)]
    else:
        out_specs += [pl.BlockSpec((tm, MLA_HEADS * hw), lambda i: (i, 0)),
                      pl.BlockSpec((tm, MLA_HEADS * hw), lambda i: (i, 0)),
                      pl.BlockSpec((tm, MLA_HEADS * V_HEAD), lambda i: (i, 0))]
        out_shape += [jax.ShapeDtypeStruct((nrows, MLA_HEADS * hw), BF16),
                      jax.ShapeDtypeStruct((nrows, MLA_HEADS * hw), BF16),
                      jax.ShapeDtypeStruct((nrows, MLA_HEADS * V_HEAD), BF16)]
    return pl.pallas_call(
        functools.partial(_mla_prep_body, absorbed=absorbed, scale=scale),
        grid=(nrows // tm,),
        in_specs=in_specs, out_specs=out_specs, out_shape=out_shape,
        compiler_params=_params("arbitrary"),
    )(proj, proj, proj, cos_t, sin_t, q_g.reshape(1, -1), kv_g.reshape(1, -1), wq_r, wk, wv)


def _attn_body(q_ref, k_ref, v_ref, o_ref, *, tq):
    qi = pl.program_id(2)
    qb = q_ref[0]

    def block(kj):
        start = pl.multiple_of(kj * tq, tq)
        s = _dot_t(qb, k_ref[0, pl.ds(start, tq), :])
        return s, v_ref[0, pl.ds(start, tq), :]

    shift = CHUNK.bit_length() - 1
    r = lax.broadcasted_iota(I32, (tq, tq), 0) >> shift
    c = lax.broadcasted_iota(I32, (tq, tq), 1) >> shift
    s, v = block(qi)
    s = jnp.where(c <= r, s, -jnp.inf)
    m = jnp.max(s, axis=1, keepdims=True)
    p = jnp.exp(s - m)
    l = jnp.sum(p, axis=1, keepdims=True)
    acc = _dot(p.astype(BF16), v)

    def body(kj, carry):
        m, l, acc = carry
        s, v = block(kj)
        m_new = jnp.maximum(m, jnp.max(s, axis=1, keepdims=True))
        alpha = jnp.exp(m - m_new)
        p = jnp.exp(s - m_new)
        l = alpha * l + jnp.sum(p, axis=1, keepdims=True)
        acc = alpha * acc + _dot(p.astype(BF16), v)
        return m_new, l, acc

    m, l, acc = lax.fori_loop(0, qi, body, (m, l, acc))
    o_ref[...] = (acc / l).astype(o_ref.dtype)


def _attn_prompt(q, k, v, *, nb, seq):
    hw = QK_NOPE + LANES
    tq = min(256, seq // 2)
    assert seq % tq == 0 and tq % CHUNK == 0
    nq = seq // tq
    q3 = q.reshape(nb, seq, MLA_HEADS * hw)
    k3 = k.reshape(nb, seq, MLA_HEADS * hw)
    v3 = v.reshape(nb, seq, MLA_HEADS * V_HEAD)
    return pl.pallas_call(
        functools.partial(_attn_body, tq=tq),
        grid=(nb, MLA_HEADS, nq),
        in_specs=[pl.BlockSpec((1, tq, hw), lambda b, h, i: (b, i, h)),
                  pl.BlockSpec((1, seq, hw), lambda b, h, i: (b, 0, h)),
                  pl.BlockSpec((1, seq, V_HEAD), lambda b, h, i: (b, 0, h))],
        out_specs=pl.BlockSpec((tq, V_HEAD), lambda b, h, i: (b * nq + i, h)),
        out_shape=jax.ShapeDtypeStruct((nb * seq, MLA_HEADS * V_HEAD), BF16),
        compiler_params=_params("arbitrary", "arbitrary", "arbitrary"),
    )(q3, k3, v3)


def _attn_sample_body(ql_ref, qr_ref, pckv_ref, pkr_ref, nckv_ref, nkr_ref, o_ref, *, past, seq):
    nh = ql_ref.shape[0]
    ql = ql_ref[...].reshape(nh * seq, ql_ref.shape[2])
    qr = qr_ref[...].reshape(nh * seq, qr_ref.shape[2])
    pckv = pckv_ref[0].astype(BF16)
    nckv = nckv_ref[...].astype(BF16)
    s_p = _dot_t(ql, pckv) + _dot_t(qr, pkr_ref[0].astype(BF16))
    s_n = _dot_t(ql, nckv) + _dot_t(qr, nkr_ref[:, :QK_ROPE].astype(BF16))
    shift = CHUNK.bit_length() - 1
    q_chunk = (past + lax.broadcasted_iota(I32, (nh * seq, 1), 0) % seq) >> shift
    kp_chunk = lax.broadcasted_iota(I32, (1, past), 1) >> shift
    kn_chunk = (past + lax.broadcasted_iota(I32, (1, seq), 1)) >> shift
    s_p = jnp.where(kp_chunk <= q_chunk, s_p, -jnp.inf)
    s_n = jnp.where(kn_chunk <= q_chunk, s_n, -jnp.inf)
    m = jnp.maximum(jnp.max(s_p, axis=1, keepdims=True), jnp.max(s_n, axis=1, keepdims=True))
    p_p = jnp.exp(s_p - m)
    p_n = jnp.exp(s_n - m)
    l = jnp.sum(p_p, axis=1, keepdims=True) + jnp.sum(p_n, axis=1, keepdims=True)
    o = (_dot(p_p.astype(BF16), pckv) + _dot(p_n.astype(BF16), nckv)) / l
    o_ref[...] = o.reshape(o_ref.shape).astype(o_ref.dtype)


def _attn_sample(qlat, qrope, past_ckv, past_kr, ckv_n, kr_r, *, nb, seq):
    past = past_ckv.shape[1]
    r = past_ckv.shape[2]
    return pl.pallas_call(
        functools.partial(_attn_sample_body, past=past, seq=seq),
        grid=(nb,),
        in_specs=[pl.BlockSpec((MLA_HEADS, seq, r), lambda b: (0, b, 0)),
                  pl.BlockSpec((MLA_HEADS, seq, QK_ROPE), lambda b: (0, b, 0)),
                  pl.BlockSpec((1, past, r), lambda b: (b, 0, 0)),
                  pl.BlockSpec((1, past, QK_ROPE), lambda b: (b, 0, 0)),
                  pl.BlockSpec((seq, r), lambda b: (b, 0)),
                  pl.BlockSpec((seq, LANES), lambda b: (b, 0))],
        out_specs=pl.BlockSpec((MLA_HEADS, seq, r), lambda b: (0, b, 0)),
        out_shape=jax.ShapeDtypeStruct((MLA_HEADS, nb * seq, r), BF16),
        compiler_params=_params("arbitrary"),
    )(qlat, qrope, past_ckv, past_kr, ckv_n, kr_r)


def _uv_body(o_ref, w_ref, y_ref):
    y_ref[...] = _dot(o_ref[0], w_ref[...]).astype(y_ref.dtype)


def _uv_sample(o_lat, wv):
    nh, nrows, r = o_lat.shape
    return pl.pallas_call(
        _uv_body,
        grid=(nh,),
        in_specs=[pl.BlockSpec((1, nrows, r), lambda h: (h, 0, 0)),
                  pl.BlockSpec((r, V_HEAD), lambda h: (0, h))],
        out_specs=pl.BlockSpec((nrows, V_HEAD), lambda h: (0, h)),
        out_shape=jax.ShapeDtypeStruct((nrows, nh * V_HEAD), BF16),
        compiler_params=_params("arbitrary"),
    )(o_lat, wv)


def _merge1_body(ysp_ref, ymp_ref, yss_ref, yms_ref, ws_ref, wm_ref, ga_ref, gb_ref, o_ref, *, npb):
    def run(ys_ref, ym_ref):
        a = _dot(ys_ref[...], ws_ref[...])
        b = _dot(ym_ref[...], wm_ref[...])
        o_ref[...] = (_sigmoid(ga_ref[...]) * a + _sigmoid(gb_ref[...]) * b).astype(o_ref.dtype)

    i = pl.program_id(0)
    pl.when(i < npb)(lambda: run(ysp_ref, ymp_ref))
    pl.when(i >= npb)(lambda: run(yss_ref, yms_ref))


def _merge1(ys_p, ym_p, ys_s, ym_s, w_ssd_out, w_mla_out, proj, offs):
    (tp, inner), ts = ys_p.shape, ys_s.shape[0]
    dm = ym_p.shape[1]
    d = w_ssd_out.shape[1]
    tm = _pick(math.gcd(tp, ts), (512, 256, 128))
    tn = 512
    npb, nsb = tp // tm, ts // tm
    prow = lambda i, j: (jnp.minimum(i, npb - 1), 0)
    srow = lambda i, j: (jnp.maximum(i - npb, 0), 0)
    return pl.pallas_call(
        functools.partial(_merge1_body, npb=npb),
        grid=(npb + nsb, d // tn),
        in_specs=[pl.BlockSpec((tm, inner), prow), pl.BlockSpec((tm, dm), prow),
                  pl.BlockSpec((tm, inner), srow), pl.BlockSpec((tm, dm), srow),
                  pl.BlockSpec((inner, tn), lambda i, j: (0, j)),
                  pl.BlockSpec((dm, tn), lambda i, j: (0, j)),
                  pl.BlockSpec((tm, tn), lambda i, j: (i, offs["ga"] // tn + j)),
                  pl.BlockSpec((tm, tn), lambda i, j: (i, offs["gb"] // tn + j))],
        out_specs=pl.BlockSpec((tm, tn), lambda i, j: (i, j)),
        out_shape=jax.ShapeDtypeStruct((tp + ts, d), BF16),
        compiler_params=_params("arbitrary", "arbitrary"),
    )(ys_p, ym_p, ys_s, ym_s, w_ssd_out, w_mla_out, proj, proj)


def _merge2_body(m_ref, w_ref, x_ref, g1_ref, sc_ref, sh_ref, ng_ref, wr_ref, x1_ref, h2_ref, lg_ref):
    rows, d = x_ref.shape
    upd = _dot(m_ref[...], w_ref[...]).reshape(rows // MOD_ROWS, MOD_ROWS, d) * g1_ref[...]
    x1 = x_ref[...] + upd.reshape(rows, d)
    x1_ref[...] = x1
    h2 = _modulate(_rms(x1) * ng_ref[...], sc_ref, sh_ref)
    h2_ref[...] = h2
    lg_ref[...] = _dot3(wr_ref[...], h2, dot=_dot_t)


def _merge2(merged, w_merge, x_all, modg, norm_g, w_router_t):
    t, d = x_all.shape
    ne = w_router_t.shape[0]
    tm = _pick(t, (512, 256, 128))
    ng = tm // MOD_ROWS
    return pl.pallas_call(
        _merge2_body,
        grid=(t // tm,),
        in_specs=[pl.BlockSpec((tm, d), lambda i: (i, 0)),
                  pl.BlockSpec((d, d), lambda i: (0, 0)),
                  pl.BlockSpec((tm, d), lambda i: (i, 0)),
                  pl.BlockSpec((ng, 1, d), lambda i: (i, 0, 2)),
                  pl.BlockSpec((ng, 1, d), lambda i: (i, 0, 4)),
                  pl.BlockSpec((ng, 1, d), lambda i: (i, 0, 3)),
                  pl.BlockSpec((1, d), lambda i: (0, 0)),
                  pl.BlockSpec((ne, d), lambda i: (0, 0))],
        out_specs=[pl.BlockSpec((tm, d), lambda i: (i, 0)),
                   pl.BlockSpec((tm, d), lambda i: (i, 0)),
                   pl.BlockSpec((ne, tm), lambda i: (0, i))],
        out_shape=[jax.ShapeDtypeStruct((t, d), F32), jax.ShapeDtypeStruct((t, d), F32),
                   jax.ShapeDtypeStruct((ne, t), F32)],
        compiler_params=_params("arbitrary"),
    )(merged, w_merge, x_all, modg, modg, modg, norm_g.reshape(1, d), w_router_t)


def _route_body(lg_ref, eb_ref, eidx_ref, pos_ref, w_ref, cnt_ref, carry):
    ne, tr = lg_ref.shape
    per_group = ne // N_GROUPS

    @pl.when(pl.program_id(0) == 0)
    def _():
        carry[...] = jnp.zeros_like(carry)

    scores = _sigmoid(lg_ref[...])
    choice = scores + eb_ref[...]
    sub = lax.broadcasted_iota(I32, (per_group, tr), 0)
    gscore, blocks = [], []
    for g in range(N_GROUPS):
        blk = choice[g * per_group:(g + 1) * per_group, :]
        m1 = jnp.max(blk, axis=0, keepdims=True)
        first = jnp.min(jnp.where(blk == m1, sub, per_group), axis=0, keepdims=True)
        m2 = jnp.max(jnp.where(sub == first, -jnp.inf, blk), axis=0, keepdims=True)
        gscore.append(m1 + m2)
        blocks.append(blk)
    masked = []
    for g in range(N_GROUPS):
        rank = jnp.zeros((1, tr), I32)
        for g2 in range(N_GROUPS):
            if g2 == g:
                continue
            beats = (gscore[g2] > gscore[g]) | ((gscore[g2] == gscore[g]) & (g2 < g))
            rank = rank + beats.astype(I32)
        masked.append(jnp.where(rank < TOPK_GROUPS, blocks[g], -jnp.inf))
    cm = jnp.concatenate(masked, axis=0)

    eid = lax.broadcasted_iota(I32, (ne, tr), 0)
    rank = jnp.zeros((ne, tr), I32)
    for e2 in range(ne):
        rowv = cm[e2:e2 + 1, :]
        beats = (rowv > cm) | ((rowv == cm) & (eid > e2))
        rank = rank + beats.astype(I32)
    sel = rank < TOP_K
    wsel = jnp.where(sel, scores, 0.0)
    wfull = wsel / jnp.sum(wsel, axis=0, keepdims=True) * ROUTED_SCALE

    r = lax.broadcasted_iota(I32, (tr, tr), 0)
    c = lax.broadcasted_iota(I32, (tr, tr), 1)
    before = jnp.where(r < c, 1.0, 0.0).astype(BF16)
    self = jnp.where(sel, 1.0, 0.0)
    pos = carry[:, 0:1] + _dot(self.astype(BF16), before)
    carry[...] = carry[...] + jnp.sum(self, axis=1, keepdims=True)
    cnt_ref[...] = carry[...]

    eid_f = eid.astype(F32)
    for k in range(TOP_K):
        pick = sel & (rank == k)
        eidx_ref[k:k + 1, :] = jnp.sum(jnp.where(pick, eid_f, 0.0), axis=0, keepdims=True).astype(I32)
        pos_ref[k:k + 1, :] = jnp.sum(jnp.where(pick, pos, 0.0), axis=0, keepdims=True).astype(I32)
        w_ref[k:k + 1, :] = jnp.sum(jnp.where(pick, wfull, 0.0), axis=0, keepdims=True)


def _route(logits_t, e_bias):
    ne, t = logits_t.shape
    tr = _pick(t, (512, 256, 128))
    return pl.pallas_call(
        _route_body,
        grid=(t // tr,),
        in_specs=[pl.BlockSpec((ne, tr), lambda i: (0, i)), pl.BlockSpec((ne, 1), lambda i: (0, 0))],
        out_specs=[pl.BlockSpec((TOP_K, tr), lambda i: (0, i)), pl.BlockSpec((TOP_K, tr), lambda i: (0, i)),
                   pl.BlockSpec((TOP_K, tr), lambda i: (0, i)), pl.BlockSpec((ne, LANES), lambda i: (0, 0))],
        out_shape=[jax.ShapeDtypeStruct((TOP_K, t), I32), jax.ShapeDtypeStruct((TOP_K, t), I32),
                   jax.ShapeDtypeStruct((TOP_K, t), F32), jax.ShapeDtypeStruct((ne, LANES), F32)],
        scratch_shapes=[pltpu.VMEM((ne, LANES), F32)],
        compiler_params=_params("arbitrary"),
    )(logits_t, e_bias.reshape(ne, 1))


def _row_copy(src, s, dst, d, sem):
    return pltpu.make_async_copy(src.at[pl.ds(s, 1), :], dst.at[pl.ds(d, 1), :], sem)


def _dispatch_body(fill_lo_ref, fill_hi_ref, h_ref, dest_ref, hs_ref, dest_s, zrow, sem, dsem, *, tb):
    i = pl.program_id(0)
    cp = pltpu.make_async_copy(dest_ref.at[i], dest_s, dsem)
    cp.start()

    @pl.when(i == 0)
    def _():
        zrow[...] = jnp.zeros_like(zrow)

        def per_expert(fn):
            def body(e, _):
                lax.fori_loop(fill_lo_ref[e], fill_hi_ref[e], lambda s, c: (fn(s), c)[1], 0)
                return 0
            lax.fori_loop(0, fill_lo_ref.shape[0], body, 0)

        per_expert(lambda s: _row_copy(zrow, 0, hs_ref, s, sem).start())
        per_expert(lambda s: _row_copy(zrow, 0, hs_ref, s, sem).wait())

    cp.wait()
    n = TOP_K * tb

    def issue(j, c):
        _row_copy(h_ref, j % tb, hs_ref, dest_s[j], sem).start()
        return c

    def drain(j, c):
        _row_copy(h_ref, 0, hs_ref, 0, sem).wait()
        return c

    lax.fori_loop(0, n, issue, 0)
    lax.fori_loop(0, n, drain, 0)


def _dispatch(h2, dest_blk, fill_lo, fill_hi, n_slots):
    t, d = h2.shape
    nblk, n = dest_blk.shape
    tb = n // TOP_K
    return pl.pallas_call(
        functools.partial(_dispatch_body, tb=tb),
        grid_spec=pltpu.PrefetchScalarGridSpec(
            num_scalar_prefetch=2,
            grid=(nblk,),
            in_specs=[pl.BlockSpec((tb, d), lambda i, lo, hi: (i, 0)),
                      pl.BlockSpec(memory_space=pl.ANY)],
            out_specs=pl.BlockSpec(memory_space=pl.ANY),
            scratch_shapes=[pltpu.SMEM((n,), I32), pltpu.VMEM((8, d), F32),
                            pltpu.SemaphoreType.DMA, pltpu.SemaphoreType.DMA]),
        out_shape=jax.ShapeDtypeStruct((n_slots, d), F32),
        compiler_params=_params("arbitrary"),
    )(fill_lo, fill_hi, h2, dest_blk)


def _experts_body(be_ref, nu_ref, x_ref, wg_ref, wu_ref, wd_ref, o_ref, wgu_s, wd_s):
    i = pl.program_id(0)
    hid = wg_ref.shape[2]

    @pl.when(i < nu_ref[0])
    def _():
        @pl.when((i == 0) | (be_ref[i] != be_ref[jnp.maximum(i - 1, 0)]))
        def _():
            wgu_s[:, :hid] = wg_ref[0].astype(BF16)
            wgu_s[:, hid:] = wu_ref[0].astype(BF16)
            wd_s[...] = wd_ref[0].astype(BF16)

        gu = _dot(x_ref[...].astype(BF16), wgu_s[...])
        act = (_silu(gu[:, :hid]) * gu[:, hid:]).astype(BF16)
        o_ref[...] = _dot(act, wd_s[...])


def _experts(hs, block_e, n_used, wg, wu, wd):
    n_slots, d = hs.shape
    ne, _, hid = wg.shape
    bm = MOE_ROWS
    nblocks = n_slots // bm
    blk = lambda i, be, nu: (jnp.minimum(i, nu[0] - 1), 0)
    return pl.pallas_call(
        _experts_body,
        grid_spec=pltpu.PrefetchScalarGridSpec(
            num_scalar_prefetch=2,
            grid=(nblocks,),
            in_specs=[pl.BlockSpec((bm, d), blk),
                      pl.BlockSpec((1, d, hid), lambda i, be, nu: (be[i], 0, 0)),
                      pl.BlockSpec((1, d, hid), lambda i, be, nu: (be[i], 0, 0)),
                      pl.BlockSpec((1, hid, d), lambda i, be, nu: (be[i], 0, 0))],
            out_specs=pl.BlockSpec((bm, d), blk),
            scratch_shapes=[pltpu.VMEM((d, 2 * hid), BF16), pltpu.VMEM((hid, d), BF16)]),
        out_shape=jax.ShapeDtypeStruct((n_slots, d), F32),
        compiler_params=_params("arbitrary"),
    )(block_e, n_used, hs, wg, wu, wd)


def _combine_body(ys_ref, dest_ref, w_ref, h_ref, x1_ref, g2_ref, wgu_ref, wd_ref, fg_ref, o_ref,
                  dest_s, gbuf, sem, dsem, *, tb):
    i = pl.program_id(0)
    cp = pltpu.make_async_copy(dest_ref.at[i], dest_s, dsem)
    cp.start()
    cp.wait()
    n = TOP_K * tb

    def issue(j, c):
        _row_copy(ys_ref, dest_s[j], gbuf, j, sem).start()
        return c

    def drain(j, c):
        _row_copy(ys_ref, 0, gbuf, 0, sem).wait()
        return c

    lax.fori_loop(0, n, issue, 0)

    hid = wd_ref.shape[0]
    gu = _dot(h_ref[...].astype(BF16), wgu_ref[...])
    moe = _dot((_silu(gu[:, :hid]) * gu[:, hid:]).astype(BF16), wd_ref[...])

    lax.fori_loop(0, n, drain, 0)
    w = w_ref[...]
    for k in range(TOP_K):
        moe = moe + gbuf[k * tb:(k + 1) * tb, :] * w[:, k:k + 1]
    rows, d = moe.shape
    upd = moe.reshape(rows // MOD_ROWS, MOD_ROWS, d) * g2_ref[...]
    x2 = x1_ref[...] + upd.reshape(rows, d)
    o_ref[...] = _rms(x2) * fg_ref[...]


def _combine(ys, dest_blk, w_tok, h2, x1, modg, wsh_gu, wsh_d, final_g):
    t, d = x1.shape
    nblk, n = dest_blk.shape
    tb = n // TOP_K
    ng = tb // MOD_ROWS
    const = lambda i: (0, 0)
    return pl.pallas_call(
        functools.partial(_combine_body, tb=tb),
        grid=(nblk,),
        in_specs=[pl.BlockSpec(memory_space=pl.ANY),
                  pl.BlockSpec(memory_space=pl.ANY),
                  pl.BlockSpec((tb, TOP_K), lambda i: (i, 0)),
                  pl.BlockSpec((tb, d), lambda i: (i, 0)),
                  pl.BlockSpec((tb, d), lambda i: (i, 0)),
                  pl.BlockSpec((ng, 1, d), lambda i: (i, 0, 5)),
                  pl.BlockSpec(wsh_gu.shape, const), pl.BlockSpec(wsh_d.shape, const),
                  pl.BlockSpec((1, d), const)],
        out_specs=pl.BlockSpec((tb, d), lambda i: (i, 0)),
        out_shape=jax.ShapeDtypeStruct((t, d), F32),
        scratch_shapes=[pltpu.SMEM((n,), I32), pltpu.VMEM((n, d), F32),
                        pltpu.SemaphoreType.DMA, pltpu.SemaphoreType.DMA],
        compiler_params=_params("arbitrary"),
    )(ys, dest_blk, w_tok, h2, x1, modg, wsh_gu, wsh_d, final_g.reshape(1, d))


def _rope_tables(pos):
    half = QK_ROPE // 2
    freqs = ROPE_THETA ** (-jnp.arange(half, dtype=F32) / half)
    ang = pos.astype(F32)[:, None] * freqs[None, :]
    cos, sin = jnp.cos(ang), jnp.sin(ang)
    return (jnp.concatenate([cos, cos, cos, cos], axis=1),
            jnp.concatenate([-sin, sin, -sin, sin], axis=1))


def _blocked(a, tb):
    k, t = a.shape
    return a.reshape(k, t // tb, tb).transpose(1, 0, 2).reshape(t // tb, k * tb)


def kernel(x_prompt, x_sample, c_prompt, c_sample, cache_conv, state_ssm, cache_ckv, cache_kr, w_ada, b_ada,
           norm1_g, norm2_g, w_in, conv_w, conv_b, dt_bias, a_log, d_skip, ssd_norm_g, w_ssd_out, q_norm_g,
           w_uq, kv_norm_g, w_uk, w_uv, w_mla_out, w_merge_out, w_router, e_bias, w_exp_gate, w_exp_up,
           w_exp_down, w_sh_gate, w_sh_up, w_sh_down, final_norm_g):
    depth = w_in.shape[0]
    assert depth == 1
    bp, seq, d = x_prompt.shape
    bs, lseq, _ = x_sample.shape
    assert lseq == MOD_ROWS and seq % MOD_ROWS == 0
    tp, ts = bp * seq, bs * lseq
    t_all = tp + ts
    nheads = dt_bias.shape[1]
    inner = nheads * SSD_HEAD_DIM
    cdim = conv_w.shape[2]
    gn = SSD_GROUPS * SSD_STATE
    qlora, kvlora = q_norm_g.shape[1], kv_norm_g.shape[1]
    ne = w_router.shape[2]
    assert nheads == 64 and 2 * nheads == LANES

    w = w_in[0]
    o_xbc, o_dt = inner, inner + cdim
    o_cq = o_dt + nheads
    o_ckv = o_cq + qlora
    o_kr = o_ckv + kvlora
    o_gate = o_kr + QK_ROPE
    perm = np.concatenate([np.arange(0, nheads, 2), np.arange(1, nheads, 2)])
    zc = lambda n: jnp.zeros((d, n), w.dtype)
    cols = [w[:, :inner], w[:, o_gate:], w[:, o_xbc:o_dt], w[:, o_cq:o_ckv], w[:, o_ckv:o_kr],
            w[:, o_dt:o_cq][:, perm], zc(LANES - nheads), w[:, o_kr:o_gate], zc(LANES - QK_ROPE)]
    used = inner + 2 * d + cdim + qlora + kvlora + 2 * LANES
    total = -(-used // 512) * 512
    cols.append(zc(total - used))
    w_r = jnp.concatenate(cols, axis=1).astype(BF16)
    offs = {"z": 0, "ga": inner, "gb": inner + d, "x": inner + 2 * d, "bc": 2 * inner + 2 * d}
    offs["cq"] = offs["bc"] + 2 * gn
    offs["ckv"] = offs["cq"] + qlora
    offs["dt"] = offs["ckv"] + kvlora
    offs["kr"] = offs["dt"] + LANES

    hw = QK_NOPE + LANES
    wq_r = jnp.pad(w_uq[0], ((0, 0), (0, 0), (0, hw - QK_NOPE - QK_ROPE))).reshape(qlora, MLA_HEADS * hw).astype(BF16)
    wk = w_uk[0].reshape(kvlora, MLA_HEADS * QK_NOPE).astype(BF16)
    wv = w_uv[0].reshape(kvlora, MLA_HEADS * V_HEAD).astype(BF16)
    pad_l = lambda v: jnp.pad(v[perm], (0, LANES - nheads)).reshape(1, LANES)
    dtb = pad_l(dt_bias[0])
    a_neg = pad_l(-jnp.exp(a_log[0]))
    dskip = jnp.repeat(d_skip[0], SSD_HEAD_DIM).reshape(1, inner)

    c_all = jnp.concatenate([c_prompt, c_sample], axis=0)
    mod = _ada(c_all, w_ada[0], b_ada[0])
    grp = np.concatenate([np.repeat(np.arange(bp), seq // MOD_ROWS), bp + np.arange(bs)])
    modg = mod[grp].reshape(t_all // MOD_ROWS, 1, 6 * d)

    x_all = jnp.concatenate([x_prompt.reshape(tp, d), x_sample.reshape(ts, d)], axis=0)
    proj = _inproj(x_all, modg, norm1_g[0], w_r, d)

    ssd_args = (conv_w[0], conv_b[0].reshape(1, cdim), dtb, a_neg, dskip, ssd_norm_g[0].reshape(1, inner))
    ys_p, conv_p, ssm_p = _ssd(proj, jnp.zeros((bp, CONV_W - 1, cdim), F32),
                               jnp.zeros((bp, inner, SSD_STATE), F32), *ssd_args,
                               nb=bp, seq=seq, row0=0, offs=offs, inner=inner)
    ys_s, conv_s, ssm_s = _ssd(proj, cache_conv[0], state_ssm[0].reshape(bs, inner, SSD_STATE),
                               *ssd_args, nb=bs, seq=lseq, row0=tp, offs=offs, inner=inner)

    past = cache_ckv.shape[2]
    cos_p, sin_p = _rope_tables(jnp.arange(seq))
    cos_s, sin_s = _rope_tables(past + jnp.arange(lseq))
    cos_t = jnp.concatenate([jnp.tile(cos_p, (bp, 1)), jnp.tile(cos_s, (bs, 1))], axis=0)
    sin_t = jnp.concatenate([jnp.tile(sin_p, (bp, 1)), jnp.tile(sin_s, (bs, 1))], axis=0)
    prep = functools.partial(_mla_prep, proj, cos_t, sin_t, q_norm_g[0], kv_norm_g[0], wq_r, wk, wv, offs=offs)
    ckv_p, kr_p, q_p, k_p, v_p = prep(row0=0, nrows=tp, absorbed=False)
    ckv_s, kr_s, qlat, qrope = prep(row0=tp, nrows=ts, absorbed=True)
    ym_p = _attn_prompt(q_p, k_p, v_p, nb=bp, seq=seq)
    o_lat = _attn_sample(qlat, qrope, cache_ckv[0], cache_kr[0], ckv_s, kr_s, nb=bs, seq=lseq)
    ym_s = _uv_sample(o_lat, wv)

    merged = _merge1(ys_p, ym_p, ys_s, ym_s, w_ssd_out[0].astype(BF16), w_mla_out[0].astype(BF16), proj, offs)
    x1, h2, logits_t = _merge2(merged, w_merge_out[0].astype(BF16), x_all, modg, norm2_g[0], w_router[0].T)

    eidx, pos, w_sel, cnt = _route(logits_t, e_bias[0])
    bm = MOE_ROWS
    counts = cnt[:, 0].astype(I32)
    padded = (counts + bm - 1) // bm * bm
    pad_end = jnp.cumsum(padded)
    pad_start = pad_end - padded
    nblocks = -(-(t_all * TOP_K) // bm) + ne
    dest = pad_start[eidx] + pos
    block_e = jnp.minimum(jnp.searchsorted(pad_end, jnp.arange(nblocks, dtype=I32) * bm, side="right"),
                          ne - 1).astype(I32)
    n_used = (pad_end[-1:] // bm).astype(I32)

    tb_d = _pick(t_all, (256, 128))
    hs = _dispatch(h2, _blocked(dest, tb_d), (pad_start + counts).astype(I32), pad_end.astype(I32), nblocks * bm)
    ys = _experts(hs, block_e, n_used, w_exp_gate[0], w_exp_up[0], w_exp_down[0])
    wsh_gu = jnp.concatenate([w_sh_gate[0], w_sh_up[0]], axis=1).astype(BF16)
    tb_c = 128
    y_all = _combine(ys, _blocked(dest, tb_c), w_sel.T, h2, x1, modg, wsh_gu, w_sh_down[0].astype(BF16),
                     final_norm_g)

    r5 = lambda a, n, l: a.reshape(1, n, l, a.shape[-1])
    return (y_all[:tp].reshape(bp, seq, d), y_all[tp:].reshape(bs, lseq, d),
            conv_p[None], ssm_p.reshape(1, bp, nheads, SSD_HEAD_DIM, SSD_STATE),
            r5(ckv_p, bp, seq), r5(kr_p[:, :QK_ROPE], bp, seq),
            conv_s[None], ssm_s.reshape(1, bs, nheads, SSD_HEAD_DIM, SSD_STATE),
            r5(ckv_s, bs, lseq), r5(kr_s[:, :QK_ROPE], bs, lseq))
```

```python
import functools
import math

import numpy as np
import jax
import jax.numpy as jnp
from jax import lax
from jax.experimental import pallas as pl
from jax.experimental.pallas import tpu as pltpu

F32 = jnp.float32
BF16 = jnp.bfloat16
I32 = jnp.int32

EPS = 1e-6
CHUNK = 64
SSD_HEAD_DIM = 64
SSD_GROUPS = 8
SSD_STATE = 128
CONV_W = 4
MLA_HEADS = 16
QK_NOPE = 128
QK_ROPE = 64
V_HEAD = 128
ROPE_THETA = 10000.0
TOP_K = 8
N_GROUPS = 8
TOPK_GROUPS = 4
ROUTED_SCALE = 2.5

LANES = 128
MOD_ROWS = 32
MOE_ROWS = 256
VMEM_LIMIT = 56 * 1024 * 1024


def _params(*sem):
    return pltpu.CompilerParams(dimension_semantics=sem, vmem_limit_bytes=VMEM_LIMIT)


def _pick(n, cands):
    for c in cands:
        if n % c == 0:
            return c
    raise ValueError(f"no tile in {cands} divides {n}")


def _sigmoid(x):
    return 1.0 / (1.0 + jnp.exp(-x))


def _silu(x):
    return x * _sigmoid(x)


def _dot(a, b):
    return jnp.dot(a, b, preferred_element_type=F32)


def _dot_t(a, b):
    return lax.dot_general(a, b, (((1,), (1,)), ((), ())), preferred_element_type=F32)


def _split(x, n):
    parts = []
    for _ in range(n - 1):
        p = x.astype(BF16)
        parts.append(p)
        x = x - p.astype(F32)
    parts.append(x.astype(BF16))
    return parts


def _dot_exact_rhs(a, b_bf16, n=3):
    acc = None
    for p in _split(a, n):
        t = _dot(p, b_bf16)
        acc = t if acc is None else acc + t
    return acc


def _dot3(a, b, dot=_dot):
    ah, al = _split(a, 2)
    bh, bl = _split(b, 2)
    return dot(ah, bh) + (dot(ah, bl) + dot(al, bh))


def _rms(x):
    return x * lax.rsqrt(jnp.mean(x * x, axis=-1, keepdims=True) + EPS)


def _modulate(y, sc_ref, sh_ref):
    rows, d = y.shape
    y3 = y.reshape(rows // MOD_ROWS, MOD_ROWS, d)
    return (y3 * (1.0 + sc_ref[...]) + sh_ref[...]).reshape(rows, d)


def _ada_body(c_ref, w_ref, b_ref, o_ref):
    o_ref[...] = _dot3(_silu(c_ref[...]), w_ref[...]) + b_ref[...]


def _ada(c_all, w_ada, b_ada):
    nb, d = c_all.shape
    n = w_ada.shape[1]
    tn = _pick(n, (1024, 512, 256, 128))
    return pl.pallas_call(
        _ada_body,
        grid=(n // tn,),
        in_specs=[pl.BlockSpec((nb, d), lambda j: (0, 0)),
                  pl.BlockSpec((d, tn), lambda j: (0, j)),
                  pl.BlockSpec((1, tn), lambda j: (0, j))],
        out_specs=pl.BlockSpec((nb, tn), lambda j: (0, j)),
        out_shape=jax.ShapeDtypeStruct((nb, n), F32),
        compiler_params=_params("arbitrary"),
    )(c_all, w_ada, b_ada.reshape(1, n))


def _inproj_body(x_ref, sc_ref, sh_ref, g_ref, w_ref, o_ref, h_scr):
    @pl.when(pl.program_id(1) == 0)
    def _():
        h = _modulate(_rms(x_ref[...]) * g_ref[...], sc_ref, sh_ref)
        h_scr[...] = h.astype(BF16)

    o_ref[...] = _dot(h_scr[...], w_ref[...])


def _inproj(x_all, modg, norm_g, w_r, d):
    t = x_all.shape[0]
    n = w_r.shape[1]
    tm = _pick(t, (1024, 512, 256, 128))
    tn = 512
    ng = tm // MOD_ROWS
    return pl.pallas_call(
        _inproj_body,
        grid=(t // tm, n // tn),
        in_specs=[pl.BlockSpec((tm, d), lambda i, j: (i, 0)),
                  pl.BlockSpec((ng, 1, d), lambda i, j: (i, 0, 1)),
                  pl.BlockSpec((ng, 1, d), lambda i, j: (i, 0, 0)),
                  pl.BlockSpec((1, d), lambda i, j: (0, 0)),
                  pl.BlockSpec((d, tn), lambda i, j: (0, j))],
        out_specs=pl.BlockSpec((tm, tn), lambda i, j: (i, j)),
        out_shape=jax.ShapeDtypeStruct((t, n), F32),
        scratch_shapes=[pltpu.VMEM((tm, d), BF16)],
        compiler_params=_params("arbitrary", "arbitrary"),
    )(x_all, modg, modg, norm_g.reshape(1, d), w_r)


def _ssd_body(z_ref, x_ref, bc_ref, dt_ref, conv0_ref, h0_ref, cw_ref, cb_ref, dtb_ref, a_ref,
              dskip_ref, ng_ref, y_ref, nconv_ref, nssm_ref,
              carry_x, carry_bc, xp_x, xp_bc, state, y_scr, *, lb, inner):
    q = CHUNK
    c = pl.program_id(1)
    nc = pl.num_programs(1)
    gn = SSD_GROUPS * SSD_STATE

    @pl.when(c == 0)
    def _():
        carry_x[...] = jnp.zeros_like(carry_x)
        carry_bc[...] = jnp.zeros_like(carry_bc)
        carry_x[8 - (CONV_W - 1):8, :] = conv0_ref[0, :, :inner]
        carry_bc[8 - (CONV_W - 1):8, :] = conv0_ref[0, :, inner:]
        state[...] = h0_ref[0]

    def pad_rows(v):
        if lb == q:
            return v
        return jnp.concatenate([v, jnp.zeros((q - lb, v.shape[1]), v.dtype)], axis=0)

    def conv(raw, carry, xp, w_lo, w_hi):
        xp[0:8, :] = carry[...]
        xp[8:8 + q, :] = raw
        acc = cb_ref[:, w_lo:w_hi]
        for j in range(CONV_W):
            acc = acc + xp[8 - j:8 - j + q, :] * cw_ref[CONV_W - 1 - j:CONV_W - j, w_lo:w_hi]
        return _silu(acc)

    x_raw = pad_rows(x_ref[...])
    bc_raw = pad_rows(bc_ref[...])
    xc = conv(x_raw, carry_x, xp_x, 0, inner)
    bcc = conv(bc_raw, carry_bc, xp_bc, inner, inner + 2 * gn)
    if lb == q:
        carry_x[...] = x_raw[q - 8:q, :]
        carry_bc[...] = bc_raw[q - 8:q, :]

    @pl.when(c == nc - 1)
    def _():
        nconv_ref[0, :, :inner] = x_raw[lb - (CONV_W - 1):lb, :]
        nconv_ref[0, :, inner:] = bc_raw[lb - (CONV_W - 1):lb, :]

    dtr = dt_ref[...] + dtb_ref[...]
    dtv = jnp.maximum(dtr, 0.0) + jnp.log1p(jnp.exp(-jnp.abs(dtr)))
    dtv = pad_rows(dtv)
    la = dtv * a_ref[...]

    row = lax.broadcasted_iota(I32, (q, q), 0)
    col = lax.broadcasted_iota(I32, (q, q), 1)
    tri = jnp.where(row >= col, 1.0, 0.0).astype(BF16)
    parts = _split(la, 3)
    a_cum = _dot(tri, parts[0]) + _dot(tri, parts[1]) + _dot(tri, parts[2])
    a_last = a_cum[q - 1:q, :]

    r2 = lax.broadcasted_iota(I32, (q, 2 * q), 0)
    c2 = lax.broadcasted_iota(I32, (q, 2 * q), 1)
    u_e = jnp.where((c2 < q) & (r2 <= c2), 1.0, 0.0).astype(BF16)
    u_o = jnp.where((c2 >= q) & (r2 <= c2 - q), 1.0, 0.0).astype(BF16)
    i_e = jnp.where(r2 == c2, 1.0, 0.0).astype(BF16)
    i_o = jnp.where(r2 == c2 - q, 1.0, 0.0).astype(BF16)
    la_t = la.T
    dt_t = dtv.T
    hp = la_t.shape[0] // 4
    acol = _dot_exact_rhs(la_t[0:hp], u_e) + _dot_exact_rhs(la_t[hp:2 * hp], u_o)
    dtrow = _dot_exact_rhs(dt_t[0:hp], i_e) + _dot_exact_rhs(dt_t[hp:2 * hp], i_o)
    w_t = (dtv * jnp.exp(a_last - a_cum)).T
    dec_b = jnp.broadcast_to(jnp.exp(jnp.sum(la_t, axis=1, keepdims=True)), (la_t.shape[0], SSD_STATE))

    lane = lax.broadcasted_iota(I32, (q, 2 * q), 1)
    causal2 = lax.broadcasted_iota(I32, (q, 2 * q), 0) >= jnp.where(lane < q, lane, lane - q)
    first_half = lane < q
    rr = lax.broadcasted_iota(I32, (2 * SSD_HEAD_DIM, 2 * SSD_HEAD_DIM), 0)
    cc = lax.broadcasted_iota(I32, (2 * SSD_HEAD_DIM, 2 * SSD_HEAD_DIM), 1)
    bd_mask = (rr < SSD_HEAD_DIM) == (cc < SSD_HEAD_DIM)
    top_rows_q = lax.broadcasted_iota(I32, (2 * SSD_HEAD_DIM, q), 0) < SSD_HEAD_DIM
    top_rows_n = lax.broadcasted_iota(I32, (2 * SSD_HEAD_DIM, SSD_STATE), 0) < SSD_HEAD_DIM

    pairs_per_group = (inner // SSD_HEAD_DIM) // SSD_GROUPS // 2
    for g in range(SSD_GROUPS):
        b_g = bcc[:, g * SSD_STATE:(g + 1) * SSD_STATE].astype(BF16)
        c_g = bcc[:, gn + g * SSD_STATE:gn + (g + 1) * SSD_STATE].astype(BF16)
        cb2 = _dot_t(c_g, jnp.concatenate([b_g, b_g], axis=0))
        for kk in range(pairs_per_group):
            k = g * pairs_per_group + kk
            lo, hi = k * 2 * SSD_HEAD_DIM, (k + 1) * 2 * SSD_HEAD_DIM
            xp = xc[:, lo:hi]
            arow = jnp.where(first_half, a_cum[:, k:k + 1], a_cum[:, hp + k:hp + k + 1])
            seg = jnp.where(causal2, arow - acol[k:k + 1, :], -jnp.inf)
            m = (jnp.exp(seg) * cb2 * dtrow[k:k + 1, :]).astype(BF16)
            xbd = jnp.where(bd_mask, jnp.concatenate([xp, xp], axis=0), 0.0).astype(BF16)
            y_diag = _dot(m, xbd)
            s_k = state[lo:hi, :]
            y_off = _dot_t(c_g, s_k.astype(BF16)) * jnp.exp(arow)
            y_scr[:, lo:hi] = y_diag + y_off + xp * dskip_ref[:, lo:hi]
            w2 = jnp.where(top_rows_q, w_t[k:k + 1, :], w_t[hp + k:hp + k + 1, :])
            contrib = _dot((xp.T * w2).astype(BF16), b_g)
            d_k = jnp.where(top_rows_n, dec_b[k:k + 1, :], dec_b[hp + k:hp + k + 1, :])
            state[lo:hi, :] = s_k * d_k + contrib

    @pl.when(c == nc - 1)
    def _():
        nssm_ref[0] = state[...]

    y = y_scr[0:lb, :] * _silu(z_ref[...])
    y_ref[...] = (_rms(y) * ng_ref[...]).astype(y_ref.dtype)


def _ssd(proj, conv0, h0, conv_w, conv_b, dtb, a_neg, dskip, norm_g, *, nb, seq, row0, offs, inner):
    q = CHUNK
    lb = min(seq, q)
    assert seq % lb == 0 and row0 % lb == 0 and lb % 16 == 0
    nc = seq // lb
    rb0 = row0 // lb
    cdim = conv_w.shape[1]
    hp_rows = inner
    kern = functools.partial(_ssd_body, lb=lb, inner=inner)
    rows = lambda b, c: rb0 + b * nc + c
    in_specs = [
        pl.BlockSpec((lb, inner), lambda b, c: (rows(b, c), offs["z"] // inner)),
        pl.BlockSpec((lb, inner), lambda b, c: (rows(b, c), offs["x"] // inner)),
        pl.BlockSpec((lb, cdim - inner), lambda b, c: (rows(b, c), offs["bc"] // (cdim - inner))),
        pl.BlockSpec((lb, LANES), lambda b, c: (rows(b, c), offs["dt"] // LANES)),
        pl.BlockSpec((1, CONV_W - 1, cdim), lambda b, c: (b, 0, 0)),
        pl.BlockSpec((1, hp_rows, SSD_STATE), lambda b, c: (b, 0, 0)),
        pl.BlockSpec((CONV_W, cdim), lambda b, c: (0, 0)),
        pl.BlockSpec((1, cdim), lambda b, c: (0, 0)),
        pl.BlockSpec((1, LANES), lambda b, c: (0, 0)),
        pl.BlockSpec((1, LANES), lambda b, c: (0, 0)),
        pl.BlockSpec((1, inner), lambda b, c: (0, 0)),
        pl.BlockSpec((1, inner), lambda b, c: (0, 0)),
    ]
    args = [proj, proj, proj, proj, conv0, h0, conv_w, conv_b, dtb, a_neg, dskip, norm_g]
    return pl.pallas_call(
        kern,
        grid=(nb, nc),
        in_specs=in_specs,
        out_specs=[pl.BlockSpec((lb, inner), lambda b, c: (b * nc + c, 0)),
                   pl.BlockSpec((1, CONV_W - 1, cdim), lambda b, c: (b, 0, 0)),
                   pl.BlockSpec((1, hp_rows, SSD_STATE), lambda b, c: (b, 0, 0))],
        out_shape=[jax.ShapeDtypeStruct((nb * seq, inner), BF16),
                   jax.ShapeDtypeStruct((nb, CONV_W - 1, cdim), F32),
                   jax.ShapeDtypeStruct((nb, hp_rows, SSD_STATE), F32)],
        scratch_shapes=[pltpu.VMEM((8, inner), F32), pltpu.VMEM((8, cdim - inner), F32),
                        pltpu.VMEM((8 + q, inner), F32), pltpu.VMEM((8 + q, cdim - inner), F32),
                        pltpu.VMEM((hp_rows, SSD_STATE), F32), pltpu.VMEM((q, inner), F32)],
        compiler_params=_params("arbitrary", "arbitrary"),
    )(*args)


def _rope128(x, cos, sin):
    lane = lax.broadcasted_iota(I32, x.shape, 1)
    half = QK_ROPE // 2
    swapped = jnp.where((lane % QK_ROPE) < half, pltpu.roll(x, LANES - half, 1), pltpu.roll(x, half, 1))
    return x * cos + swapped * sin


def _mla_prep_body(cq_ref, ckv_ref, kr_ref, cos_ref, sin_ref, qg_ref, kvg_ref, wq_ref, wk_ref, wv_ref,
                   *out_refs, absorbed, scale):
    cos, sin = cos_ref[...], sin_ref[...]
    ckv_n = _rms(ckv_ref[...]) * kvg_ref[...]
    kr_r = _rope128(kr_ref[...], cos, sin)
    qn = (_rms(cq_ref[...]) * qg_ref[...]).astype(BF16)
    hw = QK_NOPE + LANES
    if absorbed:
        ckvn_ref, krr_ref, qlat_ref, qrope_ref = out_refs
    else:
        ckvn_ref, krr_ref, q_ref, k_ref, v_ref = out_refs
        ckv_b = ckv_n.astype(BF16)
        v_ref[...] = _dot(ckv_b, wv_ref[...]).astype(BF16)
        kr_b = kr_r.astype(BF16)
    ckvn_ref[...] = ckv_n
    krr_ref[...] = kr_r
    for h in range(MLA_HEADS):
        qh = _dot(qn, wq_ref[:, h * hw:(h + 1) * hw]) * scale
        q_nope = qh[:, :QK_NOPE]
        q_rope = _rope128(qh[:, QK_NOPE:], cos, sin)
        if absorbed:
            qlat_ref[h] = _dot_t(q_nope.astype(BF16), wk_ref[:, h * QK_NOPE:(h + 1) * QK_NOPE]).astype(BF16)
            qrope_ref[h] = q_rope[:, :QK_ROPE].astype(BF16)
        else:
            q_ref[:, h * hw:h * hw + QK_NOPE] = q_nope.astype(BF16)
            q_ref[:, h * hw + QK_NOPE:(h + 1) * hw] = q_rope.astype(BF16)
            k_ref[:, h * hw:h * hw + QK_NOPE] = _dot(ckv_b, wk_ref[:, h * QK_NOPE:(h + 1) * QK_NOPE]).astype(BF16)
            k_ref[:, h * hw + QK_NOPE:(h + 1) * hw] = kr_b


def _mla_prep(proj, cos_t, sin_t, q_g, kv_g, wq_r, wk, wv, *, row0, nrows, offs, absorbed):
    qlora, kvlora = q_g.shape[0], kv_g.shape[0]
    tm = _pick(nrows, (512, 256, 128))
    assert row0 % tm == 0
    rb0 = row0 // tm
    hw = QK_NOPE + LANES
    scale = 1.0 / math.sqrt(QK_NOPE + QK_ROPE)
    const = lambda i: (0, 0)
    in_specs = [pl.BlockSpec((tm, qlora), lambda i: (rb0 + i, offs["cq"] // qlora)),
                pl.BlockSpec((tm, kvlora), lambda i: (rb0 + i, offs["ckv"] // kvlora)),
                pl.BlockSpec((tm, LANES), lambda i: (rb0 + i, offs["kr"] // LANES)),
                pl.BlockSpec((tm, LANES), lambda i: (rb0 + i, 0)),
                pl.BlockSpec((tm, LANES), lambda i: (rb0 + i, 0)),
                pl.BlockSpec((1, qlora), const), pl.BlockSpec((1, kvlora), const),
                pl.BlockSpec(wq_r.shape, const), pl.BlockSpec(wk.shape, const), pl.BlockSpec(wv.shape, const)]
    out_specs = [pl.BlockSpec((tm, kvlora), lambda i: (i, 0)), pl.BlockSpec((tm, LANES), lambda i: (i, 0))]
    out_shape = [jax.ShapeDtypeStruct((nrows, kvlora), F32), jax.ShapeDtypeStruct((nrows, LANES), F32)]
    if absorbed:
        out_specs += [pl.BlockSpec((MLA_HEADS, tm, kvlora), lambda i: (0, i, 0)),
                      pl.BlockSpec((MLA_HEADS, tm, QK_ROPE), lambda i: (0, i, 0))]
        out_shape += [jax.ShapeDtypeStruct((MLA_HEADS, nrows, kvlora), BF16),
                      jax.ShapeDtypeStruct((MLA_HEADS, nrows, QK_ROPE), BF16)]
    else:
        out_specs += [pl.BlockSpec((tm, MLA_HEADS * hw), lambda i: (i, 0)),
                      pl.BlockSpec((tm, MLA_HEADS * hw), lambda i: (i, 0)),
                      pl.BlockSpec((tm, MLA_HEADS * V_HEAD), lambda i: (i, 0))]
        out_shape += [jax.ShapeDtypeStruct((nrows, MLA_HEADS * hw), BF16),
                      jax.ShapeDtypeStruct((nrows, MLA_HEADS * hw), BF16),
                      jax.ShapeDtypeStruct((nrows, MLA_HEADS * V_HEAD), BF16)]
    return pl.pallas_call(
        functools.partial(_mla_prep_body, absorbed=absorbed, scale=scale),
        grid=(nrows // tm,),
        in_specs=in_specs, out_specs=out_specs, out_shape=out_shape,
        compiler_params=_params("arbitrary"),
    )(proj, proj, proj, cos_t, sin_t, q_g.reshape(1, -1), kv_g.reshape(1, -1), wq_r, wk, wv)


def _attn_body(q_ref, k_ref, v_ref, o_ref, *, tq):
    qi = pl.program_id(2)
    qb = q_ref[0]

    def block(kj):
        start = pl.multiple_of(kj * tq, tq)
        s = _dot_t(qb, k_ref[0, pl.ds(start, tq), :])
        return s, v_ref[0, pl.ds(start, tq), :]

    shift = CHUNK.bit_length() - 1
    r = lax.broadcasted_iota(I32, (tq, tq), 0) >> shift
    c = lax.broadcasted_iota(I32, (tq, tq), 1) >> shift
    s, v = block(qi)
    s = jnp.where(c <= r, s, -jnp.inf)
    m = jnp.max(s, axis=1, keepdims=True)
    p = jnp.exp(s - m)
    l = jnp.sum(p, axis=1, keepdims=True)
    acc = _dot(p.astype(BF16), v)

    def body(kj, carry):
        m, l, acc = carry
        s, v = block(kj)
        m_new = jnp.maximum(m, jnp.max(s, axis=1, keepdims=True))
        alpha = jnp.exp(m - m_new)
        p = jnp.exp(s - m_new)
        l = alpha * l + jnp.sum(p, axis=1, keepdims=True)
        acc = alpha * acc + _dot(p.astype(BF16), v)
        return m_new, l, acc

    m, l, acc = lax.fori_loop(0, qi, body, (m, l, acc))
    o_ref[...] = (acc / l).astype(o_ref.dtype)


def _attn_prompt(q, k, v, *, nb, seq):
    hw = QK_NOPE + LANES
    tq = min(256, seq // 2)
    assert seq % tq == 0 and tq % CHUNK == 0
    nq = seq // tq
    q3 = q.reshape(nb, seq, MLA_HEADS * hw)
    k3 = k.reshape(nb, seq, MLA_HEADS * hw)
    v3 = v.reshape(nb, seq, MLA_HEADS * V_HEAD)
    return pl.pallas_call(
        functools.partial(_attn_body, tq=tq),
        grid=(nb, MLA_HEADS, nq),
        in_specs=[pl.BlockSpec((1, tq, hw), lambda b, h, i: (b, i, h)),
                  pl.BlockSpec((1, seq, hw), lambda b, h, i: (b, 0, h)),
                  pl.BlockSpec((1, seq, V_HEAD), lambda b, h, i: (b, 0, h))],
        out_specs=pl.BlockSpec((tq, V_HEAD), lambda b, h, i: (b * nq + i, h)),
        out_shape=jax.ShapeDtypeStruct((nb * seq, MLA_HEADS * V_HEAD), BF16),
        compiler_params=_params("arbitrary", "arbitrary", "arbitrary"),
    )(q3, k3, v3)


def _attn_sample_body(ql_ref, qr_ref, pckv_ref, pkr_ref, nckv_ref, nkr_ref, o_ref, *, past, seq):
    nh = ql_ref.shape[0]
    ql = ql_ref[...].reshape(nh * seq, ql_ref.shape[2])
    qr = qr_ref[...].reshape(nh * seq, qr_ref.shape[2])
    pckv = pckv_ref[0].astype(BF16)
    nckv = nckv_ref[...].astype(BF16)
    s_p = _dot_t(ql, pckv) + _dot_t(qr, pkr_ref[0].astype(BF16))
    s_n = _dot_t(ql, nckv) + _dot_t(qr, nkr_ref[:, :QK_ROPE].astype(BF16))
    shift = CHUNK.bit_length() - 1
    q_chunk = (past + lax.broadcasted_iota(I32, (nh * seq, 1), 0) % seq) >> shift
    kp_chunk = lax.broadcasted_iota(I32, (1, past), 1) >> shift
    kn_chunk = (past + lax.broadcasted_iota(I32, (1, seq), 1)) >> shift
    s_p = jnp.where(kp_chunk <= q_chunk, s_p, -jnp.inf)
    s_n = jnp.where(kn_chunk <= q_chunk, s_n, -jnp.inf)
    m = jnp.maximum(jnp.max(s_p, axis=1, keepdims=True), jnp.max(s_n, axis=1, keepdims=True))
    p_p = jnp.exp(s_p - m)
    p_n = jnp.exp(s_n - m)
    l = jnp.sum(p_p, axis=1, keepdims=True) + jnp.sum(p_n, axis=1, keepdims=True)
    o = (_dot(p_p.astype(BF16), pckv) + _dot(p_n.astype(BF16), nckv)) / l
    o_ref[...] = o.reshape(o_ref.shape).astype(o_ref.dtype)


def _attn_sample(qlat, qrope, past_ckv, past_kr, ckv_n, kr_r, *, nb, seq):
    past = past_ckv.shape[1]
    r = past_ckv.shape[2]
    return pl.pallas_call(
        functools.partial(_attn_sample_body, past=past, seq=seq),
        grid=(nb,),
        in_specs=[pl.BlockSpec((MLA_HEADS, seq, r), lambda b: (0, b, 0)),
                  pl.BlockSpec((MLA_HEADS, seq, QK_ROPE), lambda b: (0, b, 0)),
                  pl.BlockSpec((1, past, r), lambda b: (b, 0, 0)),
                  pl.BlockSpec((1, past, QK_ROPE), lambda b: (b, 0, 0)),
                  pl.BlockSpec((seq, r), lambda b: (b, 0)),
                  pl.BlockSpec((seq, LANES), lambda b: (b, 0))],
        out_specs=pl.BlockSpec((MLA_HEADS, seq, r), lambda b: (0, b, 0)),
        out_shape=jax.ShapeDtypeStruct((MLA_HEADS, nb * seq, r), BF16),
        compiler_params=_params("arbitrary"),
    )(qlat, qrope, past_ckv, past_kr, ckv_n, kr_r)


def _uv_body(o_ref, w_ref, y_ref):
    y_ref[...] = _dot(o_ref[0], w_ref[...]).astype(y_ref.dtype)


def _uv_sample(o_lat, wv):
    nh, nrows, r = o_lat.shape
    return pl.pallas_call(
        _uv_body,
        grid=(nh,),
        in_specs=[pl.BlockSpec((1, nrows, r), lambda h: (h, 0, 0)),
                  pl.BlockSpec((r, V_HEAD), lambda h: (0, h))],
        out_specs=pl.BlockSpec((nrows, V_HEAD), lambda h: (0, h)),
        out_shape=jax.ShapeDtypeStruct((nrows, nh * V_HEAD), BF16),
        compiler_params=_params("arbitrary"),
    )(o_lat, wv)


def _merge1_body(ysp_ref, ymp_ref, yss_ref, yms_ref, ws_ref, wm_ref, ga_ref, gb_ref, o_ref, *, npb):
    def run(ys_ref, ym_ref):
        a = _dot(ys_ref[...], ws_ref[...])
        b = _dot(ym_ref[...], wm_ref[...])
        o_ref[...] = (_sigmoid(ga_ref[...]) * a + _sigmoid(gb_ref[...]) * b).astype(o_ref.dtype)

    i = pl.program_id(0)
    pl.when(i < npb)(lambda: run(ysp_ref, ymp_ref))
    pl.when(i >= npb)(lambda: run(yss_ref, yms_ref))


def _merge1(ys_p, ym_p, ys_s, ym_s, w_ssd_out, w_mla_out, proj, offs):
    (tp, inner), ts = ys_p.shape, ys_s.shape[0]
    dm = ym_p.shape[1]
    d = w_ssd_out.shape[1]
    tm = _pick(math.gcd(tp, ts), (512, 256, 128))
    tn = 512
    npb, nsb = tp // tm, ts // tm
    prow = lambda i, j: (jnp.minimum(i, npb - 1), 0)
    srow = lambda i, j: (jnp.maximum(i - npb, 0), 0)
    return pl.pallas_call(
        functools.partial(_merge1_body, npb=npb),
        grid=(npb + nsb, d // tn),
        in_specs=[pl.BlockSpec((tm, inner), prow), pl.BlockSpec((tm, dm), prow),
                  pl.BlockSpec((tm, inner), srow), pl.BlockSpec((tm, dm), srow),
                  pl.BlockSpec((inner, tn), lambda i, j: (0, j)),
                  pl.BlockSpec((dm, tn), lambda i, j: (0, j)),
                  pl.BlockSpec((tm, tn), lambda i, j: (i, offs["ga"] // tn + j)),
                  pl.BlockSpec((tm, tn), lambda i, j: (i, offs["gb"] // tn + j))],
        out_specs=pl.BlockSpec((tm, tn), lambda i, j: (i, j)),
        out_shape=jax.ShapeDtypeStruct((tp + ts, d), BF16),
        compiler_params=_params("arbitrary", "arbitrary"),
    )(ys_p, ym_p, ys_s, ym_s, w_ssd_out, w_mla_out, proj, proj)


def _merge2_body(m_ref, w_ref, x_ref, g1_ref, sc_ref, sh_ref, ng_ref, wr_ref, x1_ref, h2_ref, lg_ref):
    rows, d = x_ref.shape
    upd = _dot(m_ref[...], w_ref[...]).reshape(rows // MOD_ROWS, MOD_ROWS, d) * g1_ref[...]
    x1 = x_ref[...] + upd.reshape(rows, d)
    x1_ref[...] = x1
    h2 = _modulate(_rms(x1) * ng_ref[...], sc_ref, sh_ref)
    h2_ref[...] = h2
    lg_ref[...] = _dot3(wr_ref[...], h2, dot=_dot_t)


def _merge2(merged, w_merge, x_all, modg, norm_g, w_router_t):
    t, d = x_all.shape
    ne = w_router_t.shape[0]
    tm = _pick(t, (512, 256, 128))
    ng = tm // MOD_ROWS
    return pl.pallas_call(
        _merge2_body,
        grid=(t // tm,),
        in_specs=[pl.BlockSpec((tm, d), lambda i: (i, 0)),
                  pl.BlockSpec((d, d), lambda i: (0, 0)),
                  pl.BlockSpec((tm, d), lambda i: (i, 0)),
                  pl.BlockSpec((ng, 1, d), lambda i: (i, 0, 2)),
                  pl.BlockSpec((ng, 1, d), lambda i: (i, 0, 4)),
                  pl.BlockSpec((ng, 1, d), lambda i: (i, 0, 3)),
                  pl.BlockSpec((1, d), lambda i: (0, 0)),
                  pl.BlockSpec((ne, d), lambda i: (0, 0))],
        out_specs=[pl.BlockSpec((tm, d), lambda i: (i, 0)),
                   pl.BlockSpec((tm, d), lambda i: (i, 0)),
                   pl.BlockSpec((ne, tm), lambda i: (0, i))],
        out_shape=[jax.ShapeDtypeStruct((t, d), F32), jax.ShapeDtypeStruct((t, d), F32),
                   jax.ShapeDtypeStruct((ne, t), F32)],
        compiler_params=_params("arbitrary"),
    )(merged, w_merge, x_all, modg, modg, modg, norm_g.reshape(1, d), w_router_t)


def _route_body(lg_ref, eb_ref, eidx_ref, pos_ref, w_ref, cnt_ref, carry):
    ne, tr = lg_ref.shape
    per_group = ne // N_GROUPS

    @pl.when(pl.program_id(0) == 0)
    def _():
        carry[...] = jnp.zeros_like(carry)

    scores = _sigmoid(lg_ref[...])
    choice = scores + eb_ref[...]
    sub = lax.broadcasted_iota(I32, (per_group, tr), 0)
    gscore, blocks = [], []
    for g in range(N_GROUPS):
        blk = choice[g * per_group:(g + 1) * per_group, :]
        m1 = jnp.max(blk, axis=0, keepdims=True)
        first = jnp.min(jnp.where(blk == m1, sub, per_group), axis=0, keepdims=True)
        m2 = jnp.max(jnp.where(sub == first, -jnp.inf, blk), axis=0, keepdims=True)
        gscore.append(m1 + m2)
        blocks.append(blk)
    masked = []
    for g in range(N_GROUPS):
        rank = jnp.zeros((1, tr), I32)
        for g2 in range(N_GROUPS):
            if g2 == g:
                continue
            beats = (gscore[g2] > gscore[g]) | ((gscore[g2] == gscore[g]) & (g2 < g))
            rank = rank + beats.astype(I32)
        masked.append(jnp.where(rank < TOPK_GROUPS, blocks[g], -jnp.inf))
    cm = jnp.concatenate(masked, axis=0)

    eid = lax.broadcasted_iota(I32, (ne, tr), 0)
    rank = jnp.zeros((ne, tr), I32)
    for e2 in range(ne):
        rowv = cm[e2:e2 + 1, :]
        beats = (rowv > cm) | ((rowv == cm) & (eid > e2))
        rank = rank + beats.astype(I32)
    sel = rank < TOP_K
    wsel = jnp.where(sel, scores, 0.0)
    wfull = wsel / jnp.sum(wsel, axis=0, keepdims=True) * ROUTED_SCALE

    r = lax.broadcasted_iota(I32, (tr, tr), 0)
    c = lax.broadcasted_iota(I32, (tr, tr), 1)
    before = jnp.where(r < c, 1.0, 0.0).astype(BF16)
    self = jnp.where(sel, 1.0, 0.0)
    pos = carry[:, 0:1] + _dot(self.astype(BF16), before)
    carry[...] = carry[...] + jnp.sum(self, axis=1, keepdims=True)
    cnt_ref[...] = carry[...]

    eid_f = eid.astype(F32)
    for k in range(TOP_K):
        pick = sel & (rank == k)
        eidx_ref[k:k + 1, :] = jnp.sum(jnp.where(pick, eid_f, 0.0), axis=0, keepdims=True).astype(I32)
        pos_ref[k:k + 1, :] = jnp.sum(jnp.where(pick, pos, 0.0), axis=0, keepdims=True).astype(I32)
        w_ref[k:k + 1, :] = jnp.sum(jnp.where(pick, wfull, 0.0), axis=0, keepdims=True)


def _route(logits_t, e_bias):
    ne, t = logits_t.shape
    tr = _pick(t, (512, 256, 128))
    return pl.pallas_call(
        _route_body,
        grid=(t // tr,),
        in_specs=[pl.BlockSpec((ne, tr), lambda i: (0, i)), pl.BlockSpec((ne, 1), lambda i: (0, 0))],
        out_specs=[pl.BlockSpec((TOP_K, tr), lambda i: (0, i)), pl.BlockSpec((TOP_K, tr), lambda i: (0, i)),
                   pl.BlockSpec((TOP_K, tr), lambda i: (0, i)), pl.BlockSpec((ne, LANES), lambda i: (0, 0))],
        out_shape=[jax.ShapeDtypeStruct((TOP_K, t), I32), jax.ShapeDtypeStruct((TOP_K, t), I32),
                   jax.ShapeDtypeStruct((TOP_K, t), F32), jax.ShapeDtypeStruct((ne, LANES), F32)],
        scratch_shapes=[pltpu.VMEM((ne, LANES), F32)],
        compiler_params=_params("arbitrary"),
    )(logits_t, e_bias.reshape(ne, 1))


def _row_copy(src, s, dst, d, sem):
    return pltpu.make_async_copy(src.at[pl.ds(s, 1), :], dst.at[pl.ds(d, 1), :], sem)


def _drain(wait_one, n, group=64):
    assert n % group == 0

    def body(j, c):
        for _ in range(group):
            wait_one()
        return c

    lax.fori_loop(0, n // group, body, 0)


def _dispatch_body(fill_lo_ref, fill_hi_ref, h_ref, dest_ref, hs_ref, dest_s, zrow, sem, dsem, *, tb):
    i = pl.program_id(0)
    cp = pltpu.make_async_copy(dest_ref.at[i], dest_s, dsem)
    cp.start()

    @pl.when(i == 0)
    def _():
        zrow[...] = jnp.zeros_like(zrow)

        def per_expert(fn):
            def body(e, _):
                lax.fori_loop(fill_lo_ref[e], fill_hi_ref[e], lambda s, c: (fn(s), c)[1], 0)
                return 0
            lax.fori_loop(0, fill_lo_ref.shape[0], body, 0)

        per_expert(lambda s: _row_copy(zrow, 0, hs_ref, s, sem).start())
        per_expert(lambda s: _row_copy(zrow, 0, hs_ref, s, sem).wait())

    cp.wait()

    def issue(t, c):
        for k in range(TOP_K):
            _row_copy(h_ref, t, hs_ref, dest_s[t * TOP_K + k], sem).start(priority=k % 2)
        return c

    lax.fori_loop(0, tb, issue, 0, unroll=4)
    _drain(lambda: _row_copy(h_ref, 0, hs_ref, 0, sem).wait(), TOP_K * tb)


def _dispatch(h2, dest_blk, fill_lo, fill_hi, n_slots):
    t, d = h2.shape
    nblk, n = dest_blk.shape
    tb = n // TOP_K
    return pl.pallas_call(
        functools.partial(_dispatch_body, tb=tb),
        grid_spec=pltpu.PrefetchScalarGridSpec(
            num_scalar_prefetch=2,
            grid=(nblk,),
            in_specs=[pl.BlockSpec((tb, d), lambda i, lo, hi: (i, 0)),
                      pl.BlockSpec(memory_space=pl.ANY)],
            out_specs=pl.BlockSpec(memory_space=pl.ANY),
            scratch_shapes=[pltpu.SMEM((n,), I32), pltpu.VMEM((8, d), F32),
                            pltpu.SemaphoreType.DMA, pltpu.SemaphoreType.DMA]),
        out_shape=jax.ShapeDtypeStruct((n_slots, d), F32),
        compiler_params=_params("arbitrary"),
    )(fill_lo, fill_hi, h2, dest_blk)


def _experts_body(be_ref, nu_ref, x_ref, wg_ref, wu_ref, wd_ref, o_ref, wgu_s, wd_s):
    i = pl.program_id(0)
    hid = wg_ref.shape[2]

    @pl.when(i < nu_ref[0])
    def _():
        @pl.when((i == 0) | (be_ref[i] != be_ref[jnp.maximum(i - 1, 0)]))
        def _():
            wgu_s[:, :hid] = wg_ref[0].astype(BF16)
            wgu_s[:, hid:] = wu_ref[0].astype(BF16)
            wd_s[...] = wd_ref[0].astype(BF16)

        gu = _dot(x_ref[...].astype(BF16), wgu_s[...])
        act = (_silu(gu[:, :hid]) * gu[:, hid:]).astype(BF16)
        o_ref[...] = _dot(act, wd_s[...])


def _experts(hs, block_e, n_used, wg, wu, wd):
    n_slots, d = hs.shape
    ne, _, hid = wg.shape
    bm = MOE_ROWS
    nblocks = n_slots // bm
    blk = lambda i, be, nu: (jnp.minimum(i, nu[0] - 1), 0)
    return pl.pallas_call(
        _experts_body,
        grid_spec=pltpu.PrefetchScalarGridSpec(
            num_scalar_prefetch=2,
            grid=(nblocks,),
            in_specs=[pl.BlockSpec((bm, d), blk),
                      pl.BlockSpec((1, d, hid), lambda i, be, nu: (be[i], 0, 0)),
                      pl.BlockSpec((1, d, hid), lambda i, be, nu: (be[i], 0, 0)),
                      pl.BlockSpec((1, hid, d), lambda i, be, nu: (be[i], 0, 0))],
            out_specs=pl.BlockSpec((bm, d), blk),
            scratch_shapes=[pltpu.VMEM((d, 2 * hid), BF16), pltpu.VMEM((hid, d), BF16)]),
        out_shape=jax.ShapeDtypeStruct((n_slots, d), F32),
        compiler_params=_params("arbitrary"),
    )(block_e, n_used, hs, wg, wu, wd)


def _combine_body(ys_ref, dest_ref, w_ref, h_ref, x1_ref, g2_ref, wgu_ref, wd_ref, fg_ref, o_ref,
                  dest_s, gbuf, sem, dsem, *, tb):
    i = pl.program_id(0)
    cp = pltpu.make_async_copy(dest_ref.at[i], dest_s, dsem)
    cp.start()
    cp.wait()

    def issue(t, c):
        for k in range(TOP_K):
            _row_copy(ys_ref, dest_s[t * TOP_K + k], gbuf, k * tb + t, sem).start(priority=k % 2)
        return c

    lax.fori_loop(0, tb, issue, 0, unroll=4)

    hid = wd_ref.shape[0]
    gu = _dot(h_ref[...].astype(BF16), wgu_ref[...])
    moe = _dot((_silu(gu[:, :hid]) * gu[:, hid:]).astype(BF16), wd_ref[...])

    _drain(lambda: _row_copy(ys_ref, 0, gbuf, 0, sem).wait(), TOP_K * tb)
    w = w_ref[...]
    for k in range(TOP_K):
        moe = moe + gbuf[k * tb:(k + 1) * tb, :] * w[:, k:k + 1]
    rows, d = moe.shape
    upd = moe.reshape(rows // MOD_ROWS, MOD_ROWS, d) * g2_ref[...]
    x2 = x1_ref[...] + upd.reshape(rows, d)
    o_ref[...] = _rms(x2) * fg_ref[...]


def _combine(ys, dest_blk, w_tok, h2, x1, modg, wsh_gu, wsh_d, final_g):
    t, d = x1.shape
    nblk, n = dest_blk.shape
    tb = n // TOP_K
    ng = tb // MOD_ROWS
    const = lambda i: (0, 0)
    return pl.pallas_call(
        functools.partial(_combine_body, tb=tb),
        grid=(nblk,),
        in_specs=[pl.BlockSpec(memory_space=pl.ANY),
                  pl.BlockSpec(memory_space=pl.ANY),
                  pl.BlockSpec((tb, TOP_K), lambda i: (i, 0)),
                  pl.BlockSpec((tb, d), lambda i: (i, 0)),
                  pl.BlockSpec((tb, d), lambda i: (i, 0)),
                  pl.BlockSpec((ng, 1, d), lambda i: (i, 0, 5)),
                  pl.BlockSpec(wsh_gu.shape, const), pl.BlockSpec(wsh_d.shape, const),
                  pl.BlockSpec((1, d), const)],
        out_specs=pl.BlockSpec((tb, d), lambda i: (i, 0)),
        out_shape=jax.ShapeDtypeStruct((t, d), F32),
        scratch_shapes=[pltpu.SMEM((n,), I32), pltpu.VMEM((n, d), F32),
                        pltpu.SemaphoreType.DMA, pltpu.SemaphoreType.DMA],
        compiler_params=_params("arbitrary"),
    )(ys, dest_blk, w_tok, h2, x1, modg, wsh_gu, wsh_d, final_g.reshape(1, d))


def _rope_tables(pos):
    half = QK_ROPE // 2
    freqs = ROPE_THETA ** (-jnp.arange(half, dtype=F32) / half)
    ang = pos.astype(F32)[:, None] * freqs[None, :]
    cos, sin = jnp.cos(ang), jnp.sin(ang)
    return (jnp.concatenate([cos, cos, cos, cos], axis=1),
            jnp.concatenate([-sin, sin, -sin, sin], axis=1))


def _blocked(a, tb):
    k, t = a.shape
    return a.T.reshape(t // tb, tb * k)


def kernel(x_prompt, x_sample, c_prompt, c_sample, cache_conv, state_ssm, cache_ckv, cache_kr, w_ada, b_ada,
           norm1_g, norm2_g, w_in, conv_w, conv_b, dt_bias, a_log, d_skip, ssd_norm_g, w_ssd_out, q_norm_g,
           w_uq, kv_norm_g, w_uk, w_uv, w_mla_out, w_merge_out, w_router, e_bias, w_exp_gate, w_exp_up,
           w_exp_down, w_sh_gate, w_sh_up, w_sh_down, final_norm_g):
    depth = w_in.shape[0]
    assert depth == 1
    bp, seq, d = x_prompt.shape
    bs, lseq, _ = x_sample.shape
    assert lseq == MOD_ROWS and seq % MOD_ROWS == 0
    tp, ts = bp * seq, bs * lseq
    t_all = tp + ts
    nheads = dt_bias.shape[1]
    inner = nheads * SSD_HEAD_DIM
    cdim = conv_w.shape[2]
    gn = SSD_GROUPS * SSD_STATE
    qlora, kvlora = q_norm_g.shape[1], kv_norm_g.shape[1]
    ne = w_router.shape[2]
    assert nheads == 64 and 2 * nheads == LANES

    w = w_in[0]
    o_xbc, o_dt = inner, inner + cdim
    o_cq = o_dt + nheads
    o_ckv = o_cq + qlora
    o_kr = o_ckv + kvlora
    o_gate = o_kr + QK_ROPE
    perm = np.concatenate([np.arange(0, nheads, 2), np.arange(1, nheads, 2)])
    zc = lambda n: jnp.zeros((d, n), w.dtype)
    cols = [w[:, :inner], w[:, o_gate:], w[:, o_xbc:o_dt], w[:, o_cq:o_ckv], w[:, o_ckv:o_kr],
            w[:, o_dt:o_cq][:, perm], zc(LANES - nheads), w[:, o_kr:o_gate], zc(LANES - QK_ROPE)]
    used = inner + 2 * d + cdim + qlora + kvlora + 2 * LANES
    total = -(-used // 512) * 512
    cols.append(zc(total - used))
    w_r = jnp.concatenate(cols, axis=1).astype(BF16)
    offs = {"z": 0, "ga": inner, "gb": inner + d, "x": inner + 2 * d, "bc": 2 * inner + 2 * d}
    offs["cq"] = offs["bc"] + 2 * gn
    offs["ckv"] = offs["cq"] + qlora
    offs["dt"] = offs["ckv"] + kvlora
    offs["kr"] = offs["dt"] + LANES

    hw = QK_NOPE + LANES
    wq_r = jnp.pad(w_uq[0], ((0, 0), (0, 0), (0, hw - QK_NOPE - QK_ROPE))).reshape(qlora, MLA_HEADS * hw).astype(BF16)
    wk = w_uk[0].reshape(kvlora, MLA_HEADS * QK_NOPE).astype(BF16)
    wv = w_uv[0].reshape(kvlora, MLA_HEADS * V_HEAD).astype(BF16)
    pad_l = lambda v: jnp.pad(v[perm], (0, LANES - nheads)).reshape(1, LANES)
    dtb = pad_l(dt_bias[0])
    a_neg = pad_l(-jnp.exp(a_log[0]))
    dskip = jnp.repeat(d_skip[0], SSD_HEAD_DIM).reshape(1, inner)

    c_all = jnp.concatenate([c_prompt, c_sample], axis=0)
    mod = _ada(c_all, w_ada[0], b_ada[0])
    grp = np.concatenate([np.repeat(np.arange(bp), seq // MOD_ROWS), bp + np.arange(bs)])
    modg = mod[grp].reshape(t_all // MOD_ROWS, 1, 6 * d)

    x_all = jnp.concatenate([x_prompt.reshape(tp, d), x_sample.reshape(ts, d)], axis=0)
    proj = _inproj(x_all, modg, norm1_g[0], w_r, d)

    ssd_args = (conv_w[0], conv_b[0].reshape(1, cdim), dtb, a_neg, dskip, ssd_norm_g[0].reshape(1, inner))
    ys_p, conv_p, ssm_p = _ssd(proj, jnp.zeros((bp, CONV_W - 1, cdim), F32),
                               jnp.zeros((bp, inner, SSD_STATE), F32), *ssd_args,
                               nb=bp, seq=seq, row0=0, offs=offs, inner=inner)
    ys_s, conv_s, ssm_s = _ssd(proj, cache_conv[0], state_ssm[0].reshape(bs, inner, SSD_STATE),
                               *ssd_args, nb=bs, seq=lseq, row0=tp, offs=offs, inner=inner)

    past = cache_ckv.shape[2]
    cos_p, sin_p = _rope_tables(jnp.arange(seq))
    cos_s, sin_s = _rope_tables(past + jnp.arange(lseq))
    cos_t = jnp.concatenate([jnp.tile(cos_p, (bp, 1)), jnp.tile(cos_s, (bs, 1))], axis=0)
    sin_t = jnp.concatenate([jnp.tile(sin_p, (bp, 1)), jnp.tile(sin_s, (bs, 1))], axis=0)
    prep = functools.partial(_mla_prep, proj, cos_t, sin_t, q_norm_g[0], kv_norm_g[0], wq_r, wk, wv, offs=offs)
    ckv_p, kr_p, q_p, k_p, v_p = prep(row0=0, nrows=tp, absorbed=False)
    ckv_s, kr_s, qlat, qrope = prep(row0=tp, nrows=ts, absorbed=True)
    ym_p = _attn_prompt(q_p, k_p, v_p, nb=bp, seq=seq)
    o_lat = _attn_sample(qlat, qrope, cache_ckv[0], cache_kr[0], ckv_s, kr_s, nb=bs, seq=lseq)
    ym_s = _uv_sample(o_lat, wv)

    merged = _merge1(ys_p, ym_p, ys_s, ym_s, w_ssd_out[0].astype(BF16), w_mla_out[0].astype(BF16), proj, offs)
    x1, h2, logits_t = _merge2(merged, w_merge_out[0].astype(BF16), x_all, modg, norm2_g[0], w_router[0].T)

    eidx, pos, w_sel, cnt = _route(logits_t, e_bias[0])
    bm = MOE_ROWS
    counts = cnt[:, 0].astype(I32)
    padded = (counts + bm - 1) // bm * bm
    pad_end = jnp.cumsum(padded)
    pad_start = pad_end - padded
    nblocks = -(-(t_all * TOP_K) // bm) + ne
    onehot = eidx[:, :, None] == jnp.arange(ne, dtype=I32)
    dest = jnp.sum(jnp.where(onehot, pad_start, 0), axis=-1) + pos
    blk_start = jnp.arange(nblocks, dtype=I32) * bm
    block_e = jnp.minimum(jnp.sum(pad_end[None, :] <= blk_start[:, None], axis=1), ne - 1).astype(I32)
    n_used = (pad_end[-1:] // bm).astype(I32)

    tb_d = _pick(t_all, (256, 128))
    hs = _dispatch(h2, _blocked(dest, tb_d), (pad_start + counts).astype(I32), pad_end.astype(I32), nblocks * bm)
    ys = _experts(hs, block_e, n_used, w_exp_gate[0], w_exp_up[0], w_exp_down[0])
    wsh_gu = jnp.concatenate([w_sh_gate[0], w_sh_up[0]], axis=1).astype(BF16)
    tb_c = 128
    y_all = _combine(ys, _blocked(dest, tb_c), w_sel.T, h2, x1, modg, wsh_gu, w_sh_down[0].astype(BF16),
                     final_norm_g)

    r5 = lambda a, n, l: a.reshape(1, n, l, a.shape[-1])
    return (y_all[:tp].reshape(bp, seq, d), y_all[tp:].reshape(bs, lseq, d),
            conv_p[None], ssm_p.reshape(1, bp, nheads, SSD_HEAD_DIM, SSD_STATE),
            r5(ckv_p, bp, seq), r5(kr_p[:, :QK_ROPE], bp, seq),
            conv_s[None], ssm_s.reshape(1, bs, nheads, SSD_HEAD_DIM, SSD_STATE),
            r5(ckv_s, bs, lseq), r5(kr_s[:, :QK_ROPE], bs, lseq))
```

```python
import functools
import math

import numpy as np
import jax
import jax.numpy as jnp
from jax import lax
from jax.experimental import pallas as pl
from jax.experimental.pallas import tpu as pltpu

F32 = jnp.float32
BF16 = jnp.bfloat16
I32 = jnp.int32
U32 = jnp.uint32

EPS = 1e-6
CHUNK = 64
SSD_HEAD_DIM = 64
SSD_GROUPS = 8
SSD_STATE = 128
CONV_W = 4
MLA_HEADS = 16
QK_NOPE = 128
QK_ROPE = 64
V_HEAD = 128
ROPE_THETA = 10000.0
TOP_K = 8
N_GROUPS = 8
TOPK_GROUPS = 4
ROUTED_SCALE = 2.5

LANES = 128
MOD_ROWS = 32
MOE_ROWS = 256
VMEM_LIMIT = 56 * 1024 * 1024


def _params(*sem):
    return pltpu.CompilerParams(dimension_semantics=sem, vmem_limit_bytes=VMEM_LIMIT)


def _pick(n, cands):
    for c in cands:
        if n % c == 0:
            return c
    raise ValueError(f"no tile in {cands} divides {n}")


def _sigmoid(x):
    return 1.0 / (1.0 + jnp.exp(-x))


def _silu(x):
    return x * _sigmoid(x)


def _dot(a, b):
    return jnp.dot(a, b, preferred_element_type=F32)


def _dot_t(a, b):
    return lax.dot_general(a, b, (((1,), (1,)), ((), ())), preferred_element_type=F32)


def _split(x, n):
    parts = []
    for _ in range(n - 1):
        p = x.astype(BF16)
        parts.append(p)
        x = x - p.astype(F32)
    parts.append(x.astype(BF16))
    return parts


def _dot_exact_rhs(a, b_bf16, n=3):
    acc = None
    for p in _split(a, n):
        t = _dot(p, b_bf16)
        acc = t if acc is None else acc + t
    return acc


def _dot3(a, b, dot=_dot):
    ah, al = _split(a, 2)
    bh, bl = _split(b, 2)
    return dot(ah, bh) + (dot(ah, bl) + dot(al, bh))


def _pack_halves(x):
    h = x.shape[1] // 2
    bits = lambda v: lax.bitcast_convert_type(v.astype(BF16).astype(F32), U32)
    return (bits(x[:, :h]) >> 16) | bits(x[:, h:])


def _unpack_halves(p):
    lo = lax.bitcast_convert_type(p << 16, F32)
    hi = lax.bitcast_convert_type(p & jnp.uint32(0xFFFF0000), F32)
    return lo, hi


def _rms(x):
    return x * lax.rsqrt(jnp.mean(x * x, axis=-1, keepdims=True) + EPS)


def _modulate(y, sc_ref, sh_ref):
    rows, d = y.shape
    y3 = y.reshape(rows // MOD_ROWS, MOD_ROWS, d)
    return (y3 * (1.0 + sc_ref[...]) + sh_ref[...]).reshape(rows, d)


def _ada_body(c_ref, w_ref, b_ref, o_ref):
    o_ref[...] = _dot3(_silu(c_ref[...]), w_ref[...]) + b_ref[...]


def _ada(c_all, w_ada, b_ada):
    nb, d = c_all.shape
    n = w_ada.shape[1]
    tn = _pick(n, (1024, 512, 256, 128))
    return pl.pallas_call(
        _ada_body,
        grid=(n // tn,),
        in_specs=[pl.BlockSpec((nb, d), lambda j: (0, 0)),
                  pl.BlockSpec((d, tn), lambda j: (0, j)),
                  pl.BlockSpec((1, tn), lambda j: (0, j))],
        out_specs=pl.BlockSpec((nb, tn), lambda j: (0, j)),
        out_shape=jax.ShapeDtypeStruct((nb, n), F32),
        compiler_params=_params("arbitrary"),
    )(c_all, w_ada, b_ada.reshape(1, n))


def _inproj_body(x_ref, sc_ref, sh_ref, g_ref, w_ref, o_ref, h_scr):
    @pl.when(pl.program_id(1) == 0)
    def _():
        h = _modulate(_rms(x_ref[...]) * g_ref[...], sc_ref, sh_ref)
        h_scr[...] = h.astype(BF16)

    o_ref[...] = _dot(h_scr[...], w_ref[...])


def _inproj(x_all, modg, norm_g, w_r, d):
    t = x_all.shape[0]
    n = w_r.shape[1]
    tm = _pick(t, (1024, 512, 256, 128))
    tn = 512
    ng = tm // MOD_ROWS
    return pl.pallas_call(
        _inproj_body,
        grid=(t // tm, n // tn),
        in_specs=[pl.BlockSpec((tm, d), lambda i, j: (i, 0)),
                  pl.BlockSpec((ng, 1, d), lambda i, j: (i, 0, 1)),
                  pl.BlockSpec((ng, 1, d), lambda i, j: (i, 0, 0)),
                  pl.BlockSpec((1, d), lambda i, j: (0, 0)),
                  pl.BlockSpec((d, tn), lambda i, j: (0, j))],
        out_specs=pl.BlockSpec((tm, tn), lambda i, j: (i, j)),
        out_shape=jax.ShapeDtypeStruct((t, n), F32),
        scratch_shapes=[pltpu.VMEM((tm, d), BF16)],
        compiler_params=_params("arbitrary", "arbitrary"),
    )(x_all, modg, modg, norm_g.reshape(1, d), w_r)


def _ssd_body(z_ref, x_ref, bc_ref, dt_ref, conv0_ref, h0_ref, cw_ref, cb_ref, dtb_ref, a_ref,
              dskip_ref, ng_ref, y_ref, nconv_ref, nssm_ref,
              carry_x, carry_bc, xp_x, xp_bc, state, y_scr, *, lb, inner):
    q = CHUNK
    c = pl.program_id(1)
    nc = pl.num_programs(1)
    gn = SSD_GROUPS * SSD_STATE

    @pl.when(c == 0)
    def _():
        carry_x[...] = jnp.zeros_like(carry_x)
        carry_bc[...] = jnp.zeros_like(carry_bc)
        carry_x[8 - (CONV_W - 1):8, :] = conv0_ref[0, :, :inner]
        carry_bc[8 - (CONV_W - 1):8, :] = conv0_ref[0, :, inner:]
        state[...] = h0_ref[0]

    def pad_rows(v):
        if lb == q:
            return v
        return jnp.concatenate([v, jnp.zeros((q - lb, v.shape[1]), v.dtype)], axis=0)

    def conv(raw, carry, xp, w_lo, w_hi):
        xp[0:8, :] = carry[...]
        xp[8:8 + q, :] = raw
        acc = cb_ref[:, w_lo:w_hi]
        for j in range(CONV_W):
            acc = acc + xp[8 - j:8 - j + q, :] * cw_ref[CONV_W - 1 - j:CONV_W - j, w_lo:w_hi]
        return _silu(acc)

    x_raw = pad_rows(x_ref[...])
    bc_raw = pad_rows(bc_ref[...])
    xc = conv(x_raw, carry_x, xp_x, 0, inner)
    bcc = conv(bc_raw, carry_bc, xp_bc, inner, inner + 2 * gn)
    if lb == q:
        carry_x[...] = x_raw[q - 8:q, :]
        carry_bc[...] = bc_raw[q - 8:q, :]

    @pl.when(c == nc - 1)
    def _():
        nconv_ref[0, :, :inner] = x_raw[lb - (CONV_W - 1):lb, :]
        nconv_ref[0, :, inner:] = bc_raw[lb - (CONV_W - 1):lb, :]

    dtr = dt_ref[...] + dtb_ref[...]
    dtv = jnp.maximum(dtr, 0.0) + jnp.log1p(jnp.exp(-jnp.abs(dtr)))
    dtv = pad_rows(dtv)
    la = dtv * a_ref[...]

    row = lax.broadcasted_iota(I32, (q, q), 0)
    col = lax.broadcasted_iota(I32, (q, q), 1)
    tri = jnp.where(row >= col, 1.0, 0.0).astype(BF16)
    parts = _split(la, 3)
    a_cum = _dot(tri, parts[0]) + _dot(tri, parts[1]) + _dot(tri, parts[2])
    a_last = a_cum[q - 1:q, :]

    r2 = lax.broadcasted_iota(I32, (q, 2 * q), 0)
    c2 = lax.broadcasted_iota(I32, (q, 2 * q), 1)
    u_e = jnp.where((c2 < q) & (r2 <= c2), 1.0, 0.0).astype(BF16)
    u_o = jnp.where((c2 >= q) & (r2 <= c2 - q), 1.0, 0.0).astype(BF16)
    i_e = jnp.where(r2 == c2, 1.0, 0.0).astype(BF16)
    i_o = jnp.where(r2 == c2 - q, 1.0, 0.0).astype(BF16)
    la_t = la.T
    dt_t = dtv.T
    hp = la_t.shape[0] // 4
    acol = _dot_exact_rhs(la_t[0:hp], u_e) + _dot_exact_rhs(la_t[hp:2 * hp], u_o)
    dtrow = _dot_exact_rhs(dt_t[0:hp], i_e) + _dot_exact_rhs(dt_t[hp:2 * hp], i_o)
    w_t = (dtv * jnp.exp(a_last - a_cum)).T
    dec_b = jnp.broadcast_to(jnp.exp(jnp.sum(la_t, axis=1, keepdims=True)), (la_t.shape[0], SSD_STATE))

    lane = lax.broadcasted_iota(I32, (q, 2 * q), 1)
    causal2 = lax.broadcasted_iota(I32, (q, 2 * q), 0) >= jnp.where(lane < q, lane, lane - q)
    first_half = lane < q
    rr = lax.broadcasted_iota(I32, (2 * SSD_HEAD_DIM, 2 * SSD_HEAD_DIM), 0)
    cc = lax.broadcasted_iota(I32, (2 * SSD_HEAD_DIM, 2 * SSD_HEAD_DIM), 1)
    bd_mask = (rr < SSD_HEAD_DIM) == (cc < SSD_HEAD_DIM)
    top_rows_q = lax.broadcasted_iota(I32, (2 * SSD_HEAD_DIM, q), 0) < SSD_HEAD_DIM
    top_rows_n = lax.broadcasted_iota(I32, (2 * SSD_HEAD_DIM, SSD_STATE), 0) < SSD_HEAD_DIM

    pairs_per_group = (inner // SSD_HEAD_DIM) // SSD_GROUPS // 2
    for g in range(SSD_GROUPS):
        b_g = bcc[:, g * SSD_STATE:(g + 1) * SSD_STATE].astype(BF16)
        c_g = bcc[:, gn + g * SSD_STATE:gn + (g + 1) * SSD_STATE].astype(BF16)
        cb2 = _dot_t(c_g, jnp.concatenate([b_g, b_g], axis=0))
        for kk in range(pairs_per_group):
            k = g * pairs_per_group + kk
            lo, hi = k * 2 * SSD_HEAD_DIM, (k + 1) * 2 * SSD_HEAD_DIM
            xp = xc[:, lo:hi]
            arow = jnp.where(first_half, a_cum[:, k:k + 1], a_cum[:, hp + k:hp + k + 1])
            seg = jnp.where(causal2, arow - acol[k:k + 1, :], -jnp.inf)
            m = (jnp.exp(seg) * cb2 * dtrow[k:k + 1, :]).astype(BF16)
            xbd = jnp.where(bd_mask, jnp.concatenate([xp, xp], axis=0), 0.0).astype(BF16)
            y_diag = _dot(m, xbd)
            s_k = state[lo:hi, :]
            y_off = _dot_t(c_g, s_k.astype(BF16)) * jnp.exp(arow)
            y_scr[:, lo:hi] = y_diag + y_off + xp * dskip_ref[:, lo:hi]
            w2 = jnp.where(top_rows_q, w_t[k:k + 1, :], w_t[hp + k:hp + k + 1, :])
            contrib = _dot((xp.T * w2).astype(BF16), b_g)
            d_k = jnp.where(top_rows_n, dec_b[k:k + 1, :], dec_b[hp + k:hp + k + 1, :])
            state[lo:hi, :] = s_k * d_k + contrib

    @pl.when(c == nc - 1)
    def _():
        nssm_ref[0] = state[...]

    y = y_scr[0:lb, :] * _silu(z_ref[...])
    y_ref[...] = (_rms(y) * ng_ref[...]).astype(y_ref.dtype)


def _ssd(proj, conv0, h0, conv_w, conv_b, dtb, a_neg, dskip, norm_g, *, nb, seq, row0, offs, inner):
    q = CHUNK
    lb = min(seq, q)
    assert seq % lb == 0 and row0 % lb == 0 and lb % 16 == 0
    nc = seq // lb
    rb0 = row0 // lb
    cdim = conv_w.shape[1]
    hp_rows = inner
    kern = functools.partial(_ssd_body, lb=lb, inner=inner)
    rows = lambda b, c: rb0 + b * nc + c
    in_specs = [
        pl.BlockSpec((lb, inner), lambda b, c: (rows(b, c), offs["z"] // inner)),
        pl.BlockSpec((lb, inner), lambda b, c: (rows(b, c), offs["x"] // inner)),
        pl.BlockSpec((lb, cdim - inner), lambda b, c: (rows(b, c), offs["bc"] // (cdim - inner))),
        pl.BlockSpec((lb, LANES), lambda b, c: (rows(b, c), offs["dt"] // LANES)),
        pl.BlockSpec((1, CONV_W - 1, cdim), lambda b, c: (b, 0, 0)),
        pl.BlockSpec((1, hp_rows, SSD_STATE), lambda b, c: (b, 0, 0)),
        pl.BlockSpec((CONV_W, cdim), lambda b, c: (0, 0)),
        pl.BlockSpec((1, cdim), lambda b, c: (0, 0)),
        pl.BlockSpec((1, LANES), lambda b, c: (0, 0)),
        pl.BlockSpec((1, LANES), lambda b, c: (0, 0)),
        pl.BlockSpec((1, inner), lambda b, c: (0, 0)),
        pl.BlockSpec((1, inner), lambda b, c: (0, 0)),
    ]
    args = [proj, proj, proj, proj, conv0, h0, conv_w, conv_b, dtb, a_neg, dskip, norm_g]
    return pl.pallas_call(
        kern,
        grid=(nb, nc),
        in_specs=in_specs,
        out_specs=[pl.BlockSpec((lb, inner), lambda b, c: (b * nc + c, 0)),
                   pl.BlockSpec((1, CONV_W - 1, cdim), lambda b, c: (b, 0, 0)),
                   pl.BlockSpec((1, hp_rows, SSD_STATE), lambda b, c: (b, 0, 0))],
        out_shape=[jax.ShapeDtypeStruct((nb * seq, inner), BF16),
                   jax.ShapeDtypeStruct((nb, CONV_W - 1, cdim), F32),
                   jax.ShapeDtypeStruct((nb, hp_rows, SSD_STATE), F32)],
        scratch_shapes=[pltpu.VMEM((8, inner), F32), pltpu.VMEM((8, cdim - inner), F32),
                        pltpu.VMEM((8 + q, inner), F32), pltpu.VMEM((8 + q, cdim - inner), F32),
                        pltpu.VMEM((hp_rows, SSD_STATE), F32), pltpu.VMEM((q, inner), F32)],
        compiler_params=_params("arbitrary", "arbitrary"),
    )(*args)


def _rope128(x, cos, sin):
    lane = lax.broadcasted_iota(I32, x.shape, 1)
    half = QK_ROPE // 2
    swapped = jnp.where((lane % QK_ROPE) < half, pltpu.roll(x, LANES - half, 1), pltpu.roll(x, half, 1))
    return x * cos + swapped * sin


def _mla_prep_body(cq_ref, ckv_ref, kr_ref, cos_ref, sin_ref, qg_ref, kvg_ref, wq_ref, wk_ref, wv_ref,
                   *out_refs, absorbed, scale):
    cos, sin = cos_ref[...], sin_ref[...]
    ckv_n = _rms(ckv_ref[...]) * kvg_ref[...]
    kr_r = _rope128(kr_ref[...], cos, sin)
    qn = (_rms(cq_ref[...]) * qg_ref[...]).astype(BF16)
    hw = QK_NOPE + LANES
    if absorbed:
        ckvn_ref, krr_ref, qlat_ref, qrope_ref = out_refs
    else:
        ckvn_ref, krr_ref, q_ref, k_ref, v_ref = out_refs
        ckv_b = ckv_n.astype(BF16)
        v_ref[...] = _dot(ckv_b, wv_ref[...]).astype(BF16)
        kr_b = kr_r.astype(BF16)
    ckvn_ref[...] = ckv_n
    krr_ref[...] = kr_r
    for h in range(MLA_HEADS):
        qh = _dot(qn, wq_ref[:, h * hw:(h + 1) * hw]) * scale
        q_nope = qh[:, :QK_NOPE]
        q_rope = _rope128(qh[:, QK_NOPE:], cos, sin)
        if absorbed:
            qlat_ref[h] = _dot_t(q_nope.astype(BF16), wk_ref[:, h * QK_NOPE:(h + 1) * QK_NOPE]).astype(BF16)
            qrope_ref[h] = q_rope[:, :QK_ROPE].astype(BF16)
        else:
            q_ref[:, h * hw:h * hw + QK_NOPE] = q_nope.astype(BF16)
            q_ref[:, h * hw + QK_NOPE:(h + 1) * hw] = q_rope.astype(BF16)
            k_ref[:, h * hw:h * hw + QK_NOPE] = _dot(ckv_b, wk_ref[:, h * QK_NOPE:(h + 1) * QK_NOPE]).astype(BF16)
            k_ref[:, h * hw + QK_NOPE:(h + 1) * hw] = kr_b


def _mla_prep(proj, cos_t, sin_t, q_g, kv_g, wq_r, wk, wv, *, row0, nrows, offs, absorbed):
    qlora, kvlora = q_g.shape[0], kv_g.shape[0]
    tm = _pick(nrows, (512, 256, 128))
    assert row0 % tm == 0
    rb0 = row0 // tm
    hw = QK_NOPE + LANES
    scale = 1.0 / math.sqrt(QK_NOPE + QK_ROPE)
    const = lambda i: (0, 0)
    in_specs = [pl.BlockSpec((tm, qlora), lambda i: (rb0 + i, offs["cq"] // qlora)),
                pl.BlockSpec((tm, kvlora), lambda i: (rb0 + i, offs["ckv"] // kvlora)),
                pl.BlockSpec((tm, LANES), lambda i: (rb0 + i, offs["kr"] // LANES)),
                pl.BlockSpec((tm, LANES), lambda i: (rb0 + i, 0)),
                pl.BlockSpec((tm, LANES), lambda i: (rb0 + i, 0)),
                pl.BlockSpec((1, qlora), const), pl.BlockSpec((1, kvlora), const),
                pl.BlockSpec(wq_r.shape, const), pl.BlockSpec(wk.shape, const), pl.BlockSpec(wv.shape, const)]
    out_specs = [pl.BlockSpec((tm, kvlora), lambda i: (i, 0)), pl.BlockSpec((tm, LANES), lambda i: (i, 0))]
    out_shape = [jax.ShapeDtypeStruct((nrows, kvlora), F32), jax.ShapeDtypeStruct((nrows, LANES), F32)]
    if absorbed:
        out_specs += [pl.BlockSpec((MLA_HEADS, tm, kvlora), lambda i: (0, i, 0)),
                      pl.BlockSpec((MLA_HEADS, tm, QK_ROPE), lambda i: (0, i, 0))]
        out_shape += [jax.ShapeDtypeStruct((MLA_HEADS, nrows, kvlora), BF16),
                      jax.ShapeDtypeStruct((MLA_HEADS, nrows, QK_ROPE), BF16)]
    else:
        out_specs += [pl.BlockSpec((tm, MLA_HEADS * hw), lambda i: (i, 0)),
                      pl.BlockSpec((tm, MLA_HEADS * hw), lambda i: (i, 0)),
                      pl.BlockSpec((tm, MLA_HEADS * V_HEAD), lambda i: (i, 0))]
        out_shape += [jax.ShapeDtypeStruct((nrows, MLA_HEADS * hw), BF16),
                      jax.ShapeDtypeStruct((nrows, MLA_HEADS * hw), BF16),
                      jax.ShapeDtypeStruct((nrows, MLA_HEADS * V_HEAD), BF16)]
    return pl.pallas_call(
        functools.partial(_mla_prep_body, absorbed=absorbed, scale=scale),
        grid=(nrows // tm,),
        in_specs=in_specs, out_specs=out_specs, out_shape=out_shape,
        compiler_params=_params("arbitrary"),
    )(proj, proj, proj, cos_t, sin_t, q_g.reshape(1, -1), kv_g.reshape(1, -1), wq_r, wk, wv)


ATT_SUB = 256


def _attn_tile(q_sub, k_ref, v_ref, s_scr, p_scr, nk):
    sub = ATT_SUB
    nfull = nk - sub
    shift = CHUNK.bit_length() - 1

    def fold(x, op):
        return op(x.reshape(sub // 8, 8, sub), axis=0)

    if nfull:
        s_scr[0:nfull, :] = _dot_t(k_ref[0, 0:nfull, :], q_sub)
    krow = lax.broadcasted_iota(I32, (sub, sub), 0) >> shift
    qcol = lax.broadcasted_iota(I32, (sub, sub), 1) >> shift
    s_scr[nfull:nk, :] = jnp.where(krow <= qcol, _dot_t(k_ref[0, nfull:nk, :], q_sub), -jnp.inf)

    mrun = None
    for j in range(nk // sub):
        f = fold(s_scr[j * sub:(j + 1) * sub, :], jnp.max)
        mrun = f if mrun is None else jnp.maximum(mrun, f)
    m = jnp.max(mrun, axis=0, keepdims=True)
    lrun = None
    for j in range(nk // sub):
        pt = jnp.exp(s_scr[j * sub:(j + 1) * sub, :] - m)
        f = fold(pt, jnp.sum)
        lrun = f if lrun is None else lrun + f
        p_scr[j * sub:(j + 1) * sub, :] = pt.astype(BF16)
    l = jnp.sum(lrun, axis=0, keepdims=True)
    acc = lax.dot_general(v_ref[0, 0:nk, :], p_scr[0:nk, :], (((0,), (0,)), ((), ())),
                          preferred_element_type=F32)
    return (acc / l).T


def _attn_body(q_ref, k_ref, v_ref, o_ref, s_scr, p_scr, *, tq, nq):
    qi = pl.program_id(2)
    sub = ATT_SUB
    for c in range(nq):
        @pl.when(qi == c)
        def _():
            for r in range(tq // sub):
                q_sub = q_ref[0, r * sub:(r + 1) * sub, :]
                out = _attn_tile(q_sub, k_ref, v_ref, s_scr, p_scr, c * tq + (r + 1) * sub)
                o_ref[r * sub:(r + 1) * sub, :] = out.astype(o_ref.dtype)


def _attn_prompt(q, k, v, *, nb, seq):
    hw = QK_NOPE + LANES
    tq = min(512, seq // 2)
    assert seq % tq == 0 and tq % ATT_SUB == 0 and ATT_SUB % CHUNK == 0
    nq = seq // tq
    q3 = q.reshape(nb, seq, MLA_HEADS * hw)
    k3 = k.reshape(nb, seq, MLA_HEADS * hw)
    v3 = v.reshape(nb, seq, MLA_HEADS * V_HEAD)
    return pl.pallas_call(
        functools.partial(_attn_body, tq=tq, nq=nq),
        grid=(nb, MLA_HEADS, nq),
        in_specs=[pl.BlockSpec((1, tq, hw), lambda b, h, i: (b, i, h)),
                  pl.BlockSpec((1, seq, hw), lambda b, h, i: (b, 0, h)),
                  pl.BlockSpec((1, seq, V_HEAD), lambda b, h, i: (b, 0, h))],
        out_specs=pl.BlockSpec((tq, V_HEAD), lambda b, h, i: (b * nq + i, h)),
        out_shape=jax.ShapeDtypeStruct((nb * seq, MLA_HEADS * V_HEAD), BF16),
        scratch_shapes=[pltpu.VMEM((seq, ATT_SUB), F32), pltpu.VMEM((seq, ATT_SUB), BF16)],
        compiler_params=_params("arbitrary", "arbitrary", "arbitrary"),
    )(q3, k3, v3)


def _attn_sample_body(ql_ref, qr_ref, pckv_ref, pkr_ref, nckv_ref, nkr_ref, o_ref, *, past, seq):
    nh = ql_ref.shape[0]
    ql = ql_ref[...].reshape(nh * seq, ql_ref.shape[2])
    qr = qr_ref[...].reshape(nh * seq, qr_ref.shape[2])
    pckv = pckv_ref[0].astype(BF16)
    nckv = nckv_ref[...].astype(BF16)
    s_p = _dot_t(ql, pckv) + _dot_t(qr, pkr_ref[0].astype(BF16))
    s_n = _dot_t(ql, nckv) + _dot_t(qr, nkr_ref[:, :QK_ROPE].astype(BF16))
    shift = CHUNK.bit_length() - 1
    q_chunk = (past + lax.broadcasted_iota(I32, (nh * seq, 1), 0) % seq) >> shift
    kp_chunk = lax.broadcasted_iota(I32, (1, past), 1) >> shift
    kn_chunk = (past + lax.broadcasted_iota(I32, (1, seq), 1)) >> shift
    s_p = jnp.where(kp_chunk <= q_chunk, s_p, -jnp.inf)
    s_n = jnp.where(kn_chunk <= q_chunk, s_n, -jnp.inf)
    m = jnp.maximum(jnp.max(s_p, axis=1, keepdims=True), jnp.max(s_n, axis=1, keepdims=True))
    p_p = jnp.exp(s_p - m)
    p_n = jnp.exp(s_n - m)
    l = jnp.sum(p_p, axis=1, keepdims=True) + jnp.sum(p_n, axis=1, keepdims=True)
    o = (_dot(p_p.astype(BF16), pckv) + _dot(p_n.astype(BF16), nckv)) / l
    o_ref[...] = o.reshape(o_ref.shape).astype(o_ref.dtype)


def _attn_sample(qlat, qrope, past_ckv, past_kr, ckv_n, kr_r, *, nb, seq):
    past = past_ckv.shape[1]
    r = past_ckv.shape[2]
    return pl.pallas_call(
        functools.partial(_attn_sample_body, past=past, seq=seq),
        grid=(nb,),
        in_specs=[pl.BlockSpec((MLA_HEADS, seq, r), lambda b: (0, b, 0)),
                  pl.BlockSpec((MLA_HEADS, seq, QK_ROPE), lambda b: (0, b, 0)),
                  pl.BlockSpec((1, past, r), lambda b: (b, 0, 0)),
                  pl.BlockSpec((1, past, QK_ROPE), lambda b: (b, 0, 0)),
                  pl.BlockSpec((seq, r), lambda b: (b, 0)),
                  pl.BlockSpec((seq, LANES), lambda b: (b, 0))],
        out_specs=pl.BlockSpec((MLA_HEADS, seq, r), lambda b: (0, b, 0)),
        out_shape=jax.ShapeDtypeStruct((MLA_HEADS, nb * seq, r), BF16),
        compiler_params=_params("arbitrary"),
    )(qlat, qrope, past_ckv, past_kr, ckv_n, kr_r)


def _uv_body(o_ref, w_ref, y_ref):
    y_ref[...] = _dot(o_ref[0], w_ref[...]).astype(y_ref.dtype)


def _uv_sample(o_lat, wv):
    nh, nrows, r = o_lat.shape
    return pl.pallas_call(
        _uv_body,
        grid=(nh,),
        in_specs=[pl.BlockSpec((1, nrows, r), lambda h: (h, 0, 0)),
                  pl.BlockSpec((r, V_HEAD), lambda h: (0, h))],
        out_specs=pl.BlockSpec((nrows, V_HEAD), lambda h: (0, h)),
        out_shape=jax.ShapeDtypeStruct((nrows, nh * V_HEAD), BF16),
        compiler_params=_params("arbitrary"),
    )(o_lat, wv)


def _merge1_body(ysp_ref, ymp_ref, yss_ref, yms_ref, ws_ref, wm_ref, ga_ref, gb_ref, o_ref, *, npb):
    def run(ys_ref, ym_ref):
        a = _dot(ys_ref[...], ws_ref[...])
        b = _dot(ym_ref[...], wm_ref[...])
        o_ref[...] = (_sigmoid(ga_ref[...]) * a + _sigmoid(gb_ref[...]) * b).astype(o_ref.dtype)

    i = pl.program_id(0)
    pl.when(i < npb)(lambda: run(ysp_ref, ymp_ref))
    pl.when(i >= npb)(lambda: run(yss_ref, yms_ref))


def _merge1(ys_p, ym_p, ys_s, ym_s, w_ssd_out, w_mla_out, proj, offs):
    (tp, inner), ts = ys_p.shape, ys_s.shape[0]
    dm = ym_p.shape[1]
    d = w_ssd_out.shape[1]
    tm = _pick(math.gcd(tp, ts), (512, 256, 128))
    tn = 512
    npb, nsb = tp // tm, ts // tm
    prow = lambda i, j: (jnp.minimum(i, npb - 1), 0)
    srow = lambda i, j: (jnp.maximum(i - npb, 0), 0)
    return pl.pallas_call(
        functools.partial(_merge1_body, npb=npb),
        grid=(npb + nsb, d // tn),
        in_specs=[pl.BlockSpec((tm, inner), prow), pl.BlockSpec((tm, dm), prow),
                  pl.BlockSpec((tm, inner), srow), pl.BlockSpec((tm, dm), srow),
                  pl.BlockSpec((inner, tn), lambda i, j: (0, j)),
                  pl.BlockSpec((dm, tn), lambda i, j: (0, j)),
                  pl.BlockSpec((tm, tn), lambda i, j: (i, offs["ga"] // tn + j)),
                  pl.BlockSpec((tm, tn), lambda i, j: (i, offs["gb"] // tn + j))],
        out_specs=pl.BlockSpec((tm, tn), lambda i, j: (i, j)),
        out_shape=jax.ShapeDtypeStruct((tp + ts, d), BF16),
        compiler_params=_params("arbitrary", "arbitrary"),
    )(ys_p, ym_p, ys_s, ym_s, w_ssd_out, w_mla_out, proj, proj)


def _merge2_body(m_ref, w_ref, x_ref, g1_ref, sc_ref, sh_ref, ng_ref, wr_ref, x1_ref, h2_ref, lg_ref):
    rows, d = x_ref.shape
    upd = _dot(m_ref[...], w_ref[...]).reshape(rows // MOD_ROWS, MOD_ROWS, d) * g1_ref[...]
    x1 = x_ref[...] + upd.reshape(rows, d)
    x1_ref[...] = x1
    h2 = _modulate(_rms(x1) * ng_ref[...], sc_ref, sh_ref)
    h2_ref[...] = _pack_halves(h2)
    lg_ref[...] = _dot3(wr_ref[...], h2, dot=_dot_t)


def _merge2(merged, w_merge, x_all, modg, norm_g, w_router_t):
    t, d = x_all.shape
    ne = w_router_t.shape[0]
    tm = _pick(t, (512, 256, 128))
    ng = tm // MOD_ROWS
    return pl.pallas_call(
        _merge2_body,
        grid=(t // tm,),
        in_specs=[pl.BlockSpec((tm, d), lambda i: (i, 0)),
                  pl.BlockSpec((d, d), lambda i: (0, 0)),
                  pl.BlockSpec((tm, d), lambda i: (i, 0)),
                  pl.BlockSpec((ng, 1, d), lambda i: (i, 0, 2)),
                  pl.BlockSpec((ng, 1, d), lambda i: (i, 0, 4)),
                  pl.BlockSpec((ng, 1, d), lambda i: (i, 0, 3)),
                  pl.BlockSpec((1, d), lambda i: (0, 0)),
                  pl.BlockSpec((ne, d), lambda i: (0, 0))],
        out_specs=[pl.BlockSpec((tm, d), lambda i: (i, 0)),
                   pl.BlockSpec((tm, d // 2), lambda i: (i, 0)),
                   pl.BlockSpec((ne, tm), lambda i: (0, i))],
        out_shape=[jax.ShapeDtypeStruct((t, d), F32), jax.ShapeDtypeStruct((t, d // 2), U32),
                   jax.ShapeDtypeStruct((ne, t), F32)],
        compiler_params=_params("arbitrary"),
    )(merged, w_merge, x_all, modg, modg, modg, norm_g.reshape(1, d), w_router_t)


def _route_body(lg_ref, eb_ref, eidx_ref, pos_ref, w_ref, cnt_ref, carry):
    ne, tr = lg_ref.shape
    per_group = ne // N_GROUPS

    @pl.when(pl.program_id(0) == 0)
    def _():
        carry[...] = jnp.zeros_like(carry)

    scores = _sigmoid(lg_ref[...])
    choice = scores + eb_ref[...]
    sub = lax.broadcasted_iota(I32, (per_group, tr), 0)
    gscore, blocks = [], []
    for g in range(N_GROUPS):
        blk = choice[g * per_group:(g + 1) * per_group, :]
        m1 = jnp.max(blk, axis=0, keepdims=True)
        first = jnp.min(jnp.where(blk == m1, sub, per_group), axis=0, keepdims=True)
        m2 = jnp.max(jnp.where(sub == first, -jnp.inf, blk), axis=0, keepdims=True)
        gscore.append(m1 + m2)
        blocks.append(blk)
    masked = []
    for g in range(N_GROUPS):
        rank = jnp.zeros((1, tr), I32)
        for g2 in range(N_GROUPS):
            if g2 == g:
                continue
            beats = (gscore[g2] > gscore[g]) | ((gscore[g2] == gscore[g]) & (g2 < g))
            rank = rank + beats.astype(I32)
        masked.append(jnp.where(rank < TOPK_GROUPS, blocks[g], -jnp.inf))
    cm = jnp.concatenate(masked, axis=0)

    eid = lax.broadcasted_iota(I32, (ne, tr), 0)
    rank = jnp.zeros((ne, tr), I32)
    for e2 in range(ne):
        rowv = cm[e2:e2 + 1, :]
        beats = (rowv > cm) | ((rowv == cm) & (eid > e2))
        rank = rank + beats.astype(I32)
    sel = rank < TOP_K
    wsel = jnp.where(sel, scores, 0.0)
    wfull = wsel / jnp.sum(wsel, axis=0, keepdims=True) * ROUTED_SCALE

    r = lax.broadcasted_iota(I32, (tr, tr), 0)
    c = lax.broadcasted_iota(I32, (tr, tr), 1)
    before = jnp.where(r < c, 1.0, 0.0).astype(BF16)
    self = jnp.where(sel, 1.0, 0.0)
    pos = carry[:, 0:1] + _dot(self.astype(BF16), before)
    carry[...] = carry[...] + jnp.sum(self, axis=1, keepdims=True)
    cnt_ref[...] = carry[...]

    eid_f = eid.astype(F32)
    for k in range(TOP_K):
        pick = sel & (rank == k)
        eidx_ref[k:k + 1, :] = jnp.sum(jnp.where(pick, eid_f, 0.0), axis=0, keepdims=True).astype(I32)
        pos_ref[k:k + 1, :] = jnp.sum(jnp.where(pick, pos, 0.0), axis=0, keepdims=True).astype(I32)
        w_ref[k:k + 1, :] = jnp.sum(jnp.where(pick, wfull, 0.0), axis=0, keepdims=True)


def _route(logits_t, e_bias):
    ne, t = logits_t.shape
    tr = _pick(t, (512, 256, 128))
    return pl.pallas_call(
        _route_body,
        grid=(t // tr,),
        in_specs=[pl.BlockSpec((ne, tr), lambda i: (0, i)), pl.BlockSpec((ne, 1), lambda i: (0, 0))],
        out_specs=[pl.BlockSpec((TOP_K, tr), lambda i: (0, i)), pl.BlockSpec((TOP_K, tr), lambda i: (0, i)),
                   pl.BlockSpec((TOP_K, tr), lambda i: (0, i)), pl.BlockSpec((ne, LANES), lambda i: (0, 0))],
        out_shape=[jax.ShapeDtypeStruct((TOP_K, t), I32), jax.ShapeDtypeStruct((TOP_K, t), I32),
                   jax.ShapeDtypeStruct((TOP_K, t), F32), jax.ShapeDtypeStruct((ne, LANES), F32)],
        scratch_shapes=[pltpu.VMEM((ne, LANES), F32)],
        compiler_params=_params("arbitrary"),
    )(logits_t, e_bias.reshape(ne, 1))


def _row_copy(src, s, dst, d, sem):
    return pltpu.make_async_copy(src.at[pl.ds(s, 1), :], dst.at[pl.ds(d, 1), :], sem)


def _drain(wait_one, n, group=64):
    assert n % group == 0

    def body(j, c):
        for _ in range(group):
            wait_one()
        return c

    lax.fori_loop(0, n // group, body, 0)


def _dispatch_body(fill_lo_ref, fill_hi_ref, h_ref, dest_ref, hs_ref, dest_s, zrow, sem, dsem, *, tb):
    i = pl.program_id(0)
    cp = pltpu.make_async_copy(dest_ref.at[i], dest_s, dsem)
    cp.start()

    @pl.when(i == 0)
    def _():
        zrow[...] = jnp.zeros_like(zrow)

        def per_expert(fn):
            def body(e, _):
                lax.fori_loop(fill_lo_ref[e], fill_hi_ref[e], lambda s, c: (fn(s), c)[1], 0)
                return 0
            lax.fori_loop(0, fill_lo_ref.shape[0], body, 0)

        per_expert(lambda s: _row_copy(zrow, 0, hs_ref, s, sem).start())
        per_expert(lambda s: _row_copy(zrow, 0, hs_ref, s, sem).wait())

    cp.wait()

    def issue(t, c):
        for k in range(TOP_K):
            _row_copy(h_ref, t, hs_ref, dest_s[t * TOP_K + k], sem).start(priority=k % 2)
        return c

    lax.fori_loop(0, tb, issue, 0, unroll=4)
    _drain(lambda: _row_copy(h_ref, 0, hs_ref, 0, sem).wait(), TOP_K * tb)


def _dispatch(h2, dest_blk, fill_lo, fill_hi, n_slots):
    t, d = h2.shape
    nblk, n = dest_blk.shape
    tb = n // TOP_K
    return pl.pallas_call(
        functools.partial(_dispatch_body, tb=tb),
        grid_spec=pltpu.PrefetchScalarGridSpec(
            num_scalar_prefetch=2,
            grid=(nblk,),
            in_specs=[pl.BlockSpec((tb, d), lambda i, lo, hi: (i, 0)),
                      pl.BlockSpec(memory_space=pl.ANY)],
            out_specs=pl.BlockSpec(memory_space=pl.ANY),
            scratch_shapes=[pltpu.SMEM((n,), I32), pltpu.VMEM((8, d), h2.dtype),
                            pltpu.SemaphoreType.DMA, pltpu.SemaphoreType.DMA]),
        out_shape=jax.ShapeDtypeStruct((n_slots, d), h2.dtype),
        compiler_params=_params("arbitrary"),
    )(fill_lo, fill_hi, h2, dest_blk)


def _experts_body(be_ref, nu_ref, x_ref, wg_ref, wu_ref, wd_ref, o_ref, wgu_s, wd_s):
    i = pl.program_id(0)
    hid = wg_ref.shape[2]

    @pl.when(i < nu_ref[0])
    def _():
        @pl.when((i == 0) | (be_ref[i] != be_ref[jnp.maximum(i - 1, 0)]))
        def _():
            wgu_s[:, :hid] = wg_ref[0].astype(BF16)
            wgu_s[:, hid:] = wu_ref[0].astype(BF16)
            wd_s[...] = wd_ref[0].astype(BF16)

        lo, hi = _unpack_halves(x_ref[...])
        x = jnp.concatenate([lo.astype(BF16), hi.astype(BF16)], axis=1)
        gu = _dot(x, wgu_s[...])
        act = (_silu(gu[:, :hid]) * gu[:, hid:]).astype(BF16)
        o_ref[...] = _pack_halves(_dot(act, wd_s[...]))


def _experts(hs, block_e, n_used, wg, wu, wd):
    n_slots, dh = hs.shape
    ne, d, hid = wg.shape
    assert d == 2 * dh
    bm = MOE_ROWS
    nblocks = n_slots // bm
    blk = lambda i, be, nu: (jnp.minimum(i, nu[0] - 1), 0)
    return pl.pallas_call(
        _experts_body,
        grid_spec=pltpu.PrefetchScalarGridSpec(
            num_scalar_prefetch=2,
            grid=(nblocks,),
            in_specs=[pl.BlockSpec((bm, dh), blk),
                      pl.BlockSpec((1, d, hid), lambda i, be, nu: (be[i], 0, 0)),
                      pl.BlockSpec((1, d, hid), lambda i, be, nu: (be[i], 0, 0)),
                      pl.BlockSpec((1, hid, d), lambda i, be, nu: (be[i], 0, 0))],
            out_specs=pl.BlockSpec((bm, dh), blk),
            scratch_shapes=[pltpu.VMEM((d, 2 * hid), BF16), pltpu.VMEM((hid, d), BF16)]),
        out_shape=jax.ShapeDtypeStruct((n_slots, dh), U32),
        compiler_params=_params("arbitrary"),
    )(block_e, n_used, hs, wg, wu, wd)


def _combine_body(ys_ref, dest_ref, w_ref, h_ref, x1_ref, g2_ref, wgu_ref, wd_ref, fg_ref, o_ref,
                  dest_s, gbuf, sem, dsem, *, tb):
    i = pl.program_id(0)
    cp = pltpu.make_async_copy(dest_ref.at[i], dest_s, dsem)
    cp.start()
    cp.wait()

    def issue(t, c):
        for k in range(TOP_K):
            _row_copy(ys_ref, dest_s[t * TOP_K + k], gbuf, k * tb + t, sem).start(priority=k % 2)
        return c

    lax.fori_loop(0, tb, issue, 0, unroll=4)

    hid = wd_ref.shape[0]
    h_lo, h_hi = _unpack_halves(h_ref[...])
    gu = _dot(jnp.concatenate([h_lo.astype(BF16), h_hi.astype(BF16)], axis=1), wgu_ref[...])
    moe = _dot((_silu(gu[:, :hid]) * gu[:, hid:]).astype(BF16), wd_ref[...])

    _drain(lambda: _row_copy(ys_ref, 0, gbuf, 0, sem).wait(), TOP_K * tb)
    w = w_ref[...]
    dh = gbuf.shape[1]
    m_lo, m_hi = moe[:, :dh], moe[:, dh:]
    for k in range(TOP_K):
        lo, hi = _unpack_halves(gbuf[k * tb:(k + 1) * tb, :])
        m_lo = m_lo + lo * w[:, k:k + 1]
        m_hi = m_hi + hi * w[:, k:k + 1]
    moe = jnp.concatenate([m_lo, m_hi], axis=1)
    rows, d = moe.shape
    upd = moe.reshape(rows // MOD_ROWS, MOD_ROWS, d) * g2_ref[...]
    x2 = x1_ref[...] + upd.reshape(rows, d)
    o_ref[...] = _rms(x2) * fg_ref[...]


def _combine(ys, dest_blk, w_tok, h2, x1, modg, wsh_gu, wsh_d, final_g):
    t, d = x1.shape
    nblk, n = dest_blk.shape
    tb = n // TOP_K
    ng = tb // MOD_ROWS
    const = lambda i: (0, 0)
    return pl.pallas_call(
        functools.partial(_combine_body, tb=tb),
        grid=(nblk,),
        in_specs=[pl.BlockSpec(memory_space=pl.ANY),
                  pl.BlockSpec(memory_space=pl.ANY),
                  pl.BlockSpec((tb, TOP_K), lambda i: (i, 0)),
                  pl.BlockSpec((tb, d // 2), lambda i: (i, 0)),
                  pl.BlockSpec((tb, d), lambda i: (i, 0)),
                  pl.BlockSpec((ng, 1, d), lambda i: (i, 0, 5)),
                  pl.BlockSpec(wsh_gu.shape, const), pl.BlockSpec(wsh_d.shape, const),
                  pl.BlockSpec((1, d), const)],
        out_specs=pl.BlockSpec((tb, d), lambda i: (i, 0)),
        out_shape=jax.ShapeDtypeStruct((t, d), F32),
        scratch_shapes=[pltpu.SMEM((n,), I32), pltpu.VMEM((n, d // 2), U32),
                        pltpu.SemaphoreType.DMA, pltpu.SemaphoreType.DMA],
        compiler_params=_params("arbitrary"),
    )(ys, dest_blk, w_tok, h2, x1, modg, wsh_gu, wsh_d, final_g.reshape(1, d))


def _rope_tables(pos):
    half = QK_ROPE // 2
    freqs = ROPE_THETA ** (-jnp.arange(half, dtype=F32) / half)
    ang = pos.astype(F32)[:, None] * freqs[None, :]
    cos, sin = jnp.cos(ang), jnp.sin(ang)
    return (jnp.concatenate([cos, cos, cos, cos], axis=1),
            jnp.concatenate([-sin, sin, -sin, sin], axis=1))


def _blocked(a, tb):
    k, t = a.shape
    return a.T.reshape(t // tb, tb * k)


def kernel(x_prompt, x_sample, c_prompt, c_sample, cache_conv, state_ssm, cache_ckv, cache_kr, w_ada, b_ada,
           norm1_g, norm2_g, w_in, conv_w, conv_b, dt_bias, a_log, d_skip, ssd_norm_g, w_ssd_out, q_norm_g,
           w_uq, kv_norm_g, w_uk, w_uv, w_mla_out, w_merge_out, w_router, e_bias, w_exp_gate, w_exp_up,
           w_exp_down, w_sh_gate, w_sh_up, w_sh_down, final_norm_g):
    depth = w_in.shape[0]
    assert depth == 1
    bp, seq, d = x_prompt.shape
    bs, lseq, _ = x_sample.shape
    assert lseq == MOD_ROWS and seq % MOD_ROWS == 0
    tp, ts = bp * seq, bs * lseq
    t_all = tp + ts
    nheads = dt_bias.shape[1]
    inner = nheads * SSD_HEAD_DIM
    cdim = conv_w.shape[2]
    gn = SSD_GROUPS * SSD_STATE
    qlora, kvlora = q_norm_g.shape[1], kv_norm_g.shape[1]
    ne = w_router.shape[2]
    assert nheads == 64 and 2 * nheads == LANES

    w = w_in[0]
    o_xbc, o_dt = inner, inner + cdim
    o_cq = o_dt + nheads
    o_ckv = o_cq + qlora
    o_kr = o_ckv + kvlora
    o_gate = o_kr + QK_ROPE
    perm = np.concatenate([np.arange(0, nheads, 2), np.arange(1, nheads, 2)])
    zc = lambda n: jnp.zeros((d, n), w.dtype)
    cols = [w[:, :inner], w[:, o_gate:], w[:, o_xbc:o_dt], w[:, o_cq:o_ckv], w[:, o_ckv:o_kr],
            w[:, o_dt:o_cq][:, perm], zc(LANES - nheads), w[:, o_kr:o_gate], zc(LANES - QK_ROPE)]
    used = inner + 2 * d + cdim + qlora + kvlora + 2 * LANES
    total = -(-used // 512) * 512
    cols.append(zc(total - used))
    w_r = jnp.concatenate(cols, axis=1).astype(BF16)
    offs = {"z": 0, "ga": inner, "gb": inner + d, "x": inner + 2 * d, "bc": 2 * inner + 2 * d}
    offs["cq"] = offs["bc"] + 2 * gn
    offs["ckv"] = offs["cq"] + qlora
    offs["dt"] = offs["ckv"] + kvlora
    offs["kr"] = offs["dt"] + LANES

    hw = QK_NOPE + LANES
    wq_r = jnp.pad(w_uq[0], ((0, 0), (0, 0), (0, hw - QK_NOPE - QK_ROPE))).reshape(qlora, MLA_HEADS * hw).astype(BF16)
    wk = w_uk[0].reshape(kvlora, MLA_HEADS * QK_NOPE).astype(BF16)
    wv = w_uv[0].reshape(kvlora, MLA_HEADS * V_HEAD).astype(BF16)
    pad_l = lambda v: jnp.pad(v[perm], (0, LANES - nheads)).reshape(1, LANES)
    dtb = pad_l(dt_bias[0])
    a_neg = pad_l(-jnp.exp(a_log[0]))
    dskip = jnp.repeat(d_skip[0], SSD_HEAD_DIM).reshape(1, inner)

    c_all = jnp.concatenate([c_prompt, c_sample], axis=0)
    mod = _ada(c_all, w_ada[0], b_ada[0])
    grp = np.concatenate([np.repeat(np.arange(bp), seq // MOD_ROWS), bp + np.arange(bs)])
    modg = mod[grp].reshape(t_all // MOD_ROWS, 1, 6 * d)

    x_all = jnp.concatenate([x_prompt.reshape(tp, d), x_sample.reshape(ts, d)], axis=0)
    proj = _inproj(x_all, modg, norm1_g[0], w_r, d)

    ssd_args = (conv_w[0], conv_b[0].reshape(1, cdim), dtb, a_neg, dskip, ssd_norm_g[0].reshape(1, inner))
    ys_p, conv_p, ssm_p = _ssd(proj, jnp.zeros((bp, CONV_W - 1, cdim), F32),
                               jnp.zeros((bp, inner, SSD_STATE), F32), *ssd_args,
                               nb=bp, seq=seq, row0=0, offs=offs, inner=inner)
    ys_s, conv_s, ssm_s = _ssd(proj, cache_conv[0], state_ssm[0].reshape(bs, inner, SSD_STATE),
                               *ssd_args, nb=bs, seq=lseq, row0=tp, offs=offs, inner=inner)

    past = cache_ckv.shape[2]
    cos_p, sin_p = _rope_tables(jnp.arange(seq))
    cos_s, sin_s = _rope_tables(past + jnp.arange(lseq))
    cos_t = jnp.concatenate([jnp.tile(cos_p, (bp, 1)), jnp.tile(cos_s, (bs, 1))], axis=0)
    sin_t = jnp.concatenate([jnp.tile(sin_p, (bp, 1)), jnp.tile(sin_s, (bs, 1))], axis=0)
    prep = functools.partial(_mla_prep, proj, cos_t, sin_t, q_norm_g[0], kv_norm_g[0], wq_r, wk, wv, offs=offs)
    ckv_p, kr_p, q_p, k_p, v_p = prep(row0=0, nrows=tp, absorbed=False)
    ckv_s, kr_s, qlat, qrope = prep(row0=tp, nrows=ts, absorbed=True)
    ym_p = _attn_prompt(q_p, k_p, v_p, nb=bp, seq=seq)
    o_lat = _attn_sample(qlat, qrope, cache_ckv[0], cache_kr[0], ckv_s, kr_s, nb=bs, seq=lseq)
    ym_s = _uv_sample(o_lat, wv)

    merged = _merge1(ys_p, ym_p, ys_s, ym_s, w_ssd_out[0].astype(BF16), w_mla_out[0].astype(BF16), proj, offs)
    x1, h2, logits_t = _merge2(merged, w_merge_out[0].astype(BF16), x_all, modg, norm2_g[0], w_router[0].T)

    eidx, pos, w_sel, cnt = _route(logits_t, e_bias[0])
    bm = MOE_ROWS
    counts = cnt[:, 0].astype(I32)
    padded = (counts + bm - 1) // bm * bm
    pad_end = jnp.cumsum(padded)
    pad_start = pad_end - padded
    nblocks = -(-(t_all * TOP_K) // bm) + ne
    onehot = eidx[:, :, None] == jnp.arange(ne, dtype=I32)
    dest = jnp.sum(jnp.where(onehot, pad_start, 0), axis=-1) + pos
    blk_start = jnp.arange(nblocks, dtype=I32) * bm
    block_e = jnp.minimum(jnp.sum(pad_end[None, :] <= blk_start[:, None], axis=1), ne - 1).astype(I32)
    n_used = (pad_end[-1:] // bm).astype(I32)

    tb_d = _pick(t_all, (256, 128))
    hs = _dispatch(h2, _blocked(dest, tb_d), (pad_start + counts).astype(I32), pad_end.astype(I32), nblocks * bm)
    ys = _experts(hs, block_e, n_used, w_exp_gate[0], w_exp_up[0], w_exp_down[0])
    wsh_gu = jnp.concatenate([w_sh_gate[0], w_sh_up[0]], axis=1).astype(BF16)
    tb_c = 128
    y_all = _combine(ys, _blocked(dest, tb_c), w_sel.T, h2, x1, modg, wsh_gu, w_sh_down[0].astype(BF16),
                     final_norm_g)

    r5 = lambda a, n, l: a.reshape(1, n, l, a.shape[-1])
    return (y_all[:tp].reshape(bp, seq, d), y_all[tp:].reshape(bs, lseq, d),
            conv_p[None], ssm_p.reshape(1, bp, nheads, SSD_HEAD_DIM, SSD_STATE),
            r5(ckv_p, bp, seq), r5(kr_p[:, :QK_ROPE], bp, seq),
            conv_s[None], ssm_s.reshape(1, bs, nheads, SSD_HEAD_DIM, SSD_STATE),
            r5(ckv_s, bs, lseq), r5(kr_s[:, :QK_ROPE], bs, lseq))
```

```python
import functools
import math

import numpy as np
import jax
import jax.numpy as jnp
from jax import lax
from jax.experimental import pallas as pl
from jax.experimental.pallas import tpu as pltpu

F32 = jnp.float32
BF16 = jnp.bfloat16
I32 = jnp.int32
U32 = jnp.uint32

EPS = 1e-6
CHUNK = 64
SSD_HEAD_DIM = 64
SSD_GROUPS = 8
SSD_STATE = 128
CONV_W = 4
MLA_HEADS = 16
QK_NOPE = 128
QK_ROPE = 64
V_HEAD = 128
ROPE_THETA = 10000.0
TOP_K = 8
N_GROUPS = 8
TOPK_GROUPS = 4
ROUTED_SCALE = 2.5

LANES = 128
SUBLANES = 8
MOD_ROWS = 32
MOE_ROWS = 512
VMEM_LIMIT = 56 * 1024 * 1024


def _params(*sem):
    return pltpu.CompilerParams(dimension_semantics=sem, vmem_limit_bytes=VMEM_LIMIT)


def _pick(n, cands):
    for c in cands:
        if n % c == 0:
            return c
    raise ValueError(f"no tile in {cands} divides {n}")


def _sigmoid(x):
    return 1.0 / (1.0 + jnp.exp(-x))


def _silu(x):
    return x * _sigmoid(x)


def _dot(a, b):
    return jnp.dot(a, b, preferred_element_type=F32)


def _dot_t(a, b):
    return lax.dot_general(a, b, (((1,), (1,)), ((), ())), preferred_element_type=F32)


def _split(x, n):
    parts = []
    for _ in range(n - 1):
        p = x.astype(BF16)
        parts.append(p)
        x = x - p.astype(F32)
    parts.append(x.astype(BF16))
    return parts


def _dot_exact_rhs(a, b_bf16, n=3):
    acc = None
    for p in _split(a, n):
        t = _dot(p, b_bf16)
        acc = t if acc is None else acc + t
    return acc


def _dot3(a, b, dot=_dot):
    ah, al = _split(a, 2)
    bh, bl = _split(b, 2)
    return dot(ah, bh) + (dot(ah, bl) + dot(al, bh))


def _pack_halves(x):
    h = x.shape[1] // 2
    bits = lambda v: lax.bitcast_convert_type(v.astype(BF16).astype(F32), U32)
    return (bits(x[:, :h]) >> 16) | bits(x[:, h:])


def _unpack_halves(p):
    lo = lax.bitcast_convert_type(p << 16, F32)
    hi = lax.bitcast_convert_type(p & jnp.uint32(0xFFFF0000), F32)
    return lo, hi


def _rms(x):
    return x * lax.rsqrt(jnp.mean(x * x, axis=-1, keepdims=True) + EPS)


def _modulate(y, sc_ref, sh_ref):
    rows, d = y.shape
    y3 = y.reshape(rows // MOD_ROWS, MOD_ROWS, d)
    return (y3 * (1.0 + sc_ref[...]) + sh_ref[...]).reshape(rows, d)


def _ada_body(c_ref, w_ref, b_ref, o_ref):
    o_ref[...] = _dot3(_silu(c_ref[...]), w_ref[...]) + b_ref[...]


def _ada(c_all, w_ada, b_ada):
    nb, d = c_all.shape
    n = w_ada.shape[1]
    tn = _pick(n, (1024, 512, 256, 128))
    return pl.pallas_call(
        _ada_body,
        grid=(n // tn,),
        in_specs=[pl.BlockSpec((nb, d), lambda j: (0, 0)),
                  pl.BlockSpec((d, tn), lambda j: (0, j)),
                  pl.BlockSpec((1, tn), lambda j: (0, j))],
        out_specs=pl.BlockSpec((nb, tn), lambda j: (0, j)),
        out_shape=jax.ShapeDtypeStruct((nb, n), F32),
        compiler_params=_params("arbitrary"),
    )(c_all, w_ada, b_ada.reshape(1, n))


def _two_stream_rows(tp, ts, cands):
    tm = _pick(math.gcd(tp, ts), cands)
    npb = tp // tm
    return tm, npb, (lambda i, *_: (jnp.minimum(i, npb - 1), 0)), (lambda i, *_: (jnp.maximum(i - npb, 0), 0))


def _inproj_body(xp_ref, xs_ref, sc_ref, sh_ref, g_ref, w_ref, o_ref, h_scr, *, npb):
    @pl.when(pl.program_id(1) == 0)
    def _():
        x = jnp.where(pl.program_id(0) < npb, xp_ref[...], xs_ref[...])
        h = _modulate(_rms(x) * g_ref[...], sc_ref, sh_ref)
        h_scr[...] = h.astype(BF16)

    o_ref[...] = _dot(h_scr[...], w_ref[...])


def _inproj(xp, xs, modg, norm_g, w_r, d):
    t = xp.shape[0] + xs.shape[0]
    n = w_r.shape[1]
    tm, npb, prow, srow = _two_stream_rows(xp.shape[0], xs.shape[0], (1024, 512, 256, 128))
    tn = 512
    ng = tm // MOD_ROWS
    return pl.pallas_call(
        functools.partial(_inproj_body, npb=npb),
        grid=(t // tm, n // tn),
        in_specs=[pl.BlockSpec((tm, d), prow), pl.BlockSpec((tm, d), srow),
                  pl.BlockSpec((ng, 1, d), lambda i, j: (i, 0, 1)),
                  pl.BlockSpec((ng, 1, d), lambda i, j: (i, 0, 0)),
                  pl.BlockSpec((1, d), lambda i, j: (0, 0)),
                  pl.BlockSpec((d, tn), lambda i, j: (0, j))],
        out_specs=pl.BlockSpec((tm, tn), lambda i, j: (i, j)),
        out_shape=jax.ShapeDtypeStruct((t, n), F32),
        scratch_shapes=[pltpu.VMEM((tm, d), BF16)],
        compiler_params=_params("arbitrary", "arbitrary"),
    )(xp, xs, modg, modg, norm_g.reshape(1, d), w_r)


def _ssd_body(z_ref, x_ref, bc_ref, dt_ref, conv0_ref, h0_ref, cw_ref, cb_ref, dtb_ref, a_ref,
              dskip_ref, ng_ref, y_ref, nconv_ref, nssm_ref,
              carry_x, carry_bc, xp_x, xp_bc, state, y_scr, *, lb, inner):
    q = CHUNK
    c = pl.program_id(1)
    nc = pl.num_programs(1)
    gn = SSD_GROUPS * SSD_STATE

    @pl.when(c == 0)
    def _():
        carry_x[...] = jnp.zeros_like(carry_x)
        carry_bc[...] = jnp.zeros_like(carry_bc)
        carry_x[8 - (CONV_W - 1):8, :] = conv0_ref[0, :, :inner]
        carry_bc[8 - (CONV_W - 1):8, :] = conv0_ref[0, :, inner:]
        state[...] = h0_ref[0]

    def pad_rows(v):
        if lb == q:
            return v
        return jnp.concatenate([v, jnp.zeros((q - lb, v.shape[1]), v.dtype)], axis=0)

    def conv(raw, carry, xp, w_lo, w_hi):
        xp[0:8, :] = carry[...]
        xp[8:8 + q, :] = raw
        acc = cb_ref[:, w_lo:w_hi]
        for j in range(CONV_W):
            acc = acc + xp[8 - j:8 - j + q, :] * cw_ref[CONV_W - 1 - j:CONV_W - j, w_lo:w_hi]
        return _silu(acc)

    x_raw = pad_rows(x_ref[...])
    bc_raw = pad_rows(bc_ref[...])
    xc = conv(x_raw, carry_x, xp_x, 0, inner)
    bcc = conv(bc_raw, carry_bc, xp_bc, inner, inner + 2 * gn)
    if lb == q:
        carry_x[...] = x_raw[q - 8:q, :]
        carry_bc[...] = bc_raw[q - 8:q, :]

    @pl.when(c == nc - 1)
    def _():
        nconv_ref[0, :, :inner] = x_raw[lb - (CONV_W - 1):lb, :]
        nconv_ref[0, :, inner:] = bc_raw[lb - (CONV_W - 1):lb, :]

    dtr = dt_ref[...] + dtb_ref[...]
    dtv = jnp.maximum(dtr, 0.0) + jnp.log1p(jnp.exp(-jnp.abs(dtr)))
    dtv = pad_rows(dtv)
    la = dtv * a_ref[...]

    row = lax.broadcasted_iota(I32, (q, q), 0)
    col = lax.broadcasted_iota(I32, (q, q), 1)
    tri = jnp.where(row >= col, 1.0, 0.0).astype(BF16)
    parts = _split(la, 3)
    a_cum = _dot(tri, parts[0]) + _dot(tri, parts[1]) + _dot(tri, parts[2])
    a_last = a_cum[q - 1:q, :]

    r2 = lax.broadcasted_iota(I32, (q, 2 * q), 0)
    c2 = lax.broadcasted_iota(I32, (q, 2 * q), 1)
    u_e = jnp.where((c2 < q) & (r2 <= c2), 1.0, 0.0).astype(BF16)
    u_o = jnp.where((c2 >= q) & (r2 <= c2 - q), 1.0, 0.0).astype(BF16)
    i_e = jnp.where(r2 == c2, 1.0, 0.0).astype(BF16)
    i_o = jnp.where(r2 == c2 - q, 1.0, 0.0).astype(BF16)
    la_t = la.T
    dt_t = dtv.T
    hp = la_t.shape[0] // 4
    acol = _dot_exact_rhs(la_t[0:hp], u_e) + _dot_exact_rhs(la_t[hp:2 * hp], u_o)
    dtrow = _dot_exact_rhs(dt_t[0:hp], i_e) + _dot_exact_rhs(dt_t[hp:2 * hp], i_o)
    w_t = (dtv * jnp.exp(a_last - a_cum)).T
    dec_b = jnp.broadcast_to(jnp.exp(jnp.sum(la_t, axis=1, keepdims=True)), (la_t.shape[0], SSD_STATE))

    lane = lax.broadcasted_iota(I32, (q, 2 * q), 1)
    causal2 = lax.broadcasted_iota(I32, (q, 2 * q), 0) >= jnp.where(lane < q, lane, lane - q)
    first_half = lane < q
    rr = lax.broadcasted_iota(I32, (2 * SSD_HEAD_DIM, 2 * SSD_HEAD_DIM), 0)
    cc = lax.broadcasted_iota(I32, (2 * SSD_HEAD_DIM, 2 * SSD_HEAD_DIM), 1)
    bd_mask = (rr < SSD_HEAD_DIM) == (cc < SSD_HEAD_DIM)
    top_rows_q = lax.broadcasted_iota(I32, (2 * SSD_HEAD_DIM, q), 0) < SSD_HEAD_DIM
    top_rows_n = lax.broadcasted_iota(I32, (2 * SSD_HEAD_DIM, SSD_STATE), 0) < SSD_HEAD_DIM

    pairs_per_group = (inner // SSD_HEAD_DIM) // SSD_GROUPS // 2
    for g in range(SSD_GROUPS):
        b_g = bcc[:, g * SSD_STATE:(g + 1) * SSD_STATE].astype(BF16)
        c_g = bcc[:, gn + g * SSD_STATE:gn + (g + 1) * SSD_STATE].astype(BF16)
        cb2 = _dot_t(c_g, jnp.concatenate([b_g, b_g], axis=0))
        for kk in range(pairs_per_group):
            k = g * pairs_per_group + kk
            lo, hi = k * 2 * SSD_HEAD_DIM, (k + 1) * 2 * SSD_HEAD_DIM
            xp = xc[:, lo:hi]
            arow = jnp.where(first_half, a_cum[:, k:k + 1], a_cum[:, hp + k:hp + k + 1])
            seg = jnp.where(causal2, arow - acol[k:k + 1, :], -jnp.inf)
            m = (jnp.exp(seg) * cb2 * dtrow[k:k + 1, :]).astype(BF16)
            xbd = jnp.where(bd_mask, jnp.concatenate([xp, xp], axis=0), 0.0).astype(BF16)
            y_diag = _dot(m, xbd)
            s_k = state[lo:hi, :]
            y_off = _dot_t(c_g, s_k.astype(BF16)) * jnp.exp(arow)
            y_scr[:, lo:hi] = y_diag + y_off + xp * dskip_ref[:, lo:hi]
            w2 = jnp.where(top_rows_q, w_t[k:k + 1, :], w_t[hp + k:hp + k + 1, :])
            contrib = _dot((xp.T * w2).astype(BF16), b_g)
            d_k = jnp.where(top_rows_n, dec_b[k:k + 1, :], dec_b[hp + k:hp + k + 1, :])
            state[lo:hi, :] = s_k * d_k + contrib

    @pl.when(c == nc - 1)
    def _():
        nssm_ref[0] = state[...]

    y = y_scr[0:lb, :] * _silu(z_ref[...])
    y_ref[...] = (_rms(y) * ng_ref[...]).astype(y_ref.dtype)


def _ssd(proj, conv0, h0, conv_w, conv_b, dtb, a_neg, dskip, norm_g, *, nb, seq, row0, offs, inner):
    q = CHUNK
    lb = min(seq, q)
    assert seq % lb == 0 and row0 % lb == 0 and lb % 16 == 0
    nc = seq // lb
    rb0 = row0 // lb
    cdim = conv_w.shape[1]
    hp_rows = inner
    kern = functools.partial(_ssd_body, lb=lb, inner=inner)
    rows = lambda b, c: rb0 + b * nc + c
    in_specs = [
        pl.BlockSpec((lb, inner), lambda b, c: (rows(b, c), offs["z"] // inner)),
        pl.BlockSpec((lb, inner), lambda b, c: (rows(b, c), offs["x"] // inner)),
        pl.BlockSpec((lb, cdim - inner), lambda b, c: (rows(b, c), offs["bc"] // (cdim - inner))),
        pl.BlockSpec((lb, LANES), lambda b, c: (rows(b, c), offs["dt"] // LANES)),
        pl.BlockSpec((1, CONV_W - 1, cdim), lambda b, c: (b, 0, 0)),
        pl.BlockSpec((1, hp_rows, SSD_STATE), lambda b, c: (b, 0, 0)),
        pl.BlockSpec((CONV_W, cdim), lambda b, c: (0, 0)),
        pl.BlockSpec((1, cdim), lambda b, c: (0, 0)),
        pl.BlockSpec((1, LANES), lambda b, c: (0, 0)),
        pl.BlockSpec((1, LANES), lambda b, c: (0, 0)),
        pl.BlockSpec((1, inner), lambda b, c: (0, 0)),
        pl.BlockSpec((1, inner), lambda b, c: (0, 0)),
    ]
    args = [proj, proj, proj, proj, conv0, h0, conv_w, conv_b, dtb, a_neg, dskip, norm_g]
    return pl.pallas_call(
        kern,
        grid=(nb, nc),
        in_specs=in_specs,
        out_specs=[pl.BlockSpec((lb, inner), lambda b, c: (b * nc + c, 0)),
                   pl.BlockSpec((1, CONV_W - 1, cdim), lambda b, c: (b, 0, 0)),
                   pl.BlockSpec((1, hp_rows, SSD_STATE), lambda b, c: (b, 0, 0))],
        out_shape=[jax.ShapeDtypeStruct((nb * seq, inner), BF16),
                   jax.ShapeDtypeStruct((nb, CONV_W - 1, cdim), F32),
                   jax.ShapeDtypeStruct((nb, hp_rows, SSD_STATE), F32)],
        scratch_shapes=[pltpu.VMEM((8, inner), F32), pltpu.VMEM((8, cdim - inner), F32),
                        pltpu.VMEM((8 + q, inner), F32), pltpu.VMEM((8 + q, cdim - inner), F32),
                        pltpu.VMEM((hp_rows, SSD_STATE), F32), pltpu.VMEM((q, inner), F32)],
        compiler_params=_params("arbitrary", "arbitrary"),
    )(*args)


def _rope128(x, cos, sin):
    lane = lax.broadcasted_iota(I32, x.shape, 1)
    half = QK_ROPE // 2
    swapped = jnp.where((lane % QK_ROPE) < half, pltpu.roll(x, LANES - half, 1), pltpu.roll(x, half, 1))
    return x * cos + swapped * sin


def _mla_prep_body(cq_ref, ckv_ref, kr_ref, cos_ref, sin_ref, qg_ref, kvg_ref, wq_ref, wk_ref, wv_ref,
                   *out_refs, absorbed, scale):
    cos, sin = cos_ref[...], sin_ref[...]
    ckv_n = _rms(ckv_ref[...]) * kvg_ref[...]
    kr_r = _rope128(kr_ref[...], cos, sin)
    qn = (_rms(cq_ref[...]) * qg_ref[...]).astype(BF16)
    hw = QK_NOPE + LANES
    if absorbed:
        ckvn_ref, krr_ref, qlat_ref, qrope_ref = out_refs
    else:
        ckvn_ref, krr_ref, q_ref, k_ref, v_ref = out_refs
        ckv_b = ckv_n.astype(BF16)
        v_ref[...] = _dot(ckv_b, wv_ref[...]).astype(BF16)
        kr_b = kr_r.astype(BF16)
    ckvn_ref[...] = ckv_n
    krr_ref[...] = kr_r
    for h in range(MLA_HEADS):
        qh = _dot(qn, wq_ref[:, h * hw:(h + 1) * hw]) * scale
        q_nope = qh[:, :QK_NOPE]
        q_rope = _rope128(qh[:, QK_NOPE:], cos, sin)
        if absorbed:
            qlat_ref[h] = _dot_t(q_nope.astype(BF16), wk_ref[:, h * QK_NOPE:(h + 1) * QK_NOPE]).astype(BF16)
            qrope_ref[h] = q_rope[:, :QK_ROPE].astype(BF16)
        else:
            q_ref[:, h * hw:h * hw + QK_NOPE] = q_nope.astype(BF16)
            q_ref[:, h * hw + QK_NOPE:(h + 1) * hw] = q_rope.astype(BF16)
            k_ref[:, h * hw:h * hw + QK_NOPE] = _dot(ckv_b, wk_ref[:, h * QK_NOPE:(h + 1) * QK_NOPE]).astype(BF16)
            k_ref[:, h * hw + QK_NOPE:(h + 1) * hw] = kr_b


def _mla_prep(proj, cos_t, sin_t, q_g, kv_g, wq_r, wk, wv, *, row0, nrows, offs, absorbed):
    qlora, kvlora = q_g.shape[0], kv_g.shape[0]
    tm = _pick(nrows, (512, 256, 128))
    assert row0 % tm == 0
    rb0 = row0 // tm
    hw = QK_NOPE + LANES
    scale = 1.0 / math.sqrt(QK_NOPE + QK_ROPE)
    const = lambda i: (0, 0)
    in_specs = [pl.BlockSpec((tm, qlora), lambda i: (rb0 + i, offs["cq"] // qlora)),
                pl.BlockSpec((tm, kvlora), lambda i: (rb0 + i, offs["ckv"] // kvlora)),
                pl.BlockSpec((tm, LANES), lambda i: (rb0 + i, offs["kr"] // LANES)),
                pl.BlockSpec((tm, LANES), lambda i: (rb0 + i, 0)),
                pl.BlockSpec((tm, LANES), lambda i: (rb0 + i, 0)),
                pl.BlockSpec((1, qlora), const), pl.BlockSpec((1, kvlora), const),
                pl.BlockSpec(wq_r.shape, const), pl.BlockSpec(wk.shape, const), pl.BlockSpec(wv.shape, const)]
    out_specs = [pl.BlockSpec((tm, kvlora), lambda i: (i, 0)), pl.BlockSpec((tm, LANES), lambda i: (i, 0))]
    out_shape = [jax.ShapeDtypeStruct((nrows, kvlora), F32), jax.ShapeDtypeStruct((nrows, LANES), F32)]
    if absorbed:
        out_specs += [pl.BlockSpec((MLA_HEADS, tm, kvlora), lambda i: (0, i, 0)),
                      pl.BlockSpec((MLA_HEADS, tm, QK_ROPE), lambda i: (0, i, 0))]
        out_shape += [jax.ShapeDtypeStruct((MLA_HEADS, nrows, kvlora), BF16),
                      jax.ShapeDtypeStruct((MLA_HEADS, nrows, QK_ROPE), BF16)]
    else:
        out_specs += [pl.BlockSpec((tm, MLA_HEADS * hw), lambda i: (i, 0)),
                      pl.BlockSpec((tm, MLA_HEADS * hw), lambda i: (i, 0)),
                      pl.BlockSpec((tm, MLA_HEADS * V_HEAD), lambda i: (i, 0))]
        out_shape += [jax.ShapeDtypeStruct((nrows, MLA_HEADS * hw), BF16),
                      jax.ShapeDtypeStruct((nrows, MLA_HEADS * hw), BF16),
                      jax.ShapeDtypeStruct((nrows, MLA_HEADS * V_HEAD), BF16)]
    return pl.pallas_call(
        functools.partial(_mla_prep_body, absorbed=absorbed, scale=scale),
        grid=(nrows // tm,),
        in_specs=in_specs, out_specs=out_specs, out_shape=out_shape,
        compiler_params=_params("arbitrary"),
    )(proj, proj, proj, cos_t, sin_t, q_g.reshape(1, -1), kv_g.reshape(1, -1), wq_r, wk, wv)


ATT_SUB = 256


def _attn_tile(q_sub, k_ref, v_ref, s_scr, p_scr, nk):
    sub = ATT_SUB
    nfull = nk - sub
    shift = CHUNK.bit_length() - 1

    def fold(x, op):
        return op(x.reshape(sub // 8, 8, sub), axis=0)

    if nfull:
        s_scr[0:nfull, :] = _dot_t(k_ref[0, 0:nfull, :], q_sub)
    krow = lax.broadcasted_iota(I32, (sub, sub), 0) >> shift
    qcol = lax.broadcasted_iota(I32, (sub, sub), 1) >> shift
    s_scr[nfull:nk, :] = jnp.where(krow <= qcol, _dot_t(k_ref[0, nfull:nk, :], q_sub), -jnp.inf)

    mrun = None
    for j in range(nk // sub):
        f = fold(s_scr[j * sub:(j + 1) * sub, :], jnp.max)
        mrun = f if mrun is None else jnp.maximum(mrun, f)
    m = jnp.max(mrun, axis=0, keepdims=True)
    lrun = None
    for j in range(nk // sub):
        pt = jnp.exp(s_scr[j * sub:(j + 1) * sub, :] - m)
        f = fold(pt, jnp.sum)
        lrun = f if lrun is None else lrun + f
        p_scr[j * sub:(j + 1) * sub, :] = pt.astype(BF16)
    l = jnp.sum(lrun, axis=0, keepdims=True)
    acc = lax.dot_general(v_ref[0, 0:nk, :], p_scr[0:nk, :], (((0,), (0,)), ((), ())),
                          preferred_element_type=F32)
    return (acc / l).T


def _attn_body(q_ref, k_ref, v_ref, o_ref, s_scr, p_scr, *, tq, nq):
    qi = pl.program_id(2)
    sub = ATT_SUB
    for c in range(nq):
        @pl.when(qi == c)
        def _():
            for r in range(tq // sub):
                q_sub = q_ref[0, r * sub:(r + 1) * sub, :]
                out = _attn_tile(q_sub, k_ref, v_ref, s_scr, p_scr, c * tq + (r + 1) * sub)
                o_ref[r * sub:(r + 1) * sub, :] = out.astype(o_ref.dtype)


def _attn_prompt(q, k, v, *, nb, seq):
    hw = QK_NOPE + LANES
    tq = min(512, seq // 2)
    assert seq % tq == 0 and tq % ATT_SUB == 0 and ATT_SUB % CHUNK == 0
    nq = seq // tq
    q3 = q.reshape(nb, seq, MLA_HEADS * hw)
    k3 = k.reshape(nb, seq, MLA_HEADS * hw)
    v3 = v.reshape(nb, seq, MLA_HEADS * V_HEAD)
    return pl.pallas_call(
        functools.partial(_attn_body, tq=tq, nq=nq),
        grid=(nb, MLA_HEADS, nq),
        in_specs=[pl.BlockSpec((1, tq, hw), lambda b, h, i: (b, i, h)),
                  pl.BlockSpec((1, seq, hw), lambda b, h, i: (b, 0, h)),
                  pl.BlockSpec((1, seq, V_HEAD), lambda b, h, i: (b, 0, h))],
        out_specs=pl.BlockSpec((tq, V_HEAD), lambda b, h, i: (b * nq + i, h)),
        out_shape=jax.ShapeDtypeStruct((nb * seq, MLA_HEADS * V_HEAD), BF16),
        scratch_shapes=[pltpu.VMEM((seq, ATT_SUB), F32), pltpu.VMEM((seq, ATT_SUB), BF16)],
        compiler_params=_params("arbitrary", "arbitrary", "arbitrary"),
    )(q3, k3, v3)


def _attn_sample_body(ql_ref, qr_ref, pckv_ref, pkr_ref, nckv_ref, nkr_ref, o_ref, *, past, seq):
    nh = ql_ref.shape[0]
    ql = ql_ref[...].reshape(nh * seq, ql_ref.shape[2])
    qr = qr_ref[...].reshape(nh * seq, qr_ref.shape[2])
    pckv = pckv_ref[0].astype(BF16)
    nckv = nckv_ref[...].astype(BF16)
    s_p = _dot_t(ql, pckv) + _dot_t(qr, pkr_ref[0].astype(BF16))
    s_n = _dot_t(ql, nckv) + _dot_t(qr, nkr_ref[:, :QK_ROPE].astype(BF16))
    shift = CHUNK.bit_length() - 1
    q_chunk = (past + lax.broadcasted_iota(I32, (nh * seq, 1), 0) % seq) >> shift
    kp_chunk = lax.broadcasted_iota(I32, (1, past), 1) >> shift
    kn_chunk = (past + lax.broadcasted_iota(I32, (1, seq), 1)) >> shift
    s_p = jnp.where(kp_chunk <= q_chunk, s_p, -jnp.inf)
    s_n = jnp.where(kn_chunk <= q_chunk, s_n, -jnp.inf)
    m = jnp.maximum(jnp.max(s_p, axis=1, keepdims=True), jnp.max(s_n, axis=1, keepdims=True))
    p_p = jnp.exp(s_p - m)
    p_n = jnp.exp(s_n - m)
    l = jnp.sum(p_p, axis=1, keepdims=True) + jnp.sum(p_n, axis=1, keepdims=True)
    o = (_dot(p_p.astype(BF16), pckv) + _dot(p_n.astype(BF16), nckv)) / l
    o_ref[...] = o.reshape(o_ref.shape).astype(o_ref.dtype)


def _attn_sample(qlat, qrope, past_ckv, past_kr, ckv_n, kr_r, *, nb, seq):
    past = past_ckv.shape[1]
    r = past_ckv.shape[2]
    return pl.pallas_call(
        functools.partial(_attn_sample_body, past=past, seq=seq),
        grid=(nb,),
        in_specs=[pl.BlockSpec((MLA_HEADS, seq, r), lambda b: (0, b, 0)),
                  pl.BlockSpec((MLA_HEADS, seq, QK_ROPE), lambda b: (0, b, 0)),
                  pl.BlockSpec((1, past, r), lambda b: (b, 0, 0)),
                  pl.BlockSpec((1, past, QK_ROPE), lambda b: (b, 0, 0)),
                  pl.BlockSpec((seq, r), lambda b: (b, 0)),
                  pl.BlockSpec((seq, LANES), lambda b: (b, 0))],
        out_specs=pl.BlockSpec((MLA_HEADS, seq, r), lambda b: (0, b, 0)),
        out_shape=jax.ShapeDtypeStruct((MLA_HEADS, nb * seq, r), BF16),
        compiler_params=_params("arbitrary"),
    )(qlat, qrope, past_ckv, past_kr, ckv_n, kr_r)


def _uv_body(o_ref, w_ref, y_ref):
    y_ref[...] = _dot(o_ref[0], w_ref[...]).astype(y_ref.dtype)


def _uv_sample(o_lat, wv):
    nh, nrows, r = o_lat.shape
    return pl.pallas_call(
        _uv_body,
        grid=(nh,),
        in_specs=[pl.BlockSpec((1, nrows, r), lambda h: (h, 0, 0)),
                  pl.BlockSpec((r, V_HEAD), lambda h: (0, h))],
        out_specs=pl.BlockSpec((nrows, V_HEAD), lambda h: (0, h)),
        out_shape=jax.ShapeDtypeStruct((nrows, nh * V_HEAD), BF16),
        compiler_params=_params("arbitrary"),
    )(o_lat, wv)


def _merge1_body(ysp_ref, ymp_ref, yss_ref, yms_ref, ws_ref, wm_ref, ga_ref, gb_ref, o_ref, *, npb):
    def run(ys_ref, ym_ref):
        a = _dot(ys_ref[...], ws_ref[...])
        b = _dot(ym_ref[...], wm_ref[...])
        o_ref[...] = (_sigmoid(ga_ref[...]) * a + _sigmoid(gb_ref[...]) * b).astype(o_ref.dtype)

    i = pl.program_id(0)
    pl.when(i < npb)(lambda: run(ysp_ref, ymp_ref))
    pl.when(i >= npb)(lambda: run(yss_ref, yms_ref))


def _merge1(ys_p, ym_p, ys_s, ym_s, w_ssd_out, w_mla_out, proj, offs):
    (tp, inner), ts = ys_p.shape, ys_s.shape[0]
    dm = ym_p.shape[1]
    d = w_ssd_out.shape[1]
    tm = _pick(math.gcd(tp, ts), (512, 256, 128))
    tn = 512
    npb, nsb = tp // tm, ts // tm
    prow = lambda i, j: (jnp.minimum(i, npb - 1), 0)
    srow = lambda i, j: (jnp.maximum(i - npb, 0), 0)
    return pl.pallas_call(
        functools.partial(_merge1_body, npb=npb),
        grid=(npb + nsb, d // tn),
        in_specs=[pl.BlockSpec((tm, inner), prow), pl.BlockSpec((tm, dm), prow),
                  pl.BlockSpec((tm, inner), srow), pl.BlockSpec((tm, dm), srow),
                  pl.BlockSpec((inner, tn), lambda i, j: (0, j)),
                  pl.BlockSpec((dm, tn), lambda i, j: (0, j)),
                  pl.BlockSpec((tm, tn), lambda i, j: (i, offs["ga"] // tn + j)),
                  pl.BlockSpec((tm, tn), lambda i, j: (i, offs["gb"] // tn + j))],
        out_specs=pl.BlockSpec((tm, tn), lambda i, j: (i, j)),
        out_shape=jax.ShapeDtypeStruct((tp + ts, d), BF16),
        compiler_params=_params("arbitrary", "arbitrary"),
    )(ys_p, ym_p, ys_s, ym_s, w_ssd_out, w_mla_out, proj, proj)


def _merge2_body(m_ref, w_ref, xp_ref, xs_ref, g1_ref, sc_ref, sh_ref, ng_ref, wr_ref, x1_ref, h2_ref, lg_ref,
                 *, npb):
    rows, d = xp_ref.shape
    x = jnp.where(pl.program_id(0) < npb, xp_ref[...], xs_ref[...])
    upd = _dot(m_ref[...], w_ref[...]).reshape(rows // MOD_ROWS, MOD_ROWS, d) * g1_ref[...]
    x1 = x + upd.reshape(rows, d)
    x1_ref[...] = x1
    h2 = _modulate(_rms(x1) * ng_ref[...], sc_ref, sh_ref)
    _store_items(h2_ref, _pack_halves(h2))
    lg_ref[...] = _dot3(wr_ref[...], h2, dot=_dot_t)


def _merge2(merged, w_merge, xp, xs, modg, norm_g, w_router_t):
    d = xp.shape[1]
    t = xp.shape[0] + xs.shape[0]
    assert d == 2 * SUBLANES * LANES
    ne = w_router_t.shape[0]
    tm, npb, prow, srow = _two_stream_rows(xp.shape[0], xs.shape[0], (512, 256, 128))
    ng = tm // MOD_ROWS
    return pl.pallas_call(
        functools.partial(_merge2_body, npb=npb),
        grid=(t // tm,),
        in_specs=[pl.BlockSpec((tm, d), lambda i: (i, 0)),
                  pl.BlockSpec((d, d), lambda i: (0, 0)),
                  pl.BlockSpec((tm, d), prow), pl.BlockSpec((tm, d), srow),
                  pl.BlockSpec((ng, 1, d), lambda i: (i, 0, 2)),
                  pl.BlockSpec((ng, 1, d), lambda i: (i, 0, 4)),
                  pl.BlockSpec((ng, 1, d), lambda i: (i, 0, 3)),
                  pl.BlockSpec((1, d), lambda i: (0, 0)),
                  pl.BlockSpec((ne, d), lambda i: (0, 0))],
        out_specs=[pl.BlockSpec((tm, d), lambda i: (i, 0)),
                   pl.BlockSpec((tm * SUBLANES, LANES), lambda i: (i, 0)),
                   pl.BlockSpec((ne, tm), lambda i: (0, i))],
        out_shape=[jax.ShapeDtypeStruct((t, d), F32), jax.ShapeDtypeStruct((t * SUBLANES, LANES), U32),
                   jax.ShapeDtypeStruct((ne, t), F32)],
        compiler_params=_params("arbitrary"),
    )(merged, w_merge, xp, xs, modg, modg, modg, norm_g.reshape(1, d), w_router_t)


def _route_body(lg_ref, eb_ref, eidx_ref, pos_ref, w_ref, cnt_ref, carry):
    ne, tr = lg_ref.shape
    per_group = ne // N_GROUPS

    @pl.when(pl.program_id(0) == 0)
    def _():
        carry[...] = jnp.zeros_like(carry)

    scores = _sigmoid(lg_ref[...])
    choice = scores + eb_ref[...]
    sub = lax.broadcasted_iota(I32, (per_group, tr), 0)
    gscore, blocks = [], []
    for g in range(N_GROUPS):
        blk = choice[g * per_group:(g + 1) * per_group, :]
        m1 = jnp.max(blk, axis=0, keepdims=True)
        first = jnp.min(jnp.where(blk == m1, sub, per_group), axis=0, keepdims=True)
        m2 = jnp.max(jnp.where(sub == first, -jnp.inf, blk), axis=0, keepdims=True)
        gscore.append(m1 + m2)
        blocks.append(blk)
    masked = []
    for g in range(N_GROUPS):
        rank = jnp.zeros((1, tr), I32)
        for g2 in range(N_GROUPS):
            if g2 == g:
                continue
            beats = (gscore[g2] > gscore[g]) | ((gscore[g2] == gscore[g]) & (g2 < g))
            rank = rank + beats.astype(I32)
        masked.append(jnp.where(rank < TOPK_GROUPS, blocks[g], -jnp.inf))
    cm = jnp.concatenate(masked, axis=0)

    eid = lax.broadcasted_iota(I32, (ne, tr), 0)
    rank = jnp.zeros((ne, tr), I32)
    for e2 in range(ne):
        rowv = cm[e2:e2 + 1, :]
        beats = (rowv > cm) | ((rowv == cm) & (eid > e2))
        rank = rank + beats.astype(I32)
    sel = rank < TOP_K
    wsel = jnp.where(sel, scores, 0.0)
    wfull = wsel / jnp.sum(wsel, axis=0, keepdims=True) * ROUTED_SCALE

    r = lax.broadcasted_iota(I32, (tr, tr), 0)
    c = lax.broadcasted_iota(I32, (tr, tr), 1)
    before = jnp.where(r < c, 1.0, 0.0).astype(BF16)
    self = jnp.where(sel, 1.0, 0.0)
    pos = carry[:, 0:1] + _dot(self.astype(BF16), before)
    carry[...] = carry[...] + jnp.sum(self, axis=1, keepdims=True)
    cnt_ref[...] = carry[...]

    eid_f = eid.astype(F32)
    for k in range(TOP_K):
        pick = sel & (rank == k)
        eidx_ref[k:k + 1, :] = jnp.sum(jnp.where(pick, eid_f, 0.0), axis=0, keepdims=True).astype(I32)
        pos_ref[k:k + 1, :] = jnp.sum(jnp.where(pick, pos, 0.0), axis=0, keepdims=True).astype(I32)
        w_ref[k:k + 1, :] = jnp.sum(jnp.where(pick, wfull, 0.0), axis=0, keepdims=True)


def _route(logits_t, e_bias):
    ne, t = logits_t.shape
    tr = _pick(t, (512, 256, 128))
    return pl.pallas_call(
        _route_body,
        grid=(t // tr,),
        in_specs=[pl.BlockSpec((ne, tr), lambda i: (0, i)), pl.BlockSpec((ne, 1), lambda i: (0, 0))],
        out_specs=[pl.BlockSpec((TOP_K, tr), lambda i: (0, i)), pl.BlockSpec((TOP_K, tr), lambda i: (0, i)),
                   pl.BlockSpec((TOP_K, tr), lambda i: (0, i)), pl.BlockSpec((ne, LANES), lambda i: (0, 0))],
        out_shape=[jax.ShapeDtypeStruct((TOP_K, t), I32), jax.ShapeDtypeStruct((TOP_K, t), I32),
                   jax.ShapeDtypeStruct((TOP_K, t), F32), jax.ShapeDtypeStruct((ne, LANES), F32)],
        scratch_shapes=[pltpu.VMEM((ne, LANES), F32)],
        compiler_params=_params("arbitrary"),
    )(logits_t, e_bias.reshape(ne, 1))


def _row_copy(src, s, dst, d, sem):
    rows = lambda i: pl.ds(pl.multiple_of(i * SUBLANES, SUBLANES), SUBLANES)
    return pltpu.make_async_copy(src.at[rows(s), :], dst.at[rows(d), :], sem)


def _load_items(ref, start, n):
    return jnp.concatenate([ref[pl.ds(start * SUBLANES + c, n, stride=SUBLANES), :] for c in range(SUBLANES)],
                           axis=1)


def _store_items(ref, val):
    n = val.shape[0]
    for c in range(SUBLANES):
        ref[pl.ds(c, n, stride=SUBLANES), :] = val[:, c * LANES:(c + 1) * LANES]


def _drain(wait_one, n, group=64):
    assert n % group == 0

    def body(j, c):
        for _ in range(group):
            wait_one()
        return c

    lax.fori_loop(0, n // group, body, 0)


def _dispatch_body(fill_lo_ref, fill_hi_ref, h_ref, dest_ref, hs_ref, dest_s, zrow, sem, dsem, *, tb):
    i = pl.program_id(0)
    cp = pltpu.make_async_copy(dest_ref.at[i], dest_s, dsem)
    cp.start()

    @pl.when(i == 0)
    def _():
        zrow[...] = jnp.zeros_like(zrow)

        def per_expert(fn):
            def body(e, _):
                lax.fori_loop(fill_lo_ref[e], fill_hi_ref[e], lambda s, c: (fn(s), c)[1], 0)
                return 0
            lax.fori_loop(0, fill_lo_ref.shape[0], body, 0)

        per_expert(lambda s: _row_copy(zrow, 0, hs_ref, s, sem).start())
        per_expert(lambda s: _row_copy(zrow, 0, hs_ref, s, sem).wait())

    cp.wait()

    def issue(t, c):
        for k in range(TOP_K):
            _row_copy(h_ref, t, hs_ref, dest_s[t * TOP_K + k], sem).start(priority=k % 2)
        return c

    lax.fori_loop(0, tb, issue, 0, unroll=4)
    _drain(lambda: _row_copy(h_ref, 0, hs_ref, 0, sem).wait(), TOP_K * tb)


def _dispatch(h2, dest_blk, fill_lo, fill_hi, n_slots):
    nblk, n = dest_blk.shape
    tb = n // TOP_K
    return pl.pallas_call(
        functools.partial(_dispatch_body, tb=tb),
        grid_spec=pltpu.PrefetchScalarGridSpec(
            num_scalar_prefetch=2,
            grid=(nblk,),
            in_specs=[pl.BlockSpec((tb * SUBLANES, LANES), lambda i, lo, hi: (i, 0)),
                      pl.BlockSpec(memory_space=pl.ANY)],
            out_specs=pl.BlockSpec(memory_space=pl.ANY),
            scratch_shapes=[pltpu.SMEM((n,), I32), pltpu.VMEM((SUBLANES, LANES), h2.dtype),
                            pltpu.SemaphoreType.DMA, pltpu.SemaphoreType.DMA]),
        out_shape=jax.ShapeDtypeStruct((n_slots * SUBLANES, LANES), h2.dtype),
        compiler_params=_params("arbitrary"),
    )(fill_lo, fill_hi, h2, dest_blk)


def _experts_body(be_ref, nu_ref, x_ref, wg_ref, wu_ref, wd_ref, o_ref, wgu_s, wd_s):
    i = pl.program_id(0)
    hid = wg_ref.shape[2]

    @pl.when(i < nu_ref[0])
    def _():
        @pl.when((i == 0) | (be_ref[i] != be_ref[jnp.maximum(i - 1, 0)]))
        def _():
            wgu_s[:, :hid] = wg_ref[0].astype(BF16)
            wgu_s[:, hid:] = wu_ref[0].astype(BF16)
            wd_s[...] = wd_ref[0].astype(BF16)

        lo, hi = _unpack_halves(_load_items(x_ref, 0, x_ref.shape[0] // SUBLANES))
        x = jnp.concatenate([lo.astype(BF16), hi.astype(BF16)], axis=1)
        gu = _dot(x, wgu_s[...])
        act = (_silu(gu[:, :hid]) * gu[:, hid:]).astype(BF16)
        _store_items(o_ref, _pack_halves(_dot(act, wd_s[...])))


def _experts(hs, block_e, n_used, wg, wu, wd):
    n_slots = hs.shape[0] // SUBLANES
    ne, d, hid = wg.shape
    assert d == 2 * SUBLANES * LANES
    bm = MOE_ROWS
    nblocks = n_slots // bm
    blk = lambda i, be, nu: (jnp.minimum(i, nu[0] - 1), 0)
    return pl.pallas_call(
        _experts_body,
        grid_spec=pltpu.PrefetchScalarGridSpec(
            num_scalar_prefetch=2,
            grid=(nblocks,),
            in_specs=[pl.BlockSpec((bm * SUBLANES, LANES), blk),
                      pl.BlockSpec((1, d, hid), lambda i, be, nu: (be[i], 0, 0)),
                      pl.BlockSpec((1, d, hid), lambda i, be, nu: (be[i], 0, 0)),
                      pl.BlockSpec((1, hid, d), lambda i, be, nu: (be[i], 0, 0))],
            out_specs=pl.BlockSpec((bm * SUBLANES, LANES), blk),
            scratch_shapes=[pltpu.VMEM((d, 2 * hid), BF16), pltpu.VMEM((hid, d), BF16)]),
        out_shape=jax.ShapeDtypeStruct(hs.shape, U32),
        compiler_params=_params("arbitrary"),
    )(block_e, n_used, hs, wg, wu, wd)


def _combine_body(ys_ref, dest_ref, w_ref, h_ref, x1_ref, g2_ref, wgu_ref, wd_ref, fg_ref, o_ref,
                  dest_s, gbuf, sem, dsem, *, tb):
    i = pl.program_id(0)
    cp = pltpu.make_async_copy(dest_ref.at[i], dest_s, dsem)
    cp.start()
    cp.wait()

    def issue(t, c):
        for k in range(TOP_K):
            _row_copy(ys_ref, dest_s[t * TOP_K + k], gbuf, k * tb + t, sem).start(priority=k % 2)
        return c

    lax.fori_loop(0, tb, issue, 0, unroll=4)

    hid = wd_ref.shape[0]
    h_lo, h_hi = _unpack_halves(_load_items(h_ref, 0, tb))
    gu = _dot(jnp.concatenate([h_lo.astype(BF16), h_hi.astype(BF16)], axis=1), wgu_ref[...])
    moe = _dot((_silu(gu[:, :hid]) * gu[:, hid:]).astype(BF16), wd_ref[...])

    _drain(lambda: _row_copy(ys_ref, 0, gbuf, 0, sem).wait(), TOP_K * tb)
    w = w_ref[...]
    dh = SUBLANES * LANES
    m_lo, m_hi = moe[:, :dh], moe[:, dh:]
    for k in range(TOP_K):
        lo, hi = _unpack_halves(_load_items(gbuf, k * tb, tb))
        m_lo = m_lo + lo * w[:, k:k + 1]
        m_hi = m_hi + hi * w[:, k:k + 1]
    moe = jnp.concatenate([m_lo, m_hi], axis=1)
    rows, d = moe.shape
    upd = moe.reshape(rows // MOD_ROWS, MOD_ROWS, d) * g2_ref[...]
    x2 = x1_ref[...] + upd.reshape(rows, d)
    o_ref[...] = _rms(x2) * fg_ref[...]


def _combine(ys, dest_blk, w_tok, h2, x1, modg, wsh_gu, wsh_d, final_g):
    t, d = x1.shape
    nblk, n = dest_blk.shape
    tb = n // TOP_K
    ng = tb // MOD_ROWS
    const = lambda i: (0, 0)
    return pl.pallas_call(
        functools.partial(_combine_body, tb=tb),
        grid=(nblk,),
        in_specs=[pl.BlockSpec(memory_space=pl.ANY),
                  pl.BlockSpec(memory_space=pl.ANY),
                  pl.BlockSpec((tb, TOP_K), lambda i: (i, 0)),
                  pl.BlockSpec((tb * SUBLANES, LANES), lambda i: (i, 0)),
                  pl.BlockSpec((tb, d), lambda i: (i, 0)),
                  pl.BlockSpec((ng, 1, d), lambda i: (i, 0, 5)),
                  pl.BlockSpec(wsh_gu.shape, const), pl.BlockSpec(wsh_d.shape, const),
                  pl.BlockSpec((1, d), const)],
        out_specs=pl.BlockSpec((tb, d), lambda i: (i, 0)),
        out_shape=jax.ShapeDtypeStruct((t, d), F32),
        scratch_shapes=[pltpu.SMEM((n,), I32), pltpu.VMEM((n * SUBLANES, LANES), U32),
                        pltpu.SemaphoreType.DMA, pltpu.SemaphoreType.DMA],
        compiler_params=_params("arbitrary"),
    )(ys, dest_blk, w_tok, h2, x1, modg, wsh_gu, wsh_d, final_g.reshape(1, d))


def _rope_tables(pos):
    half = QK_ROPE // 2
    freqs = ROPE_THETA ** (-jnp.arange(half, dtype=F32) / half)
    ang = pos.astype(F32)[:, None] * freqs[None, :]
    cos, sin = jnp.cos(ang), jnp.sin(ang)
    return (jnp.concatenate([cos, cos, cos, cos], axis=1),
            jnp.concatenate([-sin, sin, -sin, sin], axis=1))


def _blocked(a, tb):
    k, t = a.shape
    return a.T.reshape(t // tb, tb * k)


def kernel(x_prompt, x_sample, c_prompt, c_sample, cache_conv, state_ssm, cache_ckv, cache_kr, w_ada, b_ada,
           norm1_g, norm2_g, w_in, conv_w, conv_b, dt_bias, a_log, d_skip, ssd_norm_g, w_ssd_out, q_norm_g,
           w_uq, kv_norm_g, w_uk, w_uv, w_mla_out, w_merge_out, w_router, e_bias, w_exp_gate, w_exp_up,
           w_exp_down, w_sh_gate, w_sh_up, w_sh_down, final_norm_g):
    depth = w_in.shape[0]
    assert depth == 1
    bp, seq, d = x_prompt.shape
    bs, lseq, _ = x_sample.shape
    assert lseq == MOD_ROWS and seq % MOD_ROWS == 0
    tp, ts = bp * seq, bs * lseq
    t_all = tp + ts
    nheads = dt_bias.shape[1]
    inner = nheads * SSD_HEAD_DIM
    cdim = conv_w.shape[2]
    gn = SSD_GROUPS * SSD_STATE
    qlora, kvlora = q_norm_g.shape[1], kv_norm_g.shape[1]
    ne = w_router.shape[2]
    assert nheads == 64 and 2 * nheads == LANES

    w = w_in[0]
    o_xbc, o_dt = inner, inner + cdim
    o_cq = o_dt + nheads
    o_ckv = o_cq + qlora
    o_kr = o_ckv + kvlora
    o_gate = o_kr + QK_ROPE
    perm = np.concatenate([np.arange(0, nheads, 2), np.arange(1, nheads, 2)])
    zc = lambda n: jnp.zeros((d, n), w.dtype)
    cols = [w[:, :inner], w[:, o_gate:], w[:, o_xbc:o_dt], w[:, o_cq:o_ckv], w[:, o_ckv:o_kr],
            w[:, o_dt:o_cq][:, perm], zc(LANES - nheads), w[:, o_kr:o_gate], zc(LANES - QK_ROPE)]
    used = inner + 2 * d + cdim + qlora + kvlora + 2 * LANES
    total = -(-used // 512) * 512
    cols.append(zc(total - used))
    w_r = jnp.concatenate(cols, axis=1).astype(BF16)
    offs = {"z": 0, "ga": inner, "gb": inner + d, "x": inner + 2 * d, "bc": 2 * inner + 2 * d}
    offs["cq"] = offs["bc"] + 2 * gn
    offs["ckv"] = offs["cq"] + qlora
    offs["dt"] = offs["ckv"] + kvlora
    offs["kr"] = offs["dt"] + LANES

    hw = QK_NOPE + LANES
    wq_r = jnp.pad(w_uq[0], ((0, 0), (0, 0), (0, hw - QK_NOPE - QK_ROPE))).reshape(qlora, MLA_HEADS * hw).astype(BF16)
    wk = w_uk[0].reshape(kvlora, MLA_HEADS * QK_NOPE).astype(BF16)
    wv = w_uv[0].reshape(kvlora, MLA_HEADS * V_HEAD).astype(BF16)
    pad_l = lambda v: jnp.pad(v[perm], (0, LANES - nheads)).reshape(1, LANES)
    dtb = pad_l(dt_bias[0])
    a_neg = pad_l(-jnp.exp(a_log[0]))
    dskip = jnp.repeat(d_skip[0], SSD_HEAD_DIM).reshape(1, inner)

    c_all = jnp.concatenate([c_prompt, c_sample], axis=0)
    mod = _ada(c_all, w_ada[0], b_ada[0])
    grp = np.concatenate([np.repeat(np.arange(bp), seq // MOD_ROWS), bp + np.arange(bs)])
    modg = mod[grp].reshape(t_all // MOD_ROWS, 1, 6 * d)

    xp, xs = x_prompt.reshape(tp, d), x_sample.reshape(ts, d)
    proj = _inproj(xp, xs, modg, norm1_g[0], w_r, d)

    ssd_args = (conv_w[0], conv_b[0].reshape(1, cdim), dtb, a_neg, dskip, ssd_norm_g[0].reshape(1, inner))
    ys_p, conv_p, ssm_p = _ssd(proj, jnp.zeros((bp, CONV_W - 1, cdim), F32),
                               jnp.zeros((bp, inner, SSD_STATE), F32), *ssd_args,
                               nb=bp, seq=seq, row0=0, offs=offs, inner=inner)
    ys_s, conv_s, ssm_s = _ssd(proj, cache_conv[0], state_ssm[0].reshape(bs, inner, SSD_STATE),
                               *ssd_args, nb=bs, seq=lseq, row0=tp, offs=offs, inner=inner)

    past = cache_ckv.shape[2]
    cos_p, sin_p = _rope_tables(jnp.arange(seq))
    cos_s, sin_s = _rope_tables(past + jnp.arange(lseq))
    cos_t = jnp.concatenate([jnp.tile(cos_p, (bp, 1)), jnp.tile(cos_s, (bs, 1))], axis=0)
    sin_t = jnp.concatenate([jnp.tile(sin_p, (bp, 1)), jnp.tile(sin_s, (bs, 1))], axis=0)
    prep = functools.partial(_mla_prep, proj, cos_t, sin_t, q_norm_g[0], kv_norm_g[0], wq_r, wk, wv, offs=offs)
    ckv_p, kr_p, q_p, k_p, v_p = prep(row0=0, nrows=tp, absorbed=False)
    ckv_s, kr_s, qlat, qrope = prep(row0=tp, nrows=ts, absorbed=True)
    ym_p = _attn_prompt(q_p, k_p, v_p, nb=bp, seq=seq)
    o_lat = _attn_sample(qlat, qrope, cache_ckv[0], cache_kr[0], ckv_s, kr_s, nb=bs, seq=lseq)
    ym_s = _uv_sample(o_lat, wv)

    merged = _merge1(ys_p, ym_p, ys_s, ym_s, w_ssd_out[0].astype(BF16), w_mla_out[0].astype(BF16), proj, offs)
    x1, h2, logits_t = _merge2(merged, w_merge_out[0].astype(BF16), xp, xs, modg, norm2_g[0], w_router[0].T)

    eidx, pos, w_sel, cnt = _route(logits_t, e_bias[0])
    bm = MOE_ROWS
    counts = cnt[:, 0].astype(I32)
    padded = (counts + bm - 1) // bm * bm
    pad_end = jnp.cumsum(padded)
    pad_start = pad_end - padded
    nblocks = -(-(t_all * TOP_K) // bm) + ne
    onehot = eidx[:, :, None] == jnp.arange(ne, dtype=I32)
    dest = jnp.sum(jnp.where(onehot, pad_start, 0), axis=-1) + pos
    blk_start = jnp.arange(nblocks, dtype=I32) * bm
    block_e = jnp.minimum(jnp.sum(pad_end[None, :] <= blk_start[:, None], axis=1), ne - 1).astype(I32)
    n_used = (pad_end[-1:] // bm).astype(I32)

    tb_d = _pick(t_all, (256, 128))
    hs = _dispatch(h2, _blocked(dest, tb_d), (pad_start + counts).astype(I32), pad_end.astype(I32), nblocks * bm)
    ys = _experts(hs, block_e, n_used, w_exp_gate[0], w_exp_up[0], w_exp_down[0])
    wsh_gu = jnp.concatenate([w_sh_gate[0], w_sh_up[0]], axis=1).astype(BF16)
    tb_c = _pick(t_all, (256, 128))
    y_all = _combine(ys, _blocked(dest, tb_c), w_sel.T, h2, x1, modg, wsh_gu, w_sh_down[0].astype(BF16),
                     final_norm_g)

    r5 = lambda a, n, l: a.reshape(1, n, l, a.shape[-1])
    return (y_all[:tp].reshape(bp, seq, d), y_all[tp:].reshape(bs, lseq, d),
            conv_p[None], ssm_p.reshape(1, bp, nheads, SSD_HEAD_DIM, SSD_STATE),
            r5(ckv_p, bp, seq), r5(kr_p[:, :QK_ROPE], bp, seq),
            conv_s[None], ssm_s.reshape(1, bs, nheads, SSD_HEAD_DIM, SSD_STATE),
            r5(ckv_s, bs, lseq), r5(kr_s[:, :QK_ROPE], bs, lseq))
```

```python
import functools
import math

import numpy as np
import jax
import jax.numpy as jnp
from jax import lax
from jax.experimental import pallas as pl
from jax.experimental.pallas import tpu as pltpu

F32 = jnp.float32
BF16 = jnp.bfloat16
I32 = jnp.int32
U32 = jnp.uint32

EPS = 1e-6
CHUNK = 64
SSD_HEAD_DIM = 64
SSD_GROUPS = 8
SSD_STATE = 128
CONV_W = 4
MLA_HEADS = 16
QK_NOPE = 128
QK_ROPE = 64
V_HEAD = 128
ROPE_THETA = 10000.0
TOP_K = 8
N_GROUPS = 8
TOPK_GROUPS = 4
ROUTED_SCALE = 2.5

LANES = 128
SUBLANES = 8
MOD_ROWS = 32
MOE_ROWS = 512
VMEM_LIMIT = 56 * 1024 * 1024


def _params(*sem):
    return pltpu.CompilerParams(dimension_semantics=sem, vmem_limit_bytes=VMEM_LIMIT)


def _pick(n, cands):
    for c in cands:
        if n % c == 0:
            return c
    raise ValueError(f"no tile in {cands} divides {n}")


def _sigmoid(x):
    return 1.0 / (1.0 + jnp.exp(-x))


def _silu(x):
    return x * _sigmoid(x)


def _dot(a, b):
    return jnp.dot(a, b, preferred_element_type=F32)


def _dot_t(a, b):
    return lax.dot_general(a, b, (((1,), (1,)), ((), ())), preferred_element_type=F32)


def _split(x, n):
    parts = []
    for _ in range(n - 1):
        p = x.astype(BF16)
        parts.append(p)
        x = x - p.astype(F32)
    parts.append(x.astype(BF16))
    return parts


def _dot_exact_rhs(a, b_bf16, n=3):
    acc = None
    for p in _split(a, n):
        t = _dot(p, b_bf16)
        acc = t if acc is None else acc + t
    return acc


def _dot3(a, b, dot=_dot):
    ah, al = _split(a, 2)
    bh, bl = _split(b, 2)
    return dot(ah, bh) + (dot(ah, bl) + dot(al, bh))


def _pack_halves(x):
    h = x.shape[1] // 2
    bits = lambda v: lax.bitcast_convert_type(v.astype(BF16).astype(F32), U32)
    return (bits(x[:, :h]) >> 16) | bits(x[:, h:])


def _unpack_halves(p):
    lo = lax.bitcast_convert_type(p << 16, F32)
    hi = lax.bitcast_convert_type(p & jnp.uint32(0xFFFF0000), F32)
    return lo, hi


def _rms(x):
    return x * lax.rsqrt(jnp.mean(x * x, axis=-1, keepdims=True) + EPS)


def _modulate(y, sc_ref, sh_ref):
    rows, d = y.shape
    y3 = y.reshape(rows // MOD_ROWS, MOD_ROWS, d)
    return (y3 * (1.0 + sc_ref[...]) + sh_ref[...]).reshape(rows, d)


def _ada_body(c_ref, w_ref, b_ref, o_ref):
    o_ref[...] = _dot3(_silu(c_ref[...]), w_ref[...]) + b_ref[...]


def _ada(c_all, w_ada, b_ada):
    nb, d = c_all.shape
    n = w_ada.shape[1]
    tn = _pick(n, (1024, 512, 256, 128))
    return pl.pallas_call(
        _ada_body,
        grid=(n // tn,),
        in_specs=[pl.BlockSpec((nb, d), lambda j: (0, 0)),
                  pl.BlockSpec((d, tn), lambda j: (0, j)),
                  pl.BlockSpec((1, tn), lambda j: (0, j))],
        out_specs=pl.BlockSpec((nb, tn), lambda j: (0, j)),
        out_shape=jax.ShapeDtypeStruct((nb, n), F32),
        compiler_params=_params("arbitrary"),
    )(c_all, w_ada, b_ada.reshape(1, n))


def _two_stream_rows(tp, ts, cands):
    tm = _pick(math.gcd(tp, ts), cands)
    npb = tp // tm
    return tm, npb, (lambda i, *_: (jnp.minimum(i, npb - 1), 0)), (lambda i, *_: (jnp.maximum(i - npb, 0), 0))


def _inproj_body(xp_ref, xs_ref, sc_ref, sh_ref, g_ref, w_ref, o_ref, h_scr, *, npb):
    @pl.when(pl.program_id(1) == 0)
    def _():
        x = jnp.where(pl.program_id(0) < npb, xp_ref[...], xs_ref[...])
        h = _modulate(_rms(x) * g_ref[...], sc_ref, sh_ref)
        h_scr[...] = h.astype(BF16)

    o_ref[...] = _dot(h_scr[...], w_ref[...])


def _inproj(xp, xs, modg, norm_g, w_r, d):
    t = xp.shape[0] + xs.shape[0]
    n = w_r.shape[1]
    tm, npb, prow, srow = _two_stream_rows(xp.shape[0], xs.shape[0], (1024, 512, 256, 128))
    tn = 512
    ng = tm // MOD_ROWS
    return pl.pallas_call(
        functools.partial(_inproj_body, npb=npb),
        grid=(t // tm, n // tn),
        in_specs=[pl.BlockSpec((tm, d), prow), pl.BlockSpec((tm, d), srow),
                  pl.BlockSpec((ng, 1, d), lambda i, j: (i, 0, 1)),
                  pl.BlockSpec((ng, 1, d), lambda i, j: (i, 0, 0)),
                  pl.BlockSpec((1, d), lambda i, j: (0, 0)),
                  pl.BlockSpec((d, tn), lambda i, j: (0, j))],
        out_specs=pl.BlockSpec((tm, tn), lambda i, j: (i, j)),
        out_shape=jax.ShapeDtypeStruct((t, n), F32),
        scratch_shapes=[pltpu.VMEM((tm, d), BF16)],
        compiler_params=_params("arbitrary", "arbitrary"),
    )(xp, xs, modg, modg, norm_g.reshape(1, d), w_r)


def _ssd_body(z_ref, x_ref, bc_ref, dt_ref, conv0_ref, h0_ref, cw_ref, cb_ref, dtb_ref, a_ref,
              dskip_ref, ng_ref, y_ref, nconv_ref, nssm_ref,
              carry_x, carry_bc, xp_x, xp_bc, state, y_scr, *, lb, inner):
    q = CHUNK
    c = pl.program_id(1)
    nc = pl.num_programs(1)
    gn = SSD_GROUPS * SSD_STATE

    @pl.when(c == 0)
    def _():
        carry_x[...] = jnp.zeros_like(carry_x)
        carry_bc[...] = jnp.zeros_like(carry_bc)
        carry_x[8 - (CONV_W - 1):8, :] = conv0_ref[0, :, :inner]
        carry_bc[8 - (CONV_W - 1):8, :] = conv0_ref[0, :, inner:]
        state[...] = h0_ref[0, 0].reshape(state.shape)

    def pad_rows(v):
        if lb == q:
            return v
        return jnp.concatenate([v, jnp.zeros((q - lb, v.shape[1]), v.dtype)], axis=0)

    def conv(raw, carry, xp, w_lo, w_hi):
        xp[0:8, :] = carry[...]
        xp[8:8 + q, :] = raw
        acc = cb_ref[:, w_lo:w_hi]
        for j in range(CONV_W):
            acc = acc + xp[8 - j:8 - j + q, :] * cw_ref[CONV_W - 1 - j:CONV_W - j, w_lo:w_hi]
        return _silu(acc)

    x_raw = pad_rows(x_ref[...])
    bc_raw = pad_rows(bc_ref[...])
    xc = conv(x_raw, carry_x, xp_x, 0, inner)
    bcc = conv(bc_raw, carry_bc, xp_bc, inner, inner + 2 * gn)
    if lb == q:
        carry_x[...] = x_raw[q - 8:q, :]
        carry_bc[...] = bc_raw[q - 8:q, :]

    @pl.when(c == nc - 1)
    def _():
        nconv_ref[0, :, :inner] = x_raw[lb - (CONV_W - 1):lb, :]
        nconv_ref[0, :, inner:] = bc_raw[lb - (CONV_W - 1):lb, :]

    dtr = dt_ref[...] + dtb_ref[...]
    dtv = jnp.maximum(dtr, 0.0) + jnp.log1p(jnp.exp(-jnp.abs(dtr)))
    dtv = pad_rows(dtv)
    la = dtv * a_ref[...]

    row = lax.broadcasted_iota(I32, (q, q), 0)
    col = lax.broadcasted_iota(I32, (q, q), 1)
    tri = jnp.where(row >= col, 1.0, 0.0).astype(BF16)
    parts = _split(la, 3)
    a_cum = _dot(tri, parts[0]) + _dot(tri, parts[1]) + _dot(tri, parts[2])
    a_last = a_cum[q - 1:q, :]

    r2 = lax.broadcasted_iota(I32, (q, 2 * q), 0)
    c2 = lax.broadcasted_iota(I32, (q, 2 * q), 1)
    u_e = jnp.where((c2 < q) & (r2 <= c2), 1.0, 0.0).astype(BF16)
    u_o = jnp.where((c2 >= q) & (r2 <= c2 - q), 1.0, 0.0).astype(BF16)
    i_e = jnp.where(r2 == c2, 1.0, 0.0).astype(BF16)
    i_o = jnp.where(r2 == c2 - q, 1.0, 0.0).astype(BF16)
    la_t = la.T
    dt_t = dtv.T
    hp = la_t.shape[0] // 4
    acol = _dot_exact_rhs(la_t[0:hp], u_e) + _dot_exact_rhs(la_t[hp:2 * hp], u_o)
    dtrow = _dot_exact_rhs(dt_t[0:hp], i_e) + _dot_exact_rhs(dt_t[hp:2 * hp], i_o)
    w_t = (dtv * jnp.exp(a_last - a_cum)).T
    dec_b = jnp.broadcast_to(jnp.exp(jnp.sum(la_t, axis=1, keepdims=True)), (la_t.shape[0], SSD_STATE))

    lane = lax.broadcasted_iota(I32, (q, 2 * q), 1)
    causal2 = lax.broadcasted_iota(I32, (q, 2 * q), 0) >= jnp.where(lane < q, lane, lane - q)
    first_half = lane < q
    rr = lax.broadcasted_iota(I32, (2 * SSD_HEAD_DIM, 2 * SSD_HEAD_DIM), 0)
    cc = lax.broadcasted_iota(I32, (2 * SSD_HEAD_DIM, 2 * SSD_HEAD_DIM), 1)
    bd_mask = (rr < SSD_HEAD_DIM) == (cc < SSD_HEAD_DIM)
    top_rows_q = lax.broadcasted_iota(I32, (2 * SSD_HEAD_DIM, q), 0) < SSD_HEAD_DIM
    top_rows_n = lax.broadcasted_iota(I32, (2 * SSD_HEAD_DIM, SSD_STATE), 0) < SSD_HEAD_DIM

    pairs_per_group = (inner // SSD_HEAD_DIM) // SSD_GROUPS // 2
    for g in range(SSD_GROUPS):
        b_g = bcc[:, g * SSD_STATE:(g + 1) * SSD_STATE].astype(BF16)
        c_g = bcc[:, gn + g * SSD_STATE:gn + (g + 1) * SSD_STATE].astype(BF16)
        cb2 = _dot_t(c_g, jnp.concatenate([b_g, b_g], axis=0))
        for kk in range(pairs_per_group):
            k = g * pairs_per_group + kk
            lo, hi = k * 2 * SSD_HEAD_DIM, (k + 1) * 2 * SSD_HEAD_DIM
            xp = xc[:, lo:hi]
            arow = jnp.where(first_half, a_cum[:, k:k + 1], a_cum[:, hp + k:hp + k + 1])
            seg = jnp.where(causal2, arow - acol[k:k + 1, :], -jnp.inf)
            m = (jnp.exp(seg) * cb2 * dtrow[k:k + 1, :]).astype(BF16)
            xbd = jnp.where(bd_mask, jnp.concatenate([xp, xp], axis=0), 0.0).astype(BF16)
            y_diag = _dot(m, xbd)
            s_k = state[lo:hi, :]
            y_off = _dot_t(c_g, s_k.astype(BF16)) * jnp.exp(arow)
            y_scr[:, lo:hi] = y_diag + y_off + xp * dskip_ref[:, lo:hi]
            w2 = jnp.where(top_rows_q, w_t[k:k + 1, :], w_t[hp + k:hp + k + 1, :])
            contrib = _dot((xp.T * w2).astype(BF16), b_g)
            d_k = jnp.where(top_rows_n, dec_b[k:k + 1, :], dec_b[hp + k:hp + k + 1, :])
            state[lo:hi, :] = s_k * d_k + contrib

    @pl.when(c == nc - 1)
    def _():
        nssm_ref[0, 0] = state[...].reshape(nssm_ref.shape[2:])

    y = y_scr[0:lb, :] * _silu(z_ref[...])
    y_ref[...] = (_rms(y) * ng_ref[...]).astype(y_ref.dtype)


def _ssd(proj, conv0, h0, conv_w, conv_b, dtb, a_neg, dskip, norm_g, *, nb, seq, row0, offs, inner):
    q = CHUNK
    lb = min(seq, q)
    assert seq % lb == 0 and row0 % lb == 0 and lb % 16 == 0
    nc = seq // lb
    rb0 = row0 // lb
    cdim = conv_w.shape[1]
    hp_rows = inner
    state_shape = (inner // SSD_HEAD_DIM, SSD_HEAD_DIM, SSD_STATE)
    assert h0.shape == (1, nb) + state_shape
    kern = functools.partial(_ssd_body, lb=lb, inner=inner)
    rows = lambda b, c: rb0 + b * nc + c
    in_specs = [
        pl.BlockSpec((lb, inner), lambda b, c: (rows(b, c), offs["z"] // inner)),
        pl.BlockSpec((lb, inner), lambda b, c: (rows(b, c), offs["x"] // inner)),
        pl.BlockSpec((lb, cdim - inner), lambda b, c: (rows(b, c), offs["bc"] // (cdim - inner))),
        pl.BlockSpec((lb, LANES), lambda b, c: (rows(b, c), offs["dt"] // LANES)),
        pl.BlockSpec((1, CONV_W - 1, cdim), lambda b, c: (b, 0, 0)),
        pl.BlockSpec((1, 1) + state_shape, lambda b, c: (0, b, 0, 0, 0)),
        pl.BlockSpec((CONV_W, cdim), lambda b, c: (0, 0)),
        pl.BlockSpec((1, cdim), lambda b, c: (0, 0)),
        pl.BlockSpec((1, LANES), lambda b, c: (0, 0)),
        pl.BlockSpec((1, LANES), lambda b, c: (0, 0)),
        pl.BlockSpec((1, inner), lambda b, c: (0, 0)),
        pl.BlockSpec((1, inner), lambda b, c: (0, 0)),
    ]
    args = [proj, proj, proj, proj, conv0, h0, conv_w, conv_b, dtb, a_neg, dskip, norm_g]
    return pl.pallas_call(
        kern,
        grid=(nb, nc),
        in_specs=in_specs,
        out_specs=[pl.BlockSpec((lb, inner), lambda b, c: (b * nc + c, 0)),
                   pl.BlockSpec((1, CONV_W - 1, cdim), lambda b, c: (b, 0, 0)),
                   pl.BlockSpec((1, 1) + state_shape, lambda b, c: (0, b, 0, 0, 0))],
        out_shape=[jax.ShapeDtypeStruct((nb * seq, inner), BF16),
                   jax.ShapeDtypeStruct((nb, CONV_W - 1, cdim), F32),
                   jax.ShapeDtypeStruct((1, nb) + state_shape, F32)],
        scratch_shapes=[pltpu.VMEM((8, inner), F32), pltpu.VMEM((8, cdim - inner), F32),
                        pltpu.VMEM((8 + q, inner), F32), pltpu.VMEM((8 + q, cdim - inner), F32),
                        pltpu.VMEM((hp_rows, SSD_STATE), F32), pltpu.VMEM((q, inner), F32)],
        compiler_params=_params("arbitrary", "arbitrary"),
    )(*args)


def _rope128(x, cos, sin):
    lane = lax.broadcasted_iota(I32, x.shape, 1)
    half = QK_ROPE // 2
    swapped = jnp.where((lane % QK_ROPE) < half, pltpu.roll(x, LANES - half, 1), pltpu.roll(x, half, 1))
    return x * cos + swapped * sin


def _mla_prep_body(cq_ref, ckv_ref, kr_ref, cos_ref, sin_ref, qg_ref, kvg_ref, wq_ref, wk_ref, wv_ref,
                   *out_refs, absorbed, scale):
    cos, sin = cos_ref[...], sin_ref[...]
    ckv_n = _rms(ckv_ref[...]) * kvg_ref[...]
    kr_r = _rope128(kr_ref[...], cos, sin)
    qn = (_rms(cq_ref[...]) * qg_ref[...]).astype(BF16)
    hw = QK_NOPE + LANES
    if absorbed:
        ckvn_ref, krr_ref, qlat_ref, qrope_ref = out_refs
    else:
        ckvn_ref, krr_ref, q_ref, k_ref, v_ref = out_refs
        ckv_b = ckv_n.astype(BF16)
        v_ref[...] = _dot(ckv_b, wv_ref[...]).astype(BF16)
        kr_b = kr_r.astype(BF16)
    ckvn_ref[...] = ckv_n
    krr_ref[...] = kr_r
    for h in range(MLA_HEADS):
        qh = _dot(qn, wq_ref[:, h * hw:(h + 1) * hw]) * scale
        q_nope = qh[:, :QK_NOPE]
        q_rope = _rope128(qh[:, QK_NOPE:], cos, sin)
        if absorbed:
            qlat_ref[h] = _dot_t(q_nope.astype(BF16), wk_ref[:, h * QK_NOPE:(h + 1) * QK_NOPE]).astype(BF16)
            qrope_ref[h] = q_rope[:, :QK_ROPE].astype(BF16)
        else:
            q_ref[:, h * hw:h * hw + QK_NOPE] = q_nope.astype(BF16)
            q_ref[:, h * hw + QK_NOPE:(h + 1) * hw] = q_rope.astype(BF16)
            k_ref[:, h * hw:h * hw + QK_NOPE] = _dot(ckv_b, wk_ref[:, h * QK_NOPE:(h + 1) * QK_NOPE]).astype(BF16)
            k_ref[:, h * hw + QK_NOPE:(h + 1) * hw] = kr_b


def _mla_prep(proj, cos_t, sin_t, q_g, kv_g, wq_r, wk, wv, *, row0, nrows, offs, absorbed):
    qlora, kvlora = q_g.shape[0], kv_g.shape[0]
    tm = _pick(nrows, (512, 256, 128))
    assert row0 % tm == 0
    rb0 = row0 // tm
    hw = QK_NOPE + LANES
    scale = 1.0 / math.sqrt(QK_NOPE + QK_ROPE)
    const = lambda i: (0, 0)
    in_specs = [pl.BlockSpec((tm, qlora), lambda i: (rb0 + i, offs["cq"] // qlora)),
                pl.BlockSpec((tm, kvlora), lambda i: (rb0 + i, offs["ckv"] // kvlora)),
                pl.BlockSpec((tm, LANES), lambda i: (rb0 + i, offs["kr"] // LANES)),
                pl.BlockSpec((tm, LANES), lambda i: (rb0 + i, 0)),
                pl.BlockSpec((tm, LANES), lambda i: (rb0 + i, 0)),
                pl.BlockSpec((1, qlora), const), pl.BlockSpec((1, kvlora), const),
                pl.BlockSpec(wq_r.shape, const), pl.BlockSpec(wk.shape, const), pl.BlockSpec(wv.shape, const)]
    out_specs = [pl.BlockSpec((tm, kvlora), lambda i: (i, 0)), pl.BlockSpec((tm, LANES), lambda i: (i, 0))]
    out_shape = [jax.ShapeDtypeStruct((nrows, kvlora), F32), jax.ShapeDtypeStruct((nrows, LANES), F32)]
    if absorbed:
        out_specs += [pl.BlockSpec((MLA_HEADS, tm, kvlora), lambda i: (0, i, 0)),
                      pl.BlockSpec((MLA_HEADS, tm, QK_ROPE), lambda i: (0, i, 0))]
        out_shape += [jax.ShapeDtypeStruct((MLA_HEADS, nrows, kvlora), BF16),
                      jax.ShapeDtypeStruct((MLA_HEADS, nrows, QK_ROPE), BF16)]
    else:
        out_specs += [pl.BlockSpec((tm, MLA_HEADS * hw), lambda i: (i, 0)),
                      pl.BlockSpec((tm, MLA_HEADS * hw), lambda i: (i, 0)),
                      pl.BlockSpec((tm, MLA_HEADS * V_HEAD), lambda i: (i, 0))]
        out_shape += [jax.ShapeDtypeStruct((nrows, MLA_HEADS * hw), BF16),
                      jax.ShapeDtypeStruct((nrows, MLA_HEADS * hw), BF16),
                      jax.ShapeDtypeStruct((nrows, MLA_HEADS * V_HEAD), BF16)]
    return pl.pallas_call(
        functools.partial(_mla_prep_body, absorbed=absorbed, scale=scale),
        grid=(nrows // tm,),
        in_specs=in_specs, out_specs=out_specs, out_shape=out_shape,
        compiler_params=_params("arbitrary"),
    )(proj, proj, proj, cos_t, sin_t, q_g.reshape(1, -1), kv_g.reshape(1, -1), wq_r, wk, wv)


ATT_SUB = 256


def _attn_tile(q_sub, k_ref, v_ref, s_scr, p_scr, nk):
    sub = ATT_SUB
    nfull = nk - sub
    shift = CHUNK.bit_length() - 1

    def fold(x, op):
        return op(x.reshape(sub // 8, 8, sub), axis=0)

    if nfull:
        s_scr[0:nfull, :] = _dot_t(k_ref[0, 0:nfull, :], q_sub)
    krow = lax.broadcasted_iota(I32, (sub, sub), 0) >> shift
    qcol = lax.broadcasted_iota(I32, (sub, sub), 1) >> shift
    s_scr[nfull:nk, :] = jnp.where(krow <= qcol, _dot_t(k_ref[0, nfull:nk, :], q_sub), -jnp.inf)

    mrun = None
    for j in range(nk // sub):
        f = fold(s_scr[j * sub:(j + 1) * sub, :], jnp.max)
        mrun = f if mrun is None else jnp.maximum(mrun, f)
    m = jnp.max(mrun, axis=0, keepdims=True)
    lrun = None
    for j in range(nk // sub):
        pt = jnp.exp(s_scr[j * sub:(j + 1) * sub, :] - m)
        f = fold(pt, jnp.sum)
        lrun = f if lrun is None else lrun + f
        p_scr[j * sub:(j + 1) * sub, :] = pt.astype(BF16)
    l = jnp.sum(lrun, axis=0, keepdims=True)
    acc = lax.dot_general(v_ref[0, 0:nk, :], p_scr[0:nk, :], (((0,), (0,)), ((), ())),
                          preferred_element_type=F32)
    return (acc / l).T


def _attn_body(q_ref, k_ref, v_ref, o_ref, s_scr, p_scr, *, tq, nq):
    qi = pl.program_id(2)
    sub = ATT_SUB
    for c in range(nq):
        @pl.when(qi == c)
        def _():
            for r in range(tq // sub):
                q_sub = q_ref[0, r * sub:(r + 1) * sub, :]
                out = _attn_tile(q_sub, k_ref, v_ref, s_scr, p_scr, c * tq + (r + 1) * sub)
                o_ref[r * sub:(r + 1) * sub, :] = out.astype(o_ref.dtype)


def _attn_prompt(q, k, v, *, nb, seq):
    hw = QK_NOPE + LANES
    tq = min(512, seq // 2)
    assert seq % tq == 0 and tq % ATT_SUB == 0 and ATT_SUB % CHUNK == 0
    nq = seq // tq
    q3 = q.reshape(nb, seq, MLA_HEADS * hw)
    k3 = k.reshape(nb, seq, MLA_HEADS * hw)
    v3 = v.reshape(nb, seq, MLA_HEADS * V_HEAD)
    return pl.pallas_call(
        functools.partial(_attn_body, tq=tq, nq=nq),
        grid=(nb, MLA_HEADS, nq),
        in_specs=[pl.BlockSpec((1, tq, hw), lambda b, h, i: (b, i, h)),
                  pl.BlockSpec((1, seq, hw), lambda b, h, i: (b, 0, h)),
                  pl.BlockSpec((1, seq, V_HEAD), lambda b, h, i: (b, 0, h))],
        out_specs=pl.BlockSpec((tq, V_HEAD), lambda b, h, i: (b * nq + i, h)),
        out_shape=jax.ShapeDtypeStruct((nb * seq, MLA_HEADS * V_HEAD), BF16),
        scratch_shapes=[pltpu.VMEM((seq, ATT_SUB), F32), pltpu.VMEM((seq, ATT_SUB), BF16)],
        compiler_params=_params("arbitrary", "arbitrary", "arbitrary"),
    )(q3, k3, v3)


def _attn_sample_body(ql_ref, qr_ref, pckv_ref, pkr_ref, nckv_ref, nkr_ref, o_ref, *, past, seq):
    nh = ql_ref.shape[0]
    ql = ql_ref[...].reshape(nh * seq, ql_ref.shape[2])
    qr = qr_ref[...].reshape(nh * seq, qr_ref.shape[2])
    pckv = pckv_ref[0].astype(BF16)
    nckv = nckv_ref[...].astype(BF16)
    s_p = _dot_t(ql, pckv) + _dot_t(qr, pkr_ref[0].astype(BF16))
    s_n = _dot_t(ql, nckv) + _dot_t(qr, nkr_ref[:, :QK_ROPE].astype(BF16))
    shift = CHUNK.bit_length() - 1
    q_chunk = (past + lax.broadcasted_iota(I32, (nh * seq, 1), 0) % seq) >> shift
    kp_chunk = lax.broadcasted_iota(I32, (1, past), 1) >> shift
    kn_chunk = (past + lax.broadcasted_iota(I32, (1, seq), 1)) >> shift
    s_p = jnp.where(kp_chunk <= q_chunk, s_p, -jnp.inf)
    s_n = jnp.where(kn_chunk <= q_chunk, s_n, -jnp.inf)
    m = jnp.maximum(jnp.max(s_p, axis=1, keepdims=True), jnp.max(s_n, axis=1, keepdims=True))
    p_p = jnp.exp(s_p - m)
    p_n = jnp.exp(s_n - m)
    l = jnp.sum(p_p, axis=1, keepdims=True) + jnp.sum(p_n, axis=1, keepdims=True)
    o = (_dot(p_p.astype(BF16), pckv) + _dot(p_n.astype(BF16), nckv)) / l
    o_ref[...] = o.reshape(o_ref.shape).astype(o_ref.dtype)


def _attn_sample(qlat, qrope, past_ckv, past_kr, ckv_n, kr_r, *, nb, seq):
    past = past_ckv.shape[1]
    r = past_ckv.shape[2]
    return pl.pallas_call(
        functools.partial(_attn_sample_body, past=past, seq=seq),
        grid=(nb,),
        in_specs=[pl.BlockSpec((MLA_HEADS, seq, r), lambda b: (0, b, 0)),
                  pl.BlockSpec((MLA_HEADS, seq, QK_ROPE), lambda b: (0, b, 0)),
                  pl.BlockSpec((1, past, r), lambda b: (b, 0, 0)),
                  pl.BlockSpec((1, past, QK_ROPE), lambda b: (b, 0, 0)),
                  pl.BlockSpec((seq, r), lambda b: (b, 0)),
                  pl.BlockSpec((seq, LANES), lambda b: (b, 0))],
        out_specs=pl.BlockSpec((MLA_HEADS, seq, r), lambda b: (0, b, 0)),
        out_shape=jax.ShapeDtypeStruct((MLA_HEADS, nb * seq, r), BF16),
        compiler_params=_params("arbitrary"),
    )(qlat, qrope, past_ckv, past_kr, ckv_n, kr_r)


def _uv_body(o_ref, w_ref, y_ref):
    y_ref[...] = _dot(o_ref[0], w_ref[...]).astype(y_ref.dtype)


def _uv_sample(o_lat, wv):
    nh, nrows, r = o_lat.shape
    return pl.pallas_call(
        _uv_body,
        grid=(nh,),
        in_specs=[pl.BlockSpec((1, nrows, r), lambda h: (h, 0, 0)),
                  pl.BlockSpec((r, V_HEAD), lambda h: (0, h))],
        out_specs=pl.BlockSpec((nrows, V_HEAD), lambda h: (0, h)),
        out_shape=jax.ShapeDtypeStruct((nrows, nh * V_HEAD), BF16),
        compiler_params=_params("arbitrary"),
    )(o_lat, wv)


def _merge1_body(ysp_ref, ymp_ref, yss_ref, yms_ref, ws_ref, wm_ref, ga_ref, gb_ref, o_ref, *, npb):
    def run(ys_ref, ym_ref):
        a = _dot(ys_ref[...], ws_ref[...])
        b = _dot(ym_ref[...], wm_ref[...])
        o_ref[...] = (_sigmoid(ga_ref[...]) * a + _sigmoid(gb_ref[...]) * b).astype(o_ref.dtype)

    i = pl.program_id(1)
    pl.when(i < npb)(lambda: run(ysp_ref, ymp_ref))
    pl.when(i >= npb)(lambda: run(yss_ref, yms_ref))


def _merge1(ys_p, ym_p, ys_s, ym_s, w_ssd_out, w_mla_out, proj, offs):
    (tp, inner), ts = ys_p.shape, ys_s.shape[0]
    dm = ym_p.shape[1]
    d = w_ssd_out.shape[1]
    tm = _pick(math.gcd(tp, ts), (512, 256, 128))
    tn = 512
    npb, nsb = tp // tm, ts // tm
    prow = lambda j, i: (jnp.minimum(i, npb - 1), 0)
    srow = lambda j, i: (jnp.maximum(i - npb, 0), 0)
    return pl.pallas_call(
        functools.partial(_merge1_body, npb=npb),
        grid=(d // tn, npb + nsb),
        in_specs=[pl.BlockSpec((tm, inner), prow), pl.BlockSpec((tm, dm), prow),
                  pl.BlockSpec((tm, inner), srow), pl.BlockSpec((tm, dm), srow),
                  pl.BlockSpec((inner, tn), lambda j, i: (0, j)),
                  pl.BlockSpec((dm, tn), lambda j, i: (0, j)),
                  pl.BlockSpec((tm, tn), lambda j, i: (i, offs["ga"] // tn + j)),
                  pl.BlockSpec((tm, tn), lambda j, i: (i, offs["gb"] // tn + j))],
        out_specs=pl.BlockSpec((tm, tn), lambda j, i: (i, j)),
        out_shape=jax.ShapeDtypeStruct((tp + ts, d), BF16),
        compiler_params=_params("arbitrary", "arbitrary"),
    )(ys_p, ym_p, ys_s, ym_s, w_ssd_out, w_mla_out, proj, proj)


def _merge2_body(m_ref, w_ref, xp_ref, xs_ref, g1_ref, sc_ref, sh_ref, ng_ref, wr_ref, x1_ref, h2_ref, lg_ref,
                 *, npb):
    rows, d = xp_ref.shape
    x = jnp.where(pl.program_id(0) < npb, xp_ref[...], xs_ref[...])
    upd = _dot(m_ref[...], w_ref[...]).reshape(rows // MOD_ROWS, MOD_ROWS, d) * g1_ref[...]
    x1 = x + upd.reshape(rows, d)
    x1_ref[...] = x1
    h2 = _modulate(_rms(x1) * ng_ref[...], sc_ref, sh_ref)
    _store_items(h2_ref, _pack_halves(h2))
    lg_ref[...] = _dot3(wr_ref[...], h2, dot=_dot_t)


def _merge2(merged, w_merge, xp, xs, modg, norm_g, w_router_t):
    d = xp.shape[1]
    t = xp.shape[0] + xs.shape[0]
    assert d == 2 * SUBLANES * LANES
    ne = w_router_t.shape[0]
    tm, npb, prow, srow = _two_stream_rows(xp.shape[0], xs.shape[0], (512, 256, 128))
    ng = tm // MOD_ROWS
    return pl.pallas_call(
        functools.partial(_merge2_body, npb=npb),
        grid=(t // tm,),
        in_specs=[pl.BlockSpec((tm, d), lambda i: (i, 0)),
                  pl.BlockSpec((d, d), lambda i: (0, 0)),
                  pl.BlockSpec((tm, d), prow), pl.BlockSpec((tm, d), srow),
                  pl.BlockSpec((ng, 1, d), lambda i: (i, 0, 2)),
                  pl.BlockSpec((ng, 1, d), lambda i: (i, 0, 4)),
                  pl.BlockSpec((ng, 1, d), lambda i: (i, 0, 3)),
                  pl.BlockSpec((1, d), lambda i: (0, 0)),
                  pl.BlockSpec((ne, d), lambda i: (0, 0))],
        out_specs=[pl.BlockSpec((tm, d), lambda i: (i, 0)),
                   pl.BlockSpec((tm * SUBLANES, LANES), lambda i: (i, 0)),
                   pl.BlockSpec((ne, tm), lambda i: (0, i))],
        out_shape=[jax.ShapeDtypeStruct((t, d), F32), jax.ShapeDtypeStruct((t * SUBLANES, LANES), U32),
                   jax.ShapeDtypeStruct((ne, t), F32)],
        compiler_params=_params("arbitrary"),
    )(merged, w_merge, xp, xs, modg, modg, modg, norm_g.reshape(1, d), w_router_t)


def _route_body(lg_ref, eb_ref, eidx_ref, pos_ref, w_ref, cnt_ref, carry):
    ne, tr = lg_ref.shape
    per_group = ne // N_GROUPS

    @pl.when(pl.program_id(0) == 0)
    def _():
        carry[...] = jnp.zeros_like(carry)

    scores = _sigmoid(lg_ref[...])
    choice = scores + eb_ref[...]
    sub = lax.broadcasted_iota(I32, (per_group, tr), 0)
    gscore, blocks = [], []
    for g in range(N_GROUPS):
        blk = choice[g * per_group:(g + 1) * per_group, :]
        m1 = jnp.max(blk, axis=0, keepdims=True)
        first = jnp.min(jnp.where(blk == m1, sub, per_group), axis=0, keepdims=True)
        m2 = jnp.max(jnp.where(sub == first, -jnp.inf, blk), axis=0, keepdims=True)
        gscore.append(m1 + m2)
        blocks.append(blk)
    masked = []
    for g in range(N_GROUPS):
        rank = jnp.zeros((1, tr), I32)
        for g2 in range(N_GROUPS):
            if g2 == g:
                continue
            beats = (gscore[g2] > gscore[g]) | ((gscore[g2] == gscore[g]) & (g2 < g))
            rank = rank + beats.astype(I32)
        masked.append(jnp.where(rank < TOPK_GROUPS, blocks[g], -jnp.inf))
    cm = jnp.concatenate(masked, axis=0)

    eid = lax.broadcasted_iota(I32, (ne, tr), 0)
    rank = jnp.zeros((ne, tr), I32)
    for e2 in range(ne):
        rowv = cm[e2:e2 + 1, :]
        beats = (rowv > cm) | ((rowv == cm) & (eid > e2))
        rank = rank + beats.astype(I32)
    sel = rank < TOP_K
    wsel = jnp.where(sel, scores, 0.0)
    wfull = wsel / jnp.sum(wsel, axis=0, keepdims=True) * ROUTED_SCALE

    r = lax.broadcasted_iota(I32, (tr, tr), 0)
    c = lax.broadcasted_iota(I32, (tr, tr), 1)
    before = jnp.where(r < c, 1.0, 0.0).astype(BF16)
    self = jnp.where(sel, 1.0, 0.0)
    pos = carry[:, 0:1] + _dot(self.astype(BF16), before)
    carry[...] = carry[...] + jnp.sum(self, axis=1, keepdims=True)
    cnt_ref[...] = carry[...]

    eid_f = eid.astype(F32)
    for k in range(TOP_K):
        pick = sel & (rank == k)
        eidx_ref[k:k + 1, :] = jnp.sum(jnp.where(pick, eid_f, 0.0), axis=0, keepdims=True).astype(I32)
        pos_ref[k:k + 1, :] = jnp.sum(jnp.where(pick, pos, 0.0), axis=0, keepdims=True).astype(I32)
        w_ref[k:k + 1, :] = jnp.sum(jnp.where(pick, wfull, 0.0), axis=0, keepdims=True)


def _route(logits_t, e_bias):
    ne, t = logits_t.shape
    tr = _pick(t, (512, 256, 128))
    return pl.pallas_call(
        _route_body,
        grid=(t // tr,),
        in_specs=[pl.BlockSpec((ne, tr), lambda i: (0, i)), pl.BlockSpec((ne, 1), lambda i: (0, 0))],
        out_specs=[pl.BlockSpec((TOP_K, tr), lambda i: (0, i)), pl.BlockSpec((TOP_K, tr), lambda i: (0, i)),
                   pl.BlockSpec((TOP_K, tr), lambda i: (0, i)), pl.BlockSpec((ne, LANES), lambda i: (0, 0))],
        out_shape=[jax.ShapeDtypeStruct((TOP_K, t), I32), jax.ShapeDtypeStruct((TOP_K, t), I32),
                   jax.ShapeDtypeStruct((TOP_K, t), F32), jax.ShapeDtypeStruct((ne, LANES), F32)],
        scratch_shapes=[pltpu.VMEM((ne, LANES), F32)],
        compiler_params=_params("arbitrary"),
    )(logits_t, e_bias.reshape(ne, 1))


def _row_copy(src, s, dst, d, sem):
    rows = lambda i: pl.ds(pl.multiple_of(i * SUBLANES, SUBLANES), SUBLANES)
    return pltpu.make_async_copy(src.at[rows(s), :], dst.at[rows(d), :], sem)


def _load_items(ref, start, n):
    return jnp.concatenate([ref[pl.ds(start * SUBLANES + c, n, stride=SUBLANES), :] for c in range(SUBLANES)],
                           axis=1)


def _store_items(ref, val):
    n = val.shape[0]
    for c in range(SUBLANES):
        ref[pl.ds(c, n, stride=SUBLANES), :] = val[:, c * LANES:(c + 1) * LANES]


def _drain(wait_one, n, group=64):
    assert n % group == 0

    def body(j, c):
        for _ in range(group):
            wait_one()
        return c

    lax.fori_loop(0, n // group, body, 0)


def _dispatch_body(fill_lo_ref, fill_hi_ref, h_ref, dest_ref, hs_ref, dest_s, zrow, sem, dsem, *, tb):
    i = pl.program_id(0)
    cp = pltpu.make_async_copy(dest_ref.at[i], dest_s, dsem)
    cp.start()

    @pl.when(i == 0)
    def _():
        zrow[...] = jnp.zeros_like(zrow)

        def per_expert(fn):
            def body(e, _):
                lax.fori_loop(fill_lo_ref[e], fill_hi_ref[e], lambda s, c: (fn(s), c)[1], 0)
                return 0
            lax.fori_loop(0, fill_lo_ref.shape[0], body, 0)

        per_expert(lambda s: _row_copy(zrow, 0, hs_ref, s, sem).start())
        per_expert(lambda s: _row_copy(zrow, 0, hs_ref, s, sem).wait())

    cp.wait()

    def issue(t, c):
        for k in range(TOP_K):
            _row_copy(h_ref, t, hs_ref, dest_s[t * TOP_K + k], sem).start(priority=k % 2)
        return c

    lax.fori_loop(0, tb, issue, 0, unroll=4)
    _drain(lambda: _row_copy(h_ref, 0, hs_ref, 0, sem).wait(), TOP_K * tb)


def _dispatch(h2, dest_blk, fill_lo, fill_hi, n_slots):
    nblk, n = dest_blk.shape
    tb = n // TOP_K
    return pl.pallas_call(
        functools.partial(_dispatch_body, tb=tb),
        grid_spec=pltpu.PrefetchScalarGridSpec(
            num_scalar_prefetch=2,
            grid=(nblk,),
            in_specs=[pl.BlockSpec((tb * SUBLANES, LANES), lambda i, lo, hi: (i, 0)),
                      pl.BlockSpec(memory_space=pl.ANY)],
            out_specs=pl.BlockSpec(memory_space=pl.ANY),
            scratch_shapes=[pltpu.SMEM((n,), I32), pltpu.VMEM((SUBLANES, LANES), h2.dtype),
                            pltpu.SemaphoreType.DMA, pltpu.SemaphoreType.DMA]),
        out_shape=jax.ShapeDtypeStruct((n_slots * SUBLANES, LANES), h2.dtype),
        compiler_params=_params("arbitrary"),
    )(fill_lo, fill_hi, h2, dest_blk)


def _experts_body(be_ref, nu_ref, x_ref, wg_ref, wu_ref, wd_ref, o_ref, wgu_s, wd_s):
    i = pl.program_id(0)
    hid = wg_ref.shape[2]

    @pl.when(i < nu_ref[0])
    def _():
        @pl.when((i == 0) | (be_ref[i] != be_ref[jnp.maximum(i - 1, 0)]))
        def _():
            wgu_s[:, :hid] = wg_ref[0].astype(BF16)
            wgu_s[:, hid:] = wu_ref[0].astype(BF16)
            wd_s[...] = wd_ref[0].astype(BF16)

        lo, hi = _unpack_halves(_load_items(x_ref, 0, x_ref.shape[0] // SUBLANES))
        x = jnp.concatenate([lo.astype(BF16), hi.astype(BF16)], axis=1)
        gu = _dot(x, wgu_s[...])
        act = (_silu(gu[:, :hid]) * gu[:, hid:]).astype(BF16)
        _store_items(o_ref, _pack_halves(_dot(act, wd_s[...])))


def _experts(hs, block_e, n_used, wg, wu, wd):
    n_slots = hs.shape[0] // SUBLANES
    ne, d, hid = wg.shape
    assert d == 2 * SUBLANES * LANES
    bm = MOE_ROWS
    nblocks = n_slots // bm
    blk = lambda i, be, nu: (jnp.minimum(i, nu[0] - 1), 0)
    return pl.pallas_call(
        _experts_body,
        grid_spec=pltpu.PrefetchScalarGridSpec(
            num_scalar_prefetch=2,
            grid=(nblocks,),
            in_specs=[pl.BlockSpec((bm * SUBLANES, LANES), blk),
                      pl.BlockSpec((1, d, hid), lambda i, be, nu: (be[i], 0, 0)),
                      pl.BlockSpec((1, d, hid), lambda i, be, nu: (be[i], 0, 0)),
                      pl.BlockSpec((1, hid, d), lambda i, be, nu: (be[i], 0, 0))],
            out_specs=pl.BlockSpec((bm * SUBLANES, LANES), blk),
            scratch_shapes=[pltpu.VMEM((d, 2 * hid), BF16), pltpu.VMEM((hid, d), BF16)]),
        out_shape=jax.ShapeDtypeStruct(hs.shape, U32),
        compiler_params=_params("arbitrary"),
    )(block_e, n_used, hs, wg, wu, wd)


def _combine_body(ys_ref, dest_ref, w_ref, h_ref, x1_ref, g2_ref, wgu_ref, wd_ref, fg_ref, op_ref, os_ref,
                  dest_s, gbuf, sem, dsem, *, tb, npb):
    i = pl.program_id(0)
    cp = pltpu.make_async_copy(dest_ref.at[i], dest_s, dsem)
    cp.start()
    cp.wait()

    def issue(t, c):
        for k in range(TOP_K):
            _row_copy(ys_ref, dest_s[t * TOP_K + k], gbuf, k * tb + t, sem).start(priority=k % 2)
        return c

    lax.fori_loop(0, tb, issue, 0, unroll=4)

    hid = wd_ref.shape[0]
    h_lo, h_hi = _unpack_halves(_load_items(h_ref, 0, tb))
    gu = _dot(jnp.concatenate([h_lo.astype(BF16), h_hi.astype(BF16)], axis=1), wgu_ref[...])
    moe = _dot((_silu(gu[:, :hid]) * gu[:, hid:]).astype(BF16), wd_ref[...])

    _drain(lambda: _row_copy(ys_ref, 0, gbuf, 0, sem).wait(), TOP_K * tb)
    w = w_ref[...]
    dh = SUBLANES * LANES
    m_lo, m_hi = moe[:, :dh], moe[:, dh:]
    for k in range(TOP_K):
        lo, hi = _unpack_halves(_load_items(gbuf, k * tb, tb))
        m_lo = m_lo + lo * w[:, k:k + 1]
        m_hi = m_hi + hi * w[:, k:k + 1]
    moe = jnp.concatenate([m_lo, m_hi], axis=1)
    rows, d = moe.shape
    upd = moe.reshape(rows // MOD_ROWS, MOD_ROWS, d) * g2_ref[...]
    x2 = x1_ref[...] + upd.reshape(rows, d)
    y = _rms(x2) * fg_ref[...]

    @pl.when(i < npb)
    def _():
        op_ref[...] = y

    @pl.when(i >= npb)
    def _():
        os_ref[...] = y


def _combine(ys, dest_blk, w_tok, h2, x1, modg, wsh_gu, wsh_d, final_g, *, tp):
    t, d = x1.shape
    nblk, n = dest_blk.shape
    tb = n // TOP_K
    assert tp % tb == 0 and (t - tp) % tb == 0
    npb = tp // tb
    ng = tb // MOD_ROWS
    const = lambda i: (0, 0)
    return pl.pallas_call(
        functools.partial(_combine_body, tb=tb, npb=npb),
        grid=(nblk,),
        in_specs=[pl.BlockSpec(memory_space=pl.ANY),
                  pl.BlockSpec(memory_space=pl.ANY),
                  pl.BlockSpec((tb, TOP_K), lambda i: (i, 0)),
                  pl.BlockSpec((tb * SUBLANES, LANES), lambda i: (i, 0)),
                  pl.BlockSpec((tb, d), lambda i: (i, 0)),
                  pl.BlockSpec((ng, 1, d), lambda i: (i, 0, 5)),
                  pl.BlockSpec(wsh_gu.shape, const), pl.BlockSpec(wsh_d.shape, const),
                  pl.BlockSpec((1, d), const)],
        out_specs=[pl.BlockSpec((tb, d), lambda i: (jnp.minimum(i, npb - 1), 0)),
                   pl.BlockSpec((tb, d), lambda i: (jnp.maximum(i - npb, 0), 0))],
        out_shape=[jax.ShapeDtypeStruct((tp, d), F32), jax.ShapeDtypeStruct((t - tp, d), F32)],
        scratch_shapes=[pltpu.SMEM((n,), I32), pltpu.VMEM((n * SUBLANES, LANES), U32),
                        pltpu.SemaphoreType.DMA, pltpu.SemaphoreType.DMA],
        compiler_params=_params("arbitrary"),
    )(ys, dest_blk, w_tok, h2, x1, modg, wsh_gu, wsh_d, final_g.reshape(1, d))


def _rope_tables(pos):
    half = QK_ROPE // 2
    freqs = ROPE_THETA ** (-jnp.arange(half, dtype=F32) / half)
    ang = pos.astype(F32)[:, None] * freqs[None, :]
    cos, sin = jnp.cos(ang), jnp.sin(ang)
    return (jnp.concatenate([cos, cos, cos, cos], axis=1),
            jnp.concatenate([-sin, sin, -sin, sin], axis=1))


def _blocked(a, tb):
    k, t = a.shape
    return a.T.reshape(t // tb, tb * k)


def kernel(x_prompt, x_sample, c_prompt, c_sample, cache_conv, state_ssm, cache_ckv, cache_kr, w_ada, b_ada,
           norm1_g, norm2_g, w_in, conv_w, conv_b, dt_bias, a_log, d_skip, ssd_norm_g, w_ssd_out, q_norm_g,
           w_uq, kv_norm_g, w_uk, w_uv, w_mla_out, w_merge_out, w_router, e_bias, w_exp_gate, w_exp_up,
           w_exp_down, w_sh_gate, w_sh_up, w_sh_down, final_norm_g):
    depth = w_in.shape[0]
    assert depth == 1
    bp, seq, d = x_prompt.shape
    bs, lseq, _ = x_sample.shape
    assert lseq == MOD_ROWS and seq % MOD_ROWS == 0
    tp, ts = bp * seq, bs * lseq
    t_all = tp + ts
    nheads = dt_bias.shape[1]
    inner = nheads * SSD_HEAD_DIM
    cdim = conv_w.shape[2]
    gn = SSD_GROUPS * SSD_STATE
    qlora, kvlora = q_norm_g.shape[1], kv_norm_g.shape[1]
    ne = w_router.shape[2]
    assert nheads == 64 and 2 * nheads == LANES

    w = w_in[0]
    o_xbc, o_dt = inner, inner + cdim
    o_cq = o_dt + nheads
    o_ckv = o_cq + qlora
    o_kr = o_ckv + kvlora
    o_gate = o_kr + QK_ROPE
    perm = np.concatenate([np.arange(0, nheads, 2), np.arange(1, nheads, 2)])
    zc = lambda n: jnp.zeros((d, n), w.dtype)
    cols = [w[:, :inner], w[:, o_gate:], w[:, o_xbc:o_dt], w[:, o_cq:o_ckv], w[:, o_ckv:o_kr],
            w[:, o_dt:o_cq][:, perm], zc(LANES - nheads), w[:, o_kr:o_gate], zc(LANES - QK_ROPE)]
    used = inner + 2 * d + cdim + qlora + kvlora + 2 * LANES
    total = -(-used // 512) * 512
    cols.append(zc(total - used))
    w_r = jnp.concatenate(cols, axis=1).astype(BF16)
    offs = {"z": 0, "ga": inner, "gb": inner + d, "x": inner + 2 * d, "bc": 2 * inner + 2 * d}
    offs["cq"] = offs["bc"] + 2 * gn
    offs["ckv"] = offs["cq"] + qlora
    offs["dt"] = offs["ckv"] + kvlora
    offs["kr"] = offs["dt"] + LANES

    hw = QK_NOPE + LANES
    wq_r = jnp.pad(w_uq[0], ((0, 0), (0, 0), (0, hw - QK_NOPE - QK_ROPE))).reshape(qlora, MLA_HEADS * hw).astype(BF16)
    wk = w_uk[0].reshape(kvlora, MLA_HEADS * QK_NOPE).astype(BF16)
    wv = w_uv[0].reshape(kvlora, MLA_HEADS * V_HEAD).astype(BF16)
    pad_l = lambda v: jnp.pad(v[perm], (0, LANES - nheads)).reshape(1, LANES)
    dtb = pad_l(dt_bias[0])
    a_neg = pad_l(-jnp.exp(a_log[0]))
    dskip = jnp.repeat(d_skip[0], SSD_HEAD_DIM).reshape(1, inner)

    c_all = jnp.concatenate([c_prompt, c_sample], axis=0)
    mod = _ada(c_all, w_ada[0], b_ada[0])
    grp = np.concatenate([np.repeat(np.arange(bp), seq // MOD_ROWS), bp + np.arange(bs)])
    modg = mod[grp].reshape(t_all // MOD_ROWS, 1, 6 * d)

    xp, xs = x_prompt.reshape(tp, d), x_sample.reshape(ts, d)
    proj = _inproj(xp, xs, modg, norm1_g[0], w_r, d)

    ssd_args = (conv_w[0], conv_b[0].reshape(1, cdim), dtb, a_neg, dskip, ssd_norm_g[0].reshape(1, inner))
    ys_p, conv_p, ssm_p = _ssd(proj, jnp.zeros((bp, CONV_W - 1, cdim), F32),
                               jnp.zeros((1, bp) + state_ssm.shape[2:], F32), *ssd_args,
                               nb=bp, seq=seq, row0=0, offs=offs, inner=inner)
    ys_s, conv_s, ssm_s = _ssd(proj, cache_conv[0], state_ssm,
                               *ssd_args, nb=bs, seq=lseq, row0=tp, offs=offs, inner=inner)

    past = cache_ckv.shape[2]
    cos_p, sin_p = _rope_tables(jnp.arange(seq))
    cos_s, sin_s = _rope_tables(past + jnp.arange(lseq))
    cos_t = jnp.concatenate([jnp.tile(cos_p, (bp, 1)), jnp.tile(cos_s, (bs, 1))], axis=0)
    sin_t = jnp.concatenate([jnp.tile(sin_p, (bp, 1)), jnp.tile(sin_s, (bs, 1))], axis=0)
    prep = functools.partial(_mla_prep, proj, cos_t, sin_t, q_norm_g[0], kv_norm_g[0], wq_r, wk, wv, offs=offs)
    ckv_p, kr_p, q_p, k_p, v_p = prep(row0=0, nrows=tp, absorbed=False)
    ckv_s, kr_s, qlat, qrope = prep(row0=tp, nrows=ts, absorbed=True)
    ym_p = _attn_prompt(q_p, k_p, v_p, nb=bp, seq=seq)
    o_lat = _attn_sample(qlat, qrope, cache_ckv[0], cache_kr[0], ckv_s, kr_s, nb=bs, seq=lseq)
    ym_s = _uv_sample(o_lat, wv)

    merged = _merge1(ys_p, ym_p, ys_s, ym_s, w_ssd_out[0].astype(BF16), w_mla_out[0].astype(BF16), proj, offs)
    x1, h2, logits_t = _merge2(merged, w_merge_out[0].astype(BF16), xp, xs, modg, norm2_g[0], w_router[0].T)

    eidx, pos, w_sel, cnt = _route(logits_t, e_bias[0])
    bm = MOE_ROWS
    counts = cnt[:, 0].astype(I32)
    padded = (counts + bm - 1) // bm * bm
    pad_end = jnp.cumsum(padded)
    pad_start = pad_end - padded
    nblocks = -(-(t_all * TOP_K) // bm) + ne
    onehot = eidx[:, :, None] == jnp.arange(ne, dtype=I32)
    dest = jnp.sum(jnp.where(onehot, pad_start, 0), axis=-1) + pos
    blk_start = jnp.arange(nblocks, dtype=I32) * bm
    block_e = jnp.minimum(jnp.sum(pad_end[None, :] <= blk_start[:, None], axis=1), ne - 1).astype(I32)
    n_used = (pad_end[-1:] // bm).astype(I32)

    tb_d = _pick(t_all, (256, 128))
    hs = _dispatch(h2, _blocked(dest, tb_d), (pad_start + counts).astype(I32), pad_end.astype(I32), nblocks * bm)
    ys = _experts(hs, block_e, n_used, w_exp_gate[0], w_exp_up[0], w_exp_down[0])
    wsh_gu = jnp.concatenate([w_sh_gate[0], w_sh_up[0]], axis=1).astype(BF16)
    tb_c = _pick(math.gcd(tp, ts), (256, 128))
    y_p, y_s = _combine(ys, _blocked(dest, tb_c), w_sel.T, h2, x1, modg, wsh_gu, w_sh_down[0].astype(BF16),
                        final_norm_g, tp=tp)

    r5 = lambda a, n, l: a.reshape(1, n, l, a.shape[-1])
    return (y_p.reshape(bp, seq, d), y_s.reshape(bs, lseq, d),
            conv_p[None], ssm_p,
            r5(ckv_p, bp, seq), r5(kr_p[:, :QK_ROPE], bp, seq),
            conv_s[None], ssm_s,
            r5(ckv_s, bs, lseq), r5(kr_s[:, :QK_ROPE], bs, lseq))
```

```python
import functools
import math

import numpy as np
import jax
import jax.numpy as jnp
from jax import lax
from jax.experimental import pallas as pl
from jax.experimental.pallas import tpu as pltpu

F32 = jnp.float32
BF16 = jnp.bfloat16
I32 = jnp.int32
U32 = jnp.uint32

EPS = 1e-6
CHUNK = 64
SSD_HEAD_DIM = 64
SSD_GROUPS = 8
SSD_STATE = 128
CONV_W = 4
MLA_HEADS = 16
QK_NOPE = 128
QK_ROPE = 64
V_HEAD = 128
ROPE_THETA = 10000.0
TOP_K = 8
N_GROUPS = 8
TOPK_GROUPS = 4
ROUTED_SCALE = 2.5

LANES = 128
SUBLANES = 8
MOD_ROWS = 32
MOE_ROWS = 512
VMEM_LIMIT = 56 * 1024 * 1024


def _params(*sem):
    return pltpu.CompilerParams(dimension_semantics=sem, vmem_limit_bytes=VMEM_LIMIT)


def _pick(n, cands):
    for c in cands:
        if n % c == 0:
            return c
    raise ValueError(f"no tile in {cands} divides {n}")


def _sigmoid(x):
    return 1.0 / (1.0 + jnp.exp(-x))


def _silu(x):
    return x * _sigmoid(x)


def _dot(a, b):
    return jnp.dot(a, b, preferred_element_type=F32)


def _dot_t(a, b):
    return lax.dot_general(a, b, (((1,), (1,)), ((), ())), preferred_element_type=F32)


def _split(x, n):
    parts = []
    for _ in range(n - 1):
        p = x.astype(BF16)
        parts.append(p)
        x = x - p.astype(F32)
    parts.append(x.astype(BF16))
    return parts


def _dot_exact_rhs(a, b_bf16, n=3):
    acc = None
    for p in _split(a, n):
        t = _dot(p, b_bf16)
        acc = t if acc is None else acc + t
    return acc


def _dot3(a, b, dot=_dot):
    ah, al = _split(a, 2)
    bh, bl = _split(b, 2)
    return dot(ah, bh) + (dot(ah, bl) + dot(al, bh))


def _pack_halves(x):
    h = x.shape[1] // 2
    bits = lambda v: lax.bitcast_convert_type(v.astype(BF16).astype(F32), U32)
    return (bits(x[:, :h]) >> 16) | bits(x[:, h:])


def _unpack_halves(p):
    lo = lax.bitcast_convert_type(p << 16, F32)
    hi = lax.bitcast_convert_type(p & jnp.uint32(0xFFFF0000), F32)
    return lo, hi


def _rms(x):
    return x * lax.rsqrt(jnp.mean(x * x, axis=-1, keepdims=True) + EPS)


def _modulate(y, sc_ref, sh_ref):
    rows, d = y.shape
    y3 = y.reshape(rows // MOD_ROWS, MOD_ROWS, d)
    return (y3 * (1.0 + sc_ref[...]) + sh_ref[...]).reshape(rows, d)


def _ada_body(c_ref, w_ref, b_ref, o_ref):
    o_ref[...] = _dot3(_silu(c_ref[...]), w_ref[...]) + b_ref[...]


def _ada(c_all, w_ada, b_ada):
    nb, d = c_all.shape
    n = w_ada.shape[1]
    tn = _pick(n, (1024, 512, 256, 128))
    return pl.pallas_call(
        _ada_body,
        grid=(n // tn,),
        in_specs=[pl.BlockSpec((nb, d), lambda j: (0, 0)),
                  pl.BlockSpec((d, tn), lambda j: (0, j)),
                  pl.BlockSpec((1, tn), lambda j: (0, j))],
        out_specs=pl.BlockSpec((nb, tn), lambda j: (0, j)),
        out_shape=jax.ShapeDtypeStruct((nb, n), F32),
        compiler_params=_params("arbitrary"),
    )(c_all, w_ada, b_ada.reshape(1, n))


def _two_stream_rows(tp, ts, cands):
    tm = _pick(math.gcd(tp, ts), cands)
    npb = tp // tm
    return tm, npb, (lambda i, *_: (jnp.minimum(i, npb - 1), 0)), (lambda i, *_: (jnp.maximum(i - npb, 0), 0))


def _inproj_body(xp_ref, xs_ref, sc_ref, sh_ref, g_ref, w_ref, o_ref, h_scr, *, npb):
    @pl.when(pl.program_id(1) == 0)
    def _():
        x = jnp.where(pl.program_id(0) < npb, xp_ref[...], xs_ref[...])
        h = _modulate(_rms(x) * g_ref[...], sc_ref, sh_ref)
        h_scr[...] = h.astype(BF16)

    o_ref[...] = _dot(h_scr[...], w_ref[...])


def _inproj(xp, xs, modg, norm_g, w_r, d):
    t = xp.shape[0] + xs.shape[0]
    n = w_r.shape[1]
    tm, npb, prow, srow = _two_stream_rows(xp.shape[0], xs.shape[0], (1024, 512, 256, 128))
    tn = 512
    ng = tm // MOD_ROWS
    return pl.pallas_call(
        functools.partial(_inproj_body, npb=npb),
        grid=(t // tm, n // tn),
        in_specs=[pl.BlockSpec((tm, d), prow), pl.BlockSpec((tm, d), srow),
                  pl.BlockSpec((ng, 1, d), lambda i, j: (i, 0, 1)),
                  pl.BlockSpec((ng, 1, d), lambda i, j: (i, 0, 0)),
                  pl.BlockSpec((1, d), lambda i, j: (0, 0)),
                  pl.BlockSpec((d, tn), lambda i, j: (0, j))],
        out_specs=pl.BlockSpec((tm, tn), lambda i, j: (i, j)),
        out_shape=jax.ShapeDtypeStruct((t, n), F32),
        scratch_shapes=[pltpu.VMEM((tm, d), BF16)],
        compiler_params=_params("arbitrary", "arbitrary"),
    )(xp, xs, modg, modg, norm_g.reshape(1, d), w_r)


def _ssd_body(z_ref, x_ref, bc_ref, dt_ref, conv0_ref, h0_ref, cw_ref, cb_ref, dtb_ref, a_ref,
              dskip_ref, ng_ref, y_ref, nconv_ref, nssm_ref,
              carry_x, carry_bc, xp_x, xp_bc, state, y_scr, *, lb, inner):
    q = CHUNK
    c = pl.program_id(1)
    nc = pl.num_programs(1)
    gn = SSD_GROUPS * SSD_STATE

    @pl.when(c == 0)
    def _():
        carry_x[...] = jnp.zeros_like(carry_x)
        carry_bc[...] = jnp.zeros_like(carry_bc)
        carry_x[8 - (CONV_W - 1):8, :] = conv0_ref[0, :, :inner]
        carry_bc[8 - (CONV_W - 1):8, :] = conv0_ref[0, :, inner:]
        state[...] = h0_ref[0, 0].reshape(state.shape)

    def pad_rows(v):
        if lb == q:
            return v
        return jnp.concatenate([v, jnp.zeros((q - lb, v.shape[1]), v.dtype)], axis=0)

    def conv(raw, carry, xp, w_lo, w_hi):
        xp[0:8, :] = carry[...]
        xp[8:8 + q, :] = raw
        acc = cb_ref[:, w_lo:w_hi]
        for j in range(CONV_W):
            acc = acc + xp[8 - j:8 - j + q, :] * cw_ref[CONV_W - 1 - j:CONV_W - j, w_lo:w_hi]
        return _silu(acc)

    x_raw = pad_rows(x_ref[...])
    bc_raw = pad_rows(bc_ref[...])
    xc = conv(x_raw, carry_x, xp_x, 0, inner)
    bcc = conv(bc_raw, carry_bc, xp_bc, inner, inner + 2 * gn)
    if lb == q:
        carry_x[...] = x_raw[q - 8:q, :]
        carry_bc[...] = bc_raw[q - 8:q, :]

    @pl.when(c == nc - 1)
    def _():
        nconv_ref[0, :, :inner] = x_raw[lb - (CONV_W - 1):lb, :]
        nconv_ref[0, :, inner:] = bc_raw[lb - (CONV_W - 1):lb, :]

    dtr = dt_ref[...] + dtb_ref[...]
    dtv = jnp.maximum(dtr, 0.0) + jnp.log1p(jnp.exp(-jnp.abs(dtr)))
    dtv = pad_rows(dtv)
    la = dtv * a_ref[...]

    row = lax.broadcasted_iota(I32, (q, q), 0)
    col = lax.broadcasted_iota(I32, (q, q), 1)
    tri = jnp.where(row >= col, 1.0, 0.0).astype(BF16)
    parts = _split(la, 3)
    a_cum = _dot(tri, parts[0]) + _dot(tri, parts[1]) + _dot(tri, parts[2])
    a_last = a_cum[q - 1:q, :]

    r2 = lax.broadcasted_iota(I32, (q, 2 * q), 0)
    c2 = lax.broadcasted_iota(I32, (q, 2 * q), 1)
    u_e = jnp.where((c2 < q) & (r2 <= c2), 1.0, 0.0).astype(BF16)
    u_o = jnp.where((c2 >= q) & (r2 <= c2 - q), 1.0, 0.0).astype(BF16)
    i_e = jnp.where(r2 == c2, 1.0, 0.0).astype(BF16)
    i_o = jnp.where(r2 == c2 - q, 1.0, 0.0).astype(BF16)
    la_t = la.T
    dt_t = dtv.T
    hp = la_t.shape[0] // 4
    acol = _dot_exact_rhs(la_t[0:hp], u_e) + _dot_exact_rhs(la_t[hp:2 * hp], u_o)
    dtrow = _dot_exact_rhs(dt_t[0:hp], i_e) + _dot_exact_rhs(dt_t[hp:2 * hp], i_o)
    w_t = (dtv * jnp.exp(a_last - a_cum)).T
    dec_b = jnp.broadcast_to(jnp.exp(jnp.sum(la_t, axis=1, keepdims=True)), (la_t.shape[0], SSD_STATE))

    lane = lax.broadcasted_iota(I32, (q, 2 * q), 1)
    causal2 = lax.broadcasted_iota(I32, (q, 2 * q), 0) >= jnp.where(lane < q, lane, lane - q)
    first_half = lane < q
    rr = lax.broadcasted_iota(I32, (2 * SSD_HEAD_DIM, 2 * SSD_HEAD_DIM), 0)
    cc = lax.broadcasted_iota(I32, (2 * SSD_HEAD_DIM, 2 * SSD_HEAD_DIM), 1)
    bd_mask = (rr < SSD_HEAD_DIM) == (cc < SSD_HEAD_DIM)
    top_rows_q = lax.broadcasted_iota(I32, (2 * SSD_HEAD_DIM, q), 0) < SSD_HEAD_DIM
    top_rows_n = lax.broadcasted_iota(I32, (2 * SSD_HEAD_DIM, SSD_STATE), 0) < SSD_HEAD_DIM

    pairs_per_group = (inner // SSD_HEAD_DIM) // SSD_GROUPS // 2
    for g in range(SSD_GROUPS):
        b_g = bcc[:, g * SSD_STATE:(g + 1) * SSD_STATE].astype(BF16)
        c_g = bcc[:, gn + g * SSD_STATE:gn + (g + 1) * SSD_STATE].astype(BF16)
        cb2 = _dot_t(c_g, jnp.concatenate([b_g, b_g], axis=0))
        for kk in range(pairs_per_group):
            k = g * pairs_per_group + kk
            lo, hi = k * 2 * SSD_HEAD_DIM, (k + 1) * 2 * SSD_HEAD_DIM
            xp = xc[:, lo:hi]
            arow = jnp.where(first_half, a_cum[:, k:k + 1], a_cum[:, hp + k:hp + k + 1])
            seg = jnp.where(causal2, arow - acol[k:k + 1, :], -jnp.inf)
            m = (jnp.exp(seg) * cb2 * dtrow[k:k + 1, :]).astype(BF16)
            xbd = jnp.where(bd_mask, jnp.concatenate([xp, xp], axis=0), 0.0).astype(BF16)
            y_diag = _dot(m, xbd)
            s_k = state[lo:hi, :]
            y_off = _dot_t(c_g, s_k.astype(BF16)) * jnp.exp(arow)
            y_scr[:, lo:hi] = y_diag + y_off + xp * dskip_ref[:, lo:hi]
            w2 = jnp.where(top_rows_q, w_t[k:k + 1, :], w_t[hp + k:hp + k + 1, :])
            contrib = _dot((xp.T * w2).astype(BF16), b_g)
            d_k = jnp.where(top_rows_n, dec_b[k:k + 1, :], dec_b[hp + k:hp + k + 1, :])
            state[lo:hi, :] = s_k * d_k + contrib

    @pl.when(c == nc - 1)
    def _():
        nssm_ref[0, 0] = state[...].reshape(nssm_ref.shape[2:])

    y = y_scr[0:lb, :] * _silu(z_ref[...])
    y_ref[...] = (_rms(y) * ng_ref[...]).astype(y_ref.dtype)


def _ssd(proj, conv0, h0, conv_w, conv_b, dtb, a_neg, dskip, norm_g, *, nb, seq, row0, offs, inner):
    q = CHUNK
    lb = min(seq, q)
    assert seq % lb == 0 and row0 % lb == 0 and lb % 16 == 0
    nc = seq // lb
    rb0 = row0 // lb
    cdim = conv_w.shape[1]
    hp_rows = inner
    state_shape = (inner // SSD_HEAD_DIM, SSD_HEAD_DIM, SSD_STATE)
    assert h0.shape == (1, nb) + state_shape
    kern = functools.partial(_ssd_body, lb=lb, inner=inner)
    rows = lambda b, c: rb0 + b * nc + c
    in_specs = [
        pl.BlockSpec((lb, inner), lambda b, c: (rows(b, c), offs["z"] // inner)),
        pl.BlockSpec((lb, inner), lambda b, c: (rows(b, c), offs["x"] // inner)),
        pl.BlockSpec((lb, cdim - inner), lambda b, c: (rows(b, c), offs["bc"] // (cdim - inner))),
        pl.BlockSpec((lb, LANES), lambda b, c: (rows(b, c), offs["dt"] // LANES)),
        pl.BlockSpec((1, CONV_W - 1, cdim), lambda b, c: (b, 0, 0)),
        pl.BlockSpec((1, 1) + state_shape, lambda b, c: (0, b, 0, 0, 0)),
        pl.BlockSpec((CONV_W, cdim), lambda b, c: (0, 0)),
        pl.BlockSpec((1, cdim), lambda b, c: (0, 0)),
        pl.BlockSpec((1, LANES), lambda b, c: (0, 0)),
        pl.BlockSpec((1, LANES), lambda b, c: (0, 0)),
        pl.BlockSpec((1, inner), lambda b, c: (0, 0)),
        pl.BlockSpec((1, inner), lambda b, c: (0, 0)),
    ]
    args = [proj, proj, proj, proj, conv0, h0, conv_w, conv_b, dtb, a_neg, dskip, norm_g]
    return pl.pallas_call(
        kern,
        grid=(nb, nc),
        in_specs=in_specs,
        out_specs=[pl.BlockSpec((lb, inner), lambda b, c: (b * nc + c, 0)),
                   pl.BlockSpec((1, CONV_W - 1, cdim), lambda b, c: (b, 0, 0)),
                   pl.BlockSpec((1, 1) + state_shape, lambda b, c: (0, b, 0, 0, 0))],
        out_shape=[jax.ShapeDtypeStruct((nb * seq, inner), BF16),
                   jax.ShapeDtypeStruct((nb, CONV_W - 1, cdim), F32),
                   jax.ShapeDtypeStruct((1, nb) + state_shape, F32)],
        scratch_shapes=[pltpu.VMEM((8, inner), F32), pltpu.VMEM((8, cdim - inner), F32),
                        pltpu.VMEM((8 + q, inner), F32), pltpu.VMEM((8 + q, cdim - inner), F32),
                        pltpu.VMEM((hp_rows, SSD_STATE), F32), pltpu.VMEM((q, inner), F32)],
        compiler_params=_params("arbitrary", "arbitrary"),
    )(*args)


def _rope128(x, cos, sin):
    lane = lax.broadcasted_iota(I32, x.shape, 1)
    half = QK_ROPE // 2
    swapped = jnp.where((lane % QK_ROPE) < half, pltpu.roll(x, LANES - half, 1), pltpu.roll(x, half, 1))
    return x * cos + swapped * sin


def _mla_prep_body(cq_ref, ckv_ref, kr_ref, cos_ref, sin_ref, qg_ref, kvg_ref, wq_ref, wk_ref, wv_ref,
                   *out_refs, absorbed, scale):
    cos, sin = cos_ref[...], sin_ref[...]
    ckv_n = _rms(ckv_ref[...]) * kvg_ref[...]
    kr_r = _rope128(kr_ref[...], cos, sin)
    qn = (_rms(cq_ref[...]) * qg_ref[...]).astype(BF16)
    hw = QK_NOPE + LANES
    if absorbed:
        ckvn_ref, krr_ref, qlat_ref, qrope_ref = out_refs
    else:
        ckvn_ref, krr_ref, q_ref, k_ref, v_ref = out_refs
        ckv_b = ckv_n.astype(BF16)
        v_ref[...] = _dot(ckv_b, wv_ref[...]).astype(BF16)
        kr_b = kr_r.astype(BF16)
    ckvn_ref[...] = ckv_n
    krr_ref[...] = kr_r
    for h in range(MLA_HEADS):
        qh = _dot(qn, wq_ref[:, h * hw:(h + 1) * hw]) * scale
        q_nope = qh[:, :QK_NOPE]
        q_rope = _rope128(qh[:, QK_NOPE:], cos, sin)
        if absorbed:
            qlat_ref[h] = _dot_t(q_nope.astype(BF16), wk_ref[:, h * QK_NOPE:(h + 1) * QK_NOPE]).astype(BF16)
            qrope_ref[h] = q_rope[:, :QK_ROPE].astype(BF16)
        else:
            q_ref[:, h * hw:h * hw + QK_NOPE] = q_nope.astype(BF16)
            q_ref[:, h * hw + QK_NOPE:(h + 1) * hw] = q_rope.astype(BF16)
            k_ref[:, h * hw:h * hw + QK_NOPE] = _dot(ckv_b, wk_ref[:, h * QK_NOPE:(h + 1) * QK_NOPE]).astype(BF16)
            k_ref[:, h * hw + QK_NOPE:(h + 1) * hw] = kr_b


def _mla_prep(proj, cos_t, sin_t, q_g, kv_g, wq_r, wk, wv, *, row0, nrows, offs, absorbed):
    qlora, kvlora = q_g.shape[0], kv_g.shape[0]
    tm = _pick(nrows, (512, 256, 128))
    assert row0 % tm == 0
    rb0 = row0 // tm
    hw = QK_NOPE + LANES
    scale = 1.0 / math.sqrt(QK_NOPE + QK_ROPE)
    const = lambda i: (0, 0)
    in_specs = [pl.BlockSpec((tm, qlora), lambda i: (rb0 + i, offs["cq"] // qlora)),
                pl.BlockSpec((tm, kvlora), lambda i: (rb0 + i, offs["ckv"] // kvlora)),
                pl.BlockSpec((tm, LANES), lambda i: (rb0 + i, offs["kr"] // LANES)),
                pl.BlockSpec((tm, LANES), lambda i: (rb0 + i, 0)),
                pl.BlockSpec((tm, LANES), lambda i: (rb0 + i, 0)),
                pl.BlockSpec((1, qlora), const), pl.BlockSpec((1, kvlora), const),
                pl.BlockSpec(wq_r.shape, const), pl.BlockSpec(wk.shape, const), pl.BlockSpec(wv.shape, const)]
    out_specs = [pl.BlockSpec((tm, kvlora), lambda i: (i, 0)), pl.BlockSpec((tm, LANES), lambda i: (i, 0))]
    out_shape = [jax.ShapeDtypeStruct((nrows, kvlora), F32), jax.ShapeDtypeStruct((nrows, LANES), F32)]
    if absorbed:
        out_specs += [pl.BlockSpec((MLA_HEADS, tm, kvlora), lambda i: (0, i, 0)),
                      pl.BlockSpec((MLA_HEADS, tm, QK_ROPE), lambda i: (0, i, 0))]
        out_shape += [jax.ShapeDtypeStruct((MLA_HEADS, nrows, kvlora), BF16),
                      jax.ShapeDtypeStruct((MLA_HEADS, nrows, QK_ROPE), BF16)]
    else:
        out_specs += [pl.BlockSpec((tm, MLA_HEADS * hw), lambda i: (i, 0)),
                      pl.BlockSpec((tm, MLA_HEADS * hw), lambda i: (i, 0)),
                      pl.BlockSpec((tm, MLA_HEADS * V_HEAD), lambda i: (i, 0))]
        out_shape += [jax.ShapeDtypeStruct((nrows, MLA_HEADS * hw), BF16),
                      jax.ShapeDtypeStruct((nrows, MLA_HEADS * hw), BF16),
                      jax.ShapeDtypeStruct((nrows, MLA_HEADS * V_HEAD), BF16)]
    return pl.pallas_call(
        functools.partial(_mla_prep_body, absorbed=absorbed, scale=scale),
        grid=(nrows // tm,),
        in_specs=in_specs, out_specs=out_specs, out_shape=out_shape,
        compiler_params=_params("arbitrary"),
    )(proj, proj, proj, cos_t, sin_t, q_g.reshape(1, -1), kv_g.reshape(1, -1), wq_r, wk, wv)


ATT_SUB = 256


def _attn_tile(q_sub, k_ref, v_ref, kcols, vcols, s_scr, p_scr, nk):
    sub = ATT_SUB
    nfull = nk - sub
    shift = CHUNK.bit_length() - 1

    def fold(x, op):
        return op(x.reshape(sub // 8, 8, sub), axis=0)

    if nfull:
        s_scr[0:nfull, :] = _dot_t(k_ref[0, 0:nfull, kcols], q_sub)
    krow = lax.broadcasted_iota(I32, (sub, sub), 0) >> shift
    qcol = lax.broadcasted_iota(I32, (sub, sub), 1) >> shift
    s_scr[nfull:nk, :] = jnp.where(krow <= qcol, _dot_t(k_ref[0, nfull:nk, kcols], q_sub), -jnp.inf)

    mrun = None
    for j in range(nk // sub):
        f = fold(s_scr[j * sub:(j + 1) * sub, :], jnp.max)
        mrun = f if mrun is None else jnp.maximum(mrun, f)
    m = jnp.max(mrun, axis=0, keepdims=True)
    lrun = None
    for j in range(nk // sub):
        pt = jnp.exp(s_scr[j * sub:(j + 1) * sub, :] - m)
        f = fold(pt, jnp.sum)
        lrun = f if lrun is None else lrun + f
        p_scr[j * sub:(j + 1) * sub, :] = pt.astype(BF16)
    l = jnp.sum(lrun, axis=0, keepdims=True)
    acc = lax.dot_general(v_ref[0, 0:nk, vcols], p_scr[0:nk, :], (((0,), (0,)), ((), ())),
                          preferred_element_type=F32)
    return (acc / l).T


ATT_HEADS = 2


def _attn_body(q_ref, k_ref, v_ref, o_ref, s_scr, p_scr, *, tq, nq):
    qi = pl.program_id(2)
    sub = ATT_SUB
    hw = q_ref.shape[2] // ATT_HEADS
    for c in range(nq):
        @pl.when(qi == c)
        def _():
            for hh in range(ATT_HEADS):
                kcols = slice(hh * hw, (hh + 1) * hw)
                vcols = slice(hh * V_HEAD, (hh + 1) * V_HEAD)
                for r in range(tq // sub):
                    q_sub = q_ref[0, r * sub:(r + 1) * sub, kcols]
                    out = _attn_tile(q_sub, k_ref, v_ref, kcols, vcols, s_scr, p_scr, c * tq + (r + 1) * sub)
                    o_ref[r * sub:(r + 1) * sub, vcols] = out.astype(o_ref.dtype)


def _attn_prompt(q, k, v, *, nb, seq):
    hw = (QK_NOPE + LANES) * ATT_HEADS
    vw = V_HEAD * ATT_HEADS
    tq = min(1024, seq // 2)
    assert seq % tq == 0 and tq % ATT_SUB == 0 and ATT_SUB % CHUNK == 0 and MLA_HEADS % ATT_HEADS == 0
    nq = seq // tq
    q3 = q.reshape(nb, seq, -1)
    k3 = k.reshape(nb, seq, -1)
    v3 = v.reshape(nb, seq, -1)
    return pl.pallas_call(
        functools.partial(_attn_body, tq=tq, nq=nq),
        grid=(nb, MLA_HEADS // ATT_HEADS, nq),
        in_specs=[pl.BlockSpec((1, tq, hw), lambda b, h, i: (b, i, h)),
                  pl.BlockSpec((1, seq, hw), lambda b, h, i: (b, 0, h)),
                  pl.BlockSpec((1, seq, vw), lambda b, h, i: (b, 0, h))],
        out_specs=pl.BlockSpec((tq, vw), lambda b, h, i: (b * nq + i, h)),
        out_shape=jax.ShapeDtypeStruct((nb * seq, MLA_HEADS * V_HEAD), BF16),
        scratch_shapes=[pltpu.VMEM((seq, ATT_SUB), F32), pltpu.VMEM((seq, ATT_SUB), BF16)],
        compiler_params=_params("arbitrary", "arbitrary", "arbitrary"),
    )(q3, k3, v3)


def _attn_sample_body(ql_ref, qr_ref, pckv_ref, pkr_ref, nckv_ref, nkr_ref, o_ref, *, past, seq):
    nh = ql_ref.shape[0]
    ql = ql_ref[...].reshape(nh * seq, ql_ref.shape[2])
    qr = qr_ref[...].reshape(nh * seq, qr_ref.shape[2])
    pckv = pckv_ref[0].astype(BF16)
    nckv = nckv_ref[...].astype(BF16)
    s_p = _dot_t(ql, pckv) + _dot_t(qr, pkr_ref[0].astype(BF16))
    s_n = _dot_t(ql, nckv) + _dot_t(qr, nkr_ref[:, :QK_ROPE].astype(BF16))
    shift = CHUNK.bit_length() - 1
    q_chunk = (past + lax.broadcasted_iota(I32, (nh * seq, 1), 0) % seq) >> shift
    kp_chunk = lax.broadcasted_iota(I32, (1, past), 1) >> shift
    kn_chunk = (past + lax.broadcasted_iota(I32, (1, seq), 1)) >> shift
    s_p = jnp.where(kp_chunk <= q_chunk, s_p, -jnp.inf)
    s_n = jnp.where(kn_chunk <= q_chunk, s_n, -jnp.inf)
    m = jnp.maximum(jnp.max(s_p, axis=1, keepdims=True), jnp.max(s_n, axis=1, keepdims=True))
    p_p = jnp.exp(s_p - m)
    p_n = jnp.exp(s_n - m)
    l = jnp.sum(p_p, axis=1, keepdims=True) + jnp.sum(p_n, axis=1, keepdims=True)
    o = (_dot(p_p.astype(BF16), pckv) + _dot(p_n.astype(BF16), nckv)) / l
    o_ref[...] = o.reshape(o_ref.shape).astype(o_ref.dtype)


def _attn_sample(qlat, qrope, past_ckv, past_kr, ckv_n, kr_r, *, nb, seq):
    past = past_ckv.shape[1]
    r = past_ckv.shape[2]
    return pl.pallas_call(
        functools.partial(_attn_sample_body, past=past, seq=seq),
        grid=(nb,),
        in_specs=[pl.BlockSpec((MLA_HEADS, seq, r), lambda b: (0, b, 0)),
                  pl.BlockSpec((MLA_HEADS, seq, QK_ROPE), lambda b: (0, b, 0)),
                  pl.BlockSpec((1, past, r), lambda b: (b, 0, 0)),
                  pl.BlockSpec((1, past, QK_ROPE), lambda b: (b, 0, 0)),
                  pl.BlockSpec((seq, r), lambda b: (b, 0)),
                  pl.BlockSpec((seq, LANES), lambda b: (b, 0))],
        out_specs=pl.BlockSpec((MLA_HEADS, seq, r), lambda b: (0, b, 0)),
        out_shape=jax.ShapeDtypeStruct((MLA_HEADS, nb * seq, r), BF16),
        compiler_params=_params("arbitrary"),
    )(qlat, qrope, past_ckv, past_kr, ckv_n, kr_r)


def _uv_body(o_ref, w_ref, y_ref):
    y_ref[...] = _dot(o_ref[0], w_ref[...]).astype(y_ref.dtype)


def _uv_sample(o_lat, wv):
    nh, nrows, r = o_lat.shape
    return pl.pallas_call(
        _uv_body,
        grid=(nh,),
        in_specs=[pl.BlockSpec((1, nrows, r), lambda h: (h, 0, 0)),
                  pl.BlockSpec((r, V_HEAD), lambda h: (0, h))],
        out_specs=pl.BlockSpec((nrows, V_HEAD), lambda h: (0, h)),
        out_shape=jax.ShapeDtypeStruct((nrows, nh * V_HEAD), BF16),
        compiler_params=_params("arbitrary"),
    )(o_lat, wv)


def _merge1_body(ysp_ref, ymp_ref, yss_ref, yms_ref, ws_ref, wm_ref, ga_ref, gb_ref, o_ref, *, npb):
    def run(ys_ref, ym_ref):
        a = _dot(ys_ref[...], ws_ref[...])
        b = _dot(ym_ref[...], wm_ref[...])
        o_ref[...] = (_sigmoid(ga_ref[...]) * a + _sigmoid(gb_ref[...]) * b).astype(o_ref.dtype)

    i = pl.program_id(1)
    pl.when(i < npb)(lambda: run(ysp_ref, ymp_ref))
    pl.when(i >= npb)(lambda: run(yss_ref, yms_ref))


def _merge1(ys_p, ym_p, ys_s, ym_s, w_ssd_out, w_mla_out, proj, offs):
    (tp, inner), ts = ys_p.shape, ys_s.shape[0]
    dm = ym_p.shape[1]
    d = w_ssd_out.shape[1]
    tm = _pick(math.gcd(tp, ts), (512, 256, 128))
    tn = 512
    npb, nsb = tp // tm, ts // tm
    prow = lambda j, i: (jnp.minimum(i, npb - 1), 0)
    srow = lambda j, i: (jnp.maximum(i - npb, 0), 0)
    return pl.pallas_call(
        functools.partial(_merge1_body, npb=npb),
        grid=(d // tn, npb + nsb),
        in_specs=[pl.BlockSpec((tm, inner), prow), pl.BlockSpec((tm, dm), prow),
                  pl.BlockSpec((tm, inner), srow), pl.BlockSpec((tm, dm), srow),
                  pl.BlockSpec((inner, tn), lambda j, i: (0, j)),
                  pl.BlockSpec((dm, tn), lambda j, i: (0, j)),
                  pl.BlockSpec((tm, tn), lambda j, i: (i, offs["ga"] // tn + j)),
                  pl.BlockSpec((tm, tn), lambda j, i: (i, offs["gb"] // tn + j))],
        out_specs=pl.BlockSpec((tm, tn), lambda j, i: (i, j)),
        out_shape=jax.ShapeDtypeStruct((tp + ts, d), BF16),
        compiler_params=_params("arbitrary", "arbitrary"),
    )(ys_p, ym_p, ys_s, ym_s, w_ssd_out, w_mla_out, proj, proj)


def _merge2_body(m_ref, w_ref, xp_ref, xs_ref, g1_ref, sc_ref, sh_ref, ng_ref, wr_ref, x1_ref, h2_ref, lg_ref,
                 *, npb):
    rows, d = xp_ref.shape
    x = jnp.where(pl.program_id(0) < npb, xp_ref[...], xs_ref[...])
    upd = _dot(m_ref[...], w_ref[...]).reshape(rows // MOD_ROWS, MOD_ROWS, d) * g1_ref[...]
    x1 = x + upd.reshape(rows, d)
    x1_ref[...] = x1
    h2 = _modulate(_rms(x1) * ng_ref[...], sc_ref, sh_ref)
    _store_items(h2_ref, _pack_halves(h2))
    lg_ref[...] = _dot3(wr_ref[...], h2, dot=_dot_t)


def _merge2(merged, w_merge, xp, xs, modg, norm_g, w_router_t):
    d = xp.shape[1]
    t = xp.shape[0] + xs.shape[0]
    assert d == 2 * SUBLANES * LANES
    ne = w_router_t.shape[0]
    tm, npb, prow, srow = _two_stream_rows(xp.shape[0], xs.shape[0], (512, 256, 128))
    ng = tm // MOD_ROWS
    return pl.pallas_call(
        functools.partial(_merge2_body, npb=npb),
        grid=(t // tm,),
        in_specs=[pl.BlockSpec((tm, d), lambda i: (i, 0)),
                  pl.BlockSpec((d, d), lambda i: (0, 0)),
                  pl.BlockSpec((tm, d), prow), pl.BlockSpec((tm, d), srow),
                  pl.BlockSpec((ng, 1, d), lambda i: (i, 0, 2)),
                  pl.BlockSpec((ng, 1, d), lambda i: (i, 0, 4)),
                  pl.BlockSpec((ng, 1, d), lambda i: (i, 0, 3)),
                  pl.BlockSpec((1, d), lambda i: (0, 0)),
                  pl.BlockSpec((ne, d), lambda i: (0, 0))],
        out_specs=[pl.BlockSpec((tm, d), lambda i: (i, 0)),
                   pl.BlockSpec((tm * SUBLANES, LANES), lambda i: (i, 0)),
                   pl.BlockSpec((ne, tm), lambda i: (0, i))],
        out_shape=[jax.ShapeDtypeStruct((t, d), F32), jax.ShapeDtypeStruct((t * SUBLANES, LANES), U32),
                   jax.ShapeDtypeStruct((ne, t), F32)],
        compiler_params=_params("arbitrary"),
    )(merged, w_merge, xp, xs, modg, modg, modg, norm_g.reshape(1, d), w_router_t)


def _route_body(lg_ref, eb_ref, eidx_ref, pos_ref, w_ref, cnt_ref, carry):
    ne, tr = lg_ref.shape
    per_group = ne // N_GROUPS

    @pl.when(pl.program_id(0) == 0)
    def _():
        carry[...] = jnp.zeros_like(carry)

    scores = _sigmoid(lg_ref[...])
    choice = scores + eb_ref[...]
    sub = lax.broadcasted_iota(I32, (per_group, tr), 0)
    gscore, blocks = [], []
    for g in range(N_GROUPS):
        blk = choice[g * per_group:(g + 1) * per_group, :]
        m1 = jnp.max(blk, axis=0, keepdims=True)
        first = jnp.min(jnp.where(blk == m1, sub, per_group), axis=0, keepdims=True)
        m2 = jnp.max(jnp.where(sub == first, -jnp.inf, blk), axis=0, keepdims=True)
        gscore.append(m1 + m2)
        blocks.append(blk)
    masked = []
    for g in range(N_GROUPS):
        rank = jnp.zeros((1, tr), I32)
        for g2 in range(N_GROUPS):
            if g2 == g:
                continue
            beats = (gscore[g2] > gscore[g]) | ((gscore[g2] == gscore[g]) & (g2 < g))
            rank = rank + beats.astype(I32)
        masked.append(jnp.where(rank < TOPK_GROUPS, blocks[g], -jnp.inf))
    cm = jnp.concatenate(masked, axis=0)

    eid = lax.broadcasted_iota(I32, (ne, tr), 0)
    rank = jnp.zeros((ne, tr), I32)
    for e2 in range(ne):
        rowv = cm[e2:e2 + 1, :]
        beats = (rowv > cm) | ((rowv == cm) & (eid > e2))
        rank = rank + beats.astype(I32)
    sel = rank < TOP_K
    wsel = jnp.where(sel, scores, 0.0)
    wfull = wsel / jnp.sum(wsel, axis=0, keepdims=True) * ROUTED_SCALE

    r = lax.broadcasted_iota(I32, (tr, tr), 0)
    c = lax.broadcasted_iota(I32, (tr, tr), 1)
    before = jnp.where(r < c, 1.0, 0.0).astype(BF16)
    self = jnp.where(sel, 1.0, 0.0)
    pos = carry[:, 0:1] + _dot(self.astype(BF16), before)
    carry[...] = carry[...] + jnp.sum(self, axis=1, keepdims=True)
    cnt_ref[...] = carry[...]

    eid_f = eid.astype(F32)
    for k in range(TOP_K):
        pick = sel & (rank == k)
        eidx_ref[k:k + 1, :] = jnp.sum(jnp.where(pick, eid_f, 0.0), axis=0, keepdims=True).astype(I32)
        pos_ref[k:k + 1, :] = jnp.sum(jnp.where(pick, pos, 0.0), axis=0, keepdims=True).astype(I32)
        w_ref[k:k + 1, :] = jnp.sum(jnp.where(pick, wfull, 0.0), axis=0, keepdims=True)


def _route(logits_t, e_bias):
    ne, t = logits_t.shape
    tr = _pick(t, (512, 256, 128))
    return pl.pallas_call(
        _route_body,
        grid=(t // tr,),
        in_specs=[pl.BlockSpec((ne, tr), lambda i: (0, i)), pl.BlockSpec((ne, 1), lambda i: (0, 0))],
        out_specs=[pl.BlockSpec((TOP_K, tr), lambda i: (0, i)), pl.BlockSpec((TOP_K, tr), lambda i: (0, i)),
                   pl.BlockSpec((TOP_K, tr), lambda i: (0, i)), pl.BlockSpec((ne, LANES), lambda i: (0, 0))],
        out_shape=[jax.ShapeDtypeStruct((TOP_K, t), I32), jax.ShapeDtypeStruct((TOP_K, t), I32),
                   jax.ShapeDtypeStruct((TOP_K, t), F32), jax.ShapeDtypeStruct((ne, LANES), F32)],
        scratch_shapes=[pltpu.VMEM((ne, LANES), F32)],
        compiler_params=_params("arbitrary"),
    )(logits_t, e_bias.reshape(ne, 1))


def _row_copy(src, s, dst, d, sem):
    rows = lambda i: pl.ds(pl.multiple_of(i * SUBLANES, SUBLANES), SUBLANES)
    return pltpu.make_async_copy(src.at[rows(s), :], dst.at[rows(d), :], sem)


def _load_items(ref, start, n):
    return jnp.concatenate([ref[pl.ds(start * SUBLANES + c, n, stride=SUBLANES), :] for c in range(SUBLANES)],
                           axis=1)


def _store_items(ref, val):
    n = val.shape[0]
    for c in range(SUBLANES):
        ref[pl.ds(c, n, stride=SUBLANES), :] = val[:, c * LANES:(c + 1) * LANES]


def _drain(wait_one, n, group=64):
    assert n % group == 0

    def body(j, c):
        for _ in range(group):
            wait_one()
        return c

    lax.fori_loop(0, n // group, body, 0)


def _dispatch_body(fill_lo_ref, fill_hi_ref, h_ref, dest_ref, hs_ref, dest_s, zrow, sem, dsem, *, tb):
    i = pl.program_id(0)
    cp = pltpu.make_async_copy(dest_ref.at[i], dest_s, dsem)
    cp.start()

    @pl.when(i == 0)
    def _():
        zrow[...] = jnp.zeros_like(zrow)

        def per_expert(fn):
            def body(e, _):
                lax.fori_loop(fill_lo_ref[e], fill_hi_ref[e], lambda s, c: (fn(s), c)[1], 0)
                return 0
            lax.fori_loop(0, fill_lo_ref.shape[0], body, 0)

        per_expert(lambda s: _row_copy(zrow, 0, hs_ref, s, sem).start())
        per_expert(lambda s: _row_copy(zrow, 0, hs_ref, s, sem).wait())

    cp.wait()

    def issue(t, c):
        for k in range(TOP_K):
            _row_copy(h_ref, t, hs_ref, dest_s[t * TOP_K + k], sem).start(priority=k % 2)
        return c

    lax.fori_loop(0, tb, issue, 0, unroll=4)
    _drain(lambda: _row_copy(h_ref, 0, hs_ref, 0, sem).wait(), TOP_K * tb)


def _dispatch(h2, dest_blk, fill_lo, fill_hi, n_slots):
    nblk, n = dest_blk.shape
    tb = n // TOP_K
    return pl.pallas_call(
        functools.partial(_dispatch_body, tb=tb),
        grid_spec=pltpu.PrefetchScalarGridSpec(
            num_scalar_prefetch=2,
            grid=(nblk,),
            in_specs=[pl.BlockSpec((tb * SUBLANES, LANES), lambda i, lo, hi: (i, 0)),
                      pl.BlockSpec(memory_space=pl.ANY)],
            out_specs=pl.BlockSpec(memory_space=pl.ANY),
            scratch_shapes=[pltpu.SMEM((n,), I32), pltpu.VMEM((SUBLANES, LANES), h2.dtype),
                            pltpu.SemaphoreType.DMA, pltpu.SemaphoreType.DMA]),
        out_shape=jax.ShapeDtypeStruct((n_slots * SUBLANES, LANES), h2.dtype),
        compiler_params=_params("arbitrary"),
    )(fill_lo, fill_hi, h2, dest_blk)


def _experts_body(be_ref, nu_ref, x_ref, wg_ref, wu_ref, wd_ref, o_ref, wgu_s, wd_s):
    i = pl.program_id(0)
    hid = wg_ref.shape[2]

    @pl.when(i < nu_ref[0])
    def _():
        @pl.when((i == 0) | (be_ref[i] != be_ref[jnp.maximum(i - 1, 0)]))
        def _():
            wgu_s[:, :hid] = wg_ref[0].astype(BF16)
            wgu_s[:, hid:] = wu_ref[0].astype(BF16)
            wd_s[...] = wd_ref[0].astype(BF16)

        lo, hi = _unpack_halves(_load_items(x_ref, 0, x_ref.shape[0] // SUBLANES))
        x = jnp.concatenate([lo.astype(BF16), hi.astype(BF16)], axis=1)
        gu = _dot(x, wgu_s[...])
        act = (_silu(gu[:, :hid]) * gu[:, hid:]).astype(BF16)
        _store_items(o_ref, _pack_halves(_dot(act, wd_s[...])))


def _experts(hs, block_e, n_used, wg, wu, wd):
    n_slots = hs.shape[0] // SUBLANES
    ne, d, hid = wg.shape
    assert d == 2 * SUBLANES * LANES
    bm = MOE_ROWS
    nblocks = n_slots // bm
    blk = lambda i, be, nu: (jnp.minimum(i, nu[0] - 1), 0)
    return pl.pallas_call(
        _experts_body,
        grid_spec=pltpu.PrefetchScalarGridSpec(
            num_scalar_prefetch=2,
            grid=(nblocks,),
            in_specs=[pl.BlockSpec((bm * SUBLANES, LANES), blk),
                      pl.BlockSpec((1, d, hid), lambda i, be, nu: (be[i], 0, 0)),
                      pl.BlockSpec((1, d, hid), lambda i, be, nu: (be[i], 0, 0)),
                      pl.BlockSpec((1, hid, d), lambda i, be, nu: (be[i], 0, 0))],
            out_specs=pl.BlockSpec((bm * SUBLANES, LANES), blk),
            scratch_shapes=[pltpu.VMEM((d, 2 * hid), BF16), pltpu.VMEM((hid, d), BF16)]),
        out_shape=jax.ShapeDtypeStruct(hs.shape, U32),
        compiler_params=_params("arbitrary"),
    )(block_e, n_used, hs, wg, wu, wd)


def _combine_body(ys_ref, dest_ref, w_ref, h_ref, x1_ref, g2_ref, wgu_ref, wd_ref, fg_ref, op_ref, os_ref,
                  dest_s, gbuf, sem, dsem, *, tb, npb):
    i = pl.program_id(0)
    cp = pltpu.make_async_copy(dest_ref.at[i], dest_s, dsem)
    cp.start()
    cp.wait()

    def issue(t, c):
        for k in range(TOP_K):
            _row_copy(ys_ref, dest_s[t * TOP_K + k], gbuf, k * tb + t, sem).start(priority=k % 2)
        return c

    lax.fori_loop(0, tb, issue, 0, unroll=4)

    hid = wd_ref.shape[0]
    h_lo, h_hi = _unpack_halves(_load_items(h_ref, 0, tb))
    gu = _dot(jnp.concatenate([h_lo.astype(BF16), h_hi.astype(BF16)], axis=1), wgu_ref[...])
    moe = _dot((_silu(gu[:, :hid]) * gu[:, hid:]).astype(BF16), wd_ref[...])

    _drain(lambda: _row_copy(ys_ref, 0, gbuf, 0, sem).wait(), TOP_K * tb)
    w = w_ref[...]
    dh = SUBLANES * LANES
    m_lo, m_hi = moe[:, :dh], moe[:, dh:]
    for k in range(TOP_K):
        lo, hi = _unpack_halves(_load_items(gbuf, k * tb, tb))
        m_lo = m_lo + lo * w[:, k:k + 1]
        m_hi = m_hi + hi * w[:, k:k + 1]
    moe = jnp.concatenate([m_lo, m_hi], axis=1)
    rows, d = moe.shape
    upd = moe.reshape(rows // MOD_ROWS, MOD_ROWS, d) * g2_ref[...]
    x2 = x1_ref[...] + upd.reshape(rows, d)
    y = _rms(x2) * fg_ref[...]

    @pl.when(i < npb)
    def _():
        op_ref[...] = y

    @pl.when(i >= npb)
    def _():
        os_ref[...] = y


def _combine(ys, dest_blk, w_tok, h2, x1, modg, wsh_gu, wsh_d, final_g, *, tp):
    t, d = x1.shape
    nblk, n = dest_blk.shape
    tb = n // TOP_K
    assert tp % tb == 0 and (t - tp) % tb == 0
    npb = tp // tb
    ng = tb // MOD_ROWS
    const = lambda i: (0, 0)
    return pl.pallas_call(
        functools.partial(_combine_body, tb=tb, npb=npb),
        grid=(nblk,),
        in_specs=[pl.BlockSpec(memory_space=pl.ANY),
                  pl.BlockSpec(memory_space=pl.ANY),
                  pl.BlockSpec((tb, TOP_K), lambda i: (i, 0)),
                  pl.BlockSpec((tb * SUBLANES, LANES), lambda i: (i, 0)),
                  pl.BlockSpec((tb, d), lambda i: (i, 0)),
                  pl.BlockSpec((ng, 1, d), lambda i: (i, 0, 5)),
                  pl.BlockSpec(wsh_gu.shape, const), pl.BlockSpec(wsh_d.shape, const),
                  pl.BlockSpec((1, d), const)],
        out_specs=[pl.BlockSpec((tb, d), lambda i: (jnp.minimum(i, npb - 1), 0)),
                   pl.BlockSpec((tb, d), lambda i: (jnp.maximum(i - npb, 0), 0))],
        out_shape=[jax.ShapeDtypeStruct((tp, d), F32), jax.ShapeDtypeStruct((t - tp, d), F32)],
        scratch_shapes=[pltpu.SMEM((n,), I32), pltpu.VMEM((n * SUBLANES, LANES), U32),
                        pltpu.SemaphoreType.DMA, pltpu.SemaphoreType.DMA],
        compiler_params=_params("arbitrary"),
    )(ys, dest_blk, w_tok, h2, x1, modg, wsh_gu, wsh_d, final_g.reshape(1, d))


def _rope_tables(pos):
    half = QK_ROPE // 2
    freqs = ROPE_THETA ** (-jnp.arange(half, dtype=F32) / half)
    ang = pos.astype(F32)[:, None] * freqs[None, :]
    cos, sin = jnp.cos(ang), jnp.sin(ang)
    return (jnp.concatenate([cos, cos, cos, cos], axis=1),
            jnp.concatenate([-sin, sin, -sin, sin], axis=1))


def _blocked(a, tb):
    k, t = a.shape
    return a.T.reshape(t // tb, tb * k)


def kernel(x_prompt, x_sample, c_prompt, c_sample, cache_conv, state_ssm, cache_ckv, cache_kr, w_ada, b_ada,
           norm1_g, norm2_g, w_in, conv_w, conv_b, dt_bias, a_log, d_skip, ssd_norm_g, w_ssd_out, q_norm_g,
           w_uq, kv_norm_g, w_uk, w_uv, w_mla_out, w_merge_out, w_router, e_bias, w_exp_gate, w_exp_up,
           w_exp_down, w_sh_gate, w_sh_up, w_sh_down, final_norm_g):
    depth = w_in.shape[0]
    assert depth == 1
    bp, seq, d = x_prompt.shape
    bs, lseq, _ = x_sample.shape
    assert lseq == MOD_ROWS and seq % MOD_ROWS == 0
    tp, ts = bp * seq, bs * lseq
    t_all = tp + ts
    nheads = dt_bias.shape[1]
    inner = nheads * SSD_HEAD_DIM
    cdim = conv_w.shape[2]
    gn = SSD_GROUPS * SSD_STATE
    qlora, kvlora = q_norm_g.shape[1], kv_norm_g.shape[1]
    ne = w_router.shape[2]
    assert nheads == 64 and 2 * nheads == LANES

    w = w_in[0]
    o_xbc, o_dt = inner, inner + cdim
    o_cq = o_dt + nheads
    o_ckv = o_cq + qlora
    o_kr = o_ckv + kvlora
    o_gate = o_kr + QK_ROPE
    perm = np.concatenate([np.arange(0, nheads, 2), np.arange(1, nheads, 2)])
    zc = lambda n: jnp.zeros((d, n), w.dtype)
    cols = [w[:, :inner], w[:, o_gate:], w[:, o_xbc:o_dt], w[:, o_cq:o_ckv], w[:, o_ckv:o_kr],
            w[:, o_dt:o_cq][:, perm], zc(LANES - nheads), w[:, o_kr:o_gate], zc(LANES - QK_ROPE)]
    used = inner + 2 * d + cdim + qlora + kvlora + 2 * LANES
    total = -(-used // 512) * 512
    cols.append(zc(total - used))
    w_r = jnp.concatenate(cols, axis=1).astype(BF16)
    offs = {"z": 0, "ga": inner, "gb": inner + d, "x": inner + 2 * d, "bc": 2 * inner + 2 * d}
    offs["cq"] = offs["bc"] + 2 * gn
    offs["ckv"] = offs["cq"] + qlora
    offs["dt"] = offs["ckv"] + kvlora
    offs["kr"] = offs["dt"] + LANES

    hw = QK_NOPE + LANES
    wq_r = jnp.pad(w_uq[0], ((0, 0), (0, 0), (0, hw - QK_NOPE - QK_ROPE))).reshape(qlora, MLA_HEADS * hw).astype(BF16)
    wk = w_uk[0].reshape(kvlora, MLA_HEADS * QK_NOPE).astype(BF16)
    wv = w_uv[0].reshape(kvlora, MLA_HEADS * V_HEAD).astype(BF16)
    pad_l = lambda v: jnp.pad(v[perm], (0, LANES - nheads)).reshape(1, LANES)
    dtb = pad_l(dt_bias[0])
    a_neg = pad_l(-jnp.exp(a_log[0]))
    dskip = jnp.repeat(d_skip[0], SSD_HEAD_DIM).reshape(1, inner)

    c_all = jnp.concatenate([c_prompt, c_sample], axis=0)
    mod = _ada(c_all, w_ada[0], b_ada[0])
    grp = np.concatenate([np.repeat(np.arange(bp), seq // MOD_ROWS), bp + np.arange(bs)])
    modg = mod[grp].reshape(t_all // MOD_ROWS, 1, 6 * d)

    xp, xs = x_prompt.reshape(tp, d), x_sample.reshape(ts, d)
    proj = _inproj(xp, xs, modg, norm1_g[0], w_r, d)

    ssd_args = (conv_w[0], conv_b[0].reshape(1, cdim), dtb, a_neg, dskip, ssd_norm_g[0].reshape(1, inner))
    ys_p, conv_p, ssm_p = _ssd(proj, jnp.zeros((bp, CONV_W - 1, cdim), F32),
                               jnp.zeros((1, bp) + state_ssm.shape[2:], F32), *ssd_args,
                               nb=bp, seq=seq, row0=0, offs=offs, inner=inner)
    ys_s, conv_s, ssm_s = _ssd(proj, cache_conv[0], state_ssm,
                               *ssd_args, nb=bs, seq=lseq, row0=tp, offs=offs, inner=inner)

    past = cache_ckv.shape[2]
    cos_p, sin_p = _rope_tables(jnp.arange(seq))
    cos_s, sin_s = _rope_tables(past + jnp.arange(lseq))
    cos_t = jnp.concatenate([jnp.tile(cos_p, (bp, 1)), jnp.tile(cos_s, (bs, 1))], axis=0)
    sin_t = jnp.concatenate([jnp.tile(sin_p, (bp, 1)), jnp.tile(sin_s, (bs, 1))], axis=0)
    prep = functools.partial(_mla_prep, proj, cos_t, sin_t, q_norm_g[0], kv_norm_g[0], wq_r, wk, wv, offs=offs)
    ckv_p, kr_p, q_p, k_p, v_p = prep(row0=0, nrows=tp, absorbed=False)
    ckv_s, kr_s, qlat, qrope = prep(row0=tp, nrows=ts, absorbed=True)
    ym_p = _attn_prompt(q_p, k_p, v_p, nb=bp, seq=seq)
    o_lat = _attn_sample(qlat, qrope, cache_ckv[0], cache_kr[0], ckv_s, kr_s, nb=bs, seq=lseq)
    ym_s = _uv_sample(o_lat, wv)

    merged = _merge1(ys_p, ym_p, ys_s, ym_s, w_ssd_out[0].astype(BF16), w_mla_out[0].astype(BF16), proj, offs)
    x1, h2, logits_t = _merge2(merged, w_merge_out[0].astype(BF16), xp, xs, modg, norm2_g[0], w_router[0].T)

    eidx, pos, w_sel, cnt = _route(logits_t, e_bias[0])
    bm = MOE_ROWS
    counts = cnt[:, 0].astype(I32)
    padded = (counts + bm - 1) // bm * bm
    pad_end = jnp.cumsum(padded)
    pad_start = pad_end - padded
    nblocks = -(-(t_all * TOP_K) // bm) + ne
    onehot = eidx[:, :, None] == jnp.arange(ne, dtype=I32)
    dest = jnp.sum(jnp.where(onehot, pad_start, 0), axis=-1) + pos
    blk_start = jnp.arange(nblocks, dtype=I32) * bm
    block_e = jnp.minimum(jnp.sum(pad_end[None, :] <= blk_start[:, None], axis=1), ne - 1).astype(I32)
    n_used = (pad_end[-1:] // bm).astype(I32)

    tb_d = _pick(t_all, (256, 128))
    hs = _dispatch(h2, _blocked(dest, tb_d), (pad_start + counts).astype(I32), pad_end.astype(I32), nblocks * bm)
    ys = _experts(hs, block_e, n_used, w_exp_gate[0], w_exp_up[0], w_exp_down[0])
    wsh_gu = jnp.concatenate([w_sh_gate[0], w_sh_up[0]], axis=1).astype(BF16)
    tb_c = _pick(math.gcd(tp, ts), (256, 128))
    y_p, y_s = _combine(ys, _blocked(dest, tb_c), w_sel.T, h2, x1, modg, wsh_gu, w_sh_down[0].astype(BF16),
                        final_norm_g, tp=tp)

    r5 = lambda a, n, l: a.reshape(1, n, l, a.shape[-1])
    return (y_p.reshape(bp, seq, d), y_s.reshape(bs, lseq, d),
            conv_p[None], ssm_p,
            r5(ckv_p, bp, seq), r5(kr_p[:, :QK_ROPE], bp, seq),
            conv_s[None], ssm_s,
            r5(ckv_s, bs, lseq), r5(kr_s[:, :QK_ROPE], bs, lseq))
```

```python
import functools
import math

import numpy as np
import jax
import jax.numpy as jnp
from jax import lax
from jax.experimental import pallas as pl
from jax.experimental.pallas import tpu as pltpu

F32 = jnp.float32
BF16 = jnp.bfloat16
I32 = jnp.int32
U32 = jnp.uint32

EPS = 1e-6
CHUNK = 64
SSD_HEAD_DIM = 64
SSD_GROUPS = 8
SSD_STATE = 128
CONV_W = 4
MLA_HEADS = 16
QK_NOPE = 128
QK_ROPE = 64
V_HEAD = 128
ROPE_THETA = 10000.0
TOP_K = 8
N_GROUPS = 8
TOPK_GROUPS = 4
ROUTED_SCALE = 2.5

LANES = 128
SUBLANES = 8
MOD_ROWS = 32
MOE_ROWS = 512
VMEM_LIMIT = 56 * 1024 * 1024


def _params(*sem):
    return pltpu.CompilerParams(dimension_semantics=sem, vmem_limit_bytes=VMEM_LIMIT)


def _pick(n, cands):
    for c in cands:
        if n % c == 0:
            return c
    raise ValueError(f"no tile in {cands} divides {n}")


def _sigmoid(x):
    return 1.0 / (1.0 + jnp.exp(-x))


def _silu(x):
    return x * _sigmoid(x)


def _dot(a, b):
    return jnp.dot(a, b, preferred_element_type=F32)


def _dot_t(a, b):
    return lax.dot_general(a, b, (((1,), (1,)), ((), ())), preferred_element_type=F32)


def _split(x, n):
    parts = []
    for _ in range(n - 1):
        p = x.astype(BF16)
        parts.append(p)
        x = x - p.astype(F32)
    parts.append(x.astype(BF16))
    return parts


def _dot_exact_rhs(a, b_bf16, n=3):
    acc = None
    for p in _split(a, n):
        t = _dot(p, b_bf16)
        acc = t if acc is None else acc + t
    return acc


def _dot3(a, b, dot=_dot):
    ah, al = _split(a, 2)
    bh, bl = _split(b, 2)
    return dot(ah, bh) + (dot(ah, bl) + dot(al, bh))


def _pack_halves(x):
    h = x.shape[1] // 2
    bits = lambda v: lax.bitcast_convert_type(v.astype(BF16).astype(F32), U32)
    return (bits(x[:, :h]) >> 16) | bits(x[:, h:])


def _unpack_halves(p):
    lo = lax.bitcast_convert_type(p << 16, F32)
    hi = lax.bitcast_convert_type(p & jnp.uint32(0xFFFF0000), F32)
    return lo, hi


def _rms(x):
    return x * lax.rsqrt(jnp.mean(x * x, axis=-1, keepdims=True) + EPS)


def _modulate(y, sc_ref, sh_ref):
    rows, d = y.shape
    y3 = y.reshape(rows // MOD_ROWS, MOD_ROWS, d)
    return (y3 * (1.0 + sc_ref[...]) + sh_ref[...]).reshape(rows, d)


def _ada_body(c_ref, w_ref, b_ref, o_ref):
    o_ref[...] = _dot3(_silu(c_ref[...]), w_ref[...]) + b_ref[...]


def _ada(c_all, w_ada, b_ada):
    nb, d = c_all.shape
    n = w_ada.shape[1]
    tn = _pick(n, (1024, 512, 256, 128))
    return pl.pallas_call(
        _ada_body,
        grid=(n // tn,),
        in_specs=[pl.BlockSpec((nb, d), lambda j: (0, 0)),
                  pl.BlockSpec((d, tn), lambda j: (0, j)),
                  pl.BlockSpec((1, tn), lambda j: (0, j))],
        out_specs=pl.BlockSpec((nb, tn), lambda j: (0, j)),
        out_shape=jax.ShapeDtypeStruct((nb, n), F32),
        compiler_params=_params("arbitrary"),
    )(c_all, w_ada, b_ada.reshape(1, n))


def _two_stream_rows(tp, ts, cands):
    tm = _pick(math.gcd(tp, ts), cands)
    npb = tp // tm
    return tm, npb, (lambda i, *_: (jnp.minimum(i, npb - 1), 0)), (lambda i, *_: (jnp.maximum(i - npb, 0), 0))


def _inproj_body(xp_ref, xs_ref, sc_ref, sh_ref, g_ref, w_ref, o_ref, h_scr, *, npb):
    @pl.when(pl.program_id(1) == 0)
    def _():
        x = jnp.where(pl.program_id(0) < npb, xp_ref[...], xs_ref[...])
        h = _modulate(_rms(x) * g_ref[...], sc_ref, sh_ref)
        h_scr[...] = h.astype(BF16)

    o_ref[...] = _dot(h_scr[...], w_ref[...])


def _inproj(xp, xs, modg, norm_g, w_r, d):
    t = xp.shape[0] + xs.shape[0]
    n = w_r.shape[1]
    tm, npb, prow, srow = _two_stream_rows(xp.shape[0], xs.shape[0], (1024, 512, 256, 128))
    tn = 512
    ng = tm // MOD_ROWS
    return pl.pallas_call(
        functools.partial(_inproj_body, npb=npb),
        grid=(t // tm, n // tn),
        in_specs=[pl.BlockSpec((tm, d), prow), pl.BlockSpec((tm, d), srow),
                  pl.BlockSpec((ng, 1, d), lambda i, j: (i, 0, 1)),
                  pl.BlockSpec((ng, 1, d), lambda i, j: (i, 0, 0)),
                  pl.BlockSpec((1, d), lambda i, j: (0, 0)),
                  pl.BlockSpec((d, tn), lambda i, j: (0, j))],
        out_specs=pl.BlockSpec((tm, tn), lambda i, j: (i, j)),
        out_shape=jax.ShapeDtypeStruct((t, n), F32),
        scratch_shapes=[pltpu.VMEM((tm, d), BF16)],
        compiler_params=_params("arbitrary", "arbitrary"),
    )(xp, xs, modg, modg, norm_g.reshape(1, d), w_r)


def _ssd_body(z_ref, x_ref, bc_ref, dt_ref, conv0_ref, h0_ref, cw_ref, cb_ref, dtb_ref, a_ref,
              dskip_ref, ng_ref, y_ref, nconv_ref, nssm_ref,
              carry_x, carry_bc, xp_x, xp_bc, state, y_scr, *, lb, inner):
    q = CHUNK
    c = pl.program_id(1)
    nc = pl.num_programs(1)
    gn = SSD_GROUPS * SSD_STATE

    @pl.when(c == 0)
    def _():
        carry_x[...] = jnp.zeros_like(carry_x)
        carry_bc[...] = jnp.zeros_like(carry_bc)
        carry_x[8 - (CONV_W - 1):8, :] = conv0_ref[0, :, :inner]
        carry_bc[8 - (CONV_W - 1):8, :] = conv0_ref[0, :, inner:]
        state[...] = h0_ref[0, 0].reshape(state.shape)

    def pad_rows(v):
        if lb == q:
            return v
        return jnp.concatenate([v, jnp.zeros((q - lb, v.shape[1]), v.dtype)], axis=0)

    def conv(raw, carry, xp, w_lo, w_hi):
        xp[0:8, :] = carry[...]
        xp[8:8 + q, :] = raw
        acc = cb_ref[:, w_lo:w_hi]
        for j in range(CONV_W):
            acc = acc + xp[8 - j:8 - j + q, :] * cw_ref[CONV_W - 1 - j:CONV_W - j, w_lo:w_hi]
        return _silu(acc)

    x_raw = pad_rows(x_ref[...])
    bc_raw = pad_rows(bc_ref[...])
    xc = conv(x_raw, carry_x, xp_x, 0, inner)
    bcc = conv(bc_raw, carry_bc, xp_bc, inner, inner + 2 * gn)
    if lb == q:
        carry_x[...] = x_raw[q - 8:q, :]
        carry_bc[...] = bc_raw[q - 8:q, :]

    @pl.when(c == nc - 1)
    def _():
        nconv_ref[0, :, :inner] = x_raw[lb - (CONV_W - 1):lb, :]
        nconv_ref[0, :, inner:] = bc_raw[lb - (CONV_W - 1):lb, :]

    dtr = dt_ref[...] + dtb_ref[...]
    dtv = jnp.maximum(dtr, 0.0) + jnp.log1p(jnp.exp(-jnp.abs(dtr)))
    dtv = pad_rows(dtv)
    la = dtv * a_ref[...]

    row = lax.broadcasted_iota(I32, (q, q), 0)
    col = lax.broadcasted_iota(I32, (q, q), 1)
    tri = jnp.where(row >= col, 1.0, 0.0).astype(BF16)
    parts = _split(la, 3)
    a_cum = _dot(tri, parts[0]) + _dot(tri, parts[1]) + _dot(tri, parts[2])
    a_last = a_cum[q - 1:q, :]

    r2 = lax.broadcasted_iota(I32, (q, 2 * q), 0)
    c2 = lax.broadcasted_iota(I32, (q, 2 * q), 1)
    u_e = jnp.where((c2 < q) & (r2 <= c2), 1.0, 0.0).astype(BF16)
    u_o = jnp.where((c2 >= q) & (r2 <= c2 - q), 1.0, 0.0).astype(BF16)
    i_e = jnp.where(r2 == c2, 1.0, 0.0).astype(BF16)
    i_o = jnp.where(r2 == c2 - q, 1.0, 0.0).astype(BF16)
    la_t = la.T
    dt_t = dtv.T
    hp = la_t.shape[0] // 4
    acol = _dot_exact_rhs(la_t[0:hp], u_e) + _dot_exact_rhs(la_t[hp:2 * hp], u_o)
    dtrow = _dot_exact_rhs(dt_t[0:hp], i_e) + _dot_exact_rhs(dt_t[hp:2 * hp], i_o)
    w_t = (dtv * jnp.exp(a_last - a_cum)).T
    dec_b = jnp.broadcast_to(jnp.exp(jnp.sum(la_t, axis=1, keepdims=True)), (la_t.shape[0], SSD_STATE))

    lane = lax.broadcasted_iota(I32, (q, 2 * q), 1)
    causal2 = lax.broadcasted_iota(I32, (q, 2 * q), 0) >= jnp.where(lane < q, lane, lane - q)
    first_half = lane < q
    rr = lax.broadcasted_iota(I32, (2 * SSD_HEAD_DIM, 2 * SSD_HEAD_DIM), 0)
    cc = lax.broadcasted_iota(I32, (2 * SSD_HEAD_DIM, 2 * SSD_HEAD_DIM), 1)
    bd_mask = (rr < SSD_HEAD_DIM) == (cc < SSD_HEAD_DIM)
    top_rows_q = lax.broadcasted_iota(I32, (2 * SSD_HEAD_DIM, q), 0) < SSD_HEAD_DIM
    top_rows_n = lax.broadcasted_iota(I32, (2 * SSD_HEAD_DIM, SSD_STATE), 0) < SSD_HEAD_DIM

    pairs_per_group = (inner // SSD_HEAD_DIM) // SSD_GROUPS // 2
    for g in range(SSD_GROUPS):
        b_g = bcc[:, g * SSD_STATE:(g + 1) * SSD_STATE].astype(BF16)
        c_g = bcc[:, gn + g * SSD_STATE:gn + (g + 1) * SSD_STATE].astype(BF16)
        cb2 = _dot_t(c_g, jnp.concatenate([b_g, b_g], axis=0))
        for kk in range(pairs_per_group):
            k = g * pairs_per_group + kk
            lo, hi = k * 2 * SSD_HEAD_DIM, (k + 1) * 2 * SSD_HEAD_DIM
            xp = xc[:, lo:hi]
            arow = jnp.where(first_half, a_cum[:, k:k + 1], a_cum[:, hp + k:hp + k + 1])
            seg = jnp.where(causal2, arow - acol[k:k + 1, :], -jnp.inf)
            m = (jnp.exp(seg) * cb2 * dtrow[k:k + 1, :]).astype(BF16)
            xbd = jnp.where(bd_mask, jnp.concatenate([xp, xp], axis=0), 0.0).astype(BF16)
            y_diag = _dot(m, xbd)
            s_k = state[lo:hi, :]
            y_off = _dot_t(c_g, s_k.astype(BF16)) * jnp.exp(arow)
            y_scr[:, lo:hi] = y_diag + y_off + xp * dskip_ref[:, lo:hi]
            w2 = jnp.where(top_rows_q, w_t[k:k + 1, :], w_t[hp + k:hp + k + 1, :])
            contrib = _dot((xp.T * w2).astype(BF16), b_g)
            d_k = jnp.where(top_rows_n, dec_b[k:k + 1, :], dec_b[hp + k:hp + k + 1, :])
            state[lo:hi, :] = s_k * d_k + contrib

    @pl.when(c == nc - 1)
    def _():
        nssm_ref[0, 0] = state[...].reshape(nssm_ref.shape[2:])

    y = y_scr[0:lb, :] * _silu(z_ref[...])
    y_ref[...] = (_rms(y) * ng_ref[...]).astype(y_ref.dtype)


def _ssd(proj, conv0, h0, conv_w, conv_b, dtb, a_neg, dskip, norm_g, *, nb, seq, row0, offs, inner):
    q = CHUNK
    lb = min(seq, q)
    assert seq % lb == 0 and row0 % lb == 0 and lb % 16 == 0
    nc = seq // lb
    rb0 = row0 // lb
    cdim = conv_w.shape[1]
    hp_rows = inner
    state_shape = (inner // SSD_HEAD_DIM, SSD_HEAD_DIM, SSD_STATE)
    assert h0.shape == (1, nb) + state_shape
    kern = functools.partial(_ssd_body, lb=lb, inner=inner)
    rows = lambda b, c: rb0 + b * nc + c
    in_specs = [
        pl.BlockSpec((lb, inner), lambda b, c: (rows(b, c), offs["z"] // inner)),
        pl.BlockSpec((lb, inner), lambda b, c: (rows(b, c), offs["x"] // inner)),
        pl.BlockSpec((lb, cdim - inner), lambda b, c: (rows(b, c), offs["bc"] // (cdim - inner))),
        pl.BlockSpec((lb, LANES), lambda b, c: (rows(b, c), offs["dt"] // LANES)),
        pl.BlockSpec((1, CONV_W - 1, cdim), lambda b, c: (b, 0, 0)),
        pl.BlockSpec((1, 1) + state_shape, lambda b, c: (0, b, 0, 0, 0)),
        pl.BlockSpec((CONV_W, cdim), lambda b, c: (0, 0)),
        pl.BlockSpec((1, cdim), lambda b, c: (0, 0)),
        pl.BlockSpec((1, LANES), lambda b, c: (0, 0)),
        pl.BlockSpec((1, LANES), lambda b, c: (0, 0)),
        pl.BlockSpec((1, inner), lambda b, c: (0, 0)),
        pl.BlockSpec((1, inner), lambda b, c: (0, 0)),
    ]
    args = [proj, proj, proj, proj, conv0, h0, conv_w, conv_b, dtb, a_neg, dskip, norm_g]
    return pl.pallas_call(
        kern,
        grid=(nb, nc),
        in_specs=in_specs,
        out_specs=[pl.BlockSpec((lb, inner), lambda b, c: (b * nc + c, 0)),
                   pl.BlockSpec((1, CONV_W - 1, cdim), lambda b, c: (b, 0, 0)),
                   pl.BlockSpec((1, 1) + state_shape, lambda b, c: (0, b, 0, 0, 0))],
        out_shape=[jax.ShapeDtypeStruct((nb * seq, inner), BF16),
                   jax.ShapeDtypeStruct((nb, CONV_W - 1, cdim), F32),
                   jax.ShapeDtypeStruct((1, nb) + state_shape, F32)],
        scratch_shapes=[pltpu.VMEM((8, inner), F32), pltpu.VMEM((8, cdim - inner), F32),
                        pltpu.VMEM((8 + q, inner), F32), pltpu.VMEM((8 + q, cdim - inner), F32),
                        pltpu.VMEM((hp_rows, SSD_STATE), F32), pltpu.VMEM((q, inner), F32)],
        compiler_params=_params("arbitrary", "arbitrary"),
    )(*args)


def _rope128(x, cos, sin):
    lane = lax.broadcasted_iota(I32, x.shape, 1)
    half = QK_ROPE // 2
    swapped = jnp.where((lane % QK_ROPE) < half, pltpu.roll(x, LANES - half, 1), pltpu.roll(x, half, 1))
    return x * cos + swapped * sin


def _mla_prep_body(cq_ref, ckv_ref, kr_ref, cos_ref, sin_ref, qg_ref, kvg_ref, wq_ref, wk_ref, wv_ref,
                   *out_refs, absorbed, scale):
    cos, sin = cos_ref[...], sin_ref[...]
    ckv_n = _rms(ckv_ref[...]) * kvg_ref[...]
    kr_r = _rope128(kr_ref[...], cos, sin)
    qn = (_rms(cq_ref[...]) * qg_ref[...]).astype(BF16)
    hw = QK_NOPE + LANES
    if absorbed:
        ckvn_ref, krr_ref, qlat_ref, qrope_ref = out_refs
    else:
        ckvn_ref, krr_ref, q_ref, k_ref, v_ref = out_refs
        ckv_b = ckv_n.astype(BF16)
        v_ref[...] = _dot(ckv_b, wv_ref[...]).astype(BF16)
        kr_b = kr_r.astype(BF16)
    ckvn_ref[...] = ckv_n
    krr_ref[...] = kr_r
    for h in range(MLA_HEADS):
        qh = _dot(qn, wq_ref[:, h * hw:(h + 1) * hw]) * scale
        q_nope = qh[:, :QK_NOPE]
        q_rope = _rope128(qh[:, QK_NOPE:], cos, sin)
        if absorbed:
            qlat_ref[h] = _dot_t(q_nope.astype(BF16), wk_ref[:, h * QK_NOPE:(h + 1) * QK_NOPE]).astype(BF16)
            qrope_ref[h] = q_rope[:, :QK_ROPE].astype(BF16)
        else:
            q_ref[:, h * hw:h * hw + QK_NOPE] = q_nope.astype(BF16)
            q_ref[:, h * hw + QK_NOPE:(h + 1) * hw] = q_rope.astype(BF16)
            k_ref[:, h * hw:h * hw + QK_NOPE] = _dot(ckv_b, wk_ref[:, h * QK_NOPE:(h + 1) * QK_NOPE]).astype(BF16)
            k_ref[:, h * hw + QK_NOPE:(h + 1) * hw] = kr_b


def _mla_prep(proj, cos_t, sin_t, q_g, kv_g, wq_r, wk, wv, *, row0, nrows, offs, absorbed):
    qlora, kvlora = q_g.shape[0], kv_g.shape[0]
    tm = _pick(nrows, (512, 256, 128))
    assert row0 % tm == 0
    rb0 = row0 // tm
    hw = QK_NOPE + LANES
    scale = 1.0 / math.sqrt(QK_NOPE + QK_ROPE)
    const = lambda i: (0, 0)
    in_specs = [pl.BlockSpec((tm, qlora), lambda i: (rb0 + i, offs["cq"] // qlora)),
                pl.BlockSpec((tm, kvlora), lambda i: (rb0 + i, offs["ckv"] // kvlora)),
                pl.BlockSpec((tm, LANES), lambda i: (rb0 + i, offs["kr"] // LANES)),
                pl.BlockSpec((tm, LANES), lambda i: (rb0 + i, 0)),
                pl.BlockSpec((tm, LANES), lambda i: (rb0 + i, 0)),
                pl.BlockSpec((1, qlora), const), pl.BlockSpec((1, kvlora), const),
                pl.BlockSpec(wq_r.shape, const), pl.BlockSpec(wk.shape, const), pl.BlockSpec(wv.shape, const)]
    out_specs = [pl.BlockSpec((tm, kvlora), lambda i: (i, 0)), pl.BlockSpec((tm, LANES), lambda i: (i, 0))]
    out_shape = [jax.ShapeDtypeStruct((nrows, kvlora), F32), jax.ShapeDtypeStruct((nrows, LANES), F32)]
    if absorbed:
        out_specs += [pl.BlockSpec((MLA_HEADS, tm, kvlora), lambda i: (0, i, 0)),
                      pl.BlockSpec((MLA_HEADS, tm, QK_ROPE), lambda i: (0, i, 0))]
        out_shape += [jax.ShapeDtypeStruct((MLA_HEADS, nrows, kvlora), BF16),
                      jax.ShapeDtypeStruct((MLA_HEADS, nrows, QK_ROPE), BF16)]
    else:
        out_specs += [pl.BlockSpec((tm, MLA_HEADS * hw), lambda i: (i, 0)),
                      pl.BlockSpec((tm, MLA_HEADS * hw), lambda i: (i, 0)),
                      pl.BlockSpec((tm, MLA_HEADS * V_HEAD), lambda i: (i, 0))]
        out_shape += [jax.ShapeDtypeStruct((nrows, MLA_HEADS * hw), BF16),
                      jax.ShapeDtypeStruct((nrows, MLA_HEADS * hw), BF16),
                      jax.ShapeDtypeStruct((nrows, MLA_HEADS * V_HEAD), BF16)]
    return pl.pallas_call(
        functools.partial(_mla_prep_body, absorbed=absorbed, scale=scale),
        grid=(nrows // tm,),
        in_specs=in_specs, out_specs=out_specs, out_shape=out_shape,
        compiler_params=_params("arbitrary"),
    )(proj, proj, proj, cos_t, sin_t, q_g.reshape(1, -1), kv_g.reshape(1, -1), wq_r, wk, wv)


ATT_SUB = 256


def _attn_tile(q_sub, k_ref, v_ref, kcols, vcols, s_scr, p_scr, nk):
    sub = ATT_SUB
    nfull = nk - sub
    shift = CHUNK.bit_length() - 1

    def fold(x, op):
        return op(x.reshape(sub // 8, 8, sub), axis=0)

    if nfull:
        s_scr[0:nfull, :] = _dot_t(k_ref[0, 0:nfull, kcols], q_sub)
    krow = lax.broadcasted_iota(I32, (sub, sub), 0) >> shift
    qcol = lax.broadcasted_iota(I32, (sub, sub), 1) >> shift
    s_scr[nfull:nk, :] = jnp.where(krow <= qcol, _dot_t(k_ref[0, nfull:nk, kcols], q_sub), -jnp.inf)

    mrun = None
    for j in range(nk // sub):
        f = fold(s_scr[j * sub:(j + 1) * sub, :], jnp.max)
        mrun = f if mrun is None else jnp.maximum(mrun, f)
    m = jnp.max(mrun, axis=0, keepdims=True)
    lrun = None
    for j in range(nk // sub):
        pt = jnp.exp(s_scr[j * sub:(j + 1) * sub, :] - m)
        f = fold(pt, jnp.sum)
        lrun = f if lrun is None else lrun + f
        p_scr[j * sub:(j + 1) * sub, :] = pt.astype(BF16)
    l = jnp.sum(lrun, axis=0, keepdims=True)
    acc = lax.dot_general(v_ref[0, 0:nk, vcols], p_scr[0:nk, :], (((0,), (0,)), ((), ())),
                          preferred_element_type=F32)
    return (acc / l).T


ATT_HEADS = 2


def _attn_body(q_ref, k_ref, v_ref, o_ref, s_scr, p_scr, *, tq, nq):
    qi = pl.program_id(2)
    sub = ATT_SUB
    hw = q_ref.shape[2] // ATT_HEADS
    for c in range(nq):
        @pl.when(qi == c)
        def _():
            for hh in range(ATT_HEADS):
                kcols = slice(hh * hw, (hh + 1) * hw)
                vcols = slice(hh * V_HEAD, (hh + 1) * V_HEAD)
                for r in range(tq // sub):
                    q_sub = q_ref[0, r * sub:(r + 1) * sub, kcols]
                    out = _attn_tile(q_sub, k_ref, v_ref, kcols, vcols, s_scr, p_scr, c * tq + (r + 1) * sub)
                    o_ref[r * sub:(r + 1) * sub, vcols] = out.astype(o_ref.dtype)


def _attn_prompt(q, k, v, *, nb, seq):
    hw = (QK_NOPE + LANES) * ATT_HEADS
    vw = V_HEAD * ATT_HEADS
    tq = min(1024, seq // 2)
    assert seq % tq == 0 and tq % ATT_SUB == 0 and ATT_SUB % CHUNK == 0 and MLA_HEADS % ATT_HEADS == 0
    nq = seq // tq
    q3 = q.reshape(nb, seq, -1)
    k3 = k.reshape(nb, seq, -1)
    v3 = v.reshape(nb, seq, -1)
    return pl.pallas_call(
        functools.partial(_attn_body, tq=tq, nq=nq),
        grid=(nb, MLA_HEADS // ATT_HEADS, nq),
        in_specs=[pl.BlockSpec((1, tq, hw), lambda b, h, i: (b, i, h)),
                  pl.BlockSpec((1, seq, hw), lambda b, h, i: (b, 0, h)),
                  pl.BlockSpec((1, seq, vw), lambda b, h, i: (b, 0, h))],
        out_specs=pl.BlockSpec((tq, vw), lambda b, h, i: (b * nq + i, h)),
        out_shape=jax.ShapeDtypeStruct((nb * seq, MLA_HEADS * V_HEAD), BF16),
        scratch_shapes=[pltpu.VMEM((seq, ATT_SUB), F32), pltpu.VMEM((seq, ATT_SUB), BF16)],
        compiler_params=_params("arbitrary", "arbitrary", "arbitrary"),
    )(q3, k3, v3)


def _attn_sample_body(ql_ref, qr_ref, pckv_ref, pkr_ref, nckv_ref, nkr_ref, o_ref, *, past, seq):
    nh = ql_ref.shape[0]
    ql = ql_ref[...].reshape(nh * seq, ql_ref.shape[2])
    qr = qr_ref[...].reshape(nh * seq, qr_ref.shape[2])
    pckv = pckv_ref[0].astype(BF16)
    nckv = nckv_ref[...].astype(BF16)
    s_p = _dot_t(ql, pckv) + _dot_t(qr, pkr_ref[0].astype(BF16))
    s_n = _dot_t(ql, nckv) + _dot_t(qr, nkr_ref[:, :QK_ROPE].astype(BF16))
    shift = CHUNK.bit_length() - 1
    q_chunk = (past + lax.broadcasted_iota(I32, (nh * seq, 1), 0) % seq) >> shift
    kp_chunk = lax.broadcasted_iota(I32, (1, past), 1) >> shift
    kn_chunk = (past + lax.broadcasted_iota(I32, (1, seq), 1)) >> shift
    s_p = jnp.where(kp_chunk <= q_chunk, s_p, -jnp.inf)
    s_n = jnp.where(kn_chunk <= q_chunk, s_n, -jnp.inf)
    m = jnp.maximum(jnp.max(s_p, axis=1, keepdims=True), jnp.max(s_n, axis=1, keepdims=True))
    p_p = jnp.exp(s_p - m)
    p_n = jnp.exp(s_n - m)
    l = jnp.sum(p_p, axis=1, keepdims=True) + jnp.sum(p_n, axis=1, keepdims=True)
    o = (_dot(p_p.astype(BF16), pckv) + _dot(p_n.astype(BF16), nckv)) / l
    o_ref[...] = o.reshape(o_ref.shape).astype(o_ref.dtype)


def _attn_sample(qlat, qrope, past_ckv, past_kr, ckv_n, kr_r, *, nb, seq):
    past = past_ckv.shape[1]
    r = past_ckv.shape[2]
    return pl.pallas_call(
        functools.partial(_attn_sample_body, past=past, seq=seq),
        grid=(nb,),
        in_specs=[pl.BlockSpec((MLA_HEADS, seq, r), lambda b: (0, b, 0)),
                  pl.BlockSpec((MLA_HEADS, seq, QK_ROPE), lambda b: (0, b, 0)),
                  pl.BlockSpec((1, past, r), lambda b: (b, 0, 0)),
                  pl.BlockSpec((1, past, QK_ROPE), lambda b: (b, 0, 0)),
                  pl.BlockSpec((seq, r), lambda b: (b, 0)),
                  pl.BlockSpec((seq, LANES), lambda b: (b, 0))],
        out_specs=pl.BlockSpec((MLA_HEADS, seq, r), lambda b: (0, b, 0)),
        out_shape=jax.ShapeDtypeStruct((MLA_HEADS, nb * seq, r), BF16),
        compiler_params=_params("arbitrary"),
    )(qlat, qrope, past_ckv, past_kr, ckv_n, kr_r)


def _uv_body(o_ref, w_ref, y_ref):
    y_ref[...] = _dot(o_ref[0], w_ref[...]).astype(y_ref.dtype)


def _uv_sample(o_lat, wv):
    nh, nrows, r = o_lat.shape
    return pl.pallas_call(
        _uv_body,
        grid=(nh,),
        in_specs=[pl.BlockSpec((1, nrows, r), lambda h: (h, 0, 0)),
                  pl.BlockSpec((r, V_HEAD), lambda h: (0, h))],
        out_specs=pl.BlockSpec((nrows, V_HEAD), lambda h: (0, h)),
        out_shape=jax.ShapeDtypeStruct((nrows, nh * V_HEAD), BF16),
        compiler_params=_params("arbitrary"),
    )(o_lat, wv)


def _merge1_body(ysp_ref, ymp_ref, yss_ref, yms_ref, ws_ref, wm_ref, ga_ref, gb_ref, o_ref, *, npb):
    def run(ys_ref, ym_ref):
        a = _dot(ys_ref[...], ws_ref[...])
        b = _dot(ym_ref[...], wm_ref[...])
        o_ref[...] = (_sigmoid(ga_ref[...]) * a + _sigmoid(gb_ref[...]) * b).astype(o_ref.dtype)

    i = pl.program_id(1)
    pl.when(i < npb)(lambda: run(ysp_ref, ymp_ref))
    pl.when(i >= npb)(lambda: run(yss_ref, yms_ref))


def _merge1(ys_p, ym_p, ys_s, ym_s, w_ssd_out, w_mla_out, proj, offs):
    (tp, inner), ts = ys_p.shape, ys_s.shape[0]
    dm = ym_p.shape[1]
    d = w_ssd_out.shape[1]
    tm = _pick(math.gcd(tp, ts), (512, 256, 128))
    tn = 512
    npb, nsb = tp // tm, ts // tm
    prow = lambda j, i: (jnp.minimum(i, npb - 1), 0)
    srow = lambda j, i: (jnp.maximum(i - npb, 0), 0)
    return pl.pallas_call(
        functools.partial(_merge1_body, npb=npb),
        grid=(d // tn, npb + nsb),
        in_specs=[pl.BlockSpec((tm, inner), prow), pl.BlockSpec((tm, dm), prow),
                  pl.BlockSpec((tm, inner), srow), pl.BlockSpec((tm, dm), srow),
                  pl.BlockSpec((inner, tn), lambda j, i: (0, j)),
                  pl.BlockSpec((dm, tn), lambda j, i: (0, j)),
                  pl.BlockSpec((tm, tn), lambda j, i: (i, offs["ga"] // tn + j)),
                  pl.BlockSpec((tm, tn), lambda j, i: (i, offs["gb"] // tn + j))],
        out_specs=pl.BlockSpec((tm, tn), lambda j, i: (i, j)),
        out_shape=jax.ShapeDtypeStruct((tp + ts, d), BF16),
        compiler_params=_params("arbitrary", "arbitrary"),
    )(ys_p, ym_p, ys_s, ym_s, w_ssd_out, w_mla_out, proj, proj)


def _merge2_body(m_ref, w_ref, xp_ref, xs_ref, g1_ref, sc_ref, sh_ref, ng_ref, wr_ref, x1_ref, h2_ref, lg_ref,
                 *, npb):
    rows, d = xp_ref.shape
    x = jnp.where(pl.program_id(0) < npb, xp_ref[...], xs_ref[...])
    upd = _dot(m_ref[...], w_ref[...]).reshape(rows // MOD_ROWS, MOD_ROWS, d) * g1_ref[...]
    x1 = x + upd.reshape(rows, d)
    x1_ref[...] = x1
    h2 = _modulate(_rms(x1) * ng_ref[...], sc_ref, sh_ref)
    _store_items(h2_ref, _pack_halves(h2))
    lg_ref[...] = _dot3(wr_ref[...], h2, dot=_dot_t)


def _merge2(merged, w_merge, xp, xs, modg, norm_g, w_router_t):
    d = xp.shape[1]
    t = xp.shape[0] + xs.shape[0]
    assert d == 2 * SUBLANES * LANES
    ne = w_router_t.shape[0]
    tm, npb, prow, srow = _two_stream_rows(xp.shape[0], xs.shape[0], (512, 256, 128))
    ng = tm // MOD_ROWS
    return pl.pallas_call(
        functools.partial(_merge2_body, npb=npb),
        grid=(t // tm,),
        in_specs=[pl.BlockSpec((tm, d), lambda i: (i, 0)),
                  pl.BlockSpec((d, d), lambda i: (0, 0)),
                  pl.BlockSpec((tm, d), prow), pl.BlockSpec((tm, d), srow),
                  pl.BlockSpec((ng, 1, d), lambda i: (i, 0, 2)),
                  pl.BlockSpec((ng, 1, d), lambda i: (i, 0, 4)),
                  pl.BlockSpec((ng, 1, d), lambda i: (i, 0, 3)),
                  pl.BlockSpec((1, d), lambda i: (0, 0)),
                  pl.BlockSpec((ne, d), lambda i: (0, 0))],
        out_specs=[pl.BlockSpec((tm, d), lambda i: (i, 0)),
                   pl.BlockSpec((tm * SUBLANES, LANES), lambda i: (i, 0)),
                   pl.BlockSpec((ne, tm), lambda i: (0, i))],
        out_shape=[jax.ShapeDtypeStruct((t, d), F32), jax.ShapeDtypeStruct((t * SUBLANES, LANES), U32),
                   jax.ShapeDtypeStruct((ne, t), F32)],
        compiler_params=_params("arbitrary"),
    )(merged, w_merge, xp, xs, modg, modg, modg, norm_g.reshape(1, d), w_router_t)


def _route_body(lg_ref, eb_ref, eidx_ref, pos_ref, w_ref, cnt_ref, carry):
    ne, tr = lg_ref.shape
    per_group = ne // N_GROUPS

    @pl.when(pl.program_id(0) == 0)
    def _():
        carry[...] = jnp.zeros_like(carry)

    scores = _sigmoid(lg_ref[...])
    choice = scores + eb_ref[...]
    sub = lax.broadcasted_iota(I32, (per_group, tr), 0)
    gscore, blocks = [], []
    for g in range(N_GROUPS):
        blk = choice[g * per_group:(g + 1) * per_group, :]
        m1 = jnp.max(blk, axis=0, keepdims=True)
        first = jnp.min(jnp.where(blk == m1, sub, per_group), axis=0, keepdims=True)
        m2 = jnp.max(jnp.where(sub == first, -jnp.inf, blk), axis=0, keepdims=True)
        gscore.append(m1 + m2)
        blocks.append(blk)
    masked = []
    for g in range(N_GROUPS):
        rank = jnp.zeros((1, tr), I32)
        for g2 in range(N_GROUPS):
            if g2 == g:
                continue
            beats = (gscore[g2] > gscore[g]) | ((gscore[g2] == gscore[g]) & (g2 < g))
            rank = rank + beats.astype(I32)
        masked.append(jnp.where(rank < TOPK_GROUPS, blocks[g], -jnp.inf))
    cm = jnp.concatenate(masked, axis=0)

    eid = lax.broadcasted_iota(I32, (ne, tr), 0)
    rank = jnp.zeros((ne, tr), I32)
    for e2 in range(ne):
        rowv = cm[e2:e2 + 1, :]
        beats = (rowv > cm) | ((rowv == cm) & (eid > e2))
        rank = rank + beats.astype(I32)
    sel = rank < TOP_K
    wsel = jnp.where(sel, scores, 0.0)
    wfull = wsel / jnp.sum(wsel, axis=0, keepdims=True) * ROUTED_SCALE

    r = lax.broadcasted_iota(I32, (tr, tr), 0)
    c = lax.broadcasted_iota(I32, (tr, tr), 1)
    before = jnp.where(r < c, 1.0, 0.0).astype(BF16)
    self = jnp.where(sel, 1.0, 0.0)
    pos = carry[:, 0:1] + _dot(self.astype(BF16), before)
    carry[...] = carry[...] + jnp.sum(self, axis=1, keepdims=True)
    cnt_ref[...] = carry[...]

    eid_f = eid.astype(F32)
    for k in range(TOP_K):
        pick = sel & (rank == k)
        eidx_ref[k:k + 1, :] = jnp.sum(jnp.where(pick, eid_f, 0.0), axis=0, keepdims=True).astype(I32)
        pos_ref[k:k + 1, :] = jnp.sum(jnp.where(pick, pos, 0.0), axis=0, keepdims=True).astype(I32)
        w_ref[k:k + 1, :] = jnp.sum(jnp.where(pick, wfull, 0.0), axis=0, keepdims=True)


def _route(logits_t, e_bias):
    ne, t = logits_t.shape
    tr = _pick(t, (512, 256, 128))
    return pl.pallas_call(
        _route_body,
        grid=(t // tr,),
        in_specs=[pl.BlockSpec((ne, tr), lambda i: (0, i)), pl.BlockSpec((ne, 1), lambda i: (0, 0))],
        out_specs=[pl.BlockSpec((TOP_K, tr), lambda i: (0, i)), pl.BlockSpec((TOP_K, tr), lambda i: (0, i)),
                   pl.BlockSpec((TOP_K, tr), lambda i: (0, i)), pl.BlockSpec((ne, LANES), lambda i: (0, 0))],
        out_shape=[jax.ShapeDtypeStruct((TOP_K, t), I32), jax.ShapeDtypeStruct((TOP_K, t), I32),
                   jax.ShapeDtypeStruct((TOP_K, t), F32), jax.ShapeDtypeStruct((ne, LANES), F32)],
        scratch_shapes=[pltpu.VMEM((ne, LANES), F32)],
        compiler_params=_params("arbitrary"),
    )(logits_t, e_bias.reshape(ne, 1))


def _row_copy(src, s, dst, d, sem):
    rows = lambda i: pl.ds(pl.multiple_of(i * SUBLANES, SUBLANES), SUBLANES)
    return pltpu.make_async_copy(src.at[rows(s), :], dst.at[rows(d), :], sem)


def _load_items(ref, start, n):
    return jnp.concatenate([ref[pl.ds(start * SUBLANES + c, n, stride=SUBLANES), :] for c in range(SUBLANES)],
                           axis=1)


def _store_items(ref, val):
    n = val.shape[0]
    for c in range(SUBLANES):
        ref[pl.ds(c, n, stride=SUBLANES), :] = val[:, c * LANES:(c + 1) * LANES]


def _drain(wait_one, n, group=64):
    assert n % group == 0

    def body(j, c):
        for _ in range(group):
            wait_one()
        return c

    lax.fori_loop(0, n // group, body, 0)


def _dispatch_body(fill_lo_ref, fill_hi_ref, h_ref, dest_ref, hs_ref, dest_s, zrow, sem, dsem, *, tb):
    i = pl.program_id(0)
    cp = pltpu.make_async_copy(dest_ref.at[i], dest_s, dsem)
    cp.start()

    @pl.when(i == 0)
    def _():
        zrow[...] = jnp.zeros_like(zrow)

        def per_expert(fn):
            def body(e, _):
                lax.fori_loop(fill_lo_ref[e], fill_hi_ref[e], lambda s, c: (fn(s), c)[1], 0)
                return 0
            lax.fori_loop(0, fill_lo_ref.shape[0], body, 0)

        per_expert(lambda s: _row_copy(zrow, 0, hs_ref, s, sem).start())
        per_expert(lambda s: _row_copy(zrow, 0, hs_ref, s, sem).wait())

    cp.wait()

    def issue(t, c):
        for k in range(TOP_K):
            _row_copy(h_ref, t, hs_ref, dest_s[t * TOP_K + k], sem).start(priority=k % 2)
        return c

    lax.fori_loop(0, tb, issue, 0, unroll=4)
    _drain(lambda: _row_copy(h_ref, 0, hs_ref, 0, sem).wait(), TOP_K * tb)


def _dispatch(h2, dest_blk, fill_lo, fill_hi, n_slots):
    nblk, n = dest_blk.shape
    tb = n // TOP_K
    return pl.pallas_call(
        functools.partial(_dispatch_body, tb=tb),
        grid_spec=pltpu.PrefetchScalarGridSpec(
            num_scalar_prefetch=2,
            grid=(nblk,),
            in_specs=[pl.BlockSpec((tb * SUBLANES, LANES), lambda i, lo, hi: (i, 0)),
                      pl.BlockSpec(memory_space=pl.ANY)],
            out_specs=pl.BlockSpec(memory_space=pl.ANY),
            scratch_shapes=[pltpu.SMEM((n,), I32), pltpu.VMEM((SUBLANES, LANES), h2.dtype),
                            pltpu.SemaphoreType.DMA, pltpu.SemaphoreType.DMA]),
        out_shape=jax.ShapeDtypeStruct((n_slots * SUBLANES, LANES), h2.dtype),
        compiler_params=_params("arbitrary"),
    )(fill_lo, fill_hi, h2, dest_blk)


def _experts_body(be_ref, nu_ref, x_ref, wg_ref, wu_ref, wd_ref, o_ref, wgu_s, wd_s):
    i = pl.program_id(0)
    hid = wg_ref.shape[2]

    @pl.when(i < nu_ref[0])
    def _():
        @pl.when((i == 0) | (be_ref[i] != be_ref[jnp.maximum(i - 1, 0)]))
        def _():
            wgu_s[:, :hid] = wg_ref[0].astype(BF16)
            wgu_s[:, hid:] = wu_ref[0].astype(BF16)
            wd_s[...] = wd_ref[0].astype(BF16)

        lo, hi = _unpack_halves(_load_items(x_ref, 0, x_ref.shape[0] // SUBLANES))
        x = jnp.concatenate([lo.astype(BF16), hi.astype(BF16)], axis=1)
        gu = _dot(x, wgu_s[...])
        act = (_silu(gu[:, :hid]) * gu[:, hid:]).astype(BF16)
        _store_items(o_ref, _pack_halves(_dot(act, wd_s[...])))


def _experts(hs, block_e, n_used, wg, wu, wd):
    n_slots = hs.shape[0] // SUBLANES
    ne, d, hid = wg.shape
    assert d == 2 * SUBLANES * LANES
    bm = MOE_ROWS
    nblocks = n_slots // bm
    blk = lambda i, be, nu: (jnp.minimum(i, nu[0] - 1), 0)
    return pl.pallas_call(
        _experts_body,
        grid_spec=pltpu.PrefetchScalarGridSpec(
            num_scalar_prefetch=2,
            grid=(nblocks,),
            in_specs=[pl.BlockSpec((bm * SUBLANES, LANES), blk),
                      pl.BlockSpec((1, d, hid), lambda i, be, nu: (be[i], 0, 0)),
                      pl.BlockSpec((1, d, hid), lambda i, be, nu: (be[i], 0, 0)),
                      pl.BlockSpec((1, hid, d), lambda i, be, nu: (be[i], 0, 0))],
            out_specs=pl.BlockSpec((bm * SUBLANES, LANES), blk),
            scratch_shapes=[pltpu.VMEM((d, 2 * hid), BF16), pltpu.VMEM((hid, d), BF16)]),
        out_shape=jax.ShapeDtypeStruct(hs.shape, U32),
        compiler_params=_params("arbitrary"),
    )(block_e, n_used, hs, wg, wu, wd)


def _combine_body(ys_ref, dest_ref, w_ref, h_ref, x1_ref, g2_ref, wgu_ref, wd_ref, fg_ref, op_ref, os_ref,
                  dest_s, gbuf, sem, dsem, *, tb, npb):
    i = pl.program_id(0)
    slot = lax.rem(i, 2)

    def start_gathers(step, buf):
        cp = pltpu.make_async_copy(dest_ref.at[step], dest_s.at[buf], dsem)
        cp.start()
        cp.wait()

        def issue(t, c):
            for k in range(TOP_K):
                _row_copy(ys_ref, dest_s[buf, t * TOP_K + k], gbuf.at[buf], k * tb + t,
                          sem.at[buf]).start(priority=k % 2)
            return c

        lax.fori_loop(0, tb, issue, 0, unroll=4)

    pl.when(i == 0)(lambda: start_gathers(0, 0))
    pl.when(i + 1 < pl.num_programs(0))(lambda: start_gathers(i + 1, 1 - slot))

    hid = wd_ref.shape[0]
    h_lo, h_hi = _unpack_halves(_load_items(h_ref, 0, tb))
    gu = _dot(jnp.concatenate([h_lo.astype(BF16), h_hi.astype(BF16)], axis=1), wgu_ref[...])
    moe = _dot((_silu(gu[:, :hid]) * gu[:, hid:]).astype(BF16), wd_ref[...])

    cur = gbuf.at[slot]
    _drain(lambda: _row_copy(ys_ref, 0, cur, 0, sem.at[slot]).wait(), TOP_K * tb)
    w = w_ref[...]
    dh = SUBLANES * LANES
    m_lo, m_hi = moe[:, :dh], moe[:, dh:]
    for k in range(TOP_K):
        lo, hi = _unpack_halves(_load_items(cur, k * tb, tb))
        m_lo = m_lo + lo * w[:, k:k + 1]
        m_hi = m_hi + hi * w[:, k:k + 1]
    moe = jnp.concatenate([m_lo, m_hi], axis=1)
    rows, d = moe.shape
    upd = moe.reshape(rows // MOD_ROWS, MOD_ROWS, d) * g2_ref[...]
    x2 = x1_ref[...] + upd.reshape(rows, d)
    y = _rms(x2) * fg_ref[...]

    @pl.when(i < npb)
    def _():
        op_ref[...] = y

    @pl.when(i >= npb)
    def _():
        os_ref[...] = y


def _combine(ys, dest_blk, w_tok, h2, x1, modg, wsh_gu, wsh_d, final_g, *, tp):
    t, d = x1.shape
    nblk, n = dest_blk.shape
    tb = n // TOP_K
    assert tp % tb == 0 and (t - tp) % tb == 0
    npb = tp // tb
    ng = tb // MOD_ROWS
    const = lambda i: (0, 0)
    return pl.pallas_call(
        functools.partial(_combine_body, tb=tb, npb=npb),
        grid=(nblk,),
        in_specs=[pl.BlockSpec(memory_space=pl.ANY),
                  pl.BlockSpec(memory_space=pl.ANY),
                  pl.BlockSpec((tb, TOP_K), lambda i: (i, 0)),
                  pl.BlockSpec((tb * SUBLANES, LANES), lambda i: (i, 0)),
                  pl.BlockSpec((tb, d), lambda i: (i, 0)),
                  pl.BlockSpec((ng, 1, d), lambda i: (i, 0, 5)),
                  pl.BlockSpec(wsh_gu.shape, const), pl.BlockSpec(wsh_d.shape, const),
                  pl.BlockSpec((1, d), const)],
        out_specs=[pl.BlockSpec((tb, d), lambda i: (jnp.minimum(i, npb - 1), 0)),
                   pl.BlockSpec((tb, d), lambda i: (jnp.maximum(i - npb, 0), 0))],
        out_shape=[jax.ShapeDtypeStruct((tp, d), F32), jax.ShapeDtypeStruct((t - tp, d), F32)],
        scratch_shapes=[pltpu.SMEM((2, n), I32), pltpu.VMEM((2, n * SUBLANES, LANES), U32),
                        pltpu.SemaphoreType.DMA((2,)), pltpu.SemaphoreType.DMA],
        compiler_params=_params("arbitrary"),
    )(ys, dest_blk, w_tok, h2, x1, modg, wsh_gu, wsh_d, final_g.reshape(1, d))


def _rope_tables(pos):
    half = QK_ROPE // 2
    freqs = ROPE_THETA ** (-jnp.arange(half, dtype=F32) / half)
    ang = pos.astype(F32)[:, None] * freqs[None, :]
    cos, sin = jnp.cos(ang), jnp.sin(ang)
    return (jnp.concatenate([cos, cos, cos, cos], axis=1),
            jnp.concatenate([-sin, sin, -sin, sin], axis=1))


def _blocked(a, tb):
    k, t = a.shape
    return a.T.reshape(t // tb, tb * k)


def kernel(x_prompt, x_sample, c_prompt, c_sample, cache_conv, state_ssm, cache_ckv, cache_kr, w_ada, b_ada,
           norm1_g, norm2_g, w_in, conv_w, conv_b, dt_bias, a_log, d_skip, ssd_norm_g, w_ssd_out, q_norm_g,
           w_uq, kv_norm_g, w_uk, w_uv, w_mla_out, w_merge_out, w_router, e_bias, w_exp_gate, w_exp_up,
           w_exp_down, w_sh_gate, w_sh_up, w_sh_down, final_norm_g):
    depth = w_in.shape[0]
    assert depth == 1
    bp, seq, d = x_prompt.shape
    bs, lseq, _ = x_sample.shape
    assert lseq == MOD_ROWS and seq % MOD_ROWS == 0
    tp, ts = bp * seq, bs * lseq
    t_all = tp + ts
    nheads = dt_bias.shape[1]
    inner = nheads * SSD_HEAD_DIM
    cdim = conv_w.shape[2]
    gn = SSD_GROUPS * SSD_STATE
    qlora, kvlora = q_norm_g.shape[1], kv_norm_g.shape[1]
    ne = w_router.shape[2]
    assert nheads == 64 and 2 * nheads == LANES

    w = w_in[0]
    o_xbc, o_dt = inner, inner + cdim
    o_cq = o_dt + nheads
    o_ckv = o_cq + qlora
    o_kr = o_ckv + kvlora
    o_gate = o_kr + QK_ROPE
    perm = np.concatenate([np.arange(0, nheads, 2), np.arange(1, nheads, 2)])
    zc = lambda n: jnp.zeros((d, n), w.dtype)
    cols = [w[:, :inner], w[:, o_gate:], w[:, o_xbc:o_dt], w[:, o_cq:o_ckv], w[:, o_ckv:o_kr],
            w[:, o_dt:o_cq][:, perm], zc(LANES - nheads), w[:, o_kr:o_gate], zc(LANES - QK_ROPE)]
    used = inner + 2 * d + cdim + qlora + kvlora + 2 * LANES
    total = -(-used // 512) * 512
    cols.append(zc(total - used))
    w_r = jnp.concatenate(cols, axis=1).astype(BF16)
    offs = {"z": 0, "ga": inner, "gb": inner + d, "x": inner + 2 * d, "bc": 2 * inner + 2 * d}
    offs["cq"] = offs["bc"] + 2 * gn
    offs["ckv"] = offs["cq"] + qlora
    offs["dt"] = offs["ckv"] + kvlora
    offs["kr"] = offs["dt"] + LANES

    hw = QK_NOPE + LANES
    wq_r = jnp.pad(w_uq[0], ((0, 0), (0, 0), (0, hw - QK_NOPE - QK_ROPE))).reshape(qlora, MLA_HEADS * hw).astype(BF16)
    wk = w_uk[0].reshape(kvlora, MLA_HEADS * QK_NOPE).astype(BF16)
    wv = w_uv[0].reshape(kvlora, MLA_HEADS * V_HEAD).astype(BF16)
    pad_l = lambda v: jnp.pad(v[perm], (0, LANES - nheads)).reshape(1, LANES)
    dtb = pad_l(dt_bias[0])
    a_neg = pad_l(-jnp.exp(a_log[0]))
    dskip = jnp.repeat(d_skip[0], SSD_HEAD_DIM).reshape(1, inner)

    c_all = jnp.concatenate([c_prompt, c_sample], axis=0)
    mod = _ada(c_all, w_ada[0], b_ada[0])
    grp = np.concatenate([np.repeat(np.arange(bp), seq // MOD_ROWS), bp + np.arange(bs)])
    modg = mod[grp].reshape(t_all // MOD_ROWS, 1, 6 * d)

    xp, xs = x_prompt.reshape(tp, d), x_sample.reshape(ts, d)
    proj = _inproj(xp, xs, modg, norm1_g[0], w_r, d)

    ssd_args = (conv_w[0], conv_b[0].reshape(1, cdim), dtb, a_neg, dskip, ssd_norm_g[0].reshape(1, inner))
    ys_p, conv_p, ssm_p = _ssd(proj, jnp.zeros((bp, CONV_W - 1, cdim), F32),
                               jnp.zeros((1, bp) + state_ssm.shape[2:], F32), *ssd_args,
                               nb=bp, seq=seq, row0=0, offs=offs, inner=inner)
    ys_s, conv_s, ssm_s = _ssd(proj, cache_conv[0], state_ssm,
                               *ssd_args, nb=bs, seq=lseq, row0=tp, offs=offs, inner=inner)

    past = cache_ckv.shape[2]
    cos_p, sin_p = _rope_tables(jnp.arange(seq))
    cos_s, sin_s = _rope_tables(past + jnp.arange(lseq))
    cos_t = jnp.concatenate([jnp.tile(cos_p, (bp, 1)), jnp.tile(cos_s, (bs, 1))], axis=0)
    sin_t = jnp.concatenate([jnp.tile(sin_p, (bp, 1)), jnp.tile(sin_s, (bs, 1))], axis=0)
    prep = functools.partial(_mla_prep, proj, cos_t, sin_t, q_norm_g[0], kv_norm_g[0], wq_r, wk, wv, offs=offs)
    ckv_p, kr_p, q_p, k_p, v_p = prep(row0=0, nrows=tp, absorbed=False)
    ckv_s, kr_s, qlat, qrope = prep(row0=tp, nrows=ts, absorbed=True)
    ym_p = _attn_prompt(q_p, k_p, v_p, nb=bp, seq=seq)
    o_lat = _attn_sample(qlat, qrope, cache_ckv[0], cache_kr[0], ckv_s, kr_s, nb=bs, seq=lseq)
    ym_s = _uv_sample(o_lat, wv)

    merged = _merge1(ys_p, ym_p, ys_s, ym_s, w_ssd_out[0].astype(BF16), w_mla_out[0].astype(BF16), proj, offs)
    x1, h2, logits_t = _merge2(merged, w_merge_out[0].astype(BF16), xp, xs, modg, norm2_g[0], w_router[0].T)

    eidx, pos, w_sel, cnt = _route(logits_t, e_bias[0])
    bm = MOE_ROWS
    counts = cnt[:, 0].astype(I32)
    padded = (counts + bm - 1) // bm * bm
    pad_end = jnp.cumsum(padded)
    pad_start = pad_end - padded
    nblocks = -(-(t_all * TOP_K) // bm) + ne
    onehot = eidx[:, :, None] == jnp.arange(ne, dtype=I32)
    dest = jnp.sum(jnp.where(onehot, pad_start, 0), axis=-1) + pos
    blk_start = jnp.arange(nblocks, dtype=I32) * bm
    block_e = jnp.minimum(jnp.sum(pad_end[None, :] <= blk_start[:, None], axis=1), ne - 1).astype(I32)
    n_used = (pad_end[-1:] // bm).astype(I32)

    tb_d = _pick(t_all, (256, 128))
    hs = _dispatch(h2, _blocked(dest, tb_d), (pad_start + counts).astype(I32), pad_end.astype(I32), nblocks * bm)
    ys = _experts(hs, block_e, n_used, w_exp_gate[0], w_exp_up[0], w_exp_down[0])
    wsh_gu = jnp.concatenate([w_sh_gate[0], w_sh_up[0]], axis=1).astype(BF16)
    tb_c = _pick(math.gcd(tp, ts), (256, 128))
    y_p, y_s = _combine(ys, _blocked(dest, tb_c), w_sel.T, h2, x1, modg, wsh_gu, w_sh_down[0].astype(BF16),
                        final_norm_g, tp=tp)

    r5 = lambda a, n, l: a.reshape(1, n, l, a.shape[-1])
    return (y_p.reshape(bp, seq, d), y_s.reshape(bs, lseq, d),
            conv_p[None], ssm_p,
            r5(ckv_p, bp, seq), r5(kr_p[:, :QK_ROPE], bp, seq),
            conv_s[None], ssm_s,
            r5(ckv_s, bs, lseq), r5(kr_s[:, :QK_ROPE], bs, lseq))
```

```python
import functools
import math

import numpy as np
import jax
import jax.numpy as jnp
from jax import lax
from jax.experimental import pallas as pl
from jax.experimental.pallas import tpu as pltpu

F32 = jnp.float32
BF16 = jnp.bfloat16
I32 = jnp.int32
U32 = jnp.uint32

EPS = 1e-6
CHUNK = 64
SSD_HEAD_DIM = 64
SSD_GROUPS = 8
SSD_STATE = 128
CONV_W = 4
MLA_HEADS = 16
QK_NOPE = 128
QK_ROPE = 64
V_HEAD = 128
ROPE_THETA = 10000.0
TOP_K = 8
N_GROUPS = 8
TOPK_GROUPS = 4
ROUTED_SCALE = 2.5

LANES = 128
SUBLANES = 8
MOD_ROWS = 32
MOE_ROWS = 512
VMEM_LIMIT = 56 * 1024 * 1024


def _params(*sem):
    return pltpu.CompilerParams(dimension_semantics=sem, vmem_limit_bytes=VMEM_LIMIT)


def _pick(n, cands):
    for c in cands:
        if n % c == 0:
            return c
    raise ValueError(f"no tile in {cands} divides {n}")


def _sigmoid(x):
    return 1.0 / (1.0 + jnp.exp(-x))


def _silu(x):
    return x * _sigmoid(x)


def _dot(a, b):
    return jnp.dot(a, b, preferred_element_type=F32)


def _dot_t(a, b):
    return lax.dot_general(a, b, (((1,), (1,)), ((), ())), preferred_element_type=F32)


def _split(x, n):
    parts = []
    for _ in range(n - 1):
        p = x.astype(BF16)
        parts.append(p)
        x = x - p.astype(F32)
    parts.append(x.astype(BF16))
    return parts


def _dot_exact_rhs(a, b_bf16, n=3):
    acc = None
    for p in _split(a, n):
        t = _dot(p, b_bf16)
        acc = t if acc is None else acc + t
    return acc


def _dot3(a, b, dot=_dot):
    ah, al = _split(a, 2)
    bh, bl = _split(b, 2)
    return dot(ah, bh) + (dot(ah, bl) + dot(al, bh))


def _pack_halves(x):
    h = x.shape[1] // 2
    bits = lambda v: lax.bitcast_convert_type(v.astype(BF16).astype(F32), U32)
    return (bits(x[:, :h]) >> 16) | bits(x[:, h:])


def _unpack_halves(p):
    lo = lax.bitcast_convert_type(p << 16, F32)
    hi = lax.bitcast_convert_type(p & jnp.uint32(0xFFFF0000), F32)
    return lo, hi


def _rms(x):
    return x * lax.rsqrt(jnp.mean(x * x, axis=-1, keepdims=True) + EPS)


def _modulate(y, sc_ref, sh_ref):
    rows, d = y.shape
    y3 = y.reshape(rows // MOD_ROWS, MOD_ROWS, d)
    return (y3 * (1.0 + sc_ref[...]) + sh_ref[...]).reshape(rows, d)


def _ada_body(c_ref, w_ref, b_ref, o_ref):
    o_ref[...] = _dot3(_silu(c_ref[...]), w_ref[...]) + b_ref[...]


def _ada(c_all, w_ada, b_ada):
    nb, d = c_all.shape
    n = w_ada.shape[1]
    tn = _pick(n, (1024, 512, 256, 128))
    return pl.pallas_call(
        _ada_body,
        grid=(n // tn,),
        in_specs=[pl.BlockSpec((nb, d), lambda j: (0, 0)),
                  pl.BlockSpec((d, tn), lambda j: (0, j)),
                  pl.BlockSpec((1, tn), lambda j: (0, j))],
        out_specs=pl.BlockSpec((nb, tn), lambda j: (0, j)),
        out_shape=jax.ShapeDtypeStruct((nb, n), F32),
        compiler_params=_params("arbitrary"),
    )(c_all, w_ada, b_ada.reshape(1, n))


def _two_stream_rows(tp, ts, cands):
    tm = _pick(math.gcd(tp, ts), cands)
    npb = tp // tm
    return tm, npb, (lambda i, *_: (jnp.minimum(i, npb - 1), 0)), (lambda i, *_: (jnp.maximum(i - npb, 0), 0))


def _inproj_body(xp_ref, xs_ref, sc_ref, sh_ref, g_ref, w_ref, o_ref, h_scr, *, npb):
    @pl.when(pl.program_id(1) == 0)
    def _():
        x = jnp.where(pl.program_id(0) < npb, xp_ref[...], xs_ref[...])
        h = _modulate(_rms(x) * g_ref[...], sc_ref, sh_ref)
        h_scr[...] = h.astype(BF16)

    o_ref[...] = _dot(h_scr[...], w_ref[...])


def _inproj(xp, xs, modg, norm_g, w_r, d):
    t = xp.shape[0] + xs.shape[0]
    n = w_r.shape[1]
    tm, npb, prow, srow = _two_stream_rows(xp.shape[0], xs.shape[0], (1024, 512, 256, 128))
    tn = 512
    ng = tm // MOD_ROWS
    return pl.pallas_call(
        functools.partial(_inproj_body, npb=npb),
        grid=(t // tm, n // tn),
        in_specs=[pl.BlockSpec((tm, d), prow), pl.BlockSpec((tm, d), srow),
                  pl.BlockSpec((ng, 1, d), lambda i, j: (i, 0, 1)),
                  pl.BlockSpec((ng, 1, d), lambda i, j: (i, 0, 0)),
                  pl.BlockSpec((1, d), lambda i, j: (0, 0)),
                  pl.BlockSpec((d, tn), lambda i, j: (0, j))],
        out_specs=pl.BlockSpec((tm, tn), lambda i, j: (i, j)),
        out_shape=jax.ShapeDtypeStruct((t, n), F32),
        scratch_shapes=[pltpu.VMEM((tm, d), BF16)],
        compiler_params=_params("arbitrary", "arbitrary"),
    )(xp, xs, modg, modg, norm_g.reshape(1, d), w_r)


def _ssd_body(z_ref, x_ref, bc_ref, dt_ref, conv0_ref, h0_ref, cw_ref, cb_ref, dtb_ref, a_ref,
              dskip_ref, ng_ref, y_ref, nconv_ref, nssm_ref,
              carry_x, carry_bc, xp_x, xp_bc, state, y_scr, *, lb, inner):
    q = CHUNK
    c = pl.program_id(1)
    nc = pl.num_programs(1)
    gn = SSD_GROUPS * SSD_STATE

    @pl.when(c == 0)
    def _():
        carry_x[...] = jnp.zeros_like(carry_x)
        carry_bc[...] = jnp.zeros_like(carry_bc)
        carry_x[8 - (CONV_W - 1):8, :] = conv0_ref[0, :, :inner]
        carry_bc[8 - (CONV_W - 1):8, :] = conv0_ref[0, :, inner:]
        state[...] = h0_ref[0, 0].reshape(state.shape)

    def pad_rows(v):
        if lb == q:
            return v
        return jnp.concatenate([v, jnp.zeros((q - lb, v.shape[1]), v.dtype)], axis=0)

    def conv(raw, carry, xp, w_lo, w_hi):
        xp[0:8, :] = carry[...]
        xp[8:8 + q, :] = raw
        acc = cb_ref[:, w_lo:w_hi]
        for j in range(CONV_W):
            acc = acc + xp[8 - j:8 - j + q, :] * cw_ref[CONV_W - 1 - j:CONV_W - j, w_lo:w_hi]
        return _silu(acc)

    x_raw = pad_rows(x_ref[...])
    bc_raw = pad_rows(bc_ref[...])
    xc = conv(x_raw, carry_x, xp_x, 0, inner)
    bcc = conv(bc_raw, carry_bc, xp_bc, inner, inner + 2 * gn)
    if lb == q:
        carry_x[...] = x_raw[q - 8:q, :]
        carry_bc[...] = bc_raw[q - 8:q, :]

    @pl.when(c == nc - 1)
    def _():
        nconv_ref[0, :, :inner] = x_raw[lb - (CONV_W - 1):lb, :]
        nconv_ref[0, :, inner:] = bc_raw[lb - (CONV_W - 1):lb, :]

    dtr = dt_ref[...] + dtb_ref[...]
    dtv = jnp.maximum(dtr, 0.0) + jnp.log1p(jnp.exp(-jnp.abs(dtr)))
    dtv = pad_rows(dtv)
    la = dtv * a_ref[...]

    row = lax.broadcasted_iota(I32, (q, q), 0)
    col = lax.broadcasted_iota(I32, (q, q), 1)
    tri = jnp.where(row >= col, 1.0, 0.0).astype(BF16)
    parts = _split(la, 3)
    a_cum = _dot(tri, parts[0]) + _dot(tri, parts[1]) + _dot(tri, parts[2])
    a_last = a_cum[q - 1:q, :]

    r2 = lax.broadcasted_iota(I32, (q, 2 * q), 0)
    c2 = lax.broadcasted_iota(I32, (q, 2 * q), 1)
    u_e = jnp.where((c2 < q) & (r2 <= c2), 1.0, 0.0).astype(BF16)
    u_o = jnp.where((c2 >= q) & (r2 <= c2 - q), 1.0, 0.0).astype(BF16)
    i_e = jnp.where(r2 == c2, 1.0, 0.0).astype(BF16)
    i_o = jnp.where(r2 == c2 - q, 1.0, 0.0).astype(BF16)
    la_t = la.T
    dt_t = dtv.T
    hp = la_t.shape[0] // 4
    acol = _dot_exact_rhs(la_t[0:hp], u_e) + _dot_exact_rhs(la_t[hp:2 * hp], u_o)
    dtrow = _dot_exact_rhs(dt_t[0:hp], i_e) + _dot_exact_rhs(dt_t[hp:2 * hp], i_o)
    w_t = (dtv * jnp.exp(a_last - a_cum)).T
    dec_b = jnp.broadcast_to(jnp.exp(jnp.sum(la_t, axis=1, keepdims=True)), (la_t.shape[0], SSD_STATE))

    lane = lax.broadcasted_iota(I32, (q, 2 * q), 1)
    causal2 = lax.broadcasted_iota(I32, (q, 2 * q), 0) >= jnp.where(lane < q, lane, lane - q)
    first_half = lane < q
    rr = lax.broadcasted_iota(I32, (2 * SSD_HEAD_DIM, 2 * SSD_HEAD_DIM), 0)
    cc = lax.broadcasted_iota(I32, (2 * SSD_HEAD_DIM, 2 * SSD_HEAD_DIM), 1)
    bd_mask = (rr < SSD_HEAD_DIM) == (cc < SSD_HEAD_DIM)
    top_rows_q = lax.broadcasted_iota(I32, (2 * SSD_HEAD_DIM, q), 0) < SSD_HEAD_DIM
    top_rows_n = lax.broadcasted_iota(I32, (2 * SSD_HEAD_DIM, SSD_STATE), 0) < SSD_HEAD_DIM

    pairs_per_group = (inner // SSD_HEAD_DIM) // SSD_GROUPS // 2
    for g in range(SSD_GROUPS):
        b_g = bcc[:, g * SSD_STATE:(g + 1) * SSD_STATE].astype(BF16)
        c_g = bcc[:, gn + g * SSD_STATE:gn + (g + 1) * SSD_STATE].astype(BF16)
        cb2 = _dot_t(c_g, jnp.concatenate([b_g, b_g], axis=0))
        for kk in range(pairs_per_group):
            k = g * pairs_per_group + kk
            lo, hi = k * 2 * SSD_HEAD_DIM, (k + 1) * 2 * SSD_HEAD_DIM
            xp = xc[:, lo:hi]
            arow = jnp.where(first_half, a_cum[:, k:k + 1], a_cum[:, hp + k:hp + k + 1])
            seg = jnp.where(causal2, arow - acol[k:k + 1, :], -jnp.inf)
            m = (jnp.exp(seg) * cb2 * dtrow[k:k + 1, :]).astype(BF16)
            xbd = jnp.where(bd_mask, jnp.concatenate([xp, xp], axis=0), 0.0).astype(BF16)
            y_diag = _dot(m, xbd)
            s_k = state[lo:hi, :]
            y_off = _dot_t(c_g, s_k.astype(BF16)) * jnp.exp(arow)
            y_scr[:, lo:hi] = y_diag + y_off + xp * dskip_ref[:, lo:hi]
            w2 = jnp.where(top_rows_q, w_t[k:k + 1, :], w_t[hp + k:hp + k + 1, :])
            contrib = _dot((xp.T * w2).astype(BF16), b_g)
            d_k = jnp.where(top_rows_n, dec_b[k:k + 1, :], dec_b[hp + k:hp + k + 1, :])
            state[lo:hi, :] = s_k * d_k + contrib

    @pl.when(c == nc - 1)
    def _():
        nssm_ref[0, 0] = state[...].reshape(nssm_ref.shape[2:])

    y = y_scr[0:lb, :] * _silu(z_ref[...])
    y_ref[...] = (_rms(y) * ng_ref[...]).astype(y_ref.dtype)


def _ssd(proj, conv0, h0, conv_w, conv_b, dtb, a_neg, dskip, norm_g, *, nb, seq, row0, offs, inner):
    q = CHUNK
    lb = min(seq, q)
    assert seq % lb == 0 and row0 % lb == 0 and lb % 16 == 0
    nc = seq // lb
    rb0 = row0 // lb
    cdim = conv_w.shape[1]
    hp_rows = inner
    state_shape = (inner // SSD_HEAD_DIM, SSD_HEAD_DIM, SSD_STATE)
    assert h0.shape == (1, nb) + state_shape
    kern = functools.partial(_ssd_body, lb=lb, inner=inner)
    rows = lambda b, c: rb0 + b * nc + c
    in_specs = [
        pl.BlockSpec((lb, inner), lambda b, c: (rows(b, c), offs["z"] // inner)),
        pl.BlockSpec((lb, inner), lambda b, c: (rows(b, c), offs["x"] // inner)),
        pl.BlockSpec((lb, cdim - inner), lambda b, c: (rows(b, c), offs["bc"] // (cdim - inner))),
        pl.BlockSpec((lb, LANES), lambda b, c: (rows(b, c), offs["dt"] // LANES)),
        pl.BlockSpec((1, CONV_W - 1, cdim), lambda b, c: (b, 0, 0)),
        pl.BlockSpec((1, 1) + state_shape, lambda b, c: (0, b, 0, 0, 0)),
        pl.BlockSpec((CONV_W, cdim), lambda b, c: (0, 0)),
        pl.BlockSpec((1, cdim), lambda b, c: (0, 0)),
        pl.BlockSpec((1, LANES), lambda b, c: (0, 0)),
        pl.BlockSpec((1, LANES), lambda b, c: (0, 0)),
        pl.BlockSpec((1, inner), lambda b, c: (0, 0)),
        pl.BlockSpec((1, inner), lambda b, c: (0, 0)),
    ]
    args = [proj, proj, proj, proj, conv0, h0, conv_w, conv_b, dtb, a_neg, dskip, norm_g]
    return pl.pallas_call(
        kern,
        grid=(nb, nc),
        in_specs=in_specs,
        out_specs=[pl.BlockSpec((lb, inner), lambda b, c: (b * nc + c, 0)),
                   pl.BlockSpec((1, CONV_W - 1, cdim), lambda b, c: (b, 0, 0)),
                   pl.BlockSpec((1, 1) + state_shape, lambda b, c: (0, b, 0, 0, 0))],
        out_shape=[jax.ShapeDtypeStruct((nb * seq, inner), BF16),
                   jax.ShapeDtypeStruct((nb, CONV_W - 1, cdim), F32),
                   jax.ShapeDtypeStruct((1, nb) + state_shape, F32)],
        scratch_shapes=[pltpu.VMEM((8, inner), F32), pltpu.VMEM((8, cdim - inner), F32),
                        pltpu.VMEM((8 + q, inner), F32), pltpu.VMEM((8 + q, cdim - inner), F32),
                        pltpu.VMEM((hp_rows, SSD_STATE), F32), pltpu.VMEM((q, inner), F32)],
        compiler_params=_params("arbitrary", "arbitrary"),
    )(*args)


def _rope128(x, cos, sin):
    lane = lax.broadcasted_iota(I32, x.shape, 1)
    half = QK_ROPE // 2
    swapped = jnp.where((lane % QK_ROPE) < half, pltpu.roll(x, LANES - half, 1), pltpu.roll(x, half, 1))
    return x * cos + swapped * sin


def _mla_prep_body(cq_ref, ckv_ref, kr_ref, cos_ref, sin_ref, qg_ref, kvg_ref, wq_ref, wk_ref, wv_ref,
                   *out_refs, absorbed, scale):
    cos, sin = cos_ref[...], sin_ref[...]
    ckv_n = _rms(ckv_ref[...]) * kvg_ref[...]
    kr_r = _rope128(kr_ref[...], cos, sin)
    qn = (_rms(cq_ref[...]) * qg_ref[...]).astype(BF16)
    hw = QK_NOPE + LANES
    if absorbed:
        ckvn_ref, krr_ref, qlat_ref, qrope_ref = out_refs
    else:
        ckvn_ref, krr_ref, q_ref, k_ref, v_ref = out_refs
        ckv_b = ckv_n.astype(BF16)
        v_ref[...] = _dot(ckv_b, wv_ref[...]).astype(BF16)
        kr_b = kr_r.astype(BF16)
    ckvn_ref[...] = ckv_n
    krr_ref[...] = kr_r
    for h in range(MLA_HEADS):
        qh = _dot(qn, wq_ref[:, h * hw:(h + 1) * hw]) * scale
        q_nope = qh[:, :QK_NOPE]
        q_rope = _rope128(qh[:, QK_NOPE:], cos, sin)
        if absorbed:
            qlat_ref[h] = _dot_t(q_nope.astype(BF16), wk_ref[:, h * QK_NOPE:(h + 1) * QK_NOPE]).astype(BF16)
            qrope_ref[h] = q_rope[:, :QK_ROPE].astype(BF16)
        else:
            q_ref[:, h * hw:h * hw + QK_NOPE] = q_nope.astype(BF16)
            q_ref[:, h * hw + QK_NOPE:(h + 1) * hw] = q_rope.astype(BF16)
            k_ref[:, h * hw:h * hw + QK_NOPE] = _dot(ckv_b, wk_ref[:, h * QK_NOPE:(h + 1) * QK_NOPE]).astype(BF16)
            k_ref[:, h * hw + QK_NOPE:(h + 1) * hw] = kr_b


def _mla_prep(proj, cos_t, sin_t, q_g, kv_g, wq_r, wk, wv, *, row0, nrows, offs, absorbed):
    qlora, kvlora = q_g.shape[0], kv_g.shape[0]
    tm = _pick(nrows, (512, 256, 128))
    assert row0 % tm == 0
    rb0 = row0 // tm
    hw = QK_NOPE + LANES
    scale = 1.0 / math.sqrt(QK_NOPE + QK_ROPE)
    const = lambda i: (0, 0)
    in_specs = [pl.BlockSpec((tm, qlora), lambda i: (rb0 + i, offs["cq"] // qlora)),
                pl.BlockSpec((tm, kvlora), lambda i: (rb0 + i, offs["ckv"] // kvlora)),
                pl.BlockSpec((tm, LANES), lambda i: (rb0 + i, offs["kr"] // LANES)),
                pl.BlockSpec((tm, LANES), lambda i: (rb0 + i, 0)),
                pl.BlockSpec((tm, LANES), lambda i: (rb0 + i, 0)),
                pl.BlockSpec((1, qlora), const), pl.BlockSpec((1, kvlora), const),
                pl.BlockSpec(wq_r.shape, const), pl.BlockSpec(wk.shape, const), pl.BlockSpec(wv.shape, const)]
    out_specs = [pl.BlockSpec((tm, kvlora), lambda i: (i, 0)), pl.BlockSpec((tm, LANES), lambda i: (i, 0))]
    out_shape = [jax.ShapeDtypeStruct((nrows, kvlora), F32), jax.ShapeDtypeStruct((nrows, LANES), F32)]
    if absorbed:
        out_specs += [pl.BlockSpec((MLA_HEADS, tm, kvlora), lambda i: (0, i, 0)),
                      pl.BlockSpec((MLA_HEADS, tm, QK_ROPE), lambda i: (0, i, 0))]
        out_shape += [jax.ShapeDtypeStruct((MLA_HEADS, nrows, kvlora), BF16),
                      jax.ShapeDtypeStruct((MLA_HEADS, nrows, QK_ROPE), BF16)]
    else:
        out_specs += [pl.BlockSpec((tm, MLA_HEADS * hw), lambda i: (i, 0)),
                      pl.BlockSpec((tm, MLA_HEADS * hw), lambda i: (i, 0)),
                      pl.BlockSpec((tm, MLA_HEADS * V_HEAD), lambda i: (i, 0))]
        out_shape += [jax.ShapeDtypeStruct((nrows, MLA_HEADS * hw), BF16),
                      jax.ShapeDtypeStruct((nrows, MLA_HEADS * hw), BF16),
                      jax.ShapeDtypeStruct((nrows, MLA_HEADS * V_HEAD), BF16)]
    return pl.pallas_call(
        functools.partial(_mla_prep_body, absorbed=absorbed, scale=scale),
        grid=(nrows // tm,),
        in_specs=in_specs, out_specs=out_specs, out_shape=out_shape,
        compiler_params=_params("arbitrary"),
    )(proj, proj, proj, cos_t, sin_t, q_g.reshape(1, -1), kv_g.reshape(1, -1), wq_r, wk, wv)


ATT_SUB = 256


def _attn_tile(q_sub, k_ref, v_ref, kcols, vcols, s_scr, p_scr, nk):
    sub = ATT_SUB
    nfull = nk - sub
    shift = CHUNK.bit_length() - 1

    def fold(x, op):
        return op(x.reshape(sub // 8, 8, sub), axis=0)

    if nfull:
        s_scr[0:nfull, :] = _dot_t(k_ref[0, 0:nfull, kcols], q_sub)
    krow = lax.broadcasted_iota(I32, (sub, sub), 0) >> shift
    qcol = lax.broadcasted_iota(I32, (sub, sub), 1) >> shift
    s_scr[nfull:nk, :] = jnp.where(krow <= qcol, _dot_t(k_ref[0, nfull:nk, kcols], q_sub), -jnp.inf)

    mrun = None
    for j in range(nk // sub):
        f = fold(s_scr[j * sub:(j + 1) * sub, :], jnp.max)
        mrun = f if mrun is None else jnp.maximum(mrun, f)
    m = jnp.max(mrun, axis=0, keepdims=True)
    lrun = None
    for j in range(nk // sub):
        pt = jnp.exp(s_scr[j * sub:(j + 1) * sub, :] - m)
        f = fold(pt, jnp.sum)
        lrun = f if lrun is None else lrun + f
        p_scr[j * sub:(j + 1) * sub, :] = pt.astype(BF16)
    l = jnp.sum(lrun, axis=0, keepdims=True)
    acc = lax.dot_general(v_ref[0, 0:nk, vcols], p_scr[0:nk, :], (((0,), (0,)), ((), ())),
                          preferred_element_type=F32)
    return (acc / l).T


ATT_HEADS = 4


def _attn_body(q_ref, k_ref, v_ref, o_ref, s_scr, p_scr, *, tq, nq):
    qi = pl.program_id(2)
    sub = ATT_SUB
    hw = q_ref.shape[2] // ATT_HEADS
    for c in range(nq):
        @pl.when(qi == c)
        def _():
            for hh in range(ATT_HEADS):
                kcols = slice(hh * hw, (hh + 1) * hw)
                vcols = slice(hh * V_HEAD, (hh + 1) * V_HEAD)
                for r in range(tq // sub):
                    q_sub = q_ref[0, r * sub:(r + 1) * sub, kcols]
                    out = _attn_tile(q_sub, k_ref, v_ref, kcols, vcols, s_scr, p_scr, c * tq + (r + 1) * sub)
                    o_ref[r * sub:(r + 1) * sub, vcols] = out.astype(o_ref.dtype)


def _attn_prompt(q, k, v, *, nb, seq):
    hw = (QK_NOPE + LANES) * ATT_HEADS
    vw = V_HEAD * ATT_HEADS
    tq = min(1024, seq // 2)
    assert seq % tq == 0 and tq % ATT_SUB == 0 and ATT_SUB % CHUNK == 0 and MLA_HEADS % ATT_HEADS == 0
    nq = seq // tq
    q3 = q.reshape(nb, seq, -1)
    k3 = k.reshape(nb, seq, -1)
    v3 = v.reshape(nb, seq, -1)
    return pl.pallas_call(
        functools.partial(_attn_body, tq=tq, nq=nq),
        grid=(nb, MLA_HEADS // ATT_HEADS, nq),
        in_specs=[pl.BlockSpec((1, tq, hw), lambda b, h, i: (b, i, h)),
                  pl.BlockSpec((1, seq, hw), lambda b, h, i: (b, 0, h)),
                  pl.BlockSpec((1, seq, vw), lambda b, h, i: (b, 0, h))],
        out_specs=pl.BlockSpec((tq, vw), lambda b, h, i: (b * nq + i, h)),
        out_shape=jax.ShapeDtypeStruct((nb * seq, MLA_HEADS * V_HEAD), BF16),
        scratch_shapes=[pltpu.VMEM((seq, ATT_SUB), F32), pltpu.VMEM((seq, ATT_SUB), BF16)],
        compiler_params=_params("arbitrary", "arbitrary", "arbitrary"),
    )(q3, k3, v3)


def _attn_sample_body(ql_ref, qr_ref, pckv_ref, pkr_ref, nckv_ref, nkr_ref, o_ref, *, past, seq):
    nh = ql_ref.shape[0]
    ql = ql_ref[...].reshape(nh * seq, ql_ref.shape[2])
    qr = qr_ref[...].reshape(nh * seq, qr_ref.shape[2])
    pckv = pckv_ref[0].astype(BF16)
    nckv = nckv_ref[...].astype(BF16)
    s_p = _dot_t(ql, pckv) + _dot_t(qr, pkr_ref[0].astype(BF16))
    s_n = _dot_t(ql, nckv) + _dot_t(qr, nkr_ref[:, :QK_ROPE].astype(BF16))
    shift = CHUNK.bit_length() - 1
    q_chunk = (past + lax.broadcasted_iota(I32, (nh * seq, 1), 0) % seq) >> shift
    kp_chunk = lax.broadcasted_iota(I32, (1, past), 1) >> shift
    kn_chunk = (past + lax.broadcasted_iota(I32, (1, seq), 1)) >> shift
    s_p = jnp.where(kp_chunk <= q_chunk, s_p, -jnp.inf)
    s_n = jnp.where(kn_chunk <= q_chunk, s_n, -jnp.inf)
    m = jnp.maximum(jnp.max(s_p, axis=1, keepdims=True), jnp.max(s_n, axis=1, keepdims=True))
    p_p = jnp.exp(s_p - m)
    p_n = jnp.exp(s_n - m)
    l = jnp.sum(p_p, axis=1, keepdims=True) + jnp.sum(p_n, axis=1, keepdims=True)
    o = (_dot(p_p.astype(BF16), pckv) + _dot(p_n.astype(BF16), nckv)) / l
    o_ref[...] = o.reshape(o_ref.shape).astype(o_ref.dtype)


def _attn_sample(qlat, qrope, past_ckv, past_kr, ckv_n, kr_r, *, nb, seq):
    past = past_ckv.shape[1]
    r = past_ckv.shape[2]
    return pl.pallas_call(
        functools.partial(_attn_sample_body, past=past, seq=seq),
        grid=(nb,),
        in_specs=[pl.BlockSpec((MLA_HEADS, seq, r), lambda b: (0, b, 0)),
                  pl.BlockSpec((MLA_HEADS, seq, QK_ROPE), lambda b: (0, b, 0)),
                  pl.BlockSpec((1, past, r), lambda b: (b, 0, 0)),
                  pl.BlockSpec((1, past, QK_ROPE), lambda b: (b, 0, 0)),
                  pl.BlockSpec((seq, r), lambda b: (b, 0)),
                  pl.BlockSpec((seq, LANES), lambda b: (b, 0))],
        out_specs=pl.BlockSpec((MLA_HEADS, seq, r), lambda b: (0, b, 0)),
        out_shape=jax.ShapeDtypeStruct((MLA_HEADS, nb * seq, r), BF16),
        compiler_params=_params("arbitrary"),
    )(qlat, qrope, past_ckv, past_kr, ckv_n, kr_r)


def _uv_body(o_ref, w_ref, y_ref):
    y_ref[...] = _dot(o_ref[0], w_ref[...]).astype(y_ref.dtype)


def _uv_sample(o_lat, wv):
    nh, nrows, r = o_lat.shape
    return pl.pallas_call(
        _uv_body,
        grid=(nh,),
        in_specs=[pl.BlockSpec((1, nrows, r), lambda h: (h, 0, 0)),
                  pl.BlockSpec((r, V_HEAD), lambda h: (0, h))],
        out_specs=pl.BlockSpec((nrows, V_HEAD), lambda h: (0, h)),
        out_shape=jax.ShapeDtypeStruct((nrows, nh * V_HEAD), BF16),
        compiler_params=_params("arbitrary"),
    )(o_lat, wv)


def _merge1_body(ysp_ref, ymp_ref, yss_ref, yms_ref, ws_ref, wm_ref, ga_ref, gb_ref, o_ref, *, npb):
    def run(ys_ref, ym_ref):
        a = _dot(ys_ref[...], ws_ref[...])
        b = _dot(ym_ref[...], wm_ref[...])
        o_ref[...] = (_sigmoid(ga_ref[...]) * a + _sigmoid(gb_ref[...]) * b).astype(o_ref.dtype)

    i = pl.program_id(1)
    pl.when(i < npb)(lambda: run(ysp_ref, ymp_ref))
    pl.when(i >= npb)(lambda: run(yss_ref, yms_ref))


def _merge1(ys_p, ym_p, ys_s, ym_s, w_ssd_out, w_mla_out, proj, offs):
    (tp, inner), ts = ys_p.shape, ys_s.shape[0]
    dm = ym_p.shape[1]
    d = w_ssd_out.shape[1]
    tm = _pick(math.gcd(tp, ts), (512, 256, 128))
    tn = 512
    npb, nsb = tp // tm, ts // tm
    prow = lambda j, i: (jnp.minimum(i, npb - 1), 0)
    srow = lambda j, i: (jnp.maximum(i - npb, 0), 0)
    return pl.pallas_call(
        functools.partial(_merge1_body, npb=npb),
        grid=(d // tn, npb + nsb),
        in_specs=[pl.BlockSpec((tm, inner), prow), pl.BlockSpec((tm, dm), prow),
                  pl.BlockSpec((tm, inner), srow), pl.BlockSpec((tm, dm), srow),
                  pl.BlockSpec((inner, tn), lambda j, i: (0, j)),
                  pl.BlockSpec((dm, tn), lambda j, i: (0, j)),
                  pl.BlockSpec((tm, tn), lambda j, i: (i, offs["ga"] // tn + j)),
                  pl.BlockSpec((tm, tn), lambda j, i: (i, offs["gb"] // tn + j))],
        out_specs=pl.BlockSpec((tm, tn), lambda j, i: (i, j)),
        out_shape=jax.ShapeDtypeStruct((tp + ts, d), BF16),
        compiler_params=_params("arbitrary", "arbitrary"),
    )(ys_p, ym_p, ys_s, ym_s, w_ssd_out, w_mla_out, proj, proj)


def _merge2_body(m_ref, w_ref, xp_ref, xs_ref, g1_ref, sc_ref, sh_ref, ng_ref, wr_ref, x1_ref, h2_ref, lg_ref,
                 *, npb):
    rows, d = xp_ref.shape
    x = jnp.where(pl.program_id(0) < npb, xp_ref[...], xs_ref[...])
    upd = _dot(m_ref[...], w_ref[...]).reshape(rows // MOD_ROWS, MOD_ROWS, d) * g1_ref[...]
    x1 = x + upd.reshape(rows, d)
    x1_ref[...] = x1
    h2 = _modulate(_rms(x1) * ng_ref[...], sc_ref, sh_ref)
    _store_items(h2_ref, _pack_halves(h2))
    lg_ref[...] = _dot3(wr_ref[...], h2, dot=_dot_t)


def _merge2(merged, w_merge, xp, xs, modg, norm_g, w_router_t):
    d = xp.shape[1]
    t = xp.shape[0] + xs.shape[0]
    assert d == 2 * SUBLANES * LANES
    ne = w_router_t.shape[0]
    tm, npb, prow, srow = _two_stream_rows(xp.shape[0], xs.shape[0], (512, 256, 128))
    ng = tm // MOD_ROWS
    return pl.pallas_call(
        functools.partial(_merge2_body, npb=npb),
        grid=(t // tm,),
        in_specs=[pl.BlockSpec((tm, d), lambda i: (i, 0)),
                  pl.BlockSpec((d, d), lambda i: (0, 0)),
                  pl.BlockSpec((tm, d), prow), pl.BlockSpec((tm, d), srow),
                  pl.BlockSpec((ng, 1, d), lambda i: (i, 0, 2)),
                  pl.BlockSpec((ng, 1, d), lambda i: (i, 0, 4)),
                  pl.BlockSpec((ng, 1, d), lambda i: (i, 0, 3)),
                  pl.BlockSpec((1, d), lambda i: (0, 0)),
                  pl.BlockSpec((ne, d), lambda i: (0, 0))],
        out_specs=[pl.BlockSpec((tm, d), lambda i: (i, 0)),
                   pl.BlockSpec((tm * SUBLANES, LANES), lambda i: (i, 0)),
                   pl.BlockSpec((ne, tm), lambda i: (0, i))],
        out_shape=[jax.ShapeDtypeStruct((t, d), F32), jax.ShapeDtypeStruct((t * SUBLANES, LANES), U32),
                   jax.ShapeDtypeStruct((ne, t), F32)],
        compiler_params=_params("arbitrary"),
    )(merged, w_merge, xp, xs, modg, modg, modg, norm_g.reshape(1, d), w_router_t)


def _route_body(lg_ref, eb_ref, eidx_ref, pos_ref, w_ref, cnt_ref, carry):
    ne, tr = lg_ref.shape
    per_group = ne // N_GROUPS

    @pl.when(pl.program_id(0) == 0)
    def _():
        carry[...] = jnp.zeros_like(carry)

    scores = _sigmoid(lg_ref[...])
    choice = scores + eb_ref[...]
    sub = lax.broadcasted_iota(I32, (per_group, tr), 0)
    gscore, blocks = [], []
    for g in range(N_GROUPS):
        blk = choice[g * per_group:(g + 1) * per_group, :]
        m1 = jnp.max(blk, axis=0, keepdims=True)
        first = jnp.min(jnp.where(blk == m1, sub, per_group), axis=0, keepdims=True)
        m2 = jnp.max(jnp.where(sub == first, -jnp.inf, blk), axis=0, keepdims=True)
        gscore.append(m1 + m2)
        blocks.append(blk)
    masked = []
    for g in range(N_GROUPS):
        rank = jnp.zeros((1, tr), I32)
        for g2 in range(N_GROUPS):
            if g2 == g:
                continue
            beats = (gscore[g2] > gscore[g]) | ((gscore[g2] == gscore[g]) & (g2 < g))
            rank = rank + beats.astype(I32)
        masked.append(jnp.where(rank < TOPK_GROUPS, blocks[g], -jnp.inf))
    cm = jnp.concatenate(masked, axis=0)

    eid = lax.broadcasted_iota(I32, (ne, tr), 0)
    rank = jnp.zeros((ne, tr), I32)
    for e2 in range(ne):
        rowv = cm[e2:e2 + 1, :]
        beats = (rowv > cm) | ((rowv == cm) & (eid > e2))
        rank = rank + beats.astype(I32)
    sel = rank < TOP_K
    wsel = jnp.where(sel, scores, 0.0)
    wfull = wsel / jnp.sum(wsel, axis=0, keepdims=True) * ROUTED_SCALE

    r = lax.broadcasted_iota(I32, (tr, tr), 0)
    c = lax.broadcasted_iota(I32, (tr, tr), 1)
    before = jnp.where(r < c, 1.0, 0.0).astype(BF16)
    self = jnp.where(sel, 1.0, 0.0)
    pos = carry[:, 0:1] + _dot(self.astype(BF16), before)
    carry[...] = carry[...] + jnp.sum(self, axis=1, keepdims=True)
    cnt_ref[...] = carry[...]

    eid_f = eid.astype(F32)
    for k in range(TOP_K):
        pick = sel & (rank == k)
        eidx_ref[k:k + 1, :] = jnp.sum(jnp.where(pick, eid_f, 0.0), axis=0, keepdims=True).astype(I32)
        pos_ref[k:k + 1, :] = jnp.sum(jnp.where(pick, pos, 0.0), axis=0, keepdims=True).astype(I32)
        w_ref[k:k + 1, :] = jnp.sum(jnp.where(pick, wfull, 0.0), axis=0, keepdims=True)


def _route(logits_t, e_bias):
    ne, t = logits_t.shape
    tr = _pick(t, (512, 256, 128))
    return pl.pallas_call(
        _route_body,
        grid=(t // tr,),
        in_specs=[pl.BlockSpec((ne, tr), lambda i: (0, i)), pl.BlockSpec((ne, 1), lambda i: (0, 0))],
        out_specs=[pl.BlockSpec((TOP_K, tr), lambda i: (0, i)), pl.BlockSpec((TOP_K, tr), lambda i: (0, i)),
                   pl.BlockSpec((TOP_K, tr), lambda i: (0, i)), pl.BlockSpec((ne, LANES), lambda i: (0, 0))],
        out_shape=[jax.ShapeDtypeStruct((TOP_K, t), I32), jax.ShapeDtypeStruct((TOP_K, t), I32),
                   jax.ShapeDtypeStruct((TOP_K, t), F32), jax.ShapeDtypeStruct((ne, LANES), F32)],
        scratch_shapes=[pltpu.VMEM((ne, LANES), F32)],
        compiler_params=_params("arbitrary"),
    )(logits_t, e_bias.reshape(ne, 1))


def _row_copy(src, s, dst, d, sem):
    rows = lambda i: pl.ds(pl.multiple_of(i * SUBLANES, SUBLANES), SUBLANES)
    return pltpu.make_async_copy(src.at[rows(s), :], dst.at[rows(d), :], sem)


def _load_items(ref, start, n):
    return jnp.concatenate([ref[pl.ds(start * SUBLANES + c, n, stride=SUBLANES), :] for c in range(SUBLANES)],
                           axis=1)


def _store_items(ref, val):
    n = val.shape[0]
    for c in range(SUBLANES):
        ref[pl.ds(c, n, stride=SUBLANES), :] = val[:, c * LANES:(c + 1) * LANES]


def _drain(wait_one, n, group=64):
    assert n % group == 0

    def body(j, c):
        for _ in range(group):
            wait_one()
        return c

    lax.fori_loop(0, n // group, body, 0)


def _dispatch_body(fill_lo_ref, fill_hi_ref, h_ref, dest_ref, hs_ref, dest_s, zrow, sem, dsem, *, tb):
    i = pl.program_id(0)
    cp = pltpu.make_async_copy(dest_ref.at[i], dest_s, dsem)
    cp.start()

    @pl.when(i == 0)
    def _():
        zrow[...] = jnp.zeros_like(zrow)

        def per_expert(fn):
            def body(e, _):
                lax.fori_loop(fill_lo_ref[e], fill_hi_ref[e], lambda s, c: (fn(s), c)[1], 0)
                return 0
            lax.fori_loop(0, fill_lo_ref.shape[0], body, 0)

        per_expert(lambda s: _row_copy(zrow, 0, hs_ref, s, sem).start())
        per_expert(lambda s: _row_copy(zrow, 0, hs_ref, s, sem).wait())

    cp.wait()

    def issue(t, c):
        for k in range(TOP_K):
            _row_copy(h_ref, t, hs_ref, dest_s[t * TOP_K + k], sem).start(priority=k % 2)
        return c

    lax.fori_loop(0, tb, issue, 0, unroll=4)
    _drain(lambda: _row_copy(h_ref, 0, hs_ref, 0, sem).wait(), TOP_K * tb)


def _dispatch(h2, dest_blk, fill_lo, fill_hi, n_slots):
    nblk, n = dest_blk.shape
    tb = n // TOP_K
    return pl.pallas_call(
        functools.partial(_dispatch_body, tb=tb),
        grid_spec=pltpu.PrefetchScalarGridSpec(
            num_scalar_prefetch=2,
            grid=(nblk,),
            in_specs=[pl.BlockSpec((tb * SUBLANES, LANES), lambda i, lo, hi: (i, 0)),
                      pl.BlockSpec(memory_space=pl.ANY)],
            out_specs=pl.BlockSpec(memory_space=pl.ANY),
            scratch_shapes=[pltpu.SMEM((n,), I32), pltpu.VMEM((SUBLANES, LANES), h2.dtype),
                            pltpu.SemaphoreType.DMA, pltpu.SemaphoreType.DMA]),
        out_shape=jax.ShapeDtypeStruct((n_slots * SUBLANES, LANES), h2.dtype),
        compiler_params=_params("arbitrary"),
    )(fill_lo, fill_hi, h2, dest_blk)


def _experts_body(be_ref, nu_ref, x_ref, wg_ref, wu_ref, wd_ref, o_ref, wgu_s, wd_s):
    i = pl.program_id(0)
    hid = wg_ref.shape[2]

    @pl.when(i < nu_ref[0])
    def _():
        @pl.when((i == 0) | (be_ref[i] != be_ref[jnp.maximum(i - 1, 0)]))
        def _():
            wgu_s[:, :hid] = wg_ref[0].astype(BF16)
            wgu_s[:, hid:] = wu_ref[0].astype(BF16)
            wd_s[...] = wd_ref[0].astype(BF16)

        lo, hi = _unpack_halves(_load_items(x_ref, 0, x_ref.shape[0] // SUBLANES))
        x = jnp.concatenate([lo.astype(BF16), hi.astype(BF16)], axis=1)
        gu = _dot(x, wgu_s[...])
        act = (_silu(gu[:, :hid]) * gu[:, hid:]).astype(BF16)
        _store_items(o_ref, _pack_halves(_dot(act, wd_s[...])))


def _experts(hs, block_e, n_used, wg, wu, wd):
    n_slots = hs.shape[0] // SUBLANES
    ne, d, hid = wg.shape
    assert d == 2 * SUBLANES * LANES
    bm = MOE_ROWS
    nblocks = n_slots // bm
    blk = lambda i, be, nu: (jnp.minimum(i, nu[0] - 1), 0)
    return pl.pallas_call(
        _experts_body,
        grid_spec=pltpu.PrefetchScalarGridSpec(
            num_scalar_prefetch=2,
            grid=(nblocks,),
            in_specs=[pl.BlockSpec((bm * SUBLANES, LANES), blk),
                      pl.BlockSpec((1, d, hid), lambda i, be, nu: (be[i], 0, 0)),
                      pl.BlockSpec((1, d, hid), lambda i, be, nu: (be[i], 0, 0)),
                      pl.BlockSpec((1, hid, d), lambda i, be, nu: (be[i], 0, 0))],
            out_specs=pl.BlockSpec((bm * SUBLANES, LANES), blk),
            scratch_shapes=[pltpu.VMEM((d, 2 * hid), BF16), pltpu.VMEM((hid, d), BF16)]),
        out_shape=jax.ShapeDtypeStruct(hs.shape, U32),
        compiler_params=_params("arbitrary"),
    )(block_e, n_used, hs, wg, wu, wd)


def _combine_body(ys_ref, dest_ref, w_ref, h_ref, x1_ref, g2_ref, wgu_ref, wd_ref, fg_ref, op_ref, os_ref,
                  dest_s, gbuf, sem, dsem, *, tb, npb):
    i = pl.program_id(0)
    cp = pltpu.make_async_copy(dest_ref.at[i], dest_s, dsem)
    cp.start()
    cp.wait()

    def issue(t, c):
        for k in range(TOP_K):
            _row_copy(ys_ref, dest_s[t * TOP_K + k], gbuf, k * tb + t, sem).start(priority=k % 2)
        return c

    lax.fori_loop(0, tb, issue, 0, unroll=4)

    hid = wd_ref.shape[0]
    h_lo, h_hi = _unpack_halves(_load_items(h_ref, 0, tb))
    gu = _dot(jnp.concatenate([h_lo.astype(BF16), h_hi.astype(BF16)], axis=1), wgu_ref[...])
    moe = _dot((_silu(gu[:, :hid]) * gu[:, hid:]).astype(BF16), wd_ref[...])

    _drain(lambda: _row_copy(ys_ref, 0, gbuf, 0, sem).wait(), TOP_K * tb)
    w = w_ref[...]
    dh = SUBLANES * LANES
    m_lo, m_hi = moe[:, :dh], moe[:, dh:]
    for k in range(TOP_K):
        lo, hi = _unpack_halves(_load_items(gbuf, k * tb, tb))
        m_lo = m_lo + lo * w[:, k:k + 1]
        m_hi = m_hi + hi * w[:, k:k + 1]
    moe = jnp.concatenate([m_lo, m_hi], axis=1)
    rows, d = moe.shape
    upd = moe.reshape(rows // MOD_ROWS, MOD_ROWS, d) * g2_ref[...]
    x2 = x1_ref[...] + upd.reshape(rows, d)
    y = _rms(x2) * fg_ref[...]

    @pl.when(i < npb)
    def _():
        op_ref[...] = y

    @pl.when(i >= npb)
    def _():
        os_ref[...] = y


def _combine(ys, dest_blk, w_tok, h2, x1, modg, wsh_gu, wsh_d, final_g, *, tp):
    t, d = x1.shape
    nblk, n = dest_blk.shape
    tb = n // TOP_K
    assert tp % tb == 0 and (t - tp) % tb == 0
    npb = tp // tb
    ng = tb // MOD_ROWS
    const = lambda i: (0, 0)
    return pl.pallas_call(
        functools.partial(_combine_body, tb=tb, npb=npb),
        grid=(nblk,),
        in_specs=[pl.BlockSpec(memory_space=pl.ANY),
                  pl.BlockSpec(memory_space=pl.ANY),
                  pl.BlockSpec((tb, TOP_K), lambda i: (i, 0)),
                  pl.BlockSpec((tb * SUBLANES, LANES), lambda i: (i, 0)),
                  pl.BlockSpec((tb, d), lambda i: (i, 0)),
                  pl.BlockSpec((ng, 1, d), lambda i: (i, 0, 5)),
                  pl.BlockSpec(wsh_gu.shape, const), pl.BlockSpec(wsh_d.shape, const),
                  pl.BlockSpec((1, d), const)],
        out_specs=[pl.BlockSpec((tb, d), lambda i: (jnp.minimum(i, npb - 1), 0)),
                   pl.BlockSpec((tb, d), lambda i: (jnp.maximum(i - npb, 0), 0))],
        out_shape=[jax.ShapeDtypeStruct((tp, d), F32), jax.ShapeDtypeStruct((t - tp, d), F32)],
        scratch_shapes=[pltpu.SMEM((n,), I32), pltpu.VMEM((n * SUBLANES, LANES), U32),
                        pltpu.SemaphoreType.DMA, pltpu.SemaphoreType.DMA],
        compiler_params=_params("arbitrary"),
    )(ys, dest_blk, w_tok, h2, x1, modg, wsh_gu, wsh_d, final_g.reshape(1, d))


def _rope_tables(pos):
    half = QK_ROPE // 2
    freqs = ROPE_THETA ** (-jnp.arange(half, dtype=F32) / half)
    ang = pos.astype(F32)[:, None] * freqs[None, :]
    cos, sin = jnp.cos(ang), jnp.sin(ang)
    return (jnp.concatenate([cos, cos, cos, cos], axis=1),
            jnp.concatenate([-sin, sin, -sin, sin], axis=1))


def _blocked(a, tb):
    k, t = a.shape
    return a.T.reshape(t // tb, tb * k)


def kernel(x_prompt, x_sample, c_prompt, c_sample, cache_conv, state_ssm, cache_ckv, cache_kr, w_ada, b_ada,
           norm1_g, norm2_g, w_in, conv_w, conv_b, dt_bias, a_log, d_skip, ssd_norm_g, w_ssd_out, q_norm_g,
           w_uq, kv_norm_g, w_uk, w_uv, w_mla_out, w_merge_out, w_router, e_bias, w_exp_gate, w_exp_up,
           w_exp_down, w_sh_gate, w_sh_up, w_sh_down, final_norm_g):
    depth = w_in.shape[0]
    assert depth == 1
    bp, seq, d = x_prompt.shape
    bs, lseq, _ = x_sample.shape
    assert lseq == MOD_ROWS and seq % MOD_ROWS == 0
    tp, ts = bp * seq, bs * lseq
    t_all = tp + ts
    nheads = dt_bias.shape[1]
    inner = nheads * SSD_HEAD_DIM
    cdim = conv_w.shape[2]
    gn = SSD_GROUPS * SSD_STATE
    qlora, kvlora = q_norm_g.shape[1], kv_norm_g.shape[1]
    ne = w_router.shape[2]
    assert nheads == 64 and 2 * nheads == LANES

    w = w_in[0]
    o_xbc, o_dt = inner, inner + cdim
    o_cq = o_dt + nheads
    o_ckv = o_cq + qlora
    o_kr = o_ckv + kvlora
    o_gate = o_kr + QK_ROPE
    perm = np.concatenate([np.arange(0, nheads, 2), np.arange(1, nheads, 2)])
    zc = lambda n: jnp.zeros((d, n), w.dtype)
    cols = [w[:, :inner], w[:, o_gate:], w[:, o_xbc:o_dt], w[:, o_cq:o_ckv], w[:, o_ckv:o_kr],
            w[:, o_dt:o_cq][:, perm], zc(LANES - nheads), w[:, o_kr:o_gate], zc(LANES - QK_ROPE)]
    used = inner + 2 * d + cdim + qlora + kvlora + 2 * LANES
    total = -(-used // 512) * 512
    cols.append(zc(total - used))
    w_r = jnp.concatenate(cols, axis=1).astype(BF16)
    offs = {"z": 0, "ga": inner, "gb": inner + d, "x": inner + 2 * d, "bc": 2 * inner + 2 * d}
    offs["cq"] = offs["bc"] + 2 * gn
    offs["ckv"] = offs["cq"] + qlora
    offs["dt"] = offs["ckv"] + kvlora
    offs["kr"] = offs["dt"] + LANES

    hw = QK_NOPE + LANES
    wq_r = jnp.pad(w_uq[0], ((0, 0), (0, 0), (0, hw - QK_NOPE - QK_ROPE))).reshape(qlora, MLA_HEADS * hw).astype(BF16)
    wk = w_uk[0].reshape(kvlora, MLA_HEADS * QK_NOPE).astype(BF16)
    wv = w_uv[0].reshape(kvlora, MLA_HEADS * V_HEAD).astype(BF16)
    pad_l = lambda v: jnp.pad(v[perm], (0, LANES - nheads)).reshape(1, LANES)
    dtb = pad_l(dt_bias[0])
    a_neg = pad_l(-jnp.exp(a_log[0]))
    dskip = jnp.repeat(d_skip[0], SSD_HEAD_DIM).reshape(1, inner)

    c_all = jnp.concatenate([c_prompt, c_sample], axis=0)
    mod = _ada(c_all, w_ada[0], b_ada[0])
    grp = np.concatenate([np.repeat(np.arange(bp), seq // MOD_ROWS), bp + np.arange(bs)])
    modg = mod[grp].reshape(t_all // MOD_ROWS, 1, 6 * d)

    xp, xs = x_prompt.reshape(tp, d), x_sample.reshape(ts, d)
    proj = _inproj(xp, xs, modg, norm1_g[0], w_r, d)

    ssd_args = (conv_w[0], conv_b[0].reshape(1, cdim), dtb, a_neg, dskip, ssd_norm_g[0].reshape(1, inner))
    ys_p, conv_p, ssm_p = _ssd(proj, jnp.zeros((bp, CONV_W - 1, cdim), F32),
                               jnp.zeros((1, bp) + state_ssm.shape[2:], F32), *ssd_args,
                               nb=bp, seq=seq, row0=0, offs=offs, inner=inner)
    ys_s, conv_s, ssm_s = _ssd(proj, cache_conv[0], state_ssm,
                               *ssd_args, nb=bs, seq=lseq, row0=tp, offs=offs, inner=inner)

    past = cache_ckv.shape[2]
    cos_p, sin_p = _rope_tables(jnp.arange(seq))
    cos_s, sin_s = _rope_tables(past + jnp.arange(lseq))
    cos_t = jnp.concatenate([jnp.tile(cos_p, (bp, 1)), jnp.tile(cos_s, (bs, 1))], axis=0)
    sin_t = jnp.concatenate([jnp.tile(sin_p, (bp, 1)), jnp.tile(sin_s, (bs, 1))], axis=0)
    prep = functools.partial(_mla_prep, proj, cos_t, sin_t, q_norm_g[0], kv_norm_g[0], wq_r, wk, wv, offs=offs)
    ckv_p, kr_p, q_p, k_p, v_p = prep(row0=0, nrows=tp, absorbed=False)
    ckv_s, kr_s, qlat, qrope = prep(row0=tp, nrows=ts, absorbed=True)
    ym_p = _attn_prompt(q_p, k_p, v_p, nb=bp, seq=seq)
    o_lat = _attn_sample(qlat, qrope, cache_ckv[0], cache_kr[0], ckv_s, kr_s, nb=bs, seq=lseq)
    ym_s = _uv_sample(o_lat, wv)

    merged = _merge1(ys_p, ym_p, ys_s, ym_s, w_ssd_out[0].astype(BF16), w_mla_out[0].astype(BF16), proj, offs)
    x1, h2, logits_t = _merge2(merged, w_merge_out[0].astype(BF16), xp, xs, modg, norm2_g[0], w_router[0].T)

    eidx, pos, w_sel, cnt = _route(logits_t, e_bias[0])
    bm = MOE_ROWS
    counts = cnt[:, 0].astype(I32)
    padded = (counts + bm - 1) // bm * bm
    pad_end = jnp.cumsum(padded)
    pad_start = pad_end - padded
    nblocks = -(-(t_all * TOP_K) // bm) + ne
    onehot = eidx[:, :, None] == jnp.arange(ne, dtype=I32)
    dest = jnp.sum(jnp.where(onehot, pad_start, 0), axis=-1) + pos
    blk_start = jnp.arange(nblocks, dtype=I32) * bm
    block_e = jnp.minimum(jnp.sum(pad_end[None, :] <= blk_start[:, None], axis=1), ne - 1).astype(I32)
    n_used = (pad_end[-1:] // bm).astype(I32)

    tb_d = _pick(t_all, (512, 256, 128))
    hs = _dispatch(h2, _blocked(dest, tb_d), (pad_start + counts).astype(I32), pad_end.astype(I32), nblocks * bm)
    ys = _experts(hs, block_e, n_used, w_exp_gate[0], w_exp_up[0], w_exp_down[0])
    wsh_gu = jnp.concatenate([w_sh_gate[0], w_sh_up[0]], axis=1).astype(BF16)
    tb_c = _pick(math.gcd(tp, ts), (256, 128))
    y_p, y_s = _combine(ys, _blocked(dest, tb_c), w_sel.T, h2, x1, modg, wsh_gu, w_sh_down[0].astype(BF16),
                        final_norm_g, tp=tp)

    r5 = lambda a, n, l: a.reshape(1, n, l, a.shape[-1])
    return (y_p.reshape(bp, seq, d), y_s.reshape(bs, lseq, d),
            conv_p[None], ssm_p,
            r5(ckv_p, bp, seq), r5(kr_p[:, :QK_ROPE], bp, seq),
            conv_s[None], ssm_s,
            r5(ckv_s, bs, lseq), r5(kr_s[:, :QK_ROPE], bs, lseq))
```

```python
import functools
import math

import numpy as np
import jax
import jax.numpy as jnp
from jax import lax
from jax.experimental import pallas as pl
from jax.experimental.pallas import tpu as pltpu

F32 = jnp.float32
BF16 = jnp.bfloat16
I32 = jnp.int32
U32 = jnp.uint32

EPS = 1e-6
CHUNK = 64
SSD_HEAD_DIM = 64
SSD_GROUPS = 8
SSD_STATE = 128
CONV_W = 4
MLA_HEADS = 16
QK_NOPE = 128
QK_ROPE = 64
V_HEAD = 128
ROPE_THETA = 10000.0
TOP_K = 8
N_GROUPS = 8
TOPK_GROUPS = 4
ROUTED_SCALE = 2.5

LANES = 128
SUBLANES = 8
MOD_ROWS = 32
MOE_ROWS = 512
VMEM_LIMIT = 56 * 1024 * 1024


def _params(*sem):
    return pltpu.CompilerParams(dimension_semantics=sem, vmem_limit_bytes=VMEM_LIMIT)


def _pick(n, cands):
    for c in cands:
        if n % c == 0:
            return c
    raise ValueError(f"no tile in {cands} divides {n}")


def _sigmoid(x):
    return 1.0 / (1.0 + jnp.exp(-x))


def _silu(x):
    return x * _sigmoid(x)


def _dot(a, b):
    return jnp.dot(a, b, preferred_element_type=F32)


def _dot_t(a, b):
    return lax.dot_general(a, b, (((1,), (1,)), ((), ())), preferred_element_type=F32)


def _split(x, n):
    parts = []
    for _ in range(n - 1):
        p = x.astype(BF16)
        parts.append(p)
        x = x - p.astype(F32)
    parts.append(x.astype(BF16))
    return parts


def _dot_exact_rhs(a, b_bf16, n=3):
    acc = None
    for p in _split(a, n):
        t = _dot(p, b_bf16)
        acc = t if acc is None else acc + t
    return acc


def _dot3(a, b, dot=_dot):
    ah, al = _split(a, 2)
    bh, bl = _split(b, 2)
    return dot(ah, bh) + (dot(ah, bl) + dot(al, bh))


def _pack_halves(x):
    h = x.shape[1] // 2
    bits = lambda v: lax.bitcast_convert_type(v.astype(BF16).astype(F32), U32)
    return (bits(x[:, :h]) >> 16) | bits(x[:, h:])


def _unpack_halves(p):
    lo = lax.bitcast_convert_type(p << 16, F32)
    hi = lax.bitcast_convert_type(p & jnp.uint32(0xFFFF0000), F32)
    return lo, hi


def _rms(x):
    return x * lax.rsqrt(jnp.mean(x * x, axis=-1, keepdims=True) + EPS)


def _modulate(y, sc_ref, sh_ref):
    rows, d = y.shape
    y3 = y.reshape(rows // MOD_ROWS, MOD_ROWS, d)
    return (y3 * (1.0 + sc_ref[...]) + sh_ref[...]).reshape(rows, d)


def _ada_body(c_ref, w_ref, b_ref, o_ref):
    o_ref[...] = _dot3(_silu(c_ref[...]), w_ref[...]) + b_ref[...]


def _ada(c_all, w_ada, b_ada):
    nb, d = c_all.shape
    n = w_ada.shape[1]
    tn = _pick(n, (1024, 512, 256, 128))
    return pl.pallas_call(
        _ada_body,
        grid=(n // tn,),
        in_specs=[pl.BlockSpec((nb, d), lambda j: (0, 0)),
                  pl.BlockSpec((d, tn), lambda j: (0, j)),
                  pl.BlockSpec((1, tn), lambda j: (0, j))],
        out_specs=pl.BlockSpec((nb, tn), lambda j: (0, j)),
        out_shape=jax.ShapeDtypeStruct((nb, n), F32),
        compiler_params=_params("arbitrary"),
    )(c_all, w_ada, b_ada.reshape(1, n))


def _two_stream_rows(tp, ts, cands):
    tm = _pick(math.gcd(tp, ts), cands)
    npb = tp // tm
    return tm, npb, (lambda i, *_: (jnp.minimum(i, npb - 1), 0)), (lambda i, *_: (jnp.maximum(i - npb, 0), 0))


def _inproj_body(xp_ref, xs_ref, sc_ref, sh_ref, g_ref, w_ref, o_ref, h_scr, *, npb):
    @pl.when(pl.program_id(1) == 0)
    def _():
        x = jnp.where(pl.program_id(0) < npb, xp_ref[...], xs_ref[...])
        h = _modulate(_rms(x) * g_ref[...], sc_ref, sh_ref)
        h_scr[...] = h.astype(BF16)

    o_ref[...] = _dot(h_scr[...], w_ref[...])


def _inproj(xp, xs, modg, norm_g, w_r, d):
    t = xp.shape[0] + xs.shape[0]
    n = w_r.shape[1]
    tm, npb, prow, srow = _two_stream_rows(xp.shape[0], xs.shape[0], (1024, 512, 256, 128))
    tn = 512
    ng = tm // MOD_ROWS
    return pl.pallas_call(
        functools.partial(_inproj_body, npb=npb),
        grid=(t // tm, n // tn),
        in_specs=[pl.BlockSpec((tm, d), prow), pl.BlockSpec((tm, d), srow),
                  pl.BlockSpec((ng, 1, d), lambda i, j: (i, 0, 1)),
                  pl.BlockSpec((ng, 1, d), lambda i, j: (i, 0, 0)),
                  pl.BlockSpec((1, d), lambda i, j: (0, 0)),
                  pl.BlockSpec((d, tn), lambda i, j: (0, j))],
        out_specs=pl.BlockSpec((tm, tn), lambda i, j: (i, j)),
        out_shape=jax.ShapeDtypeStruct((t, n), F32),
        scratch_shapes=[pltpu.VMEM((tm, d), BF16)],
        compiler_params=_params("arbitrary", "arbitrary"),
    )(xp, xs, modg, modg, norm_g.reshape(1, d), w_r)


def _ssd_body(z_ref, x_ref, bc_ref, dt_ref, conv0_ref, h0_ref, cw_ref, cb_ref, dtb_ref, a_ref,
              dskip_ref, ng_ref, y_ref, nconv_ref, nssm_ref,
              carry_x, carry_bc, xp_x, xp_bc, state, y_scr, *, lb, inner):
    q = CHUNK
    c = pl.program_id(1)
    nc = pl.num_programs(1)
    gn = SSD_GROUPS * SSD_STATE

    @pl.when(c == 0)
    def _():
        carry_x[...] = jnp.zeros_like(carry_x)
        carry_bc[...] = jnp.zeros_like(carry_bc)
        carry_x[8 - (CONV_W - 1):8, :] = conv0_ref[0, :, :inner]
        carry_bc[8 - (CONV_W - 1):8, :] = conv0_ref[0, :, inner:]
        state[...] = h0_ref[0, 0].reshape(state.shape)

    def pad_rows(v):
        if lb == q:
            return v
        return jnp.concatenate([v, jnp.zeros((q - lb, v.shape[1]), v.dtype)], axis=0)

    def conv(raw, carry, xp, w_lo, w_hi):
        xp[0:8, :] = carry[...]
        xp[8:8 + q, :] = raw
        acc = cb_ref[:, w_lo:w_hi]
        for j in range(CONV_W):
            acc = acc + xp[8 - j:8 - j + q, :] * cw_ref[CONV_W - 1 - j:CONV_W - j, w_lo:w_hi]
        return _silu(acc)

    x_raw = pad_rows(x_ref[...])
    bc_raw = pad_rows(bc_ref[...])
    xc = conv(x_raw, carry_x, xp_x, 0, inner)
    bcc = conv(bc_raw, carry_bc, xp_bc, inner, inner + 2 * gn)
    if lb == q:
        carry_x[...] = x_raw[q - 8:q, :]
        carry_bc[...] = bc_raw[q - 8:q, :]

    @pl.when(c == nc - 1)
    def _():
        nconv_ref[0, :, :inner] = x_raw[lb - (CONV_W - 1):lb, :]
        nconv_ref[0, :, inner:] = bc_raw[lb - (CONV_W - 1):lb, :]

    dtr = dt_ref[...] + dtb_ref[...]
    dtv = jnp.maximum(dtr, 0.0) + jnp.log1p(jnp.exp(-jnp.abs(dtr)))
    dtv = pad_rows(dtv)
    la = dtv * a_ref[...]

    row = lax.broadcasted_iota(I32, (q, q), 0)
    col = lax.broadcasted_iota(I32, (q, q), 1)
    tri = jnp.where(row >= col, 1.0, 0.0).astype(BF16)
    parts = _split(la, 3)
    a_cum = _dot(tri, parts[0]) + _dot(tri, parts[1]) + _dot(tri, parts[2])
    a_last = a_cum[q - 1:q, :]

    r2 = lax.broadcasted_iota(I32, (q, 2 * q), 0)
    c2 = lax.broadcasted_iota(I32, (q, 2 * q), 1)
    u_e = jnp.where((c2 < q) & (r2 <= c2), 1.0, 0.0).astype(BF16)
    u_o = jnp.where((c2 >= q) & (r2 <= c2 - q), 1.0, 0.0).astype(BF16)
    i_e = jnp.where(r2 == c2, 1.0, 0.0).astype(BF16)
    i_o = jnp.where(r2 == c2 - q, 1.0, 0.0).astype(BF16)
    la_t = la.T
    dt_t = dtv.T
    hp = la_t.shape[0] // 4
    acol = _dot_exact_rhs(la_t[0:hp], u_e) + _dot_exact_rhs(la_t[hp:2 * hp], u_o)
    dtrow = _dot_exact_rhs(dt_t[0:hp], i_e) + _dot_exact_rhs(dt_t[hp:2 * hp], i_o)
    w_t = (dtv * jnp.exp(a_last - a_cum)).T
    dec_b = jnp.broadcast_to(jnp.exp(jnp.sum(la_t, axis=1, keepdims=True)), (la_t.shape[0], SSD_STATE))

    lane = lax.broadcasted_iota(I32, (q, 2 * q), 1)
    causal2 = lax.broadcasted_iota(I32, (q, 2 * q), 0) >= jnp.where(lane < q, lane, lane - q)
    first_half = lane < q
    rr = lax.broadcasted_iota(I32, (2 * SSD_HEAD_DIM, 2 * SSD_HEAD_DIM), 0)
    cc = lax.broadcasted_iota(I32, (2 * SSD_HEAD_DIM, 2 * SSD_HEAD_DIM), 1)
    bd_mask = (rr < SSD_HEAD_DIM) == (cc < SSD_HEAD_DIM)
    top_rows_q = lax.broadcasted_iota(I32, (2 * SSD_HEAD_DIM, q), 0) < SSD_HEAD_DIM
    top_rows_n = lax.broadcasted_iota(I32, (2 * SSD_HEAD_DIM, SSD_STATE), 0) < SSD_HEAD_DIM

    pairs_per_group = (inner // SSD_HEAD_DIM) // SSD_GROUPS // 2
    for g in range(SSD_GROUPS):
        b_g = bcc[:, g * SSD_STATE:(g + 1) * SSD_STATE].astype(BF16)
        c_g = bcc[:, gn + g * SSD_STATE:gn + (g + 1) * SSD_STATE].astype(BF16)
        cb2 = _dot_t(c_g, jnp.concatenate([b_g, b_g], axis=0))
        for kk in range(pairs_per_group):
            k = g * pairs_per_group + kk
            lo, hi = k * 2 * SSD_HEAD_DIM, (k + 1) * 2 * SSD_HEAD_DIM
            xp = xc[:, lo:hi]
            arow = jnp.where(first_half, a_cum[:, k:k + 1], a_cum[:, hp + k:hp + k + 1])
            seg = jnp.where(causal2, arow - acol[k:k + 1, :], -jnp.inf)
            m = (jnp.exp(seg) * cb2 * dtrow[k:k + 1, :]).astype(BF16)
            xbd = jnp.where(bd_mask, jnp.concatenate([xp, xp], axis=0), 0.0).astype(BF16)
            y_diag = _dot(m, xbd)
            s_k = state[lo:hi, :]
            y_off = _dot_t(c_g, s_k.astype(BF16)) * jnp.exp(arow)
            y_scr[:, lo:hi] = y_diag + y_off + xp * dskip_ref[:, lo:hi]
            w2 = jnp.where(top_rows_q, w_t[k:k + 1, :], w_t[hp + k:hp + k + 1, :])
            contrib = _dot((xp.T * w2).astype(BF16), b_g)
            d_k = jnp.where(top_rows_n, dec_b[k:k + 1, :], dec_b[hp + k:hp + k + 1, :])
            state[lo:hi, :] = s_k * d_k + contrib

    @pl.when(c == nc - 1)
    def _():
        nssm_ref[0, 0] = state[...].reshape(nssm_ref.shape[2:])

    y = y_scr[0:lb, :] * _silu(z_ref[...])
    y_ref[...] = (_rms(y) * ng_ref[...]).astype(y_ref.dtype)


def _ssd(proj, conv0, h0, conv_w, conv_b, dtb, a_neg, dskip, norm_g, *, nb, seq, row0, offs, inner):
    q = CHUNK
    lb = min(seq, q)
    assert seq % lb == 0 and row0 % lb == 0 and lb % 16 == 0
    nc = seq // lb
    rb0 = row0 // lb
    cdim = conv_w.shape[1]
    hp_rows = inner
    state_shape = (inner // SSD_HEAD_DIM, SSD_HEAD_DIM, SSD_STATE)
    assert h0.shape == (1, nb) + state_shape
    kern = functools.partial(_ssd_body, lb=lb, inner=inner)
    rows = lambda b, c: rb0 + b * nc + c
    in_specs = [
        pl.BlockSpec((lb, inner), lambda b, c: (rows(b, c), offs["z"] // inner)),
        pl.BlockSpec((lb, inner), lambda b, c: (rows(b, c), offs["x"] // inner)),
        pl.BlockSpec((lb, cdim - inner), lambda b, c: (rows(b, c), offs["bc"] // (cdim - inner))),
        pl.BlockSpec((lb, LANES), lambda b, c: (rows(b, c), offs["dt"] // LANES)),
        pl.BlockSpec((1, CONV_W - 1, cdim), lambda b, c: (b, 0, 0)),
        pl.BlockSpec((1, 1) + state_shape, lambda b, c: (0, b, 0, 0, 0)),
        pl.BlockSpec((CONV_W, cdim), lambda b, c: (0, 0)),
        pl.BlockSpec((1, cdim), lambda b, c: (0, 0)),
        pl.BlockSpec((1, LANES), lambda b, c: (0, 0)),
        pl.BlockSpec((1, LANES), lambda b, c: (0, 0)),
        pl.BlockSpec((1, inner), lambda b, c: (0, 0)),
        pl.BlockSpec((1, inner), lambda b, c: (0, 0)),
    ]
    args = [proj, proj, proj, proj, conv0, h0, conv_w, conv_b, dtb, a_neg, dskip, norm_g]
    return pl.pallas_call(
        kern,
        grid=(nb, nc),
        in_specs=in_specs,
        out_specs=[pl.BlockSpec((lb, inner), lambda b, c: (b * nc + c, 0)),
                   pl.BlockSpec((1, CONV_W - 1, cdim), lambda b, c: (b, 0, 0)),
                   pl.BlockSpec((1, 1) + state_shape, lambda b, c: (0, b, 0, 0, 0))],
        out_shape=[jax.ShapeDtypeStruct((nb * seq, inner), BF16),
                   jax.ShapeDtypeStruct((nb, CONV_W - 1, cdim), F32),
                   jax.ShapeDtypeStruct((1, nb) + state_shape, F32)],
        scratch_shapes=[pltpu.VMEM((8, inner), F32), pltpu.VMEM((8, cdim - inner), F32),
                        pltpu.VMEM((8 + q, inner), F32), pltpu.VMEM((8 + q, cdim - inner), F32),
                        pltpu.VMEM((hp_rows, SSD_STATE), F32), pltpu.VMEM((q, inner), F32)],
        compiler_params=_params("arbitrary", "arbitrary"),
    )(*args)


def _rope128(x, cos, sin):
    lane = lax.broadcasted_iota(I32, x.shape, 1)
    half = QK_ROPE // 2
    swapped = jnp.where((lane % QK_ROPE) < half, pltpu.roll(x, LANES - half, 1), pltpu.roll(x, half, 1))
    return x * cos + swapped * sin


def _mla_prep_body(cq_ref, ckv_ref, kr_ref, cos_ref, sin_ref, qg_ref, kvg_ref, wq_ref, wk_ref, wv_ref,
                   *out_refs, absorbed, scale):
    cos, sin = cos_ref[...], sin_ref[...]
    ckv_n = _rms(ckv_ref[...]) * kvg_ref[...]
    kr_r = _rope128(kr_ref[...], cos, sin)
    qn = (_rms(cq_ref[...]) * qg_ref[...]).astype(BF16)
    hw = QK_NOPE + LANES
    if absorbed:
        ckvn_ref, krr_ref, qlat_ref, qrope_ref = out_refs
    else:
        ckvn_ref, krr_ref, q_ref, k_ref, v_ref = out_refs
        ckv_b = ckv_n.astype(BF16)
        v_ref[...] = _dot(ckv_b, wv_ref[...]).astype(BF16)
        kr_b = kr_r.astype(BF16)
    ckvn_ref[...] = ckv_n
    krr_ref[...] = kr_r
    for h in range(MLA_HEADS):
        qh = _dot(qn, wq_ref[:, h * hw:(h + 1) * hw]) * scale
        q_nope = qh[:, :QK_NOPE]
        q_rope = _rope128(qh[:, QK_NOPE:], cos, sin)
        if absorbed:
            qlat_ref[h] = _dot_t(q_nope.astype(BF16), wk_ref[:, h * QK_NOPE:(h + 1) * QK_NOPE]).astype(BF16)
            qrope_ref[h] = q_rope[:, :QK_ROPE].astype(BF16)
        else:
            q_ref[:, h * hw:h * hw + QK_NOPE] = q_nope.astype(BF16)
            q_ref[:, h * hw + QK_NOPE:(h + 1) * hw] = q_rope.astype(BF16)
            k_ref[:, h * hw:h * hw + QK_NOPE] = _dot(ckv_b, wk_ref[:, h * QK_NOPE:(h + 1) * QK_NOPE]).astype(BF16)
            k_ref[:, h * hw + QK_NOPE:(h + 1) * hw] = kr_b


def _mla_prep(proj, cos_t, sin_t, q_g, kv_g, wq_r, wk, wv, *, row0, nrows, offs, absorbed):
    qlora, kvlora = q_g.shape[0], kv_g.shape[0]
    tm = _pick(nrows, (512, 256, 128))
    assert row0 % tm == 0
    rb0 = row0 // tm
    hw = QK_NOPE + LANES
    scale = 1.0 / math.sqrt(QK_NOPE + QK_ROPE)
    const = lambda i: (0, 0)
    in_specs = [pl.BlockSpec((tm, qlora), lambda i: (rb0 + i, offs["cq"] // qlora)),
                pl.BlockSpec((tm, kvlora), lambda i: (rb0 + i, offs["ckv"] // kvlora)),
                pl.BlockSpec((tm, LANES), lambda i: (rb0 + i, offs["kr"] // LANES)),
                pl.BlockSpec((tm, LANES), lambda i: (rb0 + i, 0)),
                pl.BlockSpec((tm, LANES), lambda i: (rb0 + i, 0)),
                pl.BlockSpec((1, qlora), const), pl.BlockSpec((1, kvlora), const),
                pl.BlockSpec(wq_r.shape, const), pl.BlockSpec(wk.shape, const), pl.BlockSpec(wv.shape, const)]
    out_specs = [pl.BlockSpec((tm, kvlora), lambda i: (i, 0)), pl.BlockSpec((tm, LANES), lambda i: (i, 0))]
    out_shape = [jax.ShapeDtypeStruct((nrows, kvlora), F32), jax.ShapeDtypeStruct((nrows, LANES), F32)]
    if absorbed:
        out_specs += [pl.BlockSpec((MLA_HEADS, tm, kvlora), lambda i: (0, i, 0)),
                      pl.BlockSpec((MLA_HEADS, tm, QK_ROPE), lambda i: (0, i, 0))]
        out_shape += [jax.ShapeDtypeStruct((MLA_HEADS, nrows, kvlora), BF16),
                      jax.ShapeDtypeStruct((MLA_HEADS, nrows, QK_ROPE), BF16)]
    else:
        out_specs += [pl.BlockSpec((tm, MLA_HEADS * hw), lambda i: (i, 0)),
                      pl.BlockSpec((tm, MLA_HEADS * hw), lambda i: (i, 0)),
                      pl.BlockSpec((tm, MLA_HEADS * V_HEAD), lambda i: (i, 0))]
        out_shape += [jax.ShapeDtypeStruct((nrows, MLA_HEADS * hw), BF16),
                      jax.ShapeDtypeStruct((nrows, MLA_HEADS * hw), BF16),
                      jax.ShapeDtypeStruct((nrows, MLA_HEADS * V_HEAD), BF16)]
    return pl.pallas_call(
        functools.partial(_mla_prep_body, absorbed=absorbed, scale=scale),
        grid=(nrows // tm,),
        in_specs=in_specs, out_specs=out_specs, out_shape=out_shape,
        compiler_params=_params("arbitrary"),
    )(proj, proj, proj, cos_t, sin_t, q_g.reshape(1, -1), kv_g.reshape(1, -1), wq_r, wk, wv)


ATT_SUB = 256


def _attn_tile(q_sub, k_ref, v_ref, kcols, vcols, s_scr, p_scr, nk):
    sub = ATT_SUB
    nfull = nk - sub
    shift = CHUNK.bit_length() - 1

    def fold(x, op):
        return op(x.reshape(sub // 8, 8, sub), axis=0)

    if nfull:
        s_scr[0:nfull, :] = _dot_t(k_ref[0, 0:nfull, kcols], q_sub)
    krow = lax.broadcasted_iota(I32, (sub, sub), 0) >> shift
    qcol = lax.broadcasted_iota(I32, (sub, sub), 1) >> shift
    s_scr[nfull:nk, :] = jnp.where(krow <= qcol, _dot_t(k_ref[0, nfull:nk, kcols], q_sub), -jnp.inf)

    mrun = None
    for j in range(nk // sub):
        f = fold(s_scr[j * sub:(j + 1) * sub, :], jnp.max)
        mrun = f if mrun is None else jnp.maximum(mrun, f)
    m = jnp.max(mrun, axis=0, keepdims=True)
    lrun = None
    for j in range(nk // sub):
        pt = jnp.exp(s_scr[j * sub:(j + 1) * sub, :] - m)
        f = fold(pt, jnp.sum)
        lrun = f if lrun is None else lrun + f
        p_scr[j * sub:(j + 1) * sub, :] = pt.astype(BF16)
    l = jnp.sum(lrun, axis=0, keepdims=True)
    acc = lax.dot_general(v_ref[0, 0:nk, vcols], p_scr[0:nk, :], (((0,), (0,)), ((), ())),
                          preferred_element_type=F32)
    return (acc / l).T


ATT_HEADS = 4


def _attn_body(q_ref, k_ref, v_ref, o_ref, s_scr, p_scr, *, tq, nq):
    qi = pl.program_id(2)
    sub = ATT_SUB
    hw = q_ref.shape[2] // ATT_HEADS
    for c in range(nq):
        @pl.when(qi == c)
        def _():
            for hh in range(ATT_HEADS):
                kcols = slice(hh * hw, (hh + 1) * hw)
                vcols = slice(hh * V_HEAD, (hh + 1) * V_HEAD)
                for r in range(tq // sub):
                    q_sub = q_ref[0, r * sub:(r + 1) * sub, kcols]
                    out = _attn_tile(q_sub, k_ref, v_ref, kcols, vcols, s_scr, p_scr, c * tq + (r + 1) * sub)
                    o_ref[r * sub:(r + 1) * sub, vcols] = out.astype(o_ref.dtype)


def _attn_prompt(q, k, v, *, nb, seq):
    hw = (QK_NOPE + LANES) * ATT_HEADS
    vw = V_HEAD * ATT_HEADS
    tq = min(1024, seq // 2)
    assert seq % tq == 0 and tq % ATT_SUB == 0 and ATT_SUB % CHUNK == 0 and MLA_HEADS % ATT_HEADS == 0
    nq = seq // tq
    q3 = q.reshape(nb, seq, -1)
    k3 = k.reshape(nb, seq, -1)
    v3 = v.reshape(nb, seq, -1)
    return pl.pallas_call(
        functools.partial(_attn_body, tq=tq, nq=nq),
        grid=(nb, MLA_HEADS // ATT_HEADS, nq),
        in_specs=[pl.BlockSpec((1, tq, hw), lambda b, h, i: (b, i, h)),
                  pl.BlockSpec((1, seq, hw), lambda b, h, i: (b, 0, h)),
                  pl.BlockSpec((1, seq, vw), lambda b, h, i: (b, 0, h))],
        out_specs=pl.BlockSpec((tq, vw), lambda b, h, i: (b * nq + i, h)),
        out_shape=jax.ShapeDtypeStruct((nb * seq, MLA_HEADS * V_HEAD), BF16),
        scratch_shapes=[pltpu.VMEM((seq, ATT_SUB), F32), pltpu.VMEM((seq, ATT_SUB), BF16)],
        compiler_params=_params("arbitrary", "arbitrary", "arbitrary"),
    )(q3, k3, v3)


def _attn_sample_body(ql_ref, qr_ref, pckv_ref, pkr_ref, nckv_ref, nkr_ref, o_ref, *, past, seq):
    nh = ql_ref.shape[0]
    ql = ql_ref[...].reshape(nh * seq, ql_ref.shape[2])
    qr = qr_ref[...].reshape(nh * seq, qr_ref.shape[2])
    pckv = pckv_ref[0].astype(BF16)
    nckv = nckv_ref[...].astype(BF16)
    s_p = _dot_t(ql, pckv) + _dot_t(qr, pkr_ref[0].astype(BF16))
    s_n = _dot_t(ql, nckv) + _dot_t(qr, nkr_ref[:, :QK_ROPE].astype(BF16))
    shift = CHUNK.bit_length() - 1
    q_chunk = (past + lax.broadcasted_iota(I32, (nh * seq, 1), 0) % seq) >> shift
    kp_chunk = lax.broadcasted_iota(I32, (1, past), 1) >> shift
    kn_chunk = (past + lax.broadcasted_iota(I32, (1, seq), 1)) >> shift
    s_p = jnp.where(kp_chunk <= q_chunk, s_p, -jnp.inf)
    s_n = jnp.where(kn_chunk <= q_chunk, s_n, -jnp.inf)
    m = jnp.maximum(jnp.max(s_p, axis=1, keepdims=True), jnp.max(s_n, axis=1, keepdims=True))
    p_p = jnp.exp(s_p - m)
    p_n = jnp.exp(s_n - m)
    l = jnp.sum(p_p, axis=1, keepdims=True) + jnp.sum(p_n, axis=1, keepdims=True)
    o = (_dot(p_p.astype(BF16), pckv) + _dot(p_n.astype(BF16), nckv)) / l
    o_ref[...] = o.reshape(o_ref.shape).astype(o_ref.dtype)


def _attn_sample(qlat, qrope, past_ckv, past_kr, ckv_n, kr_r, *, nb, seq):
    past = past_ckv.shape[1]
    r = past_ckv.shape[2]
    return pl.pallas_call(
        functools.partial(_attn_sample_body, past=past, seq=seq),
        grid=(nb,),
        in_specs=[pl.BlockSpec((MLA_HEADS, seq, r), lambda b: (0, b, 0)),
                  pl.BlockSpec((MLA_HEADS, seq, QK_ROPE), lambda b: (0, b, 0)),
                  pl.BlockSpec((1, past, r), lambda b: (b, 0, 0)),
                  pl.BlockSpec((1, past, QK_ROPE), lambda b: (b, 0, 0)),
                  pl.BlockSpec((seq, r), lambda b: (b, 0)),
                  pl.BlockSpec((seq, LANES), lambda b: (b, 0))],
        out_specs=pl.BlockSpec((MLA_HEADS, seq, r), lambda b: (0, b, 0)),
        out_shape=jax.ShapeDtypeStruct((MLA_HEADS, nb * seq, r), BF16),
        compiler_params=_params("arbitrary"),
    )(qlat, qrope, past_ckv, past_kr, ckv_n, kr_r)


def _uv_body(o_ref, w_ref, y_ref):
    y_ref[...] = _dot(o_ref[0], w_ref[...]).astype(y_ref.dtype)


def _uv_sample(o_lat, wv):
    nh, nrows, r = o_lat.shape
    return pl.pallas_call(
        _uv_body,
        grid=(nh,),
        in_specs=[pl.BlockSpec((1, nrows, r), lambda h: (h, 0, 0)),
                  pl.BlockSpec((r, V_HEAD), lambda h: (0, h))],
        out_specs=pl.BlockSpec((nrows, V_HEAD), lambda h: (0, h)),
        out_shape=jax.ShapeDtypeStruct((nrows, nh * V_HEAD), BF16),
        compiler_params=_params("arbitrary"),
    )(o_lat, wv)


def _merge1_body(ysp_ref, ymp_ref, yss_ref, yms_ref, ws_ref, wm_ref, ga_ref, gb_ref, o_ref, *, npb):
    def run(ys_ref, ym_ref):
        a = _dot(ys_ref[...], ws_ref[...])
        b = _dot(ym_ref[...], wm_ref[...])
        o_ref[...] = (_sigmoid(ga_ref[...]) * a + _sigmoid(gb_ref[...]) * b).astype(o_ref.dtype)

    i = pl.program_id(1)
    pl.when(i < npb)(lambda: run(ysp_ref, ymp_ref))
    pl.when(i >= npb)(lambda: run(yss_ref, yms_ref))


def _merge1(ys_p, ym_p, ys_s, ym_s, w_ssd_out, w_mla_out, proj, offs):
    (tp, inner), ts = ys_p.shape, ys_s.shape[0]
    dm = ym_p.shape[1]
    d = w_ssd_out.shape[1]
    tm = _pick(math.gcd(tp, ts), (512, 256, 128))
    tn = 512
    npb, nsb = tp // tm, ts // tm
    prow = lambda j, i: (jnp.minimum(i, npb - 1), 0)
    srow = lambda j, i: (jnp.maximum(i - npb, 0), 0)
    return pl.pallas_call(
        functools.partial(_merge1_body, npb=npb),
        grid=(d // tn, npb + nsb),
        in_specs=[pl.BlockSpec((tm, inner), prow), pl.BlockSpec((tm, dm), prow),
                  pl.BlockSpec((tm, inner), srow), pl.BlockSpec((tm, dm), srow),
                  pl.BlockSpec((inner, tn), lambda j, i: (0, j)),
                  pl.BlockSpec((dm, tn), lambda j, i: (0, j)),
                  pl.BlockSpec((tm, tn), lambda j, i: (i, offs["ga"] // tn + j)),
                  pl.BlockSpec((tm, tn), lambda j, i: (i, offs["gb"] // tn + j))],
        out_specs=pl.BlockSpec((tm, tn), lambda j, i: (i, j)),
        out_shape=jax.ShapeDtypeStruct((tp + ts, d), BF16),
        compiler_params=_params("arbitrary", "arbitrary"),
    )(ys_p, ym_p, ys_s, ym_s, w_ssd_out, w_mla_out, proj, proj)


def _merge2_body(m_ref, w_ref, xp_ref, xs_ref, g1_ref, sc_ref, sh_ref, ng_ref, wr_ref, x1_ref, h2_ref, lg_ref,
                 *, npb):
    rows, d = xp_ref.shape
    x = jnp.where(pl.program_id(0) < npb, xp_ref[...], xs_ref[...])
    upd = _dot(m_ref[...], w_ref[...]).reshape(rows // MOD_ROWS, MOD_ROWS, d) * g1_ref[...]
    x1 = x + upd.reshape(rows, d)
    x1_ref[...] = x1
    h2 = _modulate(_rms(x1) * ng_ref[...], sc_ref, sh_ref)
    _store_items(h2_ref, _pack_halves(h2))
    lg_ref[...] = _dot3(wr_ref[...], h2, dot=_dot_t)


def _merge2(merged, w_merge, xp, xs, modg, norm_g, w_router_t):
    d = xp.shape[1]
    t = xp.shape[0] + xs.shape[0]
    assert d == 2 * SUBLANES * LANES
    ne = w_router_t.shape[0]
    tm, npb, prow, srow = _two_stream_rows(xp.shape[0], xs.shape[0], (512, 256, 128))
    ng = tm // MOD_ROWS
    return pl.pallas_call(
        functools.partial(_merge2_body, npb=npb),
        grid=(t // tm,),
        in_specs=[pl.BlockSpec((tm, d), lambda i: (i, 0)),
                  pl.BlockSpec((d, d), lambda i: (0, 0)),
                  pl.BlockSpec((tm, d), prow), pl.BlockSpec((tm, d), srow),
                  pl.BlockSpec((ng, 1, d), lambda i: (i, 0, 2)),
                  pl.BlockSpec((ng, 1, d), lambda i: (i, 0, 4)),
                  pl.BlockSpec((ng, 1, d), lambda i: (i, 0, 3)),
                  pl.BlockSpec((1, d), lambda i: (0, 0)),
                  pl.BlockSpec((ne, d), lambda i: (0, 0))],
        out_specs=[pl.BlockSpec((tm, d), lambda i: (i, 0)),
                   pl.BlockSpec((tm * SUBLANES, LANES), lambda i: (i, 0)),
                   pl.BlockSpec((ne, tm), lambda i: (0, i))],
        out_shape=[jax.ShapeDtypeStruct((t, d), F32), jax.ShapeDtypeStruct((t * SUBLANES, LANES), U32),
                   jax.ShapeDtypeStruct((ne, t), F32)],
        compiler_params=_params("arbitrary"),
    )(merged, w_merge, xp, xs, modg, modg, modg, norm_g.reshape(1, d), w_router_t)


def _route_body(lg_ref, eb_ref, eidx_ref, pos_ref, w_ref, cnt_ref, carry):
    ne, tr = lg_ref.shape
    per_group = ne // N_GROUPS

    @pl.when(pl.program_id(0) == 0)
    def _():
        carry[...] = jnp.zeros_like(carry)

    scores = _sigmoid(lg_ref[...])
    choice = scores + eb_ref[...]
    sub = lax.broadcasted_iota(I32, (per_group, tr), 0)
    gscore, blocks = [], []
    for g in range(N_GROUPS):
        blk = choice[g * per_group:(g + 1) * per_group, :]
        m1 = jnp.max(blk, axis=0, keepdims=True)
        first = jnp.min(jnp.where(blk == m1, sub, per_group), axis=0, keepdims=True)
        m2 = jnp.max(jnp.where(sub == first, -jnp.inf, blk), axis=0, keepdims=True)
        gscore.append(m1 + m2)
        blocks.append(blk)
    masked = []
    for g in range(N_GROUPS):
        rank = jnp.zeros((1, tr), I32)
        for g2 in range(N_GROUPS):
            if g2 == g:
                continue
            beats = (gscore[g2] > gscore[g]) | ((gscore[g2] == gscore[g]) & (g2 < g))
            rank = rank + beats.astype(I32)
        masked.append(jnp.where(rank < TOPK_GROUPS, blocks[g], -jnp.inf))
    cm = jnp.concatenate(masked, axis=0)

    eid = lax.broadcasted_iota(I32, (ne, tr), 0)
    rank = jnp.zeros((ne, tr), I32)
    for e2 in range(ne):
        rowv = cm[e2:e2 + 1, :]
        beats = (rowv > cm) | ((rowv == cm) & (eid > e2))
        rank = rank + beats.astype(I32)
    sel = rank < TOP_K
    wsel = jnp.where(sel, scores, 0.0)
    wfull = wsel / jnp.sum(wsel, axis=0, keepdims=True) * ROUTED_SCALE

    r = lax.broadcasted_iota(I32, (tr, tr), 0)
    c = lax.broadcasted_iota(I32, (tr, tr), 1)
    before = jnp.where(r < c, 1.0, 0.0).astype(BF16)
    self = jnp.where(sel, 1.0, 0.0)
    pos = carry[:, 0:1] + _dot(self.astype(BF16), before)
    carry[...] = carry[...] + jnp.sum(self, axis=1, keepdims=True)
    cnt_ref[...] = carry[...]

    eid_f = eid.astype(F32)
    for k in range(TOP_K):
        pick = sel & (rank == k)
        eidx_ref[k:k + 1, :] = jnp.sum(jnp.where(pick, eid_f, 0.0), axis=0, keepdims=True).astype(I32)
        pos_ref[k:k + 1, :] = jnp.sum(jnp.where(pick, pos, 0.0), axis=0, keepdims=True).astype(I32)
        w_ref[k:k + 1, :] = jnp.sum(jnp.where(pick, wfull, 0.0), axis=0, keepdims=True)


def _route(logits_t, e_bias):
    ne, t = logits_t.shape
    tr = _pick(t, (512, 256, 128))
    return pl.pallas_call(
        _route_body,
        grid=(t // tr,),
        in_specs=[pl.BlockSpec((ne, tr), lambda i: (0, i)), pl.BlockSpec((ne, 1), lambda i: (0, 0))],
        out_specs=[pl.BlockSpec((TOP_K, tr), lambda i: (0, i)), pl.BlockSpec((TOP_K, tr), lambda i: (0, i)),
                   pl.BlockSpec((TOP_K, tr), lambda i: (0, i)), pl.BlockSpec((ne, LANES), lambda i: (0, 0))],
        out_shape=[jax.ShapeDtypeStruct((TOP_K, t), I32), jax.ShapeDtypeStruct((TOP_K, t), I32),
                   jax.ShapeDtypeStruct((TOP_K, t), F32), jax.ShapeDtypeStruct((ne, LANES), F32)],
        scratch_shapes=[pltpu.VMEM((ne, LANES), F32)],
        compiler_params=_params("arbitrary"),
    )(logits_t, e_bias.reshape(ne, 1))


def _row_copy(src, s, dst, d, sem):
    rows = lambda i: pl.ds(pl.multiple_of(i * SUBLANES, SUBLANES), SUBLANES)
    return pltpu.make_async_copy(src.at[rows(s), :], dst.at[rows(d), :], sem)


def _load_items(ref, start, n):
    return jnp.concatenate([ref[pl.ds(start * SUBLANES + c, n, stride=SUBLANES), :] for c in range(SUBLANES)],
                           axis=1)


def _store_items(ref, val):
    n = val.shape[0]
    for c in range(SUBLANES):
        ref[pl.ds(c, n, stride=SUBLANES), :] = val[:, c * LANES:(c + 1) * LANES]


def _drain(wait_one, n, group=64):
    assert n % group == 0

    def body(j, c):
        for _ in range(group):
            wait_one()
        return c

    lax.fori_loop(0, n // group, body, 0)


def _dispatch_body(fill_lo_ref, fill_hi_ref, h_ref, dest_ref, hs_ref, dest_s, zrow, sem, dsem, *, tb):
    i = pl.program_id(0)
    cp = pltpu.make_async_copy(dest_ref.at[i], dest_s, dsem)
    cp.start()

    @pl.when(i == 0)
    def _():
        zrow[...] = jnp.zeros_like(zrow)

        def per_expert(fn):
            def body(e, _):
                lax.fori_loop(fill_lo_ref[e], fill_hi_ref[e], lambda s, c: (fn(s), c)[1], 0)
                return 0
            lax.fori_loop(0, fill_lo_ref.shape[0], body, 0)

        per_expert(lambda s: _row_copy(zrow, 0, hs_ref, s, sem).start())
        per_expert(lambda s: _row_copy(zrow, 0, hs_ref, s, sem).wait())

    cp.wait()

    def issue(t, c):
        for k in range(TOP_K):
            _row_copy(h_ref, t, hs_ref, dest_s[t * TOP_K + k], sem).start(priority=k % 2)
        return c

    lax.fori_loop(0, tb, issue, 0, unroll=4)
    _drain(lambda: _row_copy(h_ref, 0, hs_ref, 0, sem).wait(), TOP_K * tb)


def _dispatch(h2, dest_blk, fill_lo, fill_hi, n_slots):
    nblk, n = dest_blk.shape
    tb = n // TOP_K
    return pl.pallas_call(
        functools.partial(_dispatch_body, tb=tb),
        grid_spec=pltpu.PrefetchScalarGridSpec(
            num_scalar_prefetch=2,
            grid=(nblk,),
            in_specs=[pl.BlockSpec((tb * SUBLANES, LANES), lambda i, lo, hi: (i, 0)),
                      pl.BlockSpec(memory_space=pl.ANY)],
            out_specs=pl.BlockSpec(memory_space=pl.ANY),
            scratch_shapes=[pltpu.SMEM((n,), I32), pltpu.VMEM((SUBLANES, LANES), h2.dtype),
                            pltpu.SemaphoreType.DMA, pltpu.SemaphoreType.DMA]),
        out_shape=jax.ShapeDtypeStruct((n_slots * SUBLANES, LANES), h2.dtype),
        compiler_params=_params("arbitrary"),
    )(fill_lo, fill_hi, h2, dest_blk)


def _experts_body(be_ref, nu_ref, x_ref, wg_ref, wu_ref, wd_ref, o_ref, wgu_s, wd_s):
    i = pl.program_id(0)
    hid = wg_ref.shape[2]

    @pl.when(i < nu_ref[0])
    def _():
        @pl.when((i == 0) | (be_ref[i] != be_ref[jnp.maximum(i - 1, 0)]))
        def _():
            wgu_s[:, :hid] = wg_ref[0].astype(BF16)
            wgu_s[:, hid:] = wu_ref[0].astype(BF16)
            wd_s[...] = wd_ref[0].astype(BF16)

        lo, hi = _unpack_halves(_load_items(x_ref, 0, x_ref.shape[0] // SUBLANES))
        x = jnp.concatenate([lo.astype(BF16), hi.astype(BF16)], axis=1)
        gu = _dot(x, wgu_s[...])
        act = (_silu(gu[:, :hid]) * gu[:, hid:]).astype(BF16)
        _store_items(o_ref, _pack_halves(_dot(act, wd_s[...])))


def _experts(hs, block_e, n_used, wg, wu, wd):
    n_slots = hs.shape[0] // SUBLANES
    ne, d, hid = wg.shape
    assert d == 2 * SUBLANES * LANES
    bm = MOE_ROWS
    nblocks = n_slots // bm
    blk = lambda i, be, nu: (jnp.minimum(i, nu[0] - 1), 0)
    return pl.pallas_call(
        _experts_body,
        grid_spec=pltpu.PrefetchScalarGridSpec(
            num_scalar_prefetch=2,
            grid=(nblocks,),
            in_specs=[pl.BlockSpec((bm * SUBLANES, LANES), blk),
                      pl.BlockSpec((1, d, hid), lambda i, be, nu: (be[i], 0, 0)),
                      pl.BlockSpec((1, d, hid), lambda i, be, nu: (be[i], 0, 0)),
                      pl.BlockSpec((1, hid, d), lambda i, be, nu: (be[i], 0, 0))],
            out_specs=pl.BlockSpec((bm * SUBLANES, LANES), blk),
            scratch_shapes=[pltpu.VMEM((d, 2 * hid), BF16), pltpu.VMEM((hid, d), BF16)]),
        out_shape=jax.ShapeDtypeStruct(hs.shape, U32),
        compiler_params=_params("arbitrary"),
    )(block_e, n_used, hs, wg, wu, wd)


def _combine_body(ys_ref, dest_ref, w_ref, h_ref, x1_ref, g2_ref, wgu_ref, wd_ref, fg_ref, op_ref, os_ref,
                  dest_s, gbuf, sem, dsem, *, tb, npb):
    i = pl.program_id(0)
    cp = pltpu.make_async_copy(dest_ref.at[i], dest_s, dsem)
    cp.start()
    cp.wait()

    def issue(t, c):
        for k in range(TOP_K):
            _row_copy(ys_ref, dest_s[t * TOP_K + k], gbuf, k * tb + t, sem).start(priority=k % 2)
        return c

    lax.fori_loop(0, tb, issue, 0, unroll=4)

    hid = wd_ref.shape[0]
    h_lo, h_hi = _unpack_halves(_load_items(h_ref, 0, tb))
    gu = _dot(jnp.concatenate([h_lo.astype(BF16), h_hi.astype(BF16)], axis=1), wgu_ref[...])
    moe = _dot((_silu(gu[:, :hid]) * gu[:, hid:]).astype(BF16), wd_ref[...])

    _drain(lambda: _row_copy(ys_ref, 0, gbuf, 0, sem).wait(), TOP_K * tb)
    w = w_ref[...]
    dh = SUBLANES * LANES
    m_lo, m_hi = moe[:, :dh], moe[:, dh:]
    for k in range(TOP_K):
        lo, hi = _unpack_halves(_load_items(gbuf, k * tb, tb))
        m_lo = m_lo + lo * w[:, k:k + 1]
        m_hi = m_hi + hi * w[:, k:k + 1]
    moe = jnp.concatenate([m_lo, m_hi], axis=1)
    rows, d = moe.shape
    upd = moe.reshape(rows // MOD_ROWS, MOD_ROWS, d) * g2_ref[...]
    x2 = x1_ref[...] + upd.reshape(rows, d)
    y = _rms(x2) * fg_ref[...]

    @pl.when(i < npb)
    def _():
        op_ref[...] = y

    @pl.when(i >= npb)
    def _():
        os_ref[...] = y


def _combine(ys, dest_blk, w_tok, h2, x1, modg, wsh_gu, wsh_d, final_g, *, tp):
    t, d = x1.shape
    nblk, n = dest_blk.shape
    tb = n // TOP_K
    assert tp % tb == 0 and (t - tp) % tb == 0
    npb = tp // tb
    ng = tb // MOD_ROWS
    const = lambda i: (0, 0)
    return pl.pallas_call(
        functools.partial(_combine_body, tb=tb, npb=npb),
        grid=(nblk,),
        in_specs=[pl.BlockSpec(memory_space=pl.ANY),
                  pl.BlockSpec(memory_space=pl.ANY),
                  pl.BlockSpec((tb, TOP_K), lambda i: (i, 0)),
                  pl.BlockSpec((tb * SUBLANES, LANES), lambda i: (i, 0)),
                  pl.BlockSpec((tb, d), lambda i: (i, 0)),
                  pl.BlockSpec((ng, 1, d), lambda i: (i, 0, 5)),
                  pl.BlockSpec(wsh_gu.shape, const), pl.BlockSpec(wsh_d.shape, const),
                  pl.BlockSpec((1, d), const)],
        out_specs=[pl.BlockSpec((tb, d), lambda i: (jnp.minimum(i, npb - 1), 0)),
                   pl.BlockSpec((tb, d), lambda i: (jnp.maximum(i - npb, 0), 0))],
        out_shape=[jax.ShapeDtypeStruct((tp, d), F32), jax.ShapeDtypeStruct((t - tp, d), F32)],
        scratch_shapes=[pltpu.SMEM((n,), I32), pltpu.VMEM((n * SUBLANES, LANES), U32),
                        pltpu.SemaphoreType.DMA, pltpu.SemaphoreType.DMA],
        compiler_params=_params("arbitrary"),
    )(ys, dest_blk, w_tok, h2, x1, modg, wsh_gu, wsh_d, final_g.reshape(1, d))


def _rope_tables(pos):
    half = QK_ROPE // 2
    freqs = ROPE_THETA ** (-jnp.arange(half, dtype=F32) / half)
    ang = pos.astype(F32)[:, None] * freqs[None, :]
    cos, sin = jnp.cos(ang), jnp.sin(ang)
    return (jnp.concatenate([cos, cos, cos, cos], axis=1),
            jnp.concatenate([-sin, sin, -sin, sin], axis=1))


def _blocked(a, tb):
    k, t = a.shape
    return a.T.reshape(t // tb, tb * k)


def kernel(x_prompt, x_sample, c_prompt, c_sample, cache_conv, state_ssm, cache_ckv, cache_kr, w_ada, b_ada,
           norm1_g, norm2_g, w_in, conv_w, conv_b, dt_bias, a_log, d_skip, ssd_norm_g, w_ssd_out, q_norm_g,
           w_uq, kv_norm_g, w_uk, w_uv, w_mla_out, w_merge_out, w_router, e_bias, w_exp_gate, w_exp_up,
           w_exp_down, w_sh_gate, w_sh_up, w_sh_down, final_norm_g):
    depth = w_in.shape[0]
    assert depth == 1
    bp, seq, d = x_prompt.shape
    bs, lseq, _ = x_sample.shape
    assert lseq == MOD_ROWS and seq % MOD_ROWS == 0
    tp, ts = bp * seq, bs * lseq
    t_all = tp + ts
    nheads = dt_bias.shape[1]
    inner = nheads * SSD_HEAD_DIM
    cdim = conv_w.shape[2]
    gn = SSD_GROUPS * SSD_STATE
    qlora, kvlora = q_norm_g.shape[1], kv_norm_g.shape[1]
    ne = w_router.shape[2]
    assert nheads == 64 and 2 * nheads == LANES

    w = w_in[0]
    o_xbc, o_dt = inner, inner + cdim
    o_cq = o_dt + nheads
    o_ckv = o_cq + qlora
    o_kr = o_ckv + kvlora
    o_gate = o_kr + QK_ROPE
    perm = np.concatenate([np.arange(0, nheads, 2), np.arange(1, nheads, 2)])
    zc = lambda n: jnp.zeros((d, n), w.dtype)
    cols = [w[:, :inner], w[:, o_gate:], w[:, o_xbc:o_dt], w[:, o_cq:o_ckv], w[:, o_ckv:o_kr],
            w[:, o_dt:o_cq][:, perm], zc(LANES - nheads), w[:, o_kr:o_gate], zc(LANES - QK_ROPE)]
    used = inner + 2 * d + cdim + qlora + kvlora + 2 * LANES
    total = -(-used // 512) * 512
    cols.append(zc(total - used))
    w_r = jnp.concatenate(cols, axis=1).astype(BF16)
    offs = {"z": 0, "ga": inner, "gb": inner + d, "x": inner + 2 * d, "bc": 2 * inner + 2 * d}
    offs["cq"] = offs["bc"] + 2 * gn
    offs["ckv"] = offs["cq"] + qlora
    offs["dt"] = offs["ckv"] + kvlora
    offs["kr"] = offs["dt"] + LANES

    hw = QK_NOPE + LANES
    wq_r = jnp.pad(w_uq[0], ((0, 0), (0, 0), (0, hw - QK_NOPE - QK_ROPE))).reshape(qlora, MLA_HEADS * hw).astype(BF16)
    wk = w_uk[0].reshape(kvlora, MLA_HEADS * QK_NOPE).astype(BF16)
    wv = w_uv[0].reshape(kvlora, MLA_HEADS * V_HEAD).astype(BF16)
    pad_l = lambda v: jnp.pad(v[perm], (0, LANES - nheads)).reshape(1, LANES)
    dtb = pad_l(dt_bias[0])
    a_neg = pad_l(-jnp.exp(a_log[0]))
    dskip = jnp.repeat(d_skip[0], SSD_HEAD_DIM).reshape(1, inner)

    c_all = jnp.concatenate([c_prompt, c_sample], axis=0)
    mod = _ada(c_all, w_ada[0], b_ada[0])
    grp = np.concatenate([np.repeat(np.arange(bp), seq // MOD_ROWS), bp + np.arange(bs)])
    modg = mod[grp].reshape(t_all // MOD_ROWS, 1, 6 * d)

    xp, xs = x_prompt.reshape(tp, d), x_sample.reshape(ts, d)
    proj = _inproj(xp, xs, modg, norm1_g[0], w_r, d)

    ssd_args = (conv_w[0], conv_b[0].reshape(1, cdim), dtb, a_neg, dskip, ssd_norm_g[0].reshape(1, inner))
    ys_p, conv_p, ssm_p = _ssd(proj, jnp.zeros((bp, CONV_W - 1, cdim), F32),
                               jnp.zeros((1, bp) + state_ssm.shape[2:], F32), *ssd_args,
                               nb=bp, seq=seq, row0=0, offs=offs, inner=inner)
    ys_s, conv_s, ssm_s = _ssd(proj, cache_conv[0], state_ssm,
                               *ssd_args, nb=bs, seq=lseq, row0=tp, offs=offs, inner=inner)

    past = cache_ckv.shape[2]
    cos_p, sin_p = _rope_tables(jnp.arange(seq))
    cos_s, sin_s = _rope_tables(past + jnp.arange(lseq))
    cos_t = jnp.concatenate([jnp.tile(cos_p, (bp, 1)), jnp.tile(cos_s, (bs, 1))], axis=0)
    sin_t = jnp.concatenate([jnp.tile(sin_p, (bp, 1)), jnp.tile(sin_s, (bs, 1))], axis=0)
    prep = functools.partial(_mla_prep, proj, cos_t, sin_t, q_norm_g[0], kv_norm_g[0], wq_r, wk, wv, offs=offs)
    ckv_p, kr_p, q_p, k_p, v_p = prep(row0=0, nrows=tp, absorbed=False)
    ckv_s, kr_s, qlat, qrope = prep(row0=tp, nrows=ts, absorbed=True)
    ym_p = _attn_prompt(q_p, k_p, v_p, nb=bp, seq=seq)
    o_lat = _attn_sample(qlat, qrope, cache_ckv[0], cache_kr[0], ckv_s, kr_s, nb=bs, seq=lseq)
    ym_s = _uv_sample(o_lat, wv)

    merged = _merge1(ys_p, ym_p, ys_s, ym_s, w_ssd_out[0].astype(BF16), w_mla_out[0].astype(BF16), proj, offs)
    x1, h2, logits_t = _merge2(merged, w_merge_out[0].astype(BF16), xp, xs, modg, norm2_g[0], w_router[0].T)

    eidx, pos, w_sel, cnt = _route(logits_t, e_bias[0])
    bm = MOE_ROWS
    counts = cnt[:, 0].astype(I32)
    padded = (counts + bm - 1) // bm * bm
    pad_end = jnp.cumsum(padded)
    pad_start = pad_end - padded
    nblocks = -(-(t_all * TOP_K) // bm) + ne
    onehot = eidx[:, :, None] == jnp.arange(ne, dtype=I32)
    dest = jnp.sum(jnp.where(onehot, pad_start, 0), axis=-1) + pos
    blk_start = jnp.arange(nblocks, dtype=I32) * bm
    block_e = jnp.minimum(jnp.sum(pad_end[None, :] <= blk_start[:, None], axis=1), ne - 1).astype(I32)
    n_used = (pad_end[-1:] // bm).astype(I32)

    tb_d = _pick(t_all, (1024, 512, 256, 128))
    hs = _dispatch(h2, _blocked(dest, tb_d), (pad_start + counts).astype(I32), pad_end.astype(I32), nblocks * bm)
    ys = _experts(hs, block_e, n_used, w_exp_gate[0], w_exp_up[0], w_exp_down[0])
    wsh_gu = jnp.concatenate([w_sh_gate[0], w_sh_up[0]], axis=1).astype(BF16)
    tb_c = _pick(math.gcd(tp, ts), (256, 128))
    y_p, y_s = _combine(ys, _blocked(dest, tb_c), w_sel.T, h2, x1, modg, wsh_gu, w_sh_down[0].astype(BF16),
                        final_norm_g, tp=tp)

    r5 = lambda a, n, l: a.reshape(1, n, l, a.shape[-1])
    return (y_p.reshape(bp, seq, d), y_s.reshape(bs, lseq, d),
            conv_p[None], ssm_p,
            r5(ckv_p, bp, seq), r5(kr_p[:, :QK_ROPE], bp, seq),
            conv_s[None], ssm_s,
            r5(ckv_s, bs, lseq), r5(kr_s[:, :QK_ROPE], bs, lseq))
```

```python
import functools
import math

import numpy as np
import jax
import jax.numpy as jnp
from jax import lax
from jax.experimental import pallas as pl
from jax.experimental.pallas import tpu as pltpu

F32 = jnp.float32
BF16 = jnp.bfloat16
I32 = jnp.int32
U32 = jnp.uint32

EPS = 1e-6
CHUNK = 64
SSD_HEAD_DIM = 64
SSD_GROUPS = 8
SSD_STATE = 128
CONV_W = 4
MLA_HEADS = 16
QK_NOPE = 128
QK_ROPE = 64
V_HEAD = 128
ROPE_THETA = 10000.0
TOP_K = 8
N_GROUPS = 8
TOPK_GROUPS = 4
ROUTED_SCALE = 2.5

LANES = 128
SUBLANES = 8
MOD_ROWS = 32
MOE_ROWS = 512
VMEM_LIMIT = 56 * 1024 * 1024


def _params(*sem):
    return pltpu.CompilerParams(dimension_semantics=sem, vmem_limit_bytes=VMEM_LIMIT)


def _pick(n, cands):
    for c in cands:
        if n % c == 0:
            return c
    raise ValueError(f"no tile in {cands} divides {n}")


def _sigmoid(x):
    return 1.0 / (1.0 + jnp.exp(-x))


def _silu(x):
    return x * _sigmoid(x)


def _dot(a, b):
    return jnp.dot(a, b, preferred_element_type=F32)


def _dot_t(a, b):
    return lax.dot_general(a, b, (((1,), (1,)), ((), ())), preferred_element_type=F32)


def _split(x, n):
    parts = []
    for _ in range(n - 1):
        p = x.astype(BF16)
        parts.append(p)
        x = x - p.astype(F32)
    parts.append(x.astype(BF16))
    return parts


def _dot_exact_rhs(a, b_bf16, n=3):
    acc = None
    for p in _split(a, n):
        t = _dot(p, b_bf16)
        acc = t if acc is None else acc + t
    return acc


def _dot3(a, b, dot=_dot):
    ah, al = _split(a, 2)
    bh, bl = _split(b, 2)
    return dot(ah, bh) + (dot(ah, bl) + dot(al, bh))


def _pack_halves(x):
    h = x.shape[1] // 2
    bits = lambda v: lax.bitcast_convert_type(v.astype(BF16).astype(F32), U32)
    return (bits(x[:, :h]) >> 16) | bits(x[:, h:])


def _unpack_halves(p):
    lo = lax.bitcast_convert_type(p << 16, F32)
    hi = lax.bitcast_convert_type(p & jnp.uint32(0xFFFF0000), F32)
    return lo, hi


def _rms(x):
    return x * lax.rsqrt(jnp.mean(x * x, axis=-1, keepdims=True) + EPS)


def _modulate(y, sc_ref, sh_ref):
    rows, d = y.shape
    y3 = y.reshape(rows // MOD_ROWS, MOD_ROWS, d)
    return (y3 * (1.0 + sc_ref[...]) + sh_ref[...]).reshape(rows, d)


def _ada_body(c_ref, w_ref, b_ref, o_ref):
    o_ref[...] = _dot3(_silu(c_ref[...]), w_ref[...]) + b_ref[...]


def _ada(c_all, w_ada, b_ada):
    nb, d = c_all.shape
    n = w_ada.shape[1]
    tn = _pick(n, (1024, 512, 256, 128))
    return pl.pallas_call(
        _ada_body,
        grid=(n // tn,),
        in_specs=[pl.BlockSpec((nb, d), lambda j: (0, 0)),
                  pl.BlockSpec((d, tn), lambda j: (0, j)),
                  pl.BlockSpec((1, tn), lambda j: (0, j))],
        out_specs=pl.BlockSpec((nb, tn), lambda j: (0, j)),
        out_shape=jax.ShapeDtypeStruct((nb, n), F32),
        compiler_params=_params("arbitrary"),
    )(c_all, w_ada, b_ada.reshape(1, n))


def _two_stream_rows(tp, ts, cands):
    tm = _pick(math.gcd(tp, ts), cands)
    npb = tp // tm
    return tm, npb, (lambda i, *_: (jnp.minimum(i, npb - 1), 0)), (lambda i, *_: (jnp.maximum(i - npb, 0), 0))


def _inproj_body(xp_ref, xs_ref, sc_ref, sh_ref, g_ref, w_ref, o_ref, h_scr, *, npb):
    @pl.when(pl.program_id(1) == 0)
    def _():
        x = jnp.where(pl.program_id(0) < npb, xp_ref[...], xs_ref[...])
        h = _modulate(_rms(x) * g_ref[...], sc_ref, sh_ref)
        h_scr[...] = h.astype(BF16)

    o_ref[...] = _dot(h_scr[...], w_ref[...])


def _inproj(xp, xs, modg, norm_g, w_r, d):
    t = xp.shape[0] + xs.shape[0]
    n = w_r.shape[1]
    tm, npb, prow, srow = _two_stream_rows(xp.shape[0], xs.shape[0], (1024, 512, 256, 128))
    tn = 512
    ng = tm // MOD_ROWS
    return pl.pallas_call(
        functools.partial(_inproj_body, npb=npb),
        grid=(t // tm, n // tn),
        in_specs=[pl.BlockSpec((tm, d), prow), pl.BlockSpec((tm, d), srow),
                  pl.BlockSpec((ng, 1, d), lambda i, j: (i, 0, 1)),
                  pl.BlockSpec((ng, 1, d), lambda i, j: (i, 0, 0)),
                  pl.BlockSpec((1, d), lambda i, j: (0, 0)),
                  pl.BlockSpec((d, tn), lambda i, j: (0, j))],
        out_specs=pl.BlockSpec((tm, tn), lambda i, j: (i, j)),
        out_shape=jax.ShapeDtypeStruct((t, n), F32),
        scratch_shapes=[pltpu.VMEM((tm, d), BF16)],
        compiler_params=_params("arbitrary", "arbitrary"),
    )(xp, xs, modg, modg, norm_g.reshape(1, d), w_r)


def _ssd_body(z_ref, x_ref, bc_ref, dt_ref, conv0_ref, h0_ref, cw_ref, cb_ref, dtb_ref, a_ref,
              dskip_ref, ng_ref, y_ref, nconv_ref, nssm_ref,
              carry_x, carry_bc, xp_x, xp_bc, state, y_scr, *, lb, inner):
    q = CHUNK
    c = pl.program_id(1)
    nc = pl.num_programs(1)
    gn = SSD_GROUPS * SSD_STATE

    @pl.when(c == 0)
    def _():
        carry_x[...] = jnp.zeros_like(carry_x)
        carry_bc[...] = jnp.zeros_like(carry_bc)
        carry_x[8 - (CONV_W - 1):8, :] = conv0_ref[0, :, :inner]
        carry_bc[8 - (CONV_W - 1):8, :] = conv0_ref[0, :, inner:]
        state[...] = h0_ref[0, 0].reshape(state.shape)

    def pad_rows(v):
        if lb == q:
            return v
        return jnp.concatenate([v, jnp.zeros((q - lb, v.shape[1]), v.dtype)], axis=0)

    def conv(raw, carry, xp, w_lo, w_hi):
        xp[0:8, :] = carry[...]
        xp[8:8 + q, :] = raw
        acc = cb_ref[:, w_lo:w_hi]
        for j in range(CONV_W):
            acc = acc + xp[8 - j:8 - j + q, :] * cw_ref[CONV_W - 1 - j:CONV_W - j, w_lo:w_hi]
        return _silu(acc)

    x_raw = pad_rows(x_ref[...])
    bc_raw = pad_rows(bc_ref[...])
    xc = conv(x_raw, carry_x, xp_x, 0, inner)
    bcc = conv(bc_raw, carry_bc, xp_bc, inner, inner + 2 * gn)
    if lb == q:
        carry_x[...] = x_raw[q - 8:q, :]
        carry_bc[...] = bc_raw[q - 8:q, :]

    @pl.when(c == nc - 1)
    def _():
        nconv_ref[0, :, :inner] = x_raw[lb - (CONV_W - 1):lb, :]
        nconv_ref[0, :, inner:] = bc_raw[lb - (CONV_W - 1):lb, :]

    dtr = dt_ref[...] + dtb_ref[...]
    dtv = jnp.maximum(dtr, 0.0) + jnp.log1p(jnp.exp(-jnp.abs(dtr)))
    dtv = pad_rows(dtv)
    la = dtv * a_ref[...]

    row = lax.broadcasted_iota(I32, (q, q), 0)
    col = lax.broadcasted_iota(I32, (q, q), 1)
    tri = jnp.where(row >= col, 1.0, 0.0).astype(BF16)
    parts = _split(la, 3)
    a_cum = _dot(tri, parts[0]) + _dot(tri, parts[1]) + _dot(tri, parts[2])
    a_last = a_cum[q - 1:q, :]

    r2 = lax.broadcasted_iota(I32, (q, 2 * q), 0)
    c2 = lax.broadcasted_iota(I32, (q, 2 * q), 1)
    u_e = jnp.where((c2 < q) & (r2 <= c2), 1.0, 0.0).astype(BF16)
    u_o = jnp.where((c2 >= q) & (r2 <= c2 - q), 1.0, 0.0).astype(BF16)
    i_e = jnp.where(r2 == c2, 1.0, 0.0).astype(BF16)
    i_o = jnp.where(r2 == c2 - q, 1.0, 0.0).astype(BF16)
    la_t = la.T
    dt_t = dtv.T
    hp = la_t.shape[0] // 4
    acol = _dot_exact_rhs(la_t[0:hp], u_e) + _dot_exact_rhs(la_t[hp:2 * hp], u_o)
    dtrow = _dot_exact_rhs(dt_t[0:hp], i_e) + _dot_exact_rhs(dt_t[hp:2 * hp], i_o)
    w_t = (dtv * jnp.exp(a_last - a_cum)).T
    dec_b = jnp.broadcast_to(jnp.exp(jnp.sum(la_t, axis=1, keepdims=True)), (la_t.shape[0], SSD_STATE))

    lane = lax.broadcasted_iota(I32, (q, 2 * q), 1)
    causal2 = lax.broadcasted_iota(I32, (q, 2 * q), 0) >= jnp.where(lane < q, lane, lane - q)
    first_half = lane < q
    rr = lax.broadcasted_iota(I32, (2 * SSD_HEAD_DIM, 2 * SSD_HEAD_DIM), 0)
    cc = lax.broadcasted_iota(I32, (2 * SSD_HEAD_DIM, 2 * SSD_HEAD_DIM), 1)
    bd_mask = (rr < SSD_HEAD_DIM) == (cc < SSD_HEAD_DIM)
    top_rows_q = lax.broadcasted_iota(I32, (2 * SSD_HEAD_DIM, q), 0) < SSD_HEAD_DIM
    top_rows_n = lax.broadcasted_iota(I32, (2 * SSD_HEAD_DIM, SSD_STATE), 0) < SSD_HEAD_DIM

    pairs_per_group = (inner // SSD_HEAD_DIM) // SSD_GROUPS // 2
    for g in range(SSD_GROUPS):
        b_g = bcc[:, g * SSD_STATE:(g + 1) * SSD_STATE].astype(BF16)
        c_g = bcc[:, gn + g * SSD_STATE:gn + (g + 1) * SSD_STATE].astype(BF16)
        cb2 = _dot_t(c_g, jnp.concatenate([b_g, b_g], axis=0))
        for kk in range(pairs_per_group):
            k = g * pairs_per_group + kk
            lo, hi = k * 2 * SSD_HEAD_DIM, (k + 1) * 2 * SSD_HEAD_DIM
            xp = xc[:, lo:hi]
            arow = jnp.where(first_half, a_cum[:, k:k + 1], a_cum[:, hp + k:hp + k + 1])
            seg = jnp.where(causal2, arow - acol[k:k + 1, :], -jnp.inf)
            m = (jnp.exp(seg) * cb2 * dtrow[k:k + 1, :]).astype(BF16)
            xbd = jnp.where(bd_mask, jnp.concatenate([xp, xp], axis=0), 0.0).astype(BF16)
            y_diag = _dot(m, xbd)
            s_k = state[lo:hi, :]
            y_off = _dot_t(c_g, s_k.astype(BF16)) * jnp.exp(arow)
            y_scr[:, lo:hi] = y_diag + y_off + xp * dskip_ref[:, lo:hi]
            w2 = jnp.where(top_rows_q, w_t[k:k + 1, :], w_t[hp + k:hp + k + 1, :])
            contrib = _dot((xp.T * w2).astype(BF16), b_g)
            d_k = jnp.where(top_rows_n, dec_b[k:k + 1, :], dec_b[hp + k:hp + k + 1, :])
            state[lo:hi, :] = s_k * d_k + contrib

    @pl.when(c == nc - 1)
    def _():
        nssm_ref[0, 0] = state[...].reshape(nssm_ref.shape[2:])

    y = y_scr[0:lb, :] * _silu(z_ref[...])
    y_ref[...] = (_rms(y) * ng_ref[...]).astype(y_ref.dtype)


def _ssd(proj, conv0, h0, conv_w, conv_b, dtb, a_neg, dskip, norm_g, *, nb, seq, row0, offs, inner):
    q = CHUNK
    lb = min(seq, q)
    assert seq % lb == 0 and row0 % lb == 0 and lb % 16 == 0
    nc = seq // lb
    rb0 = row0 // lb
    cdim = conv_w.shape[1]
    hp_rows = inner
    state_shape = (inner // SSD_HEAD_DIM, SSD_HEAD_DIM, SSD_STATE)
    assert h0.shape == (1, nb) + state_shape
    kern = functools.partial(_ssd_body, lb=lb, inner=inner)
    rows = lambda b, c: rb0 + b * nc + c
    in_specs = [
        pl.BlockSpec((lb, inner), lambda b, c: (rows(b, c), offs["z"] // inner)),
        pl.BlockSpec((lb, inner), lambda b, c: (rows(b, c), offs["x"] // inner)),
        pl.BlockSpec((lb, cdim - inner), lambda b, c: (rows(b, c), offs["bc"] // (cdim - inner))),
        pl.BlockSpec((lb, LANES), lambda b, c: (rows(b, c), offs["dt"] // LANES)),
        pl.BlockSpec((1, CONV_W - 1, cdim), lambda b, c: (b, 0, 0)),
        pl.BlockSpec((1, 1) + state_shape, lambda b, c: (0, b, 0, 0, 0)),
        pl.BlockSpec((CONV_W, cdim), lambda b, c: (0, 0)),
        pl.BlockSpec((1, cdim), lambda b, c: (0, 0)),
        pl.BlockSpec((1, LANES), lambda b, c: (0, 0)),
        pl.BlockSpec((1, LANES), lambda b, c: (0, 0)),
        pl.BlockSpec((1, inner), lambda b, c: (0, 0)),
        pl.BlockSpec((1, inner), lambda b, c: (0, 0)),
    ]
    args = [proj, proj, proj, proj, conv0, h0, conv_w, conv_b, dtb, a_neg, dskip, norm_g]
    return pl.pallas_call(
        kern,
        grid=(nb, nc),
        in_specs=in_specs,
        out_specs=[pl.BlockSpec((lb, inner), lambda b, c: (b * nc + c, 0)),
                   pl.BlockSpec((1, CONV_W - 1, cdim), lambda b, c: (b, 0, 0)),
                   pl.BlockSpec((1, 1) + state_shape, lambda b, c: (0, b, 0, 0, 0))],
        out_shape=[jax.ShapeDtypeStruct((nb * seq, inner), BF16),
                   jax.ShapeDtypeStruct((nb, CONV_W - 1, cdim), F32),
                   jax.ShapeDtypeStruct((1, nb) + state_shape, F32)],
        scratch_shapes=[pltpu.VMEM((8, inner), F32), pltpu.VMEM((8, cdim - inner), F32),
                        pltpu.VMEM((8 + q, inner), F32), pltpu.VMEM((8 + q, cdim - inner), F32),
                        pltpu.VMEM((hp_rows, SSD_STATE), F32), pltpu.VMEM((q, inner), F32)],
        compiler_params=_params("arbitrary", "arbitrary"),
    )(*args)


def _rope128(x, cos, sin):
    lane = lax.broadcasted_iota(I32, x.shape, 1)
    half = QK_ROPE // 2
    swapped = jnp.where((lane % QK_ROPE) < half, pltpu.roll(x, LANES - half, 1), pltpu.roll(x, half, 1))
    return x * cos + swapped * sin


def _mla_prep_body(cq_ref, ckv_ref, kr_ref, cos_ref, sin_ref, qg_ref, kvg_ref, wq_ref, wk_ref, wv_ref,
                   *out_refs, absorbed, scale):
    cos, sin = cos_ref[...], sin_ref[...]
    ckv_n = _rms(ckv_ref[...]) * kvg_ref[...]
    kr_r = _rope128(kr_ref[...], cos, sin)
    qn = (_rms(cq_ref[...]) * qg_ref[...]).astype(BF16)
    hw = QK_NOPE + LANES
    if absorbed:
        ckvn_ref, krr_ref, qlat_ref, qrope_ref = out_refs
    else:
        ckvn_ref, krr_ref, q_ref, k_ref, v_ref = out_refs
        ckv_b = ckv_n.astype(BF16)
        v_ref[...] = _dot(ckv_b, wv_ref[...]).astype(BF16)
        kr_b = kr_r.astype(BF16)
    ckvn_ref[...] = ckv_n
    krr_ref[...] = kr_r
    for h in range(MLA_HEADS):
        qh = _dot(qn, wq_ref[:, h * hw:(h + 1) * hw]) * scale
        q_nope = qh[:, :QK_NOPE]
        q_rope = _rope128(qh[:, QK_NOPE:], cos, sin)
        if absorbed:
            qlat_ref[h] = _dot_t(q_nope.astype(BF16), wk_ref[:, h * QK_NOPE:(h + 1) * QK_NOPE]).astype(BF16)
            qrope_ref[h] = q_rope[:, :QK_ROPE].astype(BF16)
        else:
            q_ref[:, h * hw:h * hw + QK_NOPE] = q_nope.astype(BF16)
            q_ref[:, h * hw + QK_NOPE:(h + 1) * hw] = q_rope.astype(BF16)
            k_ref[:, h * hw:h * hw + QK_NOPE] = _dot(ckv_b, wk_ref[:, h * QK_NOPE:(h + 1) * QK_NOPE]).astype(BF16)
            k_ref[:, h * hw + QK_NOPE:(h + 1) * hw] = kr_b


def _mla_prep(proj, cos_t, sin_t, q_g, kv_g, wq_r, wk, wv, *, row0, nrows, offs, absorbed):
    qlora, kvlora = q_g.shape[0], kv_g.shape[0]
    tm = _pick(nrows, (512, 256, 128))
    assert row0 % tm == 0
    rb0 = row0 // tm
    hw = QK_NOPE + LANES
    scale = 1.0 / math.sqrt(QK_NOPE + QK_ROPE)
    const = lambda i: (0, 0)
    in_specs = [pl.BlockSpec((tm, qlora), lambda i: (rb0 + i, offs["cq"] // qlora)),
                pl.BlockSpec((tm, kvlora), lambda i: (rb0 + i, offs["ckv"] // kvlora)),
                pl.BlockSpec((tm, LANES), lambda i: (rb0 + i, offs["kr"] // LANES)),
                pl.BlockSpec((tm, LANES), lambda i: (rb0 + i, 0)),
                pl.BlockSpec((tm, LANES), lambda i: (rb0 + i, 0)),
                pl.BlockSpec((1, qlora), const), pl.BlockSpec((1, kvlora), const),
                pl.BlockSpec(wq_r.shape, const), pl.BlockSpec(wk.shape, const), pl.BlockSpec(wv.shape, const)]
    out_specs = [pl.BlockSpec((tm, kvlora), lambda i: (i, 0)), pl.BlockSpec((tm, LANES), lambda i: (i, 0))]
    out_shape = [jax.ShapeDtypeStruct((nrows, kvlora), F32), jax.ShapeDtypeStruct((nrows, LANES), F32)]
    if absorbed:
        out_specs += [pl.BlockSpec((MLA_HEADS, tm, kvlora), lambda i: (0, i, 0)),
                      pl.BlockSpec((MLA_HEADS, tm, QK_ROPE), lambda i: (0, i, 0))]
        out_shape += [jax.ShapeDtypeStruct((MLA_HEADS, nrows, kvlora), BF16),
                      jax.ShapeDtypeStruct((MLA_HEADS, nrows, QK_ROPE), BF16)]
    else:
        out_specs += [pl.BlockSpec((tm, MLA_HEADS * hw), lambda i: (i, 0)),
                      pl.BlockSpec((tm, MLA_HEADS * hw), lambda i: (i, 0)),
                      pl.BlockSpec((tm, MLA_HEADS * V_HEAD), lambda i: (i, 0))]
        out_shape += [jax.ShapeDtypeStruct((nrows, MLA_HEADS * hw), BF16),
                      jax.ShapeDtypeStruct((nrows, MLA_HEADS * hw), BF16),
                      jax.ShapeDtypeStruct((nrows, MLA_HEADS * V_HEAD), BF16)]
    return pl.pallas_call(
        functools.partial(_mla_prep_body, absorbed=absorbed, scale=scale),
        grid=(nrows // tm,),
        in_specs=in_specs, out_specs=out_specs, out_shape=out_shape,
        compiler_params=_params("arbitrary"),
    )(proj, proj, proj, cos_t, sin_t, q_g.reshape(1, -1), kv_g.reshape(1, -1), wq_r, wk, wv)


ATT_SUB = 256


def _attn_tile(q_sub, k_ref, v_ref, kcols, vcols, s_scr, p_scr, nk):
    sub = ATT_SUB
    nfull = nk - sub
    shift = CHUNK.bit_length() - 1

    def fold(x, op):
        return op(x.reshape(sub // 8, 8, sub), axis=0)

    if nfull:
        s_scr[0:nfull, :] = _dot_t(k_ref[0, 0:nfull, kcols], q_sub)
    krow = lax.broadcasted_iota(I32, (sub, sub), 0) >> shift
    qcol = lax.broadcasted_iota(I32, (sub, sub), 1) >> shift
    s_scr[nfull:nk, :] = jnp.where(krow <= qcol, _dot_t(k_ref[0, nfull:nk, kcols], q_sub), -jnp.inf)

    mrun = None
    for j in range(nk // sub):
        f = fold(s_scr[j * sub:(j + 1) * sub, :], jnp.max)
        mrun = f if mrun is None else jnp.maximum(mrun, f)
    m = jnp.max(mrun, axis=0, keepdims=True)
    lrun = None
    for j in range(nk // sub):
        pt = jnp.exp(s_scr[j * sub:(j + 1) * sub, :] - m)
        f = fold(pt, jnp.sum)
        lrun = f if lrun is None else lrun + f
        p_scr[j * sub:(j + 1) * sub, :] = pt.astype(BF16)
    l = jnp.sum(lrun, axis=0, keepdims=True)
    acc = lax.dot_general(v_ref[0, 0:nk, vcols], p_scr[0:nk, :], (((0,), (0,)), ((), ())),
                          preferred_element_type=F32)
    return (acc / l).T


ATT_HEADS = 4


def _attn_body(q_ref, k_ref, v_ref, o_ref, s_scr, p_scr, *, tq, nq):
    qi = pl.program_id(2)
    sub = ATT_SUB
    hw = q_ref.shape[2] // ATT_HEADS
    for c in range(nq):
        @pl.when(qi == c)
        def _():
            for hh in range(ATT_HEADS):
                kcols = slice(hh * hw, (hh + 1) * hw)
                vcols = slice(hh * V_HEAD, (hh + 1) * V_HEAD)
                for r in range(tq // sub):
                    q_sub = q_ref[0, r * sub:(r + 1) * sub, kcols]
                    out = _attn_tile(q_sub, k_ref, v_ref, kcols, vcols, s_scr, p_scr, c * tq + (r + 1) * sub)
                    o_ref[r * sub:(r + 1) * sub, vcols] = out.astype(o_ref.dtype)


def _attn_prompt(q, k, v, *, nb, seq):
    hw = (QK_NOPE + LANES) * ATT_HEADS
    vw = V_HEAD * ATT_HEADS
    tq = min(1024, seq // 2)
    assert seq % tq == 0 and tq % ATT_SUB == 0 and ATT_SUB % CHUNK == 0 and MLA_HEADS % ATT_HEADS == 0
    nq = seq // tq
    q3 = q.reshape(nb, seq, -1)
    k3 = k.reshape(nb, seq, -1)
    v3 = v.reshape(nb, seq, -1)
    return pl.pallas_call(
        functools.partial(_attn_body, tq=tq, nq=nq),
        grid=(nb, MLA_HEADS // ATT_HEADS, nq),
        in_specs=[pl.BlockSpec((1, tq, hw), lambda b, h, i: (b, i, h)),
                  pl.BlockSpec((1, seq, hw), lambda b, h, i: (b, 0, h)),
                  pl.BlockSpec((1, seq, vw), lambda b, h, i: (b, 0, h))],
        out_specs=pl.BlockSpec((tq, vw), lambda b, h, i: (b * nq + i, h)),
        out_shape=jax.ShapeDtypeStruct((nb * seq, MLA_HEADS * V_HEAD), BF16),
        scratch_shapes=[pltpu.VMEM((seq, ATT_SUB), F32), pltpu.VMEM((seq, ATT_SUB), BF16)],
        compiler_params=_params("arbitrary", "arbitrary", "arbitrary"),
    )(q3, k3, v3)


def _attn_sample_body(ql_ref, qr_ref, pckv_ref, pkr_ref, nckv_ref, nkr_ref, o_ref, *, past, seq):
    nh = ql_ref.shape[0]
    ql = ql_ref[...].reshape(nh * seq, ql_ref.shape[2])
    qr = qr_ref[...].reshape(nh * seq, qr_ref.shape[2])
    pckv = pckv_ref[0].astype(BF16)
    nckv = nckv_ref[...].astype(BF16)
    s_p = _dot_t(ql, pckv) + _dot_t(qr, pkr_ref[0].astype(BF16))
    s_n = _dot_t(ql, nckv) + _dot_t(qr, nkr_ref[:, :QK_ROPE].astype(BF16))
    shift = CHUNK.bit_length() - 1
    q_chunk = (past + lax.broadcasted_iota(I32, (nh * seq, 1), 0) % seq) >> shift
    kp_chunk = lax.broadcasted_iota(I32, (1, past), 1) >> shift
    kn_chunk = (past + lax.broadcasted_iota(I32, (1, seq), 1)) >> shift
    s_p = jnp.where(kp_chunk <= q_chunk, s_p, -jnp.inf)
    s_n = jnp.where(kn_chunk <= q_chunk, s_n, -jnp.inf)
    m = jnp.maximum(jnp.max(s_p, axis=1, keepdims=True), jnp.max(s_n, axis=1, keepdims=True))
    p_p = jnp.exp(s_p - m)
    p_n = jnp.exp(s_n - m)
    l = jnp.sum(p_p, axis=1, keepdims=True) + jnp.sum(p_n, axis=1, keepdims=True)
    o = (_dot(p_p.astype(BF16), pckv) + _dot(p_n.astype(BF16), nckv)) / l
    o_ref[...] = o.reshape(o_ref.shape).astype(o_ref.dtype)


def _attn_sample(qlat, qrope, past_ckv, past_kr, ckv_n, kr_r, *, nb, seq):
    past = past_ckv.shape[1]
    r = past_ckv.shape[2]
    return pl.pallas_call(
        functools.partial(_attn_sample_body, past=past, seq=seq),
        grid=(nb,),
        in_specs=[pl.BlockSpec((MLA_HEADS, seq, r), lambda b: (0, b, 0)),
                  pl.BlockSpec((MLA_HEADS, seq, QK_ROPE), lambda b: (0, b, 0)),
                  pl.BlockSpec((1, past, r), lambda b: (b, 0, 0)),
                  pl.BlockSpec((1, past, QK_ROPE), lambda b: (b, 0, 0)),
                  pl.BlockSpec((seq, r), lambda b: (b, 0)),
                  pl.BlockSpec((seq, LANES), lambda b: (b, 0))],
        out_specs=pl.BlockSpec((MLA_HEADS, seq, r), lambda b: (0, b, 0)),
        out_shape=jax.ShapeDtypeStruct((MLA_HEADS, nb * seq, r), BF16),
        compiler_params=_params("arbitrary"),
    )(qlat, qrope, past_ckv, past_kr, ckv_n, kr_r)


def _uv_body(o_ref, w_ref, y_ref):
    y_ref[...] = _dot(o_ref[0], w_ref[...]).astype(y_ref.dtype)


def _uv_sample(o_lat, wv):
    nh, nrows, r = o_lat.shape
    return pl.pallas_call(
        _uv_body,
        grid=(nh,),
        in_specs=[pl.BlockSpec((1, nrows, r), lambda h: (h, 0, 0)),
                  pl.BlockSpec((r, V_HEAD), lambda h: (0, h))],
        out_specs=pl.BlockSpec((nrows, V_HEAD), lambda h: (0, h)),
        out_shape=jax.ShapeDtypeStruct((nrows, nh * V_HEAD), BF16),
        compiler_params=_params("arbitrary"),
    )(o_lat, wv)


def _merge1_body(ysp_ref, ymp_ref, yss_ref, yms_ref, ws_ref, wm_ref, ga_ref, gb_ref, o_ref, *, npb):
    def run(ys_ref, ym_ref):
        a = _dot(ys_ref[...], ws_ref[...])
        b = _dot(ym_ref[...], wm_ref[...])
        o_ref[...] = (_sigmoid(ga_ref[...]) * a + _sigmoid(gb_ref[...]) * b).astype(o_ref.dtype)

    i = pl.program_id(1)
    pl.when(i < npb)(lambda: run(ysp_ref, ymp_ref))
    pl.when(i >= npb)(lambda: run(yss_ref, yms_ref))


def _merge1(ys_p, ym_p, ys_s, ym_s, w_ssd_out, w_mla_out, proj, offs):
    (tp, inner), ts = ys_p.shape, ys_s.shape[0]
    dm = ym_p.shape[1]
    d = w_ssd_out.shape[1]
    tm = _pick(math.gcd(tp, ts), (512, 256, 128))
    tn = 512
    npb, nsb = tp // tm, ts // tm
    prow = lambda j, i: (jnp.minimum(i, npb - 1), 0)
    srow = lambda j, i: (jnp.maximum(i - npb, 0), 0)
    return pl.pallas_call(
        functools.partial(_merge1_body, npb=npb),
        grid=(d // tn, npb + nsb),
        in_specs=[pl.BlockSpec((tm, inner), prow), pl.BlockSpec((tm, dm), prow),
                  pl.BlockSpec((tm, inner), srow), pl.BlockSpec((tm, dm), srow),
                  pl.BlockSpec((inner, tn), lambda j, i: (0, j)),
                  pl.BlockSpec((dm, tn), lambda j, i: (0, j)),
                  pl.BlockSpec((tm, tn), lambda j, i: (i, offs["ga"] // tn + j)),
                  pl.BlockSpec((tm, tn), lambda j, i: (i, offs["gb"] // tn + j))],
        out_specs=pl.BlockSpec((tm, tn), lambda j, i: (i, j)),
        out_shape=jax.ShapeDtypeStruct((tp + ts, d), BF16),
        compiler_params=_params("arbitrary", "arbitrary"),
    )(ys_p, ym_p, ys_s, ym_s, w_ssd_out, w_mla_out, proj, proj)


def _merge2_body(m_ref, w_ref, xp_ref, xs_ref, g1_ref, sc_ref, sh_ref, ng_ref, wr_ref, x1_ref, h2_ref, lg_ref,
                 *, npb):
    rows, d = xp_ref.shape
    x = jnp.where(pl.program_id(0) < npb, xp_ref[...], xs_ref[...])
    upd = _dot(m_ref[...], w_ref[...]).reshape(rows // MOD_ROWS, MOD_ROWS, d) * g1_ref[...]
    x1 = x + upd.reshape(rows, d)
    x1_ref[...] = x1
    h2 = _modulate(_rms(x1) * ng_ref[...], sc_ref, sh_ref)
    _store_items(h2_ref, _pack_halves(h2))
    lg_ref[...] = _dot3(wr_ref[...], h2, dot=_dot_t)


def _merge2(merged, w_merge, xp, xs, modg, norm_g, w_router_t):
    d = xp.shape[1]
    t = xp.shape[0] + xs.shape[0]
    assert d == 2 * SUBLANES * LANES
    ne = w_router_t.shape[0]
    tm, npb, prow, srow = _two_stream_rows(xp.shape[0], xs.shape[0], (512, 256, 128))
    ng = tm // MOD_ROWS
    return pl.pallas_call(
        functools.partial(_merge2_body, npb=npb),
        grid=(t // tm,),
        in_specs=[pl.BlockSpec((tm, d), lambda i: (i, 0)),
                  pl.BlockSpec((d, d), lambda i: (0, 0)),
                  pl.BlockSpec((tm, d), prow), pl.BlockSpec((tm, d), srow),
                  pl.BlockSpec((ng, 1, d), lambda i: (i, 0, 2)),
                  pl.BlockSpec((ng, 1, d), lambda i: (i, 0, 4)),
                  pl.BlockSpec((ng, 1, d), lambda i: (i, 0, 3)),
                  pl.BlockSpec((1, d), lambda i: (0, 0)),
                  pl.BlockSpec((ne, d), lambda i: (0, 0))],
        out_specs=[pl.BlockSpec((tm, d), lambda i: (i, 0)),
                   pl.BlockSpec((tm * SUBLANES, LANES), lambda i: (i, 0)),
                   pl.BlockSpec((ne, tm), lambda i: (0, i))],
        out_shape=[jax.ShapeDtypeStruct((t, d), F32), jax.ShapeDtypeStruct((t * SUBLANES, LANES), U32),
                   jax.ShapeDtypeStruct((ne, t), F32)],
        compiler_params=_params("arbitrary"),
    )(merged, w_merge, xp, xs, modg, modg, modg, norm_g.reshape(1, d), w_router_t)


def _route_body(lg_ref, eb_ref, eidx_ref, pos_ref, w_ref, cnt_ref, carry):
    ne, tr = lg_ref.shape
    per_group = ne // N_GROUPS

    @pl.when(pl.program_id(0) == 0)
    def _():
        carry[...] = jnp.zeros_like(carry)

    scores = _sigmoid(lg_ref[...])
    choice = scores + eb_ref[...]
    sub = lax.broadcasted_iota(I32, (per_group, tr), 0)
    gscore, blocks = [], []
    for g in range(N_GROUPS):
        blk = choice[g * per_group:(g + 1) * per_group, :]
        m1 = jnp.max(blk, axis=0, keepdims=True)
        first = jnp.min(jnp.where(blk == m1, sub, per_group), axis=0, keepdims=True)
        m2 = jnp.max(jnp.where(sub == first, -jnp.inf, blk), axis=0, keepdims=True)
        gscore.append(m1 + m2)
        blocks.append(blk)
    masked = []
    for g in range(N_GROUPS):
        rank = jnp.zeros((1, tr), I32)
        for g2 in range(N_GROUPS):
            if g2 == g:
                continue
            beats = (gscore[g2] > gscore[g]) | ((gscore[g2] == gscore[g]) & (g2 < g))
            rank = rank + beats.astype(I32)
        masked.append(jnp.where(rank < TOPK_GROUPS, blocks[g], -jnp.inf))
    cm = jnp.concatenate(masked, axis=0)

    eid = lax.broadcasted_iota(I32, (ne, tr), 0)
    rank = jnp.zeros((ne, tr), I32)
    for e2 in range(ne):
        rowv = cm[e2:e2 + 1, :]
        beats = (rowv > cm) | ((rowv == cm) & (eid > e2))
        rank = rank + beats.astype(I32)
    sel = rank < TOP_K
    wsel = jnp.where(sel, scores, 0.0)
    wfull = wsel / jnp.sum(wsel, axis=0, keepdims=True) * ROUTED_SCALE

    r = lax.broadcasted_iota(I32, (tr, tr), 0)
    c = lax.broadcasted_iota(I32, (tr, tr), 1)
    before = jnp.where(r < c, 1.0, 0.0).astype(BF16)
    self = jnp.where(sel, 1.0, 0.0)
    pos = carry[:, 0:1] + _dot(self.astype(BF16), before)
    carry[...] = carry[...] + jnp.sum(self, axis=1, keepdims=True)
    cnt_ref[...] = carry[...]

    eid_f = eid.astype(F32)
    for k in range(TOP_K):
        pick = sel & (rank == k)
        eidx_ref[k:k + 1, :] = jnp.sum(jnp.where(pick, eid_f, 0.0), axis=0, keepdims=True).astype(I32)
        pos_ref[k:k + 1, :] = jnp.sum(jnp.where(pick, pos, 0.0), axis=0, keepdims=True).astype(I32)
        w_ref[k:k + 1, :] = jnp.sum(jnp.where(pick, wfull, 0.0), axis=0, keepdims=True)


def _route(logits_t, e_bias):
    ne, t = logits_t.shape
    tr = _pick(t, (512, 256, 128))
    return pl.pallas_call(
        _route_body,
        grid=(t // tr,),
        in_specs=[pl.BlockSpec((ne, tr), lambda i: (0, i)), pl.BlockSpec((ne, 1), lambda i: (0, 0))],
        out_specs=[pl.BlockSpec((TOP_K, tr), lambda i: (0, i)), pl.BlockSpec((TOP_K, tr), lambda i: (0, i)),
                   pl.BlockSpec((TOP_K, tr), lambda i: (0, i)), pl.BlockSpec((ne, LANES), lambda i: (0, 0))],
        out_shape=[jax.ShapeDtypeStruct((TOP_K, t), I32), jax.ShapeDtypeStruct((TOP_K, t), I32),
                   jax.ShapeDtypeStruct((TOP_K, t), F32), jax.ShapeDtypeStruct((ne, LANES), F32)],
        scratch_shapes=[pltpu.VMEM((ne, LANES), F32)],
        compiler_params=_params("arbitrary"),
    )(logits_t, e_bias.reshape(ne, 1))


def _row_copy(src, s, dst, d, sem):
    rows = lambda i: pl.ds(pl.multiple_of(i * SUBLANES, SUBLANES), SUBLANES)
    return pltpu.make_async_copy(src.at[rows(s), :], dst.at[rows(d), :], sem)


def _load_items(ref, start, n):
    return jnp.concatenate([ref[pl.ds(start * SUBLANES + c, n, stride=SUBLANES), :] for c in range(SUBLANES)],
                           axis=1)


def _store_items(ref, val):
    n = val.shape[0]
    for c in range(SUBLANES):
        ref[pl.ds(c, n, stride=SUBLANES), :] = val[:, c * LANES:(c + 1) * LANES]


def _drain(wait_one, n, group=64):
    assert n % group == 0

    def body(j, c):
        for _ in range(group):
            wait_one()
        return c

    lax.fori_loop(0, n // group, body, 0)


def _dispatch_body(fill_lo_ref, fill_hi_ref, h_ref, dest_ref, hs_ref, dest_s, zrow, sem, dsem, *, tb):
    i = pl.program_id(0)
    cp = pltpu.make_async_copy(dest_ref.at[i], dest_s, dsem)
    cp.start()

    @pl.when(i == 0)
    def _():
        zrow[...] = jnp.zeros_like(zrow)

        def per_expert(fn):
            def body(e, _):
                lax.fori_loop(fill_lo_ref[e], fill_hi_ref[e], lambda s, c: (fn(s), c)[1], 0)
                return 0
            lax.fori_loop(0, fill_lo_ref.shape[0], body, 0)

        per_expert(lambda s: _row_copy(zrow, 0, hs_ref, s, sem).start())
        per_expert(lambda s: _row_copy(zrow, 0, hs_ref, s, sem).wait())

    cp.wait()

    def issue(t, c):
        for k in range(TOP_K):
            _row_copy(h_ref, t, hs_ref, dest_s[t * TOP_K + k], sem).start(priority=k % 2)
        return c

    lax.fori_loop(0, tb, issue, 0, unroll=4)
    _drain(lambda: _row_copy(h_ref, 0, hs_ref, 0, sem).wait(), TOP_K * tb)


def _dispatch(h2, dest_blk, fill_lo, fill_hi, n_slots):
    nblk, n = dest_blk.shape
    tb = n // TOP_K
    return pl.pallas_call(
        functools.partial(_dispatch_body, tb=tb),
        grid_spec=pltpu.PrefetchScalarGridSpec(
            num_scalar_prefetch=2,
            grid=(nblk,),
            in_specs=[pl.BlockSpec((tb * SUBLANES, LANES), lambda i, lo, hi: (i, 0)),
                      pl.BlockSpec(memory_space=pl.ANY)],
            out_specs=pl.BlockSpec(memory_space=pl.ANY),
            scratch_shapes=[pltpu.SMEM((n,), I32), pltpu.VMEM((SUBLANES, LANES), h2.dtype),
                            pltpu.SemaphoreType.DMA, pltpu.SemaphoreType.DMA]),
        out_shape=jax.ShapeDtypeStruct((n_slots * SUBLANES, LANES), h2.dtype),
        compiler_params=_params("arbitrary"),
    )(fill_lo, fill_hi, h2, dest_blk)


def _experts_body(be_ref, nu_ref, x_ref, wg_ref, wu_ref, wd_ref, o_ref):
    i = pl.program_id(0)

    @pl.when(i < nu_ref[0])
    def _():
        lo, hi = _unpack_halves(_load_items(x_ref, 0, x_ref.shape[0] // SUBLANES))
        x = jnp.concatenate([lo.astype(BF16), hi.astype(BF16)], axis=1)
        act = (_silu(_dot(x, wg_ref[0])) * _dot(x, wu_ref[0])).astype(BF16)
        _store_items(o_ref, _pack_halves(_dot(act, wd_ref[0])))


def _experts(hs, block_e, n_used, wg, wu, wd):
    n_slots = hs.shape[0] // SUBLANES
    ne, d, hid = wg.shape
    assert d == 2 * SUBLANES * LANES
    bm = MOE_ROWS
    nblocks = n_slots // bm
    blk = lambda i, be, nu: (jnp.minimum(i, nu[0] - 1), 0)
    return pl.pallas_call(
        _experts_body,
        grid_spec=pltpu.PrefetchScalarGridSpec(
            num_scalar_prefetch=2,
            grid=(nblocks,),
            in_specs=[pl.BlockSpec((bm * SUBLANES, LANES), blk),
                      pl.BlockSpec((1, d, hid), lambda i, be, nu: (be[i], 0, 0)),
                      pl.BlockSpec((1, d, hid), lambda i, be, nu: (be[i], 0, 0)),
                      pl.BlockSpec((1, hid, d), lambda i, be, nu: (be[i], 0, 0))],
            out_specs=pl.BlockSpec((bm * SUBLANES, LANES), blk)),
        out_shape=jax.ShapeDtypeStruct(hs.shape, U32),
        compiler_params=_params("arbitrary"),
    )(block_e, n_used, hs, wg, wu, wd)


def _combine_body(ys_ref, dest_ref, w_ref, h_ref, x1_ref, g2_ref, wgu_ref, wd_ref, fg_ref, op_ref, os_ref,
                  dest_s, gbuf, sem, dsem, *, tb, npb):
    i = pl.program_id(0)
    cp = pltpu.make_async_copy(dest_ref.at[i], dest_s, dsem)
    cp.start()
    cp.wait()

    def issue(t, c):
        for k in range(TOP_K):
            _row_copy(ys_ref, dest_s[t * TOP_K + k], gbuf, k * tb + t, sem).start(priority=k % 2)
        return c

    lax.fori_loop(0, tb, issue, 0, unroll=4)

    hid = wd_ref.shape[0]
    h_lo, h_hi = _unpack_halves(_load_items(h_ref, 0, tb))
    gu = _dot(jnp.concatenate([h_lo.astype(BF16), h_hi.astype(BF16)], axis=1), wgu_ref[...])
    moe = _dot((_silu(gu[:, :hid]) * gu[:, hid:]).astype(BF16), wd_ref[...])

    _drain(lambda: _row_copy(ys_ref, 0, gbuf, 0, sem).wait(), TOP_K * tb)
    w = w_ref[...]
    dh = SUBLANES * LANES
    m_lo, m_hi = moe[:, :dh], moe[:, dh:]
    for k in range(TOP_K):
        lo, hi = _unpack_halves(_load_items(gbuf, k * tb, tb))
        m_lo = m_lo + lo * w[:, k:k + 1]
        m_hi = m_hi + hi * w[:, k:k + 1]
    moe = jnp.concatenate([m_lo, m_hi], axis=1)
    rows, d = moe.shape
    upd = moe.reshape(rows // MOD_ROWS, MOD_ROWS, d) * g2_ref[...]
    x2 = x1_ref[...] + upd.reshape(rows, d)
    y = _rms(x2) * fg_ref[...]

    @pl.when(i < npb)
    def _():
        op_ref[...] = y

    @pl.when(i >= npb)
    def _():
        os_ref[...] = y


def _combine(ys, dest_blk, w_tok, h2, x1, modg, wsh_gu, wsh_d, final_g, *, tp):
    t, d = x1.shape
    nblk, n = dest_blk.shape
    tb = n // TOP_K
    assert tp % tb == 0 and (t - tp) % tb == 0
    npb = tp // tb
    ng = tb // MOD_ROWS
    const = lambda i: (0, 0)
    return pl.pallas_call(
        functools.partial(_combine_body, tb=tb, npb=npb),
        grid=(nblk,),
        in_specs=[pl.BlockSpec(memory_space=pl.ANY),
                  pl.BlockSpec(memory_space=pl.ANY),
                  pl.BlockSpec((tb, TOP_K), lambda i: (i, 0)),
                  pl.BlockSpec((tb * SUBLANES, LANES), lambda i: (i, 0)),
                  pl.BlockSpec((tb, d), lambda i: (i, 0)),
                  pl.BlockSpec((ng, 1, d), lambda i: (i, 0, 5)),
                  pl.BlockSpec(wsh_gu.shape, const), pl.BlockSpec(wsh_d.shape, const),
                  pl.BlockSpec((1, d), const)],
        out_specs=[pl.BlockSpec((tb, d), lambda i: (jnp.minimum(i, npb - 1), 0)),
                   pl.BlockSpec((tb, d), lambda i: (jnp.maximum(i - npb, 0), 0))],
        out_shape=[jax.ShapeDtypeStruct((tp, d), F32), jax.ShapeDtypeStruct((t - tp, d), F32)],
        scratch_shapes=[pltpu.SMEM((n,), I32), pltpu.VMEM((n * SUBLANES, LANES), U32),
                        pltpu.SemaphoreType.DMA, pltpu.SemaphoreType.DMA],
        compiler_params=_params("arbitrary"),
    )(ys, dest_blk, w_tok, h2, x1, modg, wsh_gu, wsh_d, final_g.reshape(1, d))


def _rope_tables(pos):
    half = QK_ROPE // 2
    freqs = ROPE_THETA ** (-jnp.arange(half, dtype=F32) / half)
    ang = pos.astype(F32)[:, None] * freqs[None, :]
    cos, sin = jnp.cos(ang), jnp.sin(ang)
    return (jnp.concatenate([cos, cos, cos, cos], axis=1),
            jnp.concatenate([-sin, sin, -sin, sin], axis=1))


def _blocked(a, tb):
    k, t = a.shape
    return a.T.reshape(t // tb, tb * k)


def kernel(x_prompt, x_sample, c_prompt, c_sample, cache_conv, state_ssm, cache_ckv, cache_kr, w_ada, b_ada,
           norm1_g, norm2_g, w_in, conv_w, conv_b, dt_bias, a_log, d_skip, ssd_norm_g, w_ssd_out, q_norm_g,
           w_uq, kv_norm_g, w_uk, w_uv, w_mla_out, w_merge_out, w_router, e_bias, w_exp_gate, w_exp_up,
           w_exp_down, w_sh_gate, w_sh_up, w_sh_down, final_norm_g):
    depth = w_in.shape[0]
    assert depth == 1
    bp, seq, d = x_prompt.shape
    bs, lseq, _ = x_sample.shape
    assert lseq == MOD_ROWS and seq % MOD_ROWS == 0
    tp, ts = bp * seq, bs * lseq
    t_all = tp + ts
    nheads = dt_bias.shape[1]
    inner = nheads * SSD_HEAD_DIM
    cdim = conv_w.shape[2]
    gn = SSD_GROUPS * SSD_STATE
    qlora, kvlora = q_norm_g.shape[1], kv_norm_g.shape[1]
    ne = w_router.shape[2]
    assert nheads == 64 and 2 * nheads == LANES

    w = w_in[0]
    o_xbc, o_dt = inner, inner + cdim
    o_cq = o_dt + nheads
    o_ckv = o_cq + qlora
    o_kr = o_ckv + kvlora
    o_gate = o_kr + QK_ROPE
    perm = np.concatenate([np.arange(0, nheads, 2), np.arange(1, nheads, 2)])
    zc = lambda n: jnp.zeros((d, n), w.dtype)
    cols = [w[:, :inner], w[:, o_gate:], w[:, o_xbc:o_dt], w[:, o_cq:o_ckv], w[:, o_ckv:o_kr],
            w[:, o_dt:o_cq][:, perm], zc(LANES - nheads), w[:, o_kr:o_gate], zc(LANES - QK_ROPE)]
    used = inner + 2 * d + cdim + qlora + kvlora + 2 * LANES
    total = -(-used // 512) * 512
    cols.append(zc(total - used))
    w_r = jnp.concatenate(cols, axis=1).astype(BF16)
    offs = {"z": 0, "ga": inner, "gb": inner + d, "x": inner + 2 * d, "bc": 2 * inner + 2 * d}
    offs["cq"] = offs["bc"] + 2 * gn
    offs["ckv"] = offs["cq"] + qlora
    offs["dt"] = offs["ckv"] + kvlora
    offs["kr"] = offs["dt"] + LANES

    hw = QK_NOPE + LANES
    wq_r = jnp.pad(w_uq[0], ((0, 0), (0, 0), (0, hw - QK_NOPE - QK_ROPE))).reshape(qlora, MLA_HEADS * hw).astype(BF16)
    wk = w_uk[0].reshape(kvlora, MLA_HEADS * QK_NOPE).astype(BF16)
    wv = w_uv[0].reshape(kvlora, MLA_HEADS * V_HEAD).astype(BF16)
    pad_l = lambda v: jnp.pad(v[perm], (0, LANES - nheads)).reshape(1, LANES)
    dtb = pad_l(dt_bias[0])
    a_neg = pad_l(-jnp.exp(a_log[0]))
    dskip = jnp.repeat(d_skip[0], SSD_HEAD_DIM).reshape(1, inner)

    c_all = jnp.concatenate([c_prompt, c_sample], axis=0)
    mod = _ada(c_all, w_ada[0], b_ada[0])
    grp = np.concatenate([np.repeat(np.arange(bp), seq // MOD_ROWS), bp + np.arange(bs)])
    modg = mod[grp].reshape(t_all // MOD_ROWS, 1, 6 * d)

    xp, xs = x_prompt.reshape(tp, d), x_sample.reshape(ts, d)
    proj = _inproj(xp, xs, modg, norm1_g[0], w_r, d)

    ssd_args = (conv_w[0], conv_b[0].reshape(1, cdim), dtb, a_neg, dskip, ssd_norm_g[0].reshape(1, inner))
    ys_p, conv_p, ssm_p = _ssd(proj, jnp.zeros((bp, CONV_W - 1, cdim), F32),
                               jnp.zeros((1, bp) + state_ssm.shape[2:], F32), *ssd_args,
                               nb=bp, seq=seq, row0=0, offs=offs, inner=inner)
    ys_s, conv_s, ssm_s = _ssd(proj, cache_conv[0], state_ssm,
                               *ssd_args, nb=bs, seq=lseq, row0=tp, offs=offs, inner=inner)

    past = cache_ckv.shape[2]
    cos_p, sin_p = _rope_tables(jnp.arange(seq))
    cos_s, sin_s = _rope_tables(past + jnp.arange(lseq))
    cos_t = jnp.concatenate([jnp.tile(cos_p, (bp, 1)), jnp.tile(cos_s, (bs, 1))], axis=0)
    sin_t = jnp.concatenate([jnp.tile(sin_p, (bp, 1)), jnp.tile(sin_s, (bs, 1))], axis=0)
    prep = functools.partial(_mla_prep, proj, cos_t, sin_t, q_norm_g[0], kv_norm_g[0], wq_r, wk, wv, offs=offs)
    ckv_p, kr_p, q_p, k_p, v_p = prep(row0=0, nrows=tp, absorbed=False)
    ckv_s, kr_s, qlat, qrope = prep(row0=tp, nrows=ts, absorbed=True)
    ym_p = _attn_prompt(q_p, k_p, v_p, nb=bp, seq=seq)
    o_lat = _attn_sample(qlat, qrope, cache_ckv[0], cache_kr[0], ckv_s, kr_s, nb=bs, seq=lseq)
    ym_s = _uv_sample(o_lat, wv)

    merged = _merge1(ys_p, ym_p, ys_s, ym_s, w_ssd_out[0].astype(BF16), w_mla_out[0].astype(BF16), proj, offs)
    x1, h2, logits_t = _merge2(merged, w_merge_out[0].astype(BF16), xp, xs, modg, norm2_g[0], w_router[0].T)

    eidx, pos, w_sel, cnt = _route(logits_t, e_bias[0])
    bm = MOE_ROWS
    counts = cnt[:, 0].astype(I32)
    padded = (counts + bm - 1) // bm * bm
    pad_end = jnp.cumsum(padded)
    pad_start = pad_end - padded
    nblocks = -(-(t_all * TOP_K) // bm) + ne
    onehot = eidx[:, :, None] == jnp.arange(ne, dtype=I32)
    dest = jnp.sum(jnp.where(onehot, pad_start, 0), axis=-1) + pos
    blk_start = jnp.arange(nblocks, dtype=I32) * bm
    block_e = jnp.minimum(jnp.sum(pad_end[None, :] <= blk_start[:, None], axis=1), ne - 1).astype(I32)
    n_used = (pad_end[-1:] // bm).astype(I32)

    tb_d = _pick(t_all, (1024, 512, 256, 128))
    hs = _dispatch(h2, _blocked(dest, tb_d), (pad_start + counts).astype(I32), pad_end.astype(I32), nblocks * bm)
    ys = _experts(hs, block_e, n_used, w_exp_gate[0].astype(BF16), w_exp_up[0].astype(BF16),
                  w_exp_down[0].astype(BF16))
    wsh_gu = jnp.concatenate([w_sh_gate[0], w_sh_up[0]], axis=1).astype(BF16)
    tb_c = _pick(math.gcd(tp, ts), (256, 128))
    y_p, y_s = _combine(ys, _blocked(dest, tb_c), w_sel.T, h2, x1, modg, wsh_gu, w_sh_down[0].astype(BF16),
                        final_norm_g, tp=tp)

    r5 = lambda a, n, l: a.reshape(1, n, l, a.shape[-1])
    return (y_p.reshape(bp, seq, d), y_s.reshape(bs, lseq, d),
            conv_p[None], ssm_p,
            r5(ckv_p, bp, seq), r5(kr_p[:, :QK_ROPE], bp, seq),
            conv_s[None], ssm_s,
            r5(ckv_s, bs, lseq), r5(kr_s[:, :QK_ROPE], bs, lseq))
```

```python
import functools
import math

import numpy as np
import jax
import jax.numpy as jnp
from jax import lax
from jax.experimental import pallas as pl
from jax.experimental.pallas import tpu as pltpu

F32 = jnp.float32
BF16 = jnp.bfloat16
I32 = jnp.int32
U32 = jnp.uint32

EPS = 1e-6
CHUNK = 64
SSD_HEAD_DIM = 64
SSD_GROUPS = 8
SSD_STATE = 128
CONV_W = 4
MLA_HEADS = 16
QK_NOPE = 128
QK_ROPE = 64
V_HEAD = 128
ROPE_THETA = 10000.0
TOP_K = 8
N_GROUPS = 8
TOPK_GROUPS = 4
ROUTED_SCALE = 2.5

LANES = 128
SUBLANES = 8
MOD_ROWS = 32
MOE_ROWS = 512
VMEM_LIMIT = 56 * 1024 * 1024


def _params(*sem):
    return pltpu.CompilerParams(dimension_semantics=sem, vmem_limit_bytes=VMEM_LIMIT)


def _pick(n, cands):
    for c in cands:
        if n % c == 0:
            return c
    raise ValueError(f"no tile in {cands} divides {n}")


def _sigmoid(x):
    return 1.0 / (1.0 + jnp.exp(-x))


def _silu(x):
    return x * _sigmoid(x)


def _dot(a, b):
    return jnp.dot(a, b, preferred_element_type=F32)


def _dot_t(a, b):
    return lax.dot_general(a, b, (((1,), (1,)), ((), ())), preferred_element_type=F32)


def _split(x, n):
    parts = []
    for _ in range(n - 1):
        p = x.astype(BF16)
        parts.append(p)
        x = x - p.astype(F32)
    parts.append(x.astype(BF16))
    return parts


def _dot_exact_rhs(a, b_bf16, n=3):
    acc = None
    for p in _split(a, n):
        t = _dot(p, b_bf16)
        acc = t if acc is None else acc + t
    return acc


def _dot3(a, b, dot=_dot):
    ah, al = _split(a, 2)
    bh, bl = _split(b, 2)
    return dot(ah, bh) + (dot(ah, bl) + dot(al, bh))


def _pack_halves(x):
    h = x.shape[1] // 2
    bits = lambda v: lax.bitcast_convert_type(v.astype(BF16).astype(F32), U32)
    return (bits(x[:, :h]) >> 16) | bits(x[:, h:])


def _unpack_halves(p):
    lo = lax.bitcast_convert_type(p << 16, F32)
    hi = lax.bitcast_convert_type(p & jnp.uint32(0xFFFF0000), F32)
    return lo, hi


def _rms(x):
    return x * lax.rsqrt(jnp.mean(x * x, axis=-1, keepdims=True) + EPS)


def _modulate(y, sc_ref, sh_ref):
    rows, d = y.shape
    y3 = y.reshape(rows // MOD_ROWS, MOD_ROWS, d)
    return (y3 * (1.0 + sc_ref[...]) + sh_ref[...]).reshape(rows, d)


def _ada_body(c_ref, w_ref, b_ref, o_ref):
    o_ref[...] = _dot3(_silu(c_ref[...]), w_ref[...]) + b_ref[...]


def _ada(c_all, w_ada, b_ada):
    nb, d = c_all.shape
    n = w_ada.shape[1]
    tn = _pick(n, (1024, 512, 256, 128))
    return pl.pallas_call(
        _ada_body,
        grid=(n // tn,),
        in_specs=[pl.BlockSpec((nb, d), lambda j: (0, 0)),
                  pl.BlockSpec((d, tn), lambda j: (0, j)),
                  pl.BlockSpec((1, tn), lambda j: (0, j))],
        out_specs=pl.BlockSpec((nb, tn), lambda j: (0, j)),
        out_shape=jax.ShapeDtypeStruct((nb, n), F32),
        compiler_params=_params("arbitrary"),
    )(c_all, w_ada, b_ada.reshape(1, n))


def _two_stream_rows(tp, ts, cands):
    tm = _pick(math.gcd(tp, ts), cands)
    npb = tp // tm
    return tm, npb, (lambda i, *_: (jnp.minimum(i, npb - 1), 0)), (lambda i, *_: (jnp.maximum(i - npb, 0), 0))


def _inproj_body(xp_ref, xs_ref, sc_ref, sh_ref, g_ref, w_ref, o_ref, h_scr, *, npb):
    @pl.when(pl.program_id(1) == 0)
    def _():
        x = jnp.where(pl.program_id(0) < npb, xp_ref[...], xs_ref[...])
        h = _modulate(_rms(x) * g_ref[...], sc_ref, sh_ref)
        h_scr[...] = h.astype(BF16)

    o_ref[...] = _dot(h_scr[...], w_ref[...])


def _inproj(xp, xs, modg, norm_g, w_r, d):
    t = xp.shape[0] + xs.shape[0]
    n = w_r.shape[1]
    tm, npb, prow, srow = _two_stream_rows(xp.shape[0], xs.shape[0], (1024, 512, 256, 128))
    tn = 512
    ng = tm // MOD_ROWS
    return pl.pallas_call(
        functools.partial(_inproj_body, npb=npb),
        grid=(t // tm, n // tn),
        in_specs=[pl.BlockSpec((tm, d), prow), pl.BlockSpec((tm, d), srow),
                  pl.BlockSpec((ng, 1, d), lambda i, j: (i, 0, 1)),
                  pl.BlockSpec((ng, 1, d), lambda i, j: (i, 0, 0)),
                  pl.BlockSpec((1, d), lambda i, j: (0, 0)),
                  pl.BlockSpec((d, tn), lambda i, j: (0, j))],
        out_specs=pl.BlockSpec((tm, tn), lambda i, j: (i, j)),
        out_shape=jax.ShapeDtypeStruct((t, n), F32),
        scratch_shapes=[pltpu.VMEM((tm, d), BF16)],
        compiler_params=_params("arbitrary", "arbitrary"),
    )(xp, xs, modg, modg, norm_g.reshape(1, d), w_r)


def _ssd_body(z_ref, x_ref, bc_ref, dt_ref, conv0_ref, h0_ref, cw_ref, cb_ref, dtb_ref, a_ref,
              dskip_ref, ng_ref, y_ref, nconv_ref, nssm_ref,
              carry_x, carry_bc, xp_x, xp_bc, state, y_scr, *, lb, inner):
    q = CHUNK
    c = pl.program_id(1)
    nc = pl.num_programs(1)
    gn = SSD_GROUPS * SSD_STATE

    @pl.when(c == 0)
    def _():
        carry_x[...] = jnp.zeros_like(carry_x)
        carry_bc[...] = jnp.zeros_like(carry_bc)
        carry_x[8 - (CONV_W - 1):8, :] = conv0_ref[0, :, :inner]
        carry_bc[8 - (CONV_W - 1):8, :] = conv0_ref[0, :, inner:]
        state[...] = h0_ref[0, 0].reshape(state.shape)

    def pad_rows(v):
        if lb == q:
            return v
        return jnp.concatenate([v, jnp.zeros((q - lb, v.shape[1]), v.dtype)], axis=0)

    def conv(raw, carry, xp, w_lo, w_hi):
        xp[0:8, :] = carry[...]
        xp[8:8 + q, :] = raw
        acc = cb_ref[:, w_lo:w_hi]
        for j in range(CONV_W):
            acc = acc + xp[8 - j:8 - j + q, :] * cw_ref[CONV_W - 1 - j:CONV_W - j, w_lo:w_hi]
        return _silu(acc)

    x_raw = pad_rows(x_ref[...])
    bc_raw = pad_rows(bc_ref[...])
    xc = conv(x_raw, carry_x, xp_x, 0, inner)
    bcc = conv(bc_raw, carry_bc, xp_bc, inner, inner + 2 * gn)
    if lb == q:
        carry_x[...] = x_raw[q - 8:q, :]
        carry_bc[...] = bc_raw[q - 8:q, :]

    @pl.when(c == nc - 1)
    def _():
        nconv_ref[0, :, :inner] = x_raw[lb - (CONV_W - 1):lb, :]
        nconv_ref[0, :, inner:] = bc_raw[lb - (CONV_W - 1):lb, :]

    dtr = dt_ref[...] + dtb_ref[...]
    dtv = jnp.maximum(dtr, 0.0) + jnp.log1p(jnp.exp(-jnp.abs(dtr)))
    dtv = pad_rows(dtv)
    la = dtv * a_ref[...]

    row = lax.broadcasted_iota(I32, (q, q), 0)
    col = lax.broadcasted_iota(I32, (q, q), 1)
    tri = jnp.where(row >= col, 1.0, 0.0).astype(BF16)
    parts = _split(la, 3)
    a_cum = _dot(tri, parts[0]) + _dot(tri, parts[1]) + _dot(tri, parts[2])
    a_last = a_cum[q - 1:q, :]

    r2 = lax.broadcasted_iota(I32, (q, 2 * q), 0)
    c2 = lax.broadcasted_iota(I32, (q, 2 * q), 1)
    u_e = jnp.where((c2 < q) & (r2 <= c2), 1.0, 0.0).astype(BF16)
    u_o = jnp.where((c2 >= q) & (r2 <= c2 - q), 1.0, 0.0).astype(BF16)
    i_e = jnp.where(r2 == c2, 1.0, 0.0).astype(BF16)
    i_o = jnp.where(r2 == c2 - q, 1.0, 0.0).astype(BF16)
    la_t = la.T
    dt_t = dtv.T
    hp = la_t.shape[0] // 4
    acol = _dot_exact_rhs(la_t[0:hp], u_e) + _dot_exact_rhs(la_t[hp:2 * hp], u_o)
    dtrow = _dot_exact_rhs(dt_t[0:hp], i_e) + _dot_exact_rhs(dt_t[hp:2 * hp], i_o)
    w_t = (dtv * jnp.exp(a_last - a_cum)).T
    dec_b = jnp.broadcast_to(jnp.exp(jnp.sum(la_t, axis=1, keepdims=True)), (la_t.shape[0], SSD_STATE))

    lane = lax.broadcasted_iota(I32, (q, 2 * q), 1)
    causal2 = lax.broadcasted_iota(I32, (q, 2 * q), 0) >= jnp.where(lane < q, lane, lane - q)
    first_half = lane < q
    rr = lax.broadcasted_iota(I32, (2 * SSD_HEAD_DIM, 2 * SSD_HEAD_DIM), 0)
    cc = lax.broadcasted_iota(I32, (2 * SSD_HEAD_DIM, 2 * SSD_HEAD_DIM), 1)
    bd_mask = (rr < SSD_HEAD_DIM) == (cc < SSD_HEAD_DIM)
    top_rows_q = lax.broadcasted_iota(I32, (2 * SSD_HEAD_DIM, q), 0) < SSD_HEAD_DIM
    top_rows_n = lax.broadcasted_iota(I32, (2 * SSD_HEAD_DIM, SSD_STATE), 0) < SSD_HEAD_DIM

    pairs_per_group = (inner // SSD_HEAD_DIM) // SSD_GROUPS // 2
    for g in range(SSD_GROUPS):
        b_g = bcc[:, g * SSD_STATE:(g + 1) * SSD_STATE].astype(BF16)
        c_g = bcc[:, gn + g * SSD_STATE:gn + (g + 1) * SSD_STATE].astype(BF16)
        cb2 = _dot_t(c_g, jnp.concatenate([b_g, b_g], axis=0))
        for kk in range(pairs_per_group):
            k = g * pairs_per_group + kk
            lo, hi = k * 2 * SSD_HEAD_DIM, (k + 1) * 2 * SSD_HEAD_DIM
            xp = xc[:, lo:hi]
            arow = jnp.where(first_half, a_cum[:, k:k + 1], a_cum[:, hp + k:hp + k + 1])
            seg = jnp.where(causal2, arow - acol[k:k + 1, :], -jnp.inf)
            m = (jnp.exp(seg) * cb2 * dtrow[k:k + 1, :]).astype(BF16)
            xbd = jnp.where(bd_mask, jnp.concatenate([xp, xp], axis=0), 0.0).astype(BF16)
            y_diag = _dot(m, xbd)
            s_k = state[lo:hi, :]
            y_off = _dot_t(c_g, s_k.astype(BF16)) * jnp.exp(arow)
            y_scr[:, lo:hi] = y_diag + y_off + xp * dskip_ref[:, lo:hi]
            w2 = jnp.where(top_rows_q, w_t[k:k + 1, :], w_t[hp + k:hp + k + 1, :])
            contrib = _dot((xp.T * w2).astype(BF16), b_g)
            d_k = jnp.where(top_rows_n, dec_b[k:k + 1, :], dec_b[hp + k:hp + k + 1, :])
            state[lo:hi, :] = s_k * d_k + contrib

    @pl.when(c == nc - 1)
    def _():
        nssm_ref[0, 0] = state[...].reshape(nssm_ref.shape[2:])

    y = y_scr[0:lb, :] * _silu(z_ref[...])
    y_ref[...] = (_rms(y) * ng_ref[...]).astype(y_ref.dtype)


def _ssd(proj, conv0, h0, conv_w, conv_b, dtb, a_neg, dskip, norm_g, *, nb, seq, row0, offs, inner):
    q = CHUNK
    lb = min(seq, q)
    assert seq % lb == 0 and row0 % lb == 0 and lb % 16 == 0
    nc = seq // lb
    rb0 = row0 // lb
    cdim = conv_w.shape[1]
    hp_rows = inner
    state_shape = (inner // SSD_HEAD_DIM, SSD_HEAD_DIM, SSD_STATE)
    assert h0.shape == (1, nb) + state_shape
    kern = functools.partial(_ssd_body, lb=lb, inner=inner)
    rows = lambda b, c: rb0 + b * nc + c
    in_specs = [
        pl.BlockSpec((lb, inner), lambda b, c: (rows(b, c), offs["z"] // inner)),
        pl.BlockSpec((lb, inner), lambda b, c: (rows(b, c), offs["x"] // inner)),
        pl.BlockSpec((lb, cdim - inner), lambda b, c: (rows(b, c), offs["bc"] // (cdim - inner))),
        pl.BlockSpec((lb, LANES), lambda b, c: (rows(b, c), offs["dt"] // LANES)),
        pl.BlockSpec((1, CONV_W - 1, cdim), lambda b, c: (b, 0, 0)),
        pl.BlockSpec((1, 1) + state_shape, lambda b, c: (0, b, 0, 0, 0)),
        pl.BlockSpec((CONV_W, cdim), lambda b, c: (0, 0)),
        pl.BlockSpec((1, cdim), lambda b, c: (0, 0)),
        pl.BlockSpec((1, LANES), lambda b, c: (0, 0)),
        pl.BlockSpec((1, LANES), lambda b, c: (0, 0)),
        pl.BlockSpec((1, inner), lambda b, c: (0, 0)),
        pl.BlockSpec((1, inner), lambda b, c: (0, 0)),
    ]
    args = [proj, proj, proj, proj, conv0, h0, conv_w, conv_b, dtb, a_neg, dskip, norm_g]
    return pl.pallas_call(
        kern,
        grid=(nb, nc),
        in_specs=in_specs,
        out_specs=[pl.BlockSpec((lb, inner), lambda b, c: (b * nc + c, 0)),
                   pl.BlockSpec((1, CONV_W - 1, cdim), lambda b, c: (b, 0, 0)),
                   pl.BlockSpec((1, 1) + state_shape, lambda b, c: (0, b, 0, 0, 0))],
        out_shape=[jax.ShapeDtypeStruct((nb * seq, inner), BF16),
                   jax.ShapeDtypeStruct((nb, CONV_W - 1, cdim), F32),
                   jax.ShapeDtypeStruct((1, nb) + state_shape, F32)],
        scratch_shapes=[pltpu.VMEM((8, inner), F32), pltpu.VMEM((8, cdim - inner), F32),
                        pltpu.VMEM((8 + q, inner), F32), pltpu.VMEM((8 + q, cdim - inner), F32),
                        pltpu.VMEM((hp_rows, SSD_STATE), F32), pltpu.VMEM((q, inner), F32)],
        compiler_params=_params("arbitrary", "arbitrary"),
    )(*args)


def _rope128(x, cos, sin):
    lane = lax.broadcasted_iota(I32, x.shape, 1)
    half = QK_ROPE // 2
    swapped = jnp.where((lane % QK_ROPE) < half, pltpu.roll(x, LANES - half, 1), pltpu.roll(x, half, 1))
    return x * cos + swapped * sin


def _mla_prep_body(cq_ref, ckv_ref, kr_ref, cos_ref, sin_ref, qg_ref, kvg_ref, wq_ref, wk_ref, wv_ref,
                   *out_refs, absorbed, scale):
    cos, sin = cos_ref[...], sin_ref[...]
    ckv_n = _rms(ckv_ref[...]) * kvg_ref[...]
    kr_r = _rope128(kr_ref[...], cos, sin)
    qn = (_rms(cq_ref[...]) * qg_ref[...]).astype(BF16)
    hw = QK_NOPE + LANES
    if absorbed:
        ckvn_ref, krr_ref, qlat_ref, qrope_ref = out_refs
    else:
        ckvn_ref, krr_ref, q_ref, k_ref, v_ref = out_refs
        ckv_b = ckv_n.astype(BF16)
        v_ref[...] = _dot(ckv_b, wv_ref[...]).astype(BF16)
        kr_b = kr_r.astype(BF16)
    ckvn_ref[...] = ckv_n
    krr_ref[...] = kr_r
    for h in range(MLA_HEADS):
        qh = _dot(qn, wq_ref[:, h * hw:(h + 1) * hw]) * scale
        q_nope = qh[:, :QK_NOPE]
        q_rope = _rope128(qh[:, QK_NOPE:], cos, sin)
        if absorbed:
            qlat_ref[h] = _dot_t(q_nope.astype(BF16), wk_ref[:, h * QK_NOPE:(h + 1) * QK_NOPE]).astype(BF16)
            qrope_ref[h] = q_rope[:, :QK_ROPE].astype(BF16)
        else:
            q_ref[:, h * hw:h * hw + QK_NOPE] = q_nope.astype(BF16)
            q_ref[:, h * hw + QK_NOPE:(h + 1) * hw] = q_rope.astype(BF16)
            k_ref[:, h * hw:h * hw + QK_NOPE] = _dot(ckv_b, wk_ref[:, h * QK_NOPE:(h + 1) * QK_NOPE]).astype(BF16)
            k_ref[:, h * hw + QK_NOPE:(h + 1) * hw] = kr_b


def _mla_prep(proj, cos_t, sin_t, q_g, kv_g, wq_r, wk, wv, *, row0, nrows, offs, absorbed):
    qlora, kvlora = q_g.shape[0], kv_g.shape[0]
    tm = _pick(nrows, (512, 256, 128))
    assert row0 % tm == 0
    rb0 = row0 // tm
    hw = QK_NOPE + LANES
    scale = 1.0 / math.sqrt(QK_NOPE + QK_ROPE)
    const = lambda i: (0, 0)
    in_specs = [pl.BlockSpec((tm, qlora), lambda i: (rb0 + i, offs["cq"] // qlora)),
                pl.BlockSpec((tm, kvlora), lambda i: (rb0 + i, offs["ckv"] // kvlora)),
                pl.BlockSpec((tm, LANES), lambda i: (rb0 + i, offs["kr"] // LANES)),
                pl.BlockSpec((tm, LANES), lambda i: (rb0 + i, 0)),
                pl.BlockSpec((tm, LANES), lambda i: (rb0 + i, 0)),
                pl.BlockSpec((1, qlora), const), pl.BlockSpec((1, kvlora), const),
                pl.BlockSpec(wq_r.shape, const), pl.BlockSpec(wk.shape, const), pl.BlockSpec(wv.shape, const)]
    out_specs = [pl.BlockSpec((tm, kvlora), lambda i: (i, 0)), pl.BlockSpec((tm, LANES), lambda i: (i, 0))]
    out_shape = [jax.ShapeDtypeStruct((nrows, kvlora), F32), jax.ShapeDtypeStruct((nrows, LANES), F32)]
    if absorbed:
        out_specs += [pl.BlockSpec((MLA_HEADS, tm, kvlora), lambda i: (0, i, 0)),
                      pl.BlockSpec((MLA_HEADS, tm, QK_ROPE), lambda i: (0, i, 0))]
        out_shape += [jax.ShapeDtypeStruct((MLA_HEADS, nrows, kvlora), BF16),
                      jax.ShapeDtypeStruct((MLA_HEADS, nrows, QK_ROPE), BF16)]
    else:
        out_specs += [pl.BlockSpec((tm, MLA_HEADS * hw), lambda i: (i, 0)),
                      pl.BlockSpec((tm, MLA_HEADS * hw), lambda i: (i, 0)),
                      pl.BlockSpec((tm, MLA_HEADS * V_HEAD), lambda i: (i, 0))]
        out_shape += [jax.ShapeDtypeStruct((nrows, MLA_HEADS * hw), BF16),
                      jax.ShapeDtypeStruct((nrows, MLA_HEADS * hw), BF16),
                      jax.ShapeDtypeStruct((nrows, MLA_HEADS * V_HEAD), BF16)]
    return pl.pallas_call(
        functools.partial(_mla_prep_body, absorbed=absorbed, scale=scale),
        grid=(nrows // tm,),
        in_specs=in_specs, out_specs=out_specs, out_shape=out_shape,
        compiler_params=_params("arbitrary"),
    )(proj, proj, proj, cos_t, sin_t, q_g.reshape(1, -1), kv_g.reshape(1, -1), wq_r, wk, wv)


ATT_SUB = 256


def _attn_tile(q_sub, k_ref, v_ref, kcols, vcols, s_scr, p_scr, nk):
    sub = ATT_SUB
    nfull = nk - sub
    shift = CHUNK.bit_length() - 1

    def fold(x, op):
        return op(x.reshape(sub // 8, 8, sub), axis=0)

    if nfull:
        s_scr[0:nfull, :] = _dot_t(k_ref[0, 0:nfull, kcols], q_sub)
    krow = lax.broadcasted_iota(I32, (sub, sub), 0) >> shift
    qcol = lax.broadcasted_iota(I32, (sub, sub), 1) >> shift
    s_scr[nfull:nk, :] = jnp.where(krow <= qcol, _dot_t(k_ref[0, nfull:nk, kcols], q_sub), -jnp.inf)

    mrun = None
    for j in range(nk // sub):
        f = fold(s_scr[j * sub:(j + 1) * sub, :], jnp.max)
        mrun = f if mrun is None else jnp.maximum(mrun, f)
    m = jnp.max(mrun, axis=0, keepdims=True)
    lrun = None
    for j in range(nk // sub):
        pt = jnp.exp(s_scr[j * sub:(j + 1) * sub, :] - m)
        f = fold(pt, jnp.sum)
        lrun = f if lrun is None else lrun + f
        p_scr[j * sub:(j + 1) * sub, :] = pt.astype(BF16)
    l = jnp.sum(lrun, axis=0, keepdims=True)
    acc = lax.dot_general(v_ref[0, 0:nk, vcols], p_scr[0:nk, :], (((0,), (0,)), ((), ())),
                          preferred_element_type=F32)
    return (acc / l).T


ATT_HEADS = 4


def _attn_body(q_ref, k_ref, v_ref, o_ref, s_scr, p_scr, *, tq, nq):
    qi = pl.program_id(2)
    sub = ATT_SUB
    hw = q_ref.shape[2] // ATT_HEADS
    for c in range(nq):
        @pl.when(qi == c)
        def _():
            for hh in range(ATT_HEADS):
                kcols = slice(hh * hw, (hh + 1) * hw)
                vcols = slice(hh * V_HEAD, (hh + 1) * V_HEAD)
                for r in range(tq // sub):
                    q_sub = q_ref[0, r * sub:(r + 1) * sub, kcols]
                    out = _attn_tile(q_sub, k_ref, v_ref, kcols, vcols, s_scr, p_scr, c * tq + (r + 1) * sub)
                    o_ref[r * sub:(r + 1) * sub, vcols] = out.astype(o_ref.dtype)


def _attn_prompt(q, k, v, *, nb, seq):
    hw = (QK_NOPE + LANES) * ATT_HEADS
    vw = V_HEAD * ATT_HEADS
    tq = min(1024, seq // 2)
    assert seq % tq == 0 and tq % ATT_SUB == 0 and ATT_SUB % CHUNK == 0 and MLA_HEADS % ATT_HEADS == 0
    nq = seq // tq
    q3 = q.reshape(nb, seq, -1)
    k3 = k.reshape(nb, seq, -1)
    v3 = v.reshape(nb, seq, -1)
    return pl.pallas_call(
        functools.partial(_attn_body, tq=tq, nq=nq),
        grid=(nb, MLA_HEADS // ATT_HEADS, nq),
        in_specs=[pl.BlockSpec((1, tq, hw), lambda b, h, i: (b, i, h)),
                  pl.BlockSpec((1, seq, hw), lambda b, h, i: (b, 0, h)),
                  pl.BlockSpec((1, seq, vw), lambda b, h, i: (b, 0, h))],
        out_specs=pl.BlockSpec((tq, vw), lambda b, h, i: (b * nq + i, h)),
        out_shape=jax.ShapeDtypeStruct((nb * seq, MLA_HEADS * V_HEAD), BF16),
        scratch_shapes=[pltpu.VMEM((seq, ATT_SUB), F32), pltpu.VMEM((seq, ATT_SUB), BF16)],
        compiler_params=_params("arbitrary", "arbitrary", "arbitrary"),
    )(q3, k3, v3)


def _attn_sample_body(ql_ref, qr_ref, pckv_ref, pkr_ref, nckv_ref, nkr_ref, o_ref, *, past, seq):
    nh = ql_ref.shape[0]
    ql = ql_ref[...].reshape(nh * seq, ql_ref.shape[2])
    qr = qr_ref[...].reshape(nh * seq, qr_ref.shape[2])
    pckv = pckv_ref[0].astype(BF16)
    nckv = nckv_ref[...].astype(BF16)
    s_p = _dot_t(ql, pckv) + _dot_t(qr, pkr_ref[0].astype(BF16))
    s_n = _dot_t(ql, nckv) + _dot_t(qr, nkr_ref[:, :QK_ROPE].astype(BF16))
    shift = CHUNK.bit_length() - 1
    q_chunk = (past + lax.broadcasted_iota(I32, (nh * seq, 1), 0) % seq) >> shift
    kp_chunk = lax.broadcasted_iota(I32, (1, past), 1) >> shift
    kn_chunk = (past + lax.broadcasted_iota(I32, (1, seq), 1)) >> shift
    s_p = jnp.where(kp_chunk <= q_chunk, s_p, -jnp.inf)
    s_n = jnp.where(kn_chunk <= q_chunk, s_n, -jnp.inf)
    m = jnp.maximum(jnp.max(s_p, axis=1, keepdims=True), jnp.max(s_n, axis=1, keepdims=True))
    p_p = jnp.exp(s_p - m)
    p_n = jnp.exp(s_n - m)
    l = jnp.sum(p_p, axis=1, keepdims=True) + jnp.sum(p_n, axis=1, keepdims=True)
    o = (_dot(p_p.astype(BF16), pckv) + _dot(p_n.astype(BF16), nckv)) / l
    o_ref[...] = o.reshape(o_ref.shape).astype(o_ref.dtype)


def _attn_sample(qlat, qrope, past_ckv, past_kr, ckv_n, kr_r, *, nb, seq):
    past = past_ckv.shape[1]
    r = past_ckv.shape[2]
    return pl.pallas_call(
        functools.partial(_attn_sample_body, past=past, seq=seq),
        grid=(nb,),
        in_specs=[pl.BlockSpec((MLA_HEADS, seq, r), lambda b: (0, b, 0)),
                  pl.BlockSpec((MLA_HEADS, seq, QK_ROPE), lambda b: (0, b, 0)),
                  pl.BlockSpec((1, past, r), lambda b: (b, 0, 0)),
                  pl.BlockSpec((1, past, QK_ROPE), lambda b: (b, 0, 0)),
                  pl.BlockSpec((seq, r), lambda b: (b, 0)),
                  pl.BlockSpec((seq, LANES), lambda b: (b, 0))],
        out_specs=pl.BlockSpec((MLA_HEADS, seq, r), lambda b: (0, b, 0)),
        out_shape=jax.ShapeDtypeStruct((MLA_HEADS, nb * seq, r), BF16),
        compiler_params=_params("arbitrary"),
    )(qlat, qrope, past_ckv, past_kr, ckv_n, kr_r)


def _uv_body(o_ref, w_ref, y_ref):
    y_ref[...] = _dot(o_ref[0], w_ref[...]).astype(y_ref.dtype)


def _uv_sample(o_lat, wv):
    nh, nrows, r = o_lat.shape
    return pl.pallas_call(
        _uv_body,
        grid=(nh,),
        in_specs=[pl.BlockSpec((1, nrows, r), lambda h: (h, 0, 0)),
                  pl.BlockSpec((r, V_HEAD), lambda h: (0, h))],
        out_specs=pl.BlockSpec((nrows, V_HEAD), lambda h: (0, h)),
        out_shape=jax.ShapeDtypeStruct((nrows, nh * V_HEAD), BF16),
        compiler_params=_params("arbitrary"),
    )(o_lat, wv)


def _merge1_body(ysp_ref, ymp_ref, yss_ref, yms_ref, ws_ref, wm_ref, ga_ref, gb_ref, o_ref, *, npb):
    def run(ys_ref, ym_ref):
        a = _dot(ys_ref[...], ws_ref[...])
        b = _dot(ym_ref[...], wm_ref[...])
        o_ref[...] = (_sigmoid(ga_ref[...]) * a + _sigmoid(gb_ref[...]) * b).astype(o_ref.dtype)

    i = pl.program_id(1)
    pl.when(i < npb)(lambda: run(ysp_ref, ymp_ref))
    pl.when(i >= npb)(lambda: run(yss_ref, yms_ref))


def _merge1(ys_p, ym_p, ys_s, ym_s, w_ssd_out, w_mla_out, proj, offs):
    (tp, inner), ts = ys_p.shape, ys_s.shape[0]
    dm = ym_p.shape[1]
    d = w_ssd_out.shape[1]
    tm = _pick(math.gcd(tp, ts), (512, 256, 128))
    tn = 512
    npb, nsb = tp // tm, ts // tm
    prow = lambda j, i: (jnp.minimum(i, npb - 1), 0)
    srow = lambda j, i: (jnp.maximum(i - npb, 0), 0)
    return pl.pallas_call(
        functools.partial(_merge1_body, npb=npb),
        grid=(d // tn, npb + nsb),
        in_specs=[pl.BlockSpec((tm, inner), prow), pl.BlockSpec((tm, dm), prow),
                  pl.BlockSpec((tm, inner), srow), pl.BlockSpec((tm, dm), srow),
                  pl.BlockSpec((inner, tn), lambda j, i: (0, j)),
                  pl.BlockSpec((dm, tn), lambda j, i: (0, j)),
                  pl.BlockSpec((tm, tn), lambda j, i: (i, offs["ga"] // tn + j)),
                  pl.BlockSpec((tm, tn), lambda j, i: (i, offs["gb"] // tn + j))],
        out_specs=pl.BlockSpec((tm, tn), lambda j, i: (i, j)),
        out_shape=jax.ShapeDtypeStruct((tp + ts, d), BF16),
        compiler_params=_params("arbitrary", "arbitrary"),
    )(ys_p, ym_p, ys_s, ym_s, w_ssd_out, w_mla_out, proj, proj)


def _merge2_body(m_ref, w_ref, xp_ref, xs_ref, g1_ref, sc_ref, sh_ref, ng_ref, wr_ref, x1_ref, h2_ref, lg_ref,
                 *, npb):
    rows, d = xp_ref.shape
    x = jnp.where(pl.program_id(0) < npb, xp_ref[...], xs_ref[...])
    upd = _dot(m_ref[...], w_ref[...]).reshape(rows // MOD_ROWS, MOD_ROWS, d) * g1_ref[...]
    x1 = x + upd.reshape(rows, d)
    x1_ref[...] = x1
    h2 = _modulate(_rms(x1) * ng_ref[...], sc_ref, sh_ref)
    _store_items(h2_ref, _pack_halves(h2))
    lg_ref[...] = _dot3(wr_ref[...], h2, dot=_dot_t)


def _merge2(merged, w_merge, xp, xs, modg, norm_g, w_router_t):
    d = xp.shape[1]
    t = xp.shape[0] + xs.shape[0]
    assert d == 2 * SUBLANES * LANES
    ne = w_router_t.shape[0]
    tm, npb, prow, srow = _two_stream_rows(xp.shape[0], xs.shape[0], (512, 256, 128))
    ng = tm // MOD_ROWS
    return pl.pallas_call(
        functools.partial(_merge2_body, npb=npb),
        grid=(t // tm,),
        in_specs=[pl.BlockSpec((tm, d), lambda i: (i, 0)),
                  pl.BlockSpec((d, d), lambda i: (0, 0)),
                  pl.BlockSpec((tm, d), prow), pl.BlockSpec((tm, d), srow),
                  pl.BlockSpec((ng, 1, d), lambda i: (i, 0, 2)),
                  pl.BlockSpec((ng, 1, d), lambda i: (i, 0, 4)),
                  pl.BlockSpec((ng, 1, d), lambda i: (i, 0, 3)),
                  pl.BlockSpec((1, d), lambda i: (0, 0)),
                  pl.BlockSpec((ne, d), lambda i: (0, 0))],
        out_specs=[pl.BlockSpec((tm, d), lambda i: (i, 0)),
                   pl.BlockSpec((tm * SUBLANES, LANES), lambda i: (i, 0)),
                   pl.BlockSpec((ne, tm), lambda i: (0, i))],
        out_shape=[jax.ShapeDtypeStruct((t, d), F32), jax.ShapeDtypeStruct((t * SUBLANES, LANES), U32),
                   jax.ShapeDtypeStruct((ne, t), F32)],
        compiler_params=_params("arbitrary"),
    )(merged, w_merge, xp, xs, modg, modg, modg, norm_g.reshape(1, d), w_router_t)


def _route_body(lg_ref, eb_ref, eidx_ref, pos_ref, w_ref, cnt_ref, carry):
    ne, tr = lg_ref.shape
    per_group = ne // N_GROUPS

    @pl.when(pl.program_id(0) == 0)
    def _():
        carry[...] = jnp.zeros_like(carry)

    scores = _sigmoid(lg_ref[...])
    choice = scores + eb_ref[...]
    sub = lax.broadcasted_iota(I32, (per_group, tr), 0)
    gscore, blocks = [], []
    for g in range(N_GROUPS):
        blk = choice[g * per_group:(g + 1) * per_group, :]
        m1 = jnp.max(blk, axis=0, keepdims=True)
        first = jnp.min(jnp.where(blk == m1, sub, per_group), axis=0, keepdims=True)
        m2 = jnp.max(jnp.where(sub == first, -jnp.inf, blk), axis=0, keepdims=True)
        gscore.append(m1 + m2)
        blocks.append(blk)
    masked = []
    for g in range(N_GROUPS):
        rank = jnp.zeros((1, tr), I32)
        for g2 in range(N_GROUPS):
            if g2 == g:
                continue
            beats = (gscore[g2] > gscore[g]) | ((gscore[g2] == gscore[g]) & (g2 < g))
            rank = rank + beats.astype(I32)
        masked.append(jnp.where(rank < TOPK_GROUPS, blocks[g], -jnp.inf))
    cm = jnp.concatenate(masked, axis=0)

    eid = lax.broadcasted_iota(I32, (ne, tr), 0)
    rank = jnp.zeros((ne, tr), I32)
    for e2 in range(ne):
        rowv = cm[e2:e2 + 1, :]
        beats = (rowv > cm) | ((rowv == cm) & (eid > e2))
        rank = rank + beats.astype(I32)
    sel = rank < TOP_K
    wsel = jnp.where(sel, scores, 0.0)
    wfull = wsel / jnp.sum(wsel, axis=0, keepdims=True) * ROUTED_SCALE

    r = lax.broadcasted_iota(I32, (tr, tr), 0)
    c = lax.broadcasted_iota(I32, (tr, tr), 1)
    before = jnp.where(r < c, 1.0, 0.0).astype(BF16)
    self = jnp.where(sel, 1.0, 0.0)
    pos = carry[:, 0:1] + _dot(self.astype(BF16), before)
    carry[...] = carry[...] + jnp.sum(self, axis=1, keepdims=True)
    cnt_ref[...] = carry[...]

    eid_f = eid.astype(F32)
    for k in range(TOP_K):
        pick = sel & (rank == k)
        eidx_ref[k:k + 1, :] = jnp.sum(jnp.where(pick, eid_f, 0.0), axis=0, keepdims=True).astype(I32)
        pos_ref[k:k + 1, :] = jnp.sum(jnp.where(pick, pos, 0.0), axis=0, keepdims=True).astype(I32)
        w_ref[k:k + 1, :] = jnp.sum(jnp.where(pick, wfull, 0.0), axis=0, keepdims=True)


def _route(logits_t, e_bias):
    ne, t = logits_t.shape
    tr = _pick(t, (512, 256, 128))
    return pl.pallas_call(
        _route_body,
        grid=(t // tr,),
        in_specs=[pl.BlockSpec((ne, tr), lambda i: (0, i)), pl.BlockSpec((ne, 1), lambda i: (0, 0))],
        out_specs=[pl.BlockSpec((TOP_K, tr), lambda i: (0, i)), pl.BlockSpec((TOP_K, tr), lambda i: (0, i)),
                   pl.BlockSpec((TOP_K, tr), lambda i: (0, i)), pl.BlockSpec((ne, LANES), lambda i: (0, 0))],
        out_shape=[jax.ShapeDtypeStruct((TOP_K, t), I32), jax.ShapeDtypeStruct((TOP_K, t), I32),
                   jax.ShapeDtypeStruct((TOP_K, t), F32), jax.ShapeDtypeStruct((ne, LANES), F32)],
        scratch_shapes=[pltpu.VMEM((ne, LANES), F32)],
        compiler_params=_params("arbitrary"),
    )(logits_t, e_bias.reshape(ne, 1))


def _row_copy(src, s, dst, d, sem):
    rows = lambda i: pl.ds(pl.multiple_of(i * SUBLANES, SUBLANES), SUBLANES)
    return pltpu.make_async_copy(src.at[rows(s), :], dst.at[rows(d), :], sem)


def _load_items(ref, start, n):
    return jnp.concatenate([ref[pl.ds(start * SUBLANES + c, n, stride=SUBLANES), :] for c in range(SUBLANES)],
                           axis=1)


def _store_items(ref, val):
    n = val.shape[0]
    for c in range(SUBLANES):
        ref[pl.ds(c, n, stride=SUBLANES), :] = val[:, c * LANES:(c + 1) * LANES]


def _drain(wait_one, n, group=64):
    assert n % group == 0

    def body(j, c):
        for _ in range(group):
            wait_one()
        return c

    lax.fori_loop(0, n // group, body, 0)


def _dispatch_body(fill_lo_ref, fill_hi_ref, h_ref, dest_ref, hs_ref, dest_s, zrow, sem, dsem, *, tb):
    i = pl.program_id(0)
    cp = pltpu.make_async_copy(dest_ref.at[i], dest_s, dsem)
    cp.start()

    @pl.when(i == 0)
    def _():
        zrow[...] = jnp.zeros_like(zrow)

        def per_expert(fn):
            def body(e, _):
                lax.fori_loop(fill_lo_ref[e], fill_hi_ref[e], lambda s, c: (fn(s), c)[1], 0)
                return 0
            lax.fori_loop(0, fill_lo_ref.shape[0], body, 0)

        per_expert(lambda s: _row_copy(zrow, 0, hs_ref, s, sem).start())
        per_expert(lambda s: _row_copy(zrow, 0, hs_ref, s, sem).wait())

    cp.wait()

    def issue(t, c):
        for k in range(TOP_K):
            _row_copy(h_ref, t, hs_ref, dest_s[t * TOP_K + k], sem).start(priority=k % 2)
        return c

    lax.fori_loop(0, tb, issue, 0, unroll=4)
    _drain(lambda: _row_copy(h_ref, 0, hs_ref, 0, sem).wait(), TOP_K * tb)


def _dispatch(h2, dest_blk, fill_lo, fill_hi, n_slots):
    nblk, n = dest_blk.shape
    tb = n // TOP_K
    return pl.pallas_call(
        functools.partial(_dispatch_body, tb=tb),
        grid_spec=pltpu.PrefetchScalarGridSpec(
            num_scalar_prefetch=2,
            grid=(nblk,),
            in_specs=[pl.BlockSpec((tb * SUBLANES, LANES), lambda i, lo, hi: (i, 0)),
                      pl.BlockSpec(memory_space=pl.ANY)],
            out_specs=pl.BlockSpec(memory_space=pl.ANY),
            scratch_shapes=[pltpu.SMEM((n,), I32), pltpu.VMEM((SUBLANES, LANES), h2.dtype),
                            pltpu.SemaphoreType.DMA, pltpu.SemaphoreType.DMA]),
        out_shape=jax.ShapeDtypeStruct((n_slots * SUBLANES, LANES), h2.dtype),
        compiler_params=_params("arbitrary"),
    )(fill_lo, fill_hi, h2, dest_blk)


def _experts_body(be_ref, nu_ref, x_ref, wg_ref, wu_ref, wd_ref, o_ref, wgu_s, wd_s):
    i = pl.program_id(0)
    hid = wg_ref.shape[2]

    @pl.when(i < nu_ref[0])
    def _():
        @pl.when((i == 0) | (be_ref[i] != be_ref[jnp.maximum(i - 1, 0)]))
        def _():
            wgu_s[:, :hid] = wg_ref[0].astype(BF16)
            wgu_s[:, hid:] = wu_ref[0].astype(BF16)
            wd_s[...] = wd_ref[0].astype(BF16)

        lo, hi = _unpack_halves(_load_items(x_ref, 0, x_ref.shape[0] // SUBLANES))
        x = jnp.concatenate([lo.astype(BF16), hi.astype(BF16)], axis=1)
        gu = _dot(x, wgu_s[...])
        act = (_silu(gu[:, :hid]) * gu[:, hid:]).astype(BF16)
        _store_items(o_ref, _pack_halves(_dot(act, wd_s[...])))


def _experts(hs, block_e, n_used, wg, wu, wd):
    n_slots = hs.shape[0] // SUBLANES
    ne, d, hid = wg.shape
    assert d == 2 * SUBLANES * LANES
    bm = MOE_ROWS
    nblocks = n_slots // bm
    blk = lambda i, be, nu: (jnp.minimum(i, nu[0] - 1), 0)
    return pl.pallas_call(
        _experts_body,
        grid_spec=pltpu.PrefetchScalarGridSpec(
            num_scalar_prefetch=2,
            grid=(nblocks,),
            in_specs=[pl.BlockSpec((bm * SUBLANES, LANES), blk),
                      pl.BlockSpec((1, d, hid), lambda i, be, nu: (be[i], 0, 0)),
                      pl.BlockSpec((1, d, hid), lambda i, be, nu: (be[i], 0, 0)),
                      pl.BlockSpec((1, hid, d), lambda i, be, nu: (be[i], 0, 0))],
            out_specs=pl.BlockSpec((bm * SUBLANES, LANES), blk),
            scratch_shapes=[pltpu.VMEM((d, 2 * hid), BF16), pltpu.VMEM((hid, d), BF16)]),
        out_shape=jax.ShapeDtypeStruct(hs.shape, U32),
        compiler_params=_params("arbitrary"),
    )(block_e, n_used, hs, wg, wu, wd)


def _combine_body(ys_ref, dest_ref, w_ref, h_ref, x1_ref, g2_ref, wgu_ref, wd_ref, fg_ref, op_ref, os_ref,
                  dest_s, gbuf, sem, dsem, *, tb, npb):
    i = pl.program_id(0)
    cp = pltpu.make_async_copy(dest_ref.at[i], dest_s, dsem)
    cp.start()
    cp.wait()

    def issue(t, c):
        for k in range(TOP_K):
            _row_copy(ys_ref, dest_s[t * TOP_K + k], gbuf, k * tb + t, sem).start(priority=1)
        return c

    lax.fori_loop(0, tb, issue, 0, unroll=4)

    hid = wd_ref.shape[0]
    h_lo, h_hi = _unpack_halves(_load_items(h_ref, 0, tb))
    gu = _dot(jnp.concatenate([h_lo.astype(BF16), h_hi.astype(BF16)], axis=1), wgu_ref[...])
    moe = _dot((_silu(gu[:, :hid]) * gu[:, hid:]).astype(BF16), wd_ref[...])

    _drain(lambda: _row_copy(ys_ref, 0, gbuf, 0, sem).wait(), TOP_K * tb)
    w = w_ref[...]
    dh = SUBLANES * LANES
    m_lo, m_hi = moe[:, :dh], moe[:, dh:]
    for k in range(TOP_K):
        lo, hi = _unpack_halves(_load_items(gbuf, k * tb, tb))
        m_lo = m_lo + lo * w[:, k:k + 1]
        m_hi = m_hi + hi * w[:, k:k + 1]
    moe = jnp.concatenate([m_lo, m_hi], axis=1)
    rows, d = moe.shape
    upd = moe.reshape(rows // MOD_ROWS, MOD_ROWS, d) * g2_ref[...]
    x2 = x1_ref[...] + upd.reshape(rows, d)
    y = _rms(x2) * fg_ref[...]

    @pl.when(i < npb)
    def _():
        op_ref[...] = y

    @pl.when(i >= npb)
    def _():
        os_ref[...] = y


def _combine(ys, dest_blk, w_tok, h2, x1, modg, wsh_gu, wsh_d, final_g, *, tp):
    t, d = x1.shape
    nblk, n = dest_blk.shape
    tb = n // TOP_K
    assert tp % tb == 0 and (t - tp) % tb == 0
    npb = tp // tb
    ng = tb // MOD_ROWS
    const = lambda i: (0, 0)
    return pl.pallas_call(
        functools.partial(_combine_body, tb=tb, npb=npb),
        grid=(nblk,),
        in_specs=[pl.BlockSpec(memory_space=pl.ANY),
                  pl.BlockSpec(memory_space=pl.ANY),
                  pl.BlockSpec((tb, TOP_K), lambda i: (i, 0)),
                  pl.BlockSpec((tb * SUBLANES, LANES), lambda i: (i, 0)),
                  pl.BlockSpec((tb, d), lambda i: (i, 0)),
                  pl.BlockSpec((ng, 1, d), lambda i: (i, 0, 5)),
                  pl.BlockSpec(wsh_gu.shape, const), pl.BlockSpec(wsh_d.shape, const),
                  pl.BlockSpec((1, d), const)],
        out_specs=[pl.BlockSpec((tb, d), lambda i: (jnp.minimum(i, npb - 1), 0)),
                   pl.BlockSpec((tb, d), lambda i: (jnp.maximum(i - npb, 0), 0))],
        out_shape=[jax.ShapeDtypeStruct((tp, d), F32), jax.ShapeDtypeStruct((t - tp, d), F32)],
        scratch_shapes=[pltpu.SMEM((n,), I32), pltpu.VMEM((n * SUBLANES, LANES), U32),
                        pltpu.SemaphoreType.DMA, pltpu.SemaphoreType.DMA],
        compiler_params=_params("arbitrary"),
    )(ys, dest_blk, w_tok, h2, x1, modg, wsh_gu, wsh_d, final_g.reshape(1, d))


def _rope_tables(pos):
    half = QK_ROPE // 2
    freqs = ROPE_THETA ** (-jnp.arange(half, dtype=F32) / half)
    ang = pos.astype(F32)[:, None] * freqs[None, :]
    cos, sin = jnp.cos(ang), jnp.sin(ang)
    return (jnp.concatenate([cos, cos, cos, cos], axis=1),
            jnp.concatenate([-sin, sin, -sin, sin], axis=1))


def _blocked(a, tb):
    k, t = a.shape
    return a.T.reshape(t // tb, tb * k)


def kernel(x_prompt, x_sample, c_prompt, c_sample, cache_conv, state_ssm, cache_ckv, cache_kr, w_ada, b_ada,
           norm1_g, norm2_g, w_in, conv_w, conv_b, dt_bias, a_log, d_skip, ssd_norm_g, w_ssd_out, q_norm_g,
           w_uq, kv_norm_g, w_uk, w_uv, w_mla_out, w_merge_out, w_router, e_bias, w_exp_gate, w_exp_up,
           w_exp_down, w_sh_gate, w_sh_up, w_sh_down, final_norm_g):
    depth = w_in.shape[0]
    assert depth == 1
    bp, seq, d = x_prompt.shape
    bs, lseq, _ = x_sample.shape
    assert lseq == MOD_ROWS and seq % MOD_ROWS == 0
    tp, ts = bp * seq, bs * lseq
    t_all = tp + ts
    nheads = dt_bias.shape[1]
    inner = nheads * SSD_HEAD_DIM
    cdim = conv_w.shape[2]
    gn = SSD_GROUPS * SSD_STATE
    qlora, kvlora = q_norm_g.shape[1], kv_norm_g.shape[1]
    ne = w_router.shape[2]
    assert nheads == 64 and 2 * nheads == LANES

    w = w_in[0]
    o_xbc, o_dt = inner, inner + cdim
    o_cq = o_dt + nheads
    o_ckv = o_cq + qlora
    o_kr = o_ckv + kvlora
    o_gate = o_kr + QK_ROPE
    perm = np.concatenate([np.arange(0, nheads, 2), np.arange(1, nheads, 2)])
    zc = lambda n: jnp.zeros((d, n), w.dtype)
    cols = [w[:, :inner], w[:, o_gate:], w[:, o_xbc:o_dt], w[:, o_cq:o_ckv], w[:, o_ckv:o_kr],
            w[:, o_dt:o_cq][:, perm], zc(LANES - nheads), w[:, o_kr:o_gate], zc(LANES - QK_ROPE)]
    used = inner + 2 * d + cdim + qlora + kvlora + 2 * LANES
    total = -(-used // 512) * 512
    cols.append(zc(total - used))
    w_r = jnp.concatenate(cols, axis=1).astype(BF16)
    offs = {"z": 0, "ga": inner, "gb": inner + d, "x": inner + 2 * d, "bc": 2 * inner + 2 * d}
    offs["cq"] = offs["bc"] + 2 * gn
    offs["ckv"] = offs["cq"] + qlora
    offs["dt"] = offs["ckv"] + kvlora
    offs["kr"] = offs["dt"] + LANES

    hw = QK_NOPE + LANES
    wq_r = jnp.pad(w_uq[0], ((0, 0), (0, 0), (0, hw - QK_NOPE - QK_ROPE))).reshape(qlora, MLA_HEADS * hw).astype(BF16)
    wk = w_uk[0].reshape(kvlora, MLA_HEADS * QK_NOPE).astype(BF16)
    wv = w_uv[0].reshape(kvlora, MLA_HEADS * V_HEAD).astype(BF16)
    pad_l = lambda v: jnp.pad(v[perm], (0, LANES - nheads)).reshape(1, LANES)
    dtb = pad_l(dt_bias[0])
    a_neg = pad_l(-jnp.exp(a_log[0]))
    dskip = jnp.repeat(d_skip[0], SSD_HEAD_DIM).reshape(1, inner)

    c_all = jnp.concatenate([c_prompt, c_sample], axis=0)
    mod = _ada(c_all, w_ada[0], b_ada[0])
    grp = np.concatenate([np.repeat(np.arange(bp), seq // MOD_ROWS), bp + np.arange(bs)])
    modg = mod[grp].reshape(t_all // MOD_ROWS, 1, 6 * d)

    xp, xs = x_prompt.reshape(tp, d), x_sample.reshape(ts, d)
    proj = _inproj(xp, xs, modg, norm1_g[0], w_r, d)

    ssd_args = (conv_w[0], conv_b[0].reshape(1, cdim), dtb, a_neg, dskip, ssd_norm_g[0].reshape(1, inner))
    ys_p, conv_p, ssm_p = _ssd(proj, jnp.zeros((bp, CONV_W - 1, cdim), F32),
                               jnp.zeros((1, bp) + state_ssm.shape[2:], F32), *ssd_args,
                               nb=bp, seq=seq, row0=0, offs=offs, inner=inner)
    ys_s, conv_s, ssm_s = _ssd(proj, cache_conv[0], state_ssm,
                               *ssd_args, nb=bs, seq=lseq, row0=tp, offs=offs, inner=inner)

    past = cache_ckv.shape[2]
    cos_p, sin_p = _rope_tables(jnp.arange(seq))
    cos_s, sin_s = _rope_tables(past + jnp.arange(lseq))
    cos_t = jnp.concatenate([jnp.tile(cos_p, (bp, 1)), jnp.tile(cos_s, (bs, 1))], axis=0)
    sin_t = jnp.concatenate([jnp.tile(sin_p, (bp, 1)), jnp.tile(sin_s, (bs, 1))], axis=0)
    prep = functools.partial(_mla_prep, proj, cos_t, sin_t, q_norm_g[0], kv_norm_g[0], wq_r, wk, wv, offs=offs)
    ckv_p, kr_p, q_p, k_p, v_p = prep(row0=0, nrows=tp, absorbed=False)
    ckv_s, kr_s, qlat, qrope = prep(row0=tp, nrows=ts, absorbed=True)
    ym_p = _attn_prompt(q_p, k_p, v_p, nb=bp, seq=seq)
    o_lat = _attn_sample(qlat, qrope, cache_ckv[0], cache_kr[0], ckv_s, kr_s, nb=bs, seq=lseq)
    ym_s = _uv_sample(o_lat, wv)

    merged = _merge1(ys_p, ym_p, ys_s, ym_s, w_ssd_out[0].astype(BF16), w_mla_out[0].astype(BF16), proj, offs)
    x1, h2, logits_t = _merge2(merged, w_merge_out[0].astype(BF16), xp, xs, modg, norm2_g[0], w_router[0].T)

    eidx, pos, w_sel, cnt = _route(logits_t, e_bias[0])
    bm = MOE_ROWS
    counts = cnt[:, 0].astype(I32)
    padded = (counts + bm - 1) // bm * bm
    pad_end = jnp.cumsum(padded)
    pad_start = pad_end - padded
    nblocks = -(-(t_all * TOP_K) // bm) + ne
    onehot = eidx[:, :, None] == jnp.arange(ne, dtype=I32)
    dest = jnp.sum(jnp.where(onehot, pad_start, 0), axis=-1) + pos
    blk_start = jnp.arange(nblocks, dtype=I32) * bm
    block_e = jnp.minimum(jnp.sum(pad_end[None, :] <= blk_start[:, None], axis=1), ne - 1).astype(I32)
    n_used = (pad_end[-1:] // bm).astype(I32)

    tb_d = _pick(t_all, (1024, 512, 256, 128))
    hs = _dispatch(h2, _blocked(dest, tb_d), (pad_start + counts).astype(I32), pad_end.astype(I32), nblocks * bm)
    ys = _experts(hs, block_e, n_used, w_exp_gate[0], w_exp_up[0], w_exp_down[0])
    wsh_gu = jnp.concatenate([w_sh_gate[0], w_sh_up[0]], axis=1).astype(BF16)
    tb_c = _pick(math.gcd(tp, ts), (256, 128))
    y_p, y_s = _combine(ys, _blocked(dest, tb_c), w_sel.T, h2, x1, modg, wsh_gu, w_sh_down[0].astype(BF16),
                        final_norm_g, tp=tp)

    r5 = lambda a, n, l: a.reshape(1, n, l, a.shape[-1])
    return (y_p.reshape(bp, seq, d), y_s.reshape(bs, lseq, d),
            conv_p[None], ssm_p,
            r5(ckv_p, bp, seq), r5(kr_p[:, :QK_ROPE], bp, seq),
            conv_s[None], ssm_s,
            r5(ckv_s, bs, lseq), r5(kr_s[:, :QK_ROPE], bs, lseq))
```
